```python
import math
import jax, jax.numpy as jnp
from jax import lax
import numpy as np

D_MODEL = 1024
BATCH = 32
SEQ = 256
DEPTH = 2
DEC_BATCH = 2
DEC_SEQ = 4096
PAST_LEN = 256

GRID_W = 64
ROPE_BASE = 10000.0
EPS = 1e-6
BLK = 128
WINDOW = 128
NEG_INF = -1e30

N_AB = (DEPTH + 1) // 2
N_C = DEPTH // 2

A_HEADS = 8
A_KV_HEADS = 2
A_GROUP = A_HEADS // A_KV_HEADS
A_HEAD_DIM = 64
A_WIDTH = A_HEADS * A_HEAD_DIM
A_KV_WIDTH = A_KV_HEADS * A_HEAD_DIM
A_SCALE = A_HEAD_DIM ** -0.5

SSD_HEADS = 16
SSD_HEAD_DIM = 64
SSD_INNER = SSD_HEADS * SSD_HEAD_DIM
SSD_GROUPS = 2
SSD_STATE = 64
CONV_K = 5
CONV_CH = SSD_INNER + 2 * SSD_GROUPS * SSD_STATE
CHUNK = 128

AB_IN = 2 * A_WIDTH + 2 * A_KV_WIDTH + SSD_INNER + CONV_CH + 2 * SSD_HEADS
AB_OUT_IN = A_WIDTH + SSD_INNER

MLA_HEADS = 16
MLA_NOPE = 64
MLA_ROPE = 32
MLA_V = 64
Q_LORA = 256
KV_LORA = 128
MLA_WIDTH = MLA_HEADS * MLA_V
MLA_IN = Q_LORA + KV_LORA + MLA_ROPE + MLA_WIDTH
MLA_SCALE = (MLA_NOPE + MLA_ROPE) ** -0.5

kernel_name = "hybrid_swa_ssd_mla_prefix_dit_step"


def rms_norm(x, w):
    xf = x.astype(jnp.float32)
    y = xf * lax.rsqrt(jnp.mean(xf * xf, axis=-1, keepdims=True) + EPS)
    return (y * w.astype(jnp.float32)).astype(x.dtype)


def split_cols(t, sizes):
    idx = [int(s) for s in np.cumsum(sizes)[:-1]]
    return jnp.split(t, idx, axis=-1)


def modulation(cond, w_ada, b_ada):
    m = jax.nn.silu(cond) @ w_ada + b_ada
    shift, scale, gate = jnp.split(m[:, None, :], 3, axis=-1)
    return shift, scale, gate


def axial_rope_tables(length, dim, dtype):
    rows = length // GRID_W
    row = jnp.repeat(jnp.arange(rows), GRID_W).astype(jnp.float32)
    col = jnp.tile(jnp.arange(GRID_W), rows).astype(jnp.float32)
    nf = dim // 4
    inv = 1.0 / (ROPE_BASE ** (jnp.arange(nf, dtype=jnp.float32) / nf))
    ar = row[:, None] * inv[None, :]
    ac = col[:, None] * inv[None, :]
    ang = jnp.concatenate([ar, ar, ac, ac], axis=-1)
    return jnp.cos(ang).astype(dtype), jnp.sin(ang).astype(dtype)


def apply_rope(x, cos, sin):
    shape = (cos.shape[0],) + (1,) * (x.ndim - 3) + (cos.shape[1],)
    half = x.shape[-1] // 2
    nf = half // 2

    def rot(u):
        return jnp.concatenate([-u[..., nf:], u[..., :nf]], axis=-1)

    xr = jnp.concatenate([rot(x[..., :half]), rot(x[..., half:])], axis=-1)
    return x * cos.reshape(shape) + xr * sin.reshape(shape)


def dense_attention(q, k, v, scale, sink=None):
    b, lq, kvh, g, dk = q.shape
    nb = lq // BLK
    qb = jnp.moveaxis(q.reshape(b, nb, BLK, kvh, g, dk), 1, 0)

    def one_block(qblk):
        s = jnp.einsum("bqhgd,bkhd->bhgqk", qblk, k).astype(jnp.float32) * scale
        if sink is not None:
            s_sink = jnp.broadcast_to(sink.astype(jnp.float32)[None, :, :, None, None], s.shape[:-1] + (1,))
            p = jax.nn.softmax(jnp.concatenate([s, s_sink], axis=-1), axis=-1)[..., :-1]
        else:
            p = jax.nn.softmax(s, axis=-1)
        return jnp.einsum("bhgqk,bkhd->bqhgd", p.astype(v.dtype), v)

    out = lax.map(one_block, qb)
    return jnp.moveaxis(out, 0, 1).reshape(b, lq, kvh, g, v.shape[-1])


def banded_attention(q, k, v, k_ctx, v_ctx, sink, scale):
    b, L, kvh, g, d = q.shape
    nb = L // BLK
    qb = q.reshape(b, nb, BLK, kvh, g, d)

    def windows(t):
        tb = t.reshape(b, nb, BLK, kvh, d)
        tp = jnp.pad(tb, ((0, 0), (1, 1), (0, 0), (0, 0), (0, 0)))
        return jnp.concatenate([tp[:, :-2], tp[:, 1:-1], tp[:, 2:]], axis=2)

    kw = windows(k)
    vw = windows(v)
    qi = jnp.arange(BLK)
    kj = jnp.arange(3 * BLK) - BLK
    rel = kj[None, :] - qi[:, None]
    kpos = jnp.arange(nb)[:, None] * BLK + kj[None, :]
    valid = (jnp.abs(rel) <= WINDOW)[None] & ((kpos >= 0) & (kpos < L))[:, None, :]
    s_loc = jnp.einsum("bnqhgd,bnkhd->bnhgqk", qb, kw).astype(jnp.float32) * scale
    s_loc = jnp.where(valid[None, :, None, None], s_loc, NEG_INF)
    s_ctx = jnp.einsum("bnqhgd,bchd->bnhgqc", qb, k_ctx).astype(jnp.float32) * scale
    s_sink = jnp.broadcast_to(sink.astype(jnp.float32)[None, None, :, :, None, None], s_loc.shape[:-1] + (1,))
    p = jax.nn.softmax(jnp.concatenate([s_loc, s_ctx, s_sink], axis=-1), axis=-1).astype(v.dtype)
    nk = 3 * BLK
    nc = k_ctx.shape[1]
    out = (jnp.einsum("bnhgqk,bnkhd->bnqhgd", p[..., :nk], vw)
           + jnp.einsum("bnhgqc,bchd->bnqhgd", p[..., nk:nk + nc], v_ctx))
    return out.reshape(b, L, kvh, g, d)


def centred_conv(u, w, bias):
    ch = u.shape[-1]
    pad = CONV_K // 2
    out = lax.conv_general_dilated(u, w[:, None, :], window_strides=(1,), padding=[(pad, pad)],
                                   dimension_numbers=("NWC", "WIO", "NWC"), feature_group_count=ch)
    return out + bias


def ssd_scan(x, dt, a_head, bm, cm, h0):
    f32 = jnp.float32
    b, L, H, P = x.shape
    G, N = bm.shape[2], bm.shape[3]
    nc = L // CHUNK
    rep = H // G
    bh = jnp.repeat(bm, rep, axis=2).astype(f32).reshape(b, nc, CHUNK, H, N)
    ch = jnp.repeat(cm, rep, axis=2).astype(f32).reshape(b, nc, CHUNK, H, N)
    xd = (x.astype(f32) * dt[..., None]).reshape(b, nc, CHUNK, H, P)
    la = (dt * a_head.astype(f32)).reshape(b, nc, CHUNK, H)
    cum = jnp.cumsum(la, axis=2)
    causal = jnp.tril(jnp.ones((CHUNK, CHUNK), dtype=bool))[:, :, None]
    diff = cum[:, :, :, None, :] - cum[:, :, None, :, :]
    decay = jnp.exp(jnp.where(causal, diff, -jnp.inf))
    scores = jnp.einsum("bclhn,bcshn->bclsh", ch, bh) * decay
    y_diag = jnp.einsum("bclsh,bcshp->bclhp", scores, xd)
    w_state = jnp.exp(cum[:, :, -1:, :] - cum)
    states = jnp.einsum("bcshn,bcsh,bcshp->bchpn", bh, w_state, xd)
    chunk_decay = jnp.exp(cum[:, :, -1, :])

    def step(hc, inp):
        st, dec = inp
        return hc * dec[:, :, None, None] + st, hc

    h_final, h_in = lax.scan(step, h0.astype(f32), (jnp.moveaxis(states, 1, 0), jnp.moveaxis(chunk_decay, 1, 0)))
    h_in = jnp.moveaxis(h_in, 0, 1)
    y_off = jnp.einsum("bclhn,bchpn,bclh->bclhp", ch, h_in, jnp.exp(cum))
    y = (y_diag + y_off).reshape(b, L, H, P)
    return y.astype(x.dtype), h_final.astype(x.dtype)


def ssd_bidir(xs, bm, cm, dt, dt_bias, a_log, d_skip, h0_f, h0_b):
    dtp = jax.nn.softplus((dt + dt_bias).astype(jnp.float32))
    a = -jnp.exp(a_log.astype(jnp.float32))
    y_f, s_f = ssd_scan(xs, dtp[:, :, 0], a[0], bm, cm, h0_f)
    flip = lambda t: jnp.flip(t, axis=1)
    y_b, s_b = ssd_scan(flip(xs), flip(dtp[:, :, 1]), a[1], flip(bm), flip(cm), h0_b)
    y = y_f + flip(y_b) + d_skip[:, None] * xs
    return y, s_f, s_b


def ab_inputs(h, w_in, conv_w, conv_b):
    b, L, _ = h.shape
    q, k, v, g, z, xbc, dt = split_cols(h @ w_in, (A_WIDTH, A_KV_WIDTH, A_KV_WIDTH, A_WIDTH, SSD_INNER, CONV_CH, 2 * SSD_HEADS))
    q = q.reshape(b, L, A_KV_HEADS, A_GROUP, A_HEAD_DIM)
    k = k.reshape(b, L, A_KV_HEADS, A_HEAD_DIM)
    v = v.reshape(b, L, A_KV_HEADS, A_HEAD_DIM)
    xbc = jax.nn.silu(centred_conv(xbc, conv_w, conv_b))
    xs, bm, cm = split_cols(xbc, (SSD_INNER, SSD_GROUPS * SSD_STATE, SSD_GROUPS * SSD_STATE))
    xs = xs.reshape(b, L, SSD_HEADS, SSD_HEAD_DIM)
    bm = bm.reshape(b, L, SSD_GROUPS, SSD_STATE)
    cm = cm.reshape(b, L, SSD_GROUPS, SSD_STATE)
    dt = dt.reshape(b, L, 2, SSD_HEADS)
    return q, k, v, g, z, xs, bm, cm, dt


def ab_output(attn, g, y, z, gnorm_w, w_out):
    b, L = g.shape[0], g.shape[1]
    a_out = attn.reshape(b, L, A_WIDTH) * jax.nn.silu(g)
    s_out = rms_norm(y.reshape(b, L, SSD_INNER) * jax.nn.silu(z), gnorm_w)
    return jnp.concatenate([a_out, s_out], axis=-1) @ w_out


def ab_context(h, w_in, sink, conv_w, conv_b, dt_bias, a_log, d_skip, gnorm_w, w_out):
    q, k, v, g, z, xs, bm, cm, dt = ab_inputs(h, w_in, conv_w, conv_b)
    attn = dense_attention(q, k, v, A_SCALE, sink.reshape(A_KV_HEADS, A_GROUP))
    h0 = jnp.zeros((h.shape[0], SSD_HEADS, SSD_HEAD_DIM, SSD_STATE), h.dtype)
    y, s_f, s_b = ssd_bidir(xs, bm, cm, dt, dt_bias, a_log, d_skip, h0, h0)
    return ab_output(attn, g, y, z, gnorm_w, w_out), k, v, s_f, s_b


def ab_latent(h, k_ctx, v_ctx, s_f0, s_b0, w_in, sink, conv_w, conv_b, dt_bias, a_log, d_skip, gnorm_w, w_out):
    q, k, v, g, z, xs, bm, cm, dt = ab_inputs(h, w_in, conv_w, conv_b)
    cos, sin = axial_rope_tables(h.shape[1], A_HEAD_DIM, h.dtype)
    attn = banded_attention(apply_rope(q, cos, sin), apply_rope(k, cos, sin), v, k_ctx, v_ctx,
                            sink.reshape(A_KV_HEADS, A_GROUP), A_SCALE)
    y, _, _ = ssd_bidir(xs, bm, cm, dt, dt_bias, a_log, d_skip, s_f0, s_b0)
    return ab_output(attn, g, y, z, gnorm_w, w_out)


def mla_inputs(h, w_in, q_norm_w, kv_norm_w, w_uq):
    b, L, _ = h.shape
    cq, ckv, kpe, g = split_cols(h @ w_in, (Q_LORA, KV_LORA, MLA_ROPE, MLA_WIDTH))
    q = (rms_norm(cq, q_norm_w) @ w_uq).reshape(b, L, MLA_HEADS, MLA_NOPE + MLA_ROPE)
    return q, rms_norm(ckv, kv_norm_w), kpe, g


def mla_expand_kv(ckv, kpe, w_ukv):
    b, L, _ = ckv.shape
    kv = (ckv @ w_ukv).reshape(b, L, MLA_HEADS, MLA_NOPE + MLA_V)
    k_nope, v = kv[..., :MLA_NOPE], kv[..., MLA_NOPE:]
    k = jnp.concatenate([k_nope, jnp.broadcast_to(kpe[:, :, None, :], (b, L, MLA_HEADS, MLA_ROPE))], axis=-1)
    return k, v


def mla_attend(q, k, v, g, w_out):
    b, L = q.shape[0], q.shape[1]
    out = dense_attention(q[:, :, :, None, :], k, v, MLA_SCALE)
    return (out.reshape(b, L, MLA_WIDTH) * jax.nn.silu(g)) @ w_out


def mla_context(h, w_in, q_norm_w, kv_norm_w, w_uq, w_ukv, w_out):
    q, ckv, kpe, g = mla_inputs(h, w_in, q_norm_w, kv_norm_w, w_uq)
    k, v = mla_expand_kv(ckv, kpe, w_ukv)
    return mla_attend(q, k, v, g, w_out), ckv, kpe


def mla_latent(h, ckv_ctx, kpe_ctx, w_in, q_norm_w, kv_norm_w, w_uq, w_ukv, w_out):
    q, ckv, kpe, g = mla_inputs(h, w_in, q_norm_w, kv_norm_w, w_uq)
    cos, sin = axial_rope_tables(h.shape[1], MLA_ROPE, h.dtype)
    q = jnp.concatenate([q[..., :MLA_NOPE], apply_rope(q[..., MLA_NOPE:], cos, sin)], axis=-1)
    k_lat, v_lat = mla_expand_kv(ckv, apply_rope(kpe, cos, sin), w_ukv)
    k_ctx, v_ctx = mla_expand_kv(ckv_ctx, kpe_ctx, w_ukv)
    k = jnp.concatenate([k_lat, k_ctx], axis=1)
    v = jnp.concatenate([v_lat, v_ctx], axis=1)
    return mla_attend(q, k, v, g, w_out)


def setup_inputs(seed: int = 0) -> dict:
    key = jax.random.key(seed)
    ks = jax.random.split(key, 29)
    f32 = jnp.float32

    def nrm(k, shape, s=1.0):
        return s * jax.random.normal(k, shape, f32)

    dt0 = jnp.exp(jax.random.uniform(ks[17], (N_AB, 2, SSD_HEADS), f32, math.log(1e-3), math.log(1e-1)))
    return {
        "x_prompt": nrm(ks[0], (BATCH, SEQ, D_MODEL)),
        "x_sample": nrm(ks[1], (DEC_BATCH, DEC_SEQ, D_MODEL)),
        "cache_a_k": nrm(ks[2], (DEC_BATCH, N_AB, PAST_LEN, A_KV_HEADS, A_HEAD_DIM)),
        "cache_a_v": nrm(ks[3], (DEC_BATCH, N_AB, PAST_LEN, A_KV_HEADS, A_HEAD_DIM)),
        "state_ssd_fwd": nrm(ks[4], (DEC_BATCH, N_AB, SSD_HEADS, SSD_HEAD_DIM, SSD_STATE), 0.5),
        "state_ssd_bwd": nrm(ks[5], (DEC_BATCH, N_AB, SSD_HEADS, SSD_HEAD_DIM, SSD_STATE), 0.5),
        "cache_mla_ckv": nrm(ks[6], (DEC_BATCH, N_C, PAST_LEN, KV_LORA)),
        "cache_mla_kpe": nrm(ks[7], (DEC_BATCH, N_C, PAST_LEN, MLA_ROPE)),
        "c": nrm(ks[8], (DEC_BATCH, D_MODEL)),
        "c_ctx": nrm(ks[9], (D_MODEL,)),
        "ada_w": nrm(ks[10], (DEPTH, D_MODEL, 3 * D_MODEL), D_MODEL ** -0.5),
        "ada_b": nrm(ks[11], (DEPTH, 3 * D_MODEL), 0.02),
        "norm_w": 1.0 + nrm(ks[12], (DEPTH, D_MODEL), 0.02),
        "ab_w_in": nrm(ks[13], (N_AB, D_MODEL, AB_IN), D_MODEL ** -0.5),
        "ab_sink": nrm(ks[14], (N_AB, A_HEADS), 0.5),
        "ab_conv_w": nrm(ks[15], (N_AB, CONV_K, CONV_CH), CONV_K ** -0.5),
        "ab_conv_b": nrm(ks[16], (N_AB, CONV_CH), 0.02),
        "ab_dt_bias": dt0 + jnp.log(-jnp.expm1(-dt0)),
        "ab_a_log": jnp.log(jax.random.uniform(ks[18], (N_AB, 2, SSD_HEADS), f32, 1.0, 16.0)),
        "ab_d_skip": 1.0 + nrm(ks[19], (N_AB, SSD_HEADS), 0.1),
        "ab_gnorm_w": 1.0 + nrm(ks[20], (N_AB, SSD_INNER), 0.02),
        "ab_w_out": nrm(ks[21], (N_AB, AB_OUT_IN, D_MODEL), AB_OUT_IN ** -0.5),
        "mla_w_in": nrm(ks[22], (N_C, D_MODEL, MLA_IN), D_MODEL ** -0.5),
        "mla_q_norm_w": 1.0 + nrm(ks[23], (N_C, Q_LORA), 0.02),
        "mla_kv_norm_w": 1.0 + nrm(ks[24], (N_C, KV_LORA), 0.02),
        "mla_w_uq": nrm(ks[25], (N_C, Q_LORA, MLA_HEADS * (MLA_NOPE + MLA_ROPE)), Q_LORA ** -0.5),
        "mla_w_ukv": nrm(ks[26], (N_C, KV_LORA, MLA_HEADS * (MLA_NOPE + MLA_V)), KV_LORA ** -0.5),
        "mla_w_out": nrm(ks[27], (N_C, MLA_WIDTH, D_MODEL), MLA_WIDTH ** -0.5),
        "final_norm_w": 1.0 + nrm(ks[28], (D_MODEL,), 0.02),
    }


def reference(x_prompt, x_sample, cache_a_k, cache_a_v, state_ssd_fwd, state_ssd_bwd, cache_mla_ckv, cache_mla_kpe,
              c, c_ctx, ada_w, ada_b, norm_w, ab_w_in, ab_sink, ab_conv_w, ab_conv_b, ab_dt_bias, ab_a_log,
              ab_d_skip, ab_gnorm_w, ab_w_out, mla_w_in, mla_q_norm_w, mla_kv_norm_w, mla_w_uq, mla_w_ukv,
              mla_w_out, final_norm_w):
    xp = x_prompt
    ks_a, vs_a, sf_list, sb_list, ckv_list, kpe_list = [], [], [], [], [], []
    for layer in range(DEPTH):
        i = layer // 2
        shift, scale, gate = modulation(c_ctx[None, :], ada_w[layer], ada_b[layer])
        h = rms_norm(xp, norm_w[layer]) * (1.0 + scale) + shift
        if layer % 2 == 0:
            out, k, v, s_f, s_b = ab_context(h, ab_w_in[i], ab_sink[i], ab_conv_w[i], ab_conv_b[i], ab_dt_bias[i],
                                             ab_a_log[i], ab_d_skip[i], ab_gnorm_w[i], ab_w_out[i])
            ks_a.append(k)
            vs_a.append(v)
            sf_list.append(s_f)
            sb_list.append(s_b)
        else:
            out, ckv, kpe = mla_context(h, mla_w_in[i], mla_q_norm_w[i], mla_kv_norm_w[i], mla_w_uq[i],
                                        mla_w_ukv[i], mla_w_out[i])
            ckv_list.append(ckv)
            kpe_list.append(kpe)
        xp = xp + gate * out
    y_prompt = rms_norm(xp, final_norm_w)
    new_cache_a_k = jnp.stack(ks_a, axis=1)
    new_cache_a_v = jnp.stack(vs_a, axis=1)
    new_state_ssd_fwd = jnp.stack(sf_list, axis=1)
    new_state_ssd_bwd = jnp.stack(sb_list, axis=1)
    new_cache_mla_ckv = jnp.stack(ckv_list, axis=1)
    new_cache_mla_kpe = jnp.stack(kpe_list, axis=1)

    xl = x_sample
    for layer in range(DEPTH):
        i = layer // 2
        shift, scale, gate = modulation(c, ada_w[layer], ada_b[layer])
        h = rms_norm(xl, norm_w[layer]) * (1.0 + scale) + shift
        if layer % 2 == 0:
            out = ab_latent(h, cache_a_k[:, i], cache_a_v[:, i], state_ssd_fwd[:, i], state_ssd_bwd[:, i],
                            ab_w_in[i], ab_sink[i], ab_conv_w[i], ab_conv_b[i], ab_dt_bias[i], ab_a_log[i],
                            ab_d_skip[i], ab_gnorm_w[i], ab_w_out[i])
        else:
            out = mla_latent(h, cache_mla_ckv[:, i], cache_mla_kpe[:, i], mla_w_in[i], mla_q_norm_w[i],
                             mla_kv_norm_w[i], mla_w_uq[i], mla_w_ukv[i], mla_w_out[i])
        xl = xl + gate * out
    y_sample = rms_norm(xl, final_norm_w)

    return (y_prompt, y_sample, new_cache_a_k, new_cache_a_v, new_state_ssd_fwd, new_state_ssd_bwd,
            new_cache_mla_ckv, new_cache_mla_kpe)
```

```python
import functools
import math

import jax
import jax.numpy as jnp
import numpy as np
from jax import lax
from jax.experimental import pallas as pl
from jax.experimental.pallas import tpu as pltpu

F32 = jnp.float32
BF16 = jnp.bfloat16

LANES = 128
SUBLANES = 8
VMEM_LIMIT_BYTES = 56 * 1024 * 1024

GRID_W = 64
ROPE_BASE = 10000.0
EPS = 1e-6
NEG_INF = -1e30
WINDOW_BLK = 128
CHUNK = 128
A_HEADS, A_KV_HEADS, A_HEAD_DIM = 8, 2, 64
A_WIDTH = A_HEADS * A_HEAD_DIM
A_KV_WIDTH = A_KV_HEADS * A_HEAD_DIM
A_SCALE = A_HEAD_DIM ** -0.5
SSD_HEADS, SSD_HEAD_DIM, SSD_GROUPS, SSD_STATE = 16, 64, 2, 64
SSD_INNER = SSD_HEADS * SSD_HEAD_DIM
CONV_K = 5
BC_WIDTH = SSD_GROUPS * SSD_STATE
CONV_CH = SSD_INNER + 2 * BC_WIDTH
MLA_HEADS, MLA_NOPE, MLA_ROPE, MLA_V = 16, 64, 32, 64
Q_LORA, KV_LORA = 256, 128
MLA_WIDTH = MLA_HEADS * MLA_V
MLA_SCALE = (MLA_NOPE + MLA_ROPE) ** -0.5
LOG2E = 1.4426950408889634


def _cparams(*sem):
    return pltpu.CompilerParams(dimension_semantics=sem, vmem_limit_bytes=VMEM_LIMIT_BYTES)


def _silu(x):
    return x * (1.0 / (1.0 + jnp.exp(-x)))


def _dot(a, b):
    return jnp.dot(a, b, preferred_element_type=F32, precision=lax.Precision.DEFAULT)


def _dot_nt(a, b):
    return lax.dot_general(a, b, (((1,), (1,)), ((), ())), preferred_element_type=F32,
                           precision=lax.Precision.DEFAULT)


def _lane_iota(shape):
    return lax.broadcasted_iota(jnp.int32, shape, len(shape) - 1)


def _row_iota(shape):
    return lax.broadcasted_iota(jnp.int32, shape, len(shape) - 2)


def _mod_kernel(cond_ref, w_ref, b_ref, o_ref):
    s = _silu(cond_ref[...])
    o_ref[...] = _dot(s.astype(BF16), w_ref[...].astype(BF16)) + b_ref[...]


def _modulation(conds, ada_w, ada_b):
    depth, d, d3 = ada_w.shape
    tn = 768
    return pl.pallas_call(
        _mod_kernel,
        grid=(depth, d3 // tn),
        in_specs=[pl.BlockSpec((SUBLANES, d), lambda l, j: (0, 0)),
                  pl.BlockSpec((None, d, tn), lambda l, j: (l, 0, j)),
                  pl.BlockSpec((None, 1, tn), lambda l, j: (l, 0, j))],
        out_specs=pl.BlockSpec((None, SUBLANES, tn), lambda l, j: (l, 0, j)),
        out_shape=jax.ShapeDtypeStruct((depth, SUBLANES, d3), F32),
        compiler_params=_cparams("arbitrary", "arbitrary"),
        name="modulation",
    )(conds, ada_w, ada_b.reshape(depth, 1, d3))


def _norm_mod(x, nw, scale, shift):
    ms = jnp.mean(x * x, axis=-1, keepdims=True)
    y = (x * lax.rsqrt(ms + EPS)) * nw
    return y * (1.0 + scale) + shift


def _rope(x, cos, sin_signed, half_period):
    outs = []
    first = (_lane_iota((x.shape[0], LANES)) & (2 * half_period - 1)) < half_period
    for j in range(x.shape[1] // LANES):
        xj = x[:, j * LANES:(j + 1) * LANES]
        up = pltpu.roll(xj, LANES - half_period, 1)
        dn = pltpu.roll(xj, half_period, 1)
        outs.append(xj * cos + jnp.where(first, up, dn) * sin_signed)
    return outs[0] if len(outs) == 1 else jnp.concatenate(outs, axis=1)


AB_COLS = (A_WIDTH, A_KV_WIDTH, A_KV_WIDTH, A_WIDTH, SSD_INNER, CONV_CH, LANES)


def _ab_in_kernel(*refs, rope):
    if rope:
        (x_ref, nw_ref, sc_ref, sh_ref, w_ref, cos_ref, sin_ref,
         q_ref, k_ref, v_ref, g_ref, z_ref, xbc_ref, dt_ref) = refs
    else:
        (x_ref, nw_ref, sc_ref, sh_ref, w_ref,
         q_ref, k_ref, v_ref, g_ref, z_ref, xbc_ref, dt_ref) = refs
    h = _norm_mod(x_ref[...], nw_ref[...], sc_ref[...], sh_ref[...]).astype(BF16)
    offs = np.concatenate([[0], np.cumsum(AB_COLS)])
    outs = (q_ref, k_ref, v_ref, g_ref, z_ref, xbc_ref, dt_ref)
    for i, o_ref in enumerate(outs):
        y = _dot(h, w_ref[:, int(offs[i]):int(offs[i + 1])])
        if rope and i in (0, 1):
            y = _rope(y, cos_ref[...], sin_ref[...], A_HEAD_DIM // 4)
        o_ref[...] = y.astype(o_ref.dtype)


def _ab_in(x, nw, scale, shift, w, rope_tabs, seq_len, tm):
    t, d = x.shape
    per_seq = seq_len // tm if scale.shape[0] > 1 else None
    row = lambda i: (i, 0)
    mod = (lambda i: (i // per_seq, 0, 0)) if per_seq else (lambda i: (0, 0, 0))
    const = lambda i: (0, 0)
    in_specs = [pl.BlockSpec((tm, d), row),
                pl.BlockSpec((1, d), const),
                pl.BlockSpec((None, 1, d), mod),
                pl.BlockSpec((None, 1, d), mod),
                pl.BlockSpec(w.shape, const)]
    args = [x, nw, scale, shift, w]
    if rope_tabs is not None:
        nt = seq_len // tm
        pos = lambda i: (i % nt, 0)
        in_specs += [pl.BlockSpec((tm, LANES), pos), pl.BlockSpec((tm, LANES), pos)]
        args += list(rope_tabs)
    dts = (BF16, F32, F32, BF16, BF16, BF16, F32)
    return pl.pallas_call(
        functools.partial(_ab_in_kernel, rope=rope_tabs is not None),
        grid=(t // tm,),
        in_specs=in_specs,
        out_specs=[pl.BlockSpec((tm, c), row) for c in AB_COLS],
        out_shape=[jax.ShapeDtypeStruct((t, c), dt) for c, dt in zip(AB_COLS, dts)],
        compiler_params=_cparams("arbitrary"),
        name="ab_in_proj",
    )(*args)


def _conv_kernel(prev_ref, main_ref, next_ref, w_ref, b_ref, o_ref, *, tiles_per_seq):
    i = pl.program_id(0) % tiles_per_seq
    tm = main_ref.shape[0]
    prev = jnp.where(i > 0, prev_ref[...].astype(F32), 0.0)
    nxt = jnp.where(i < tiles_per_seq - 1, next_ref[...].astype(F32), 0.0)
    ext = jnp.concatenate([prev, main_ref[...].astype(F32), nxt], axis=0)
    pad = CONV_K // 2
    acc = b_ref[...] + ext[SUBLANES - pad:SUBLANES - pad + tm] * w_ref[0:1, :]
    for j in range(1, CONV_K):
        acc = acc + ext[SUBLANES - pad + j:SUBLANES - pad + j + tm] * w_ref[j:j + 1, :]
    o_ref[...] = _silu(acc).astype(o_ref.dtype)


def _conv(xbc, w, b, seq_len, tm):
    t, c = xbc.shape
    tps = seq_len // tm
    hb = tm // SUBLANES
    nblk8 = t // SUBLANES
    return pl.pallas_call(
        functools.partial(_conv_kernel, tiles_per_seq=tps),
        grid=(t // tm,),
        in_specs=[pl.BlockSpec((SUBLANES, c), lambda i: (jnp.maximum(i * hb - 1, 0), 0)),
                  pl.BlockSpec((tm, c), lambda i: (i, 0)),
                  pl.BlockSpec((SUBLANES, c), lambda i: (jnp.minimum((i + 1) * hb, nblk8 - 1), 0)),
                  pl.BlockSpec((SUBLANES, c), lambda i: (0, 0)),
                  pl.BlockSpec((1, c), lambda i: (0, 0))],
        out_specs=pl.BlockSpec((tm, c), lambda i: (i, 0)),
        out_shape=jax.ShapeDtypeStruct((t, c), BF16),
        compiler_params=_cparams("arbitrary"),
        name="ssd_conv",
    )(xbc, xbc, xbc, w, b)


def _ssd_kernel(x_ref, b_ref, c_ref, dt_ref, par_ref, dskip_ref, h0_ref, y_ref, hout_ref, h_scr, *,
                direction, nc, has_h0):
    ci = pl.program_id(1)
    hp = SSD_HEADS // SSD_GROUPS
    lane = _lane_iota((CHUNK, LANES))
    rowi = _row_iota((CHUNK, LANES))
    lo_half = lane < SSD_HEAD_DIM

    @pl.when(ci == 0)
    def _init():
        if has_h0:
            for pr in range(SSD_HEADS // 2):
                both = jnp.concatenate([h0_ref[2 * pr], h0_ref[2 * pr + 1]], axis=0)
                both = jnp.concatenate([both, jnp.zeros_like(both)], axis=1)
                st = both.T[0:SSD_STATE]
                lo_st = _lane_iota(st.shape) < SSD_HEAD_DIM
                h_scr[2 * pr] = jnp.where(lo_st, st, 0.0)
                h_scr[2 * pr + 1] = jnp.where(lo_st, 0.0, st)
        else:
            h_scr[...] = jnp.zeros_like(h_scr)

    dtp_in = dt_ref[...] + par_ref[0:1, :]
    dtp = jnp.maximum(dtp_in, 0.0) + jnp.log1p(jnp.exp(-jnp.abs(dtp_in)))
    la = dtp * (-jnp.exp(par_ref[1:2, :]))
    tri = (rowi >= lane) if direction == 0 else (rowi <= lane)
    tri = jnp.where(tri, 1.0, 0.0).astype(BF16)
    p1 = la.astype(BF16)
    r1 = la - p1.astype(F32)
    p2 = r1.astype(BF16)
    p3 = (r1 - p2.astype(F32)).astype(BF16)
    cum = _dot(tri, p1) + _dot(tri, p2) + _dot(tri, p3)
    cum_t = cum.T
    dtp_t = dtp.T
    end = CHUNK - 1 if direction == 0 else 0
    w_t = jnp.exp(cum_t[:, end:end + 1] - cum_t) * dtp_t
    causal = (rowi >= lane) if direction == 0 else (rowi <= lane)

    bm = b_ref[...]
    cm = c_ref[...]
    bm_t = bm.astype(F32).T
    cb = []
    cmask = []
    for g in range(SSD_GROUPS):
        in_g = (lane >= g * SSD_STATE) & (lane < (g + 1) * SSD_STATE)
        cg = jnp.where(in_g, cm, jnp.zeros_like(cm))
        cmask.append(cg.astype(F32))
        cb.append(_dot_nt(cg, bm))

    for pr in range(SSD_HEADS // 2):
        xp = x_ref[:, pr * LANES:(pr + 1) * LANES]
        y_pair = None
        for e in range(2):
            h = 2 * pr + e
            g = h // hp
            col = direction * SSD_HEADS + h
            xe = jnp.where(lo_half if e == 0 else ~lo_half, xp, jnp.zeros_like(xp))
            cumcol = jnp.broadcast_to(cum[:, col:col + 1], (CHUNK, LANES))
            diff = cumcol - cum_t[col:col + 1, :]
            dec = jnp.exp(jnp.where(causal, diff, NEG_INF))
            sc = cb[g] * dec * dtp_t[col:col + 1, :]
            ce = cmask[g] * jnp.exp(cumcol)
            hs = h_scr[h]
            hsb = hs.astype(BF16)
            zpad = jnp.zeros_like(hsb)
            a = jnp.concatenate([sc.astype(BF16), ce.astype(BF16)], axis=1)
            wmat = jnp.concatenate([xe, hsb, zpad] if g == 0 else [xe, zpad, hsb], axis=0)
            ye = _dot(a, wmat)
            y_pair = ye if y_pair is None else y_pair + ye
            bw = (bm_t[g * SSD_STATE:(g + 1) * SSD_STATE, :] * w_t[col:col + 1, :]).astype(BF16)
            new = _dot(bw, xe)
            cdec = jnp.exp(cumcol[end:end + 1, :])
            h_scr[h] = hs * cdec + new
        if direction == 0:
            y_pair = y_pair + dskip_ref[:, pr * LANES:(pr + 1) * LANES] * xp.astype(F32)
        y_ref[:, pr * LANES:(pr + 1) * LANES] = y_pair.astype(y_ref.dtype)

    @pl.when(ci == nc - 1)
    def _final():
        for pr in range(SSD_HEADS // 2):
            st = h_scr[2 * pr] + h_scr[2 * pr + 1]
            st = jnp.concatenate([st, jnp.zeros_like(st)], axis=0).T
            hout_ref[2 * pr] = st[0:SSD_HEAD_DIM, 0:SSD_STATE]
            hout_ref[2 * pr + 1] = st[SSD_HEAD_DIM:2 * SSD_HEAD_DIM, 0:SSD_STATE]


def _ssd(xbc, dt, par, dskip, h0, nb, seq_len, direction):
    t = xbc.shape[0]
    nc = seq_len // CHUNK
    cidx = (lambda b, c: b * nc + c) if direction == 0 else (lambda b, c: b * nc + nc - 1 - c)
    xcols = SSD_INNER // LANES
    has_h0 = h0 is not None
    if not has_h0:
        h0 = jnp.zeros((1, SSD_HEADS, SSD_HEAD_DIM, SSD_STATE), F32)
    h0_map = (lambda b, c: (b, 0, 0, 0)) if has_h0 else (lambda b, c: (0, 0, 0, 0))
    st_shape = (None, SSD_HEADS, SSD_HEAD_DIM, SSD_STATE)
    return pl.pallas_call(
        functools.partial(_ssd_kernel, direction=direction, nc=nc, has_h0=has_h0),
        grid=(nb, nc),
        in_specs=[pl.BlockSpec((CHUNK, SSD_INNER), lambda b, c: (cidx(b, c), 0)),
                  pl.BlockSpec((CHUNK, LANES), lambda b, c: (cidx(b, c), xcols)),
                  pl.BlockSpec((CHUNK, LANES), lambda b, c: (cidx(b, c), xcols + 1)),
                  pl.BlockSpec((CHUNK, LANES), lambda b, c: (cidx(b, c), 0)),
                  pl.BlockSpec((SUBLANES, LANES), lambda b, c: (0, 0)),
                  pl.BlockSpec((1, SSD_INNER), lambda b, c: (0, 0)),
                  pl.BlockSpec(st_shape, h0_map)],
        out_specs=[pl.BlockSpec((CHUNK, SSD_INNER), lambda b, c: (cidx(b, c), 0)),
                   pl.BlockSpec(st_shape, lambda b, c: (b, 0, 0, 0)),
                   pl.BlockSpec((None, SSD_HEADS, SSD_STATE, LANES), lambda b, c: (b, 0, 0, 0))],
        out_shape=[jax.ShapeDtypeStruct((t, SSD_INNER), BF16),
                   jax.ShapeDtypeStruct((nb, SSD_HEADS, SSD_HEAD_DIM, SSD_STATE), F32),
                   jax.ShapeDtypeStruct((nb, SSD_HEADS, SSD_STATE, LANES), F32)],
        compiler_params=_cparams("arbitrary", "arbitrary"),
        name="ssd_scan_fwd" if direction == 0 else "ssd_scan_bwd",
    )(xbc, xbc, xbc, dt, par, dskip, h0)[:2]


def _kv_variants(k, v):
    lane = _lane_iota(k.shape)
    lo = lane < A_HEAD_DIM
    k_sw = pltpu.roll(k, A_HEAD_DIM, 1)
    v_sw = pltpu.roll(v, A_HEAD_DIM, 1)
    ks, vs = {}, {}
    for j in range(A_KV_HEADS):
        for e in range(2):
            src_k, src_v = (k, v) if j == e else (k_sw, v_sw)
            half = lo if e == 0 else ~lo
            ks[j, e] = jnp.where(half, src_k, 0.0).astype(BF16)
            vs[j, e] = src_v.astype(BF16)
    return ks, vs


def _gqa_heads(q_ref, g_ref, o_ref, sink_ref, ks, vs, valid):
    m = q_ref.shape[0]
    lane = _lane_iota((m, LANES))
    lo = lane < A_HEAD_DIM
    group = A_HEADS // A_KV_HEADS
    for pr in range(A_HEADS // 2):
        qp = q_ref[:, pr * LANES:(pr + 1) * LANES]
        outs = []
        for e in range(2):
            hd = 2 * pr + e
            j = hd // group
            s = _dot_nt(qp, ks[j, e]) * A_SCALE
            if valid is not None:
                s = jnp.where(valid, s, NEG_INF)
            sink = sink_ref[hd]
            mx = jnp.maximum(jnp.max(s, axis=1, keepdims=True), sink)
            p = jnp.exp(s - mx)
            den = jnp.sum(p, axis=1, keepdims=True) + jnp.exp(sink - mx)
            outs.append(_dot(p.astype(BF16), vs[j, e]) / den)
        attn = jnp.where(lo, outs[0], outs[1])
        gate = g_ref[:, pr * LANES:(pr + 1) * LANES].astype(F32)
        o_ref[:, pr * LANES:(pr + 1) * LANES] = (attn * _silu(gate)).astype(o_ref.dtype)


def _attn_ctx_kernel(sink_ref, q_ref, k_ref, v_ref, g_ref, o_ref):
    ks, vs = _kv_variants(k_ref[...], v_ref[...])
    _gqa_heads(q_ref, g_ref, o_ref, sink_ref, ks, vs, None)


def _attn_ctx(sink, q, k, v, g, seq_len):
    t = q.shape[0]
    row = lambda b: (b, 0)
    return pl.pallas_call(
        _attn_ctx_kernel,
        grid=(t // seq_len,),
        in_specs=[pl.BlockSpec(memory_space=pltpu.SMEM),
                  pl.BlockSpec((seq_len, A_WIDTH), row),
                  pl.BlockSpec((seq_len, A_KV_WIDTH), row),
                  pl.BlockSpec((seq_len, A_KV_WIDTH), row),
                  pl.BlockSpec((seq_len, A_WIDTH), row)],
        out_specs=pl.BlockSpec((seq_len, A_WIDTH), row),
        out_shape=jax.ShapeDtypeStruct((t, A_WIDTH), BF16),
        compiler_params=_cparams("arbitrary"),
        name="swa_context_attention",
    )(sink, q, k, v, g)


def _attn_band_kernel(sink_ref, q_ref, kp_ref, kc_ref, kn_ref, vp_ref, vc_ref, vn_ref, kx_ref, vx_ref, g_ref,
                      o_ref, *, nblk):
    n = pl.program_id(1)
    blk = WINDOW_BLK
    k = jnp.concatenate([kp_ref[...], kc_ref[...], kn_ref[...], kx_ref[...]], axis=0)
    v = jnp.concatenate([vp_ref[...], vc_ref[...], vn_ref[...], vx_ref[...]], axis=0)
    nkeys = 3 * blk + kx_ref.shape[0]
    qi = _row_iota((blk, nkeys))
    col = _lane_iota((blk, nkeys))
    ok_prev = (col >= blk) | ((col >= qi) & (n > 0))
    ok_next = (col < 2 * blk) | (col >= 3 * blk) | ((col - 2 * blk <= qi) & (n < nblk - 1))
    valid = ok_prev & ok_next
    ks, vs = _kv_variants(k, v)
    _gqa_heads(q_ref, g_ref, o_ref, sink_ref, ks, vs, valid)


def _attn_band(sink, q, k, v, k_ctx, v_ctx, g, nb, seq_len):
    t = q.shape[0]
    blk = WINDOW_BLK
    nblk = seq_len // blk
    nctx = k_ctx.shape[1]
    cur = lambda b, n: (b * nblk + n, 0)
    prv = lambda b, n: (b * nblk + jnp.maximum(n - 1, 0), 0)
    nxt = lambda b, n: (b * nblk + jnp.minimum(n + 1, nblk - 1), 0)
    ctx = lambda b, n: (b, 0, 0)
    kvspec = lambda f: pl.BlockSpec((blk, A_KV_WIDTH), f)
    return pl.pallas_call(
        functools.partial(_attn_band_kernel, nblk=nblk),
        grid=(nb, nblk),
        in_specs=[pl.BlockSpec(memory_space=pltpu.SMEM),
                  pl.BlockSpec((blk, A_WIDTH), cur),
                  kvspec(prv), kvspec(cur), kvspec(nxt),
                  kvspec(prv), kvspec(cur), kvspec(nxt),
                  pl.BlockSpec((None, nctx, A_KV_WIDTH), ctx),
                  pl.BlockSpec((None, nctx, A_KV_WIDTH), ctx),
                  pl.BlockSpec((blk, A_WIDTH), cur)],
        out_specs=pl.BlockSpec((blk, A_WIDTH), cur),
        out_shape=jax.ShapeDtypeStruct((t, A_WIDTH), BF16),
        compiler_params=_cparams("arbitrary", "arbitrary"),
        name="swa_banded_attention",
    )(sink, q, k, k, k, v, v, v, k_ctx, v_ctx, g)


def _ab_out_kernel(a_ref, yf_ref, yb_ref, z_ref, gnw_ref, w_ref, x_ref, gate_ref, o_ref):
    y = (yf_ref[...].astype(F32) + yb_ref[...].astype(F32)) * _silu(z_ref[...].astype(F32))
    ms = jnp.mean(y * y, axis=-1, keepdims=True)
    s = ((y * lax.rsqrt(ms + EPS)) * gnw_ref[...]).astype(BF16)
    out = _dot(a_ref[...], w_ref[0:A_WIDTH, :]) + _dot(s, w_ref[A_WIDTH:, :])
    o_ref[...] = x_ref[...] + gate_ref[...] * out


def _ab_out(a, yf, yb, z, gnw, w, x, gate, seq_len, tm):
    t, d = x.shape
    per_seq = seq_len // tm if gate.shape[0] > 1 else None
    row = lambda i: (i, 0)
    const = lambda i: (0, 0)
    mod = (lambda i: (i // per_seq, 0, 0)) if per_seq else (lambda i: (0, 0, 0))
    return pl.pallas_call(
        _ab_out_kernel,
        grid=(t // tm,),
        in_specs=[pl.BlockSpec((tm, A_WIDTH), row),
                  pl.BlockSpec((tm, SSD_INNER), row),
                  pl.BlockSpec((tm, SSD_INNER), row),
                  pl.BlockSpec((tm, SSD_INNER), row),
                  pl.BlockSpec((1, SSD_INNER), const),
                  pl.BlockSpec(w.shape, const),
                  pl.BlockSpec((tm, d), row),
                  pl.BlockSpec((None, 1, d), mod)],
        out_specs=pl.BlockSpec((tm, d), row),
        out_shape=jax.ShapeDtypeStruct((t, d), F32),
        compiler_params=_cparams("arbitrary"),
        name="ab_out_proj",
    )(a, yf, yb, z, gnw, w, x, gate)


MLA_COLS = (Q_LORA, KV_LORA, LANES, MLA_WIDTH)


def _mla_in_kernel(*refs, rope):
    if rope:
        (x_ref, nw_ref, sc_ref, sh_ref, w_ref, qnw_ref, kvnw_ref, wuq_ref, cos_ref, sin_ref,
         qn_ref, qpe_ref, ckv_ref, kpe_ref, g_ref, ckv32_ref, kpe32_ref) = refs
    else:
        (x_ref, nw_ref, sc_ref, sh_ref, w_ref, qnw_ref, kvnw_ref, wuq_ref,
         qn_ref, qpe_ref, ckv_ref, kpe_ref, g_ref, ckv32_ref, kpe32_ref) = refs
    h = _norm_mod(x_ref[...], nw_ref[...], sc_ref[...], sh_ref[...]).astype(BF16)
    offs = [int(o) for o in np.concatenate([[0], np.cumsum(MLA_COLS)])]
    cq = _dot(h, w_ref[:, offs[0]:offs[1]])
    ckv = _dot(h, w_ref[:, offs[1]:offs[2]])
    kpe = _dot(h, w_ref[:, offs[2]:offs[3]])
    g_ref[...] = _dot(h, w_ref[:, offs[3]:offs[4]]).astype(g_ref.dtype)

    def rms(u, w):
        return (u * lax.rsqrt(jnp.mean(u * u, axis=-1, keepdims=True) + EPS)) * w

    cqn = rms(cq, qnw_ref[...]).astype(BF16)
    nope_w = MLA_HEADS * MLA_NOPE
    qn_ref[...] = _dot(cqn, wuq_ref[:, 0:nope_w]).astype(qn_ref.dtype)
    qpe = _dot(cqn, wuq_ref[:, nope_w:])
    if rope:
        qpe = _rope(qpe, cos_ref[...], sin_ref[...], MLA_ROPE // 4)
        kpe = _rope(kpe, cos_ref[...], sin_ref[...], MLA_ROPE // 4)
    qpe_ref[...] = qpe.astype(qpe_ref.dtype)
    ckvn = rms(ckv, kvnw_ref[...])
    ckv_ref[...] = ckvn.astype(ckv_ref.dtype)
    kpe_ref[...] = kpe.astype(kpe_ref.dtype)
    ckv32_ref[...] = ckvn
    kpe32_ref[...] = kpe


def _mla_in(x, nw, scale, shift, w, qnw, kvnw, wuq, rope_tabs, seq_len, tm):
    t, d = x.shape
    per_seq = seq_len // tm if scale.shape[0] > 1 else None
    row = lambda i: (i, 0)
    const = lambda i: (0, 0)
    mod = (lambda i: (i // per_seq, 0, 0)) if per_seq else (lambda i: (0, 0, 0))
    in_specs = [pl.BlockSpec((tm, d), row), pl.BlockSpec((1, d), const),
                pl.BlockSpec((None, 1, d), mod), pl.BlockSpec((None, 1, d), mod),
                pl.BlockSpec(w.shape, const), pl.BlockSpec((1, Q_LORA), const),
                pl.BlockSpec((1, KV_LORA), const), pl.BlockSpec(wuq.shape, const)]
    args = [x, nw, scale, shift, w, qnw, kvnw, wuq]
    if rope_tabs is not None:
        nt = seq_len // tm
        pos = lambda i: (i % nt, 0)
        in_specs += [pl.BlockSpec((tm, LANES), pos), pl.BlockSpec((tm, LANES), pos)]
        args += list(rope_tabs)
    widths = (MLA_HEADS * MLA_NOPE, MLA_HEADS * MLA_ROPE, KV_LORA, LANES, MLA_WIDTH, KV_LORA, LANES)
    dts = (BF16, BF16, BF16, BF16, BF16, F32, F32)
    return pl.pallas_call(
        functools.partial(_mla_in_kernel, rope=rope_tabs is not None),
        grid=(t // tm,),
        in_specs=in_specs,
        out_specs=[pl.BlockSpec((tm, c), row) for c in widths],
        out_shape=[jax.ShapeDtypeStruct((t, c), dt) for c, dt in zip(widths, dts)],
        compiler_params=_cparams("arbitrary"),
        name="mla_in_proj",
    )(*args)


def _mla_attn_kernel(qn_ref, qpe_ref, ckv_ref, kpe_ref, wuk_ref, wuv_ref, g_ref, o_ref, kcat_scr, v_scr, *,
                     pairs_per_step):
    pg = pl.program_id(1)
    qb = pl.program_id(2)
    tq = qn_ref.shape[0]
    lane = _lane_iota((tq, LANES))
    c_exp = MLA_SCALE * LOG2E

    @pl.when(qb == 0)
    def _expand():
        ckv = ckv_ref[...]
        for i in range(pairs_per_step):
            kcat_scr[i, :, 0:LANES] = _dot(ckv, wuk_ref[:, i * LANES:(i + 1) * LANES]).astype(BF16)
            kcat_scr[i, :, LANES:2 * LANES] = kpe_ref[...]
            v_scr[i] = _dot(ckv, wuv_ref[:, i * LANES:(i + 1) * LANES]).astype(BF16)

    for i in range(pairs_per_step):
        pair = pg * pairs_per_step + i
        qn = qn_ref[:, i * LANES:(i + 1) * LANES]
        qpe = qpe_ref[:, (i // 2) * LANES:(i // 2 + 1) * LANES]
        outs = []
        for e in range(2):
            if pairs_per_step % 2 == 0:
                slot = 2 * (i % 2) + e
            else:
                slot = 2 * (pair % 2) + e
            nope_half = (lane < MLA_NOPE) if e == 0 else (lane >= MLA_NOPE)
            q_ext = jnp.concatenate(
                [jnp.where(nope_half, qn, jnp.zeros_like(qn)),
                 jnp.where((lane >> 5) == slot, qpe, jnp.zeros_like(qpe))], axis=1)
            s = _dot_nt(q_ext, kcat_scr[i])
            mx = jnp.max(s, axis=1, keepdims=True)
            p = jnp.exp2((s - mx) * c_exp)
            den = jnp.sum(p, axis=1, keepdims=True)
            outs.append(_dot(p.astype(BF16), v_scr[i]) / den)
        attn = jnp.where(lane < MLA_V, outs[0], outs[1])
        gate = g_ref[:, i * LANES:(i + 1) * LANES].astype(F32)
        o_ref[:, i * LANES:(i + 1) * LANES] = (attn * _silu(gate)).astype(o_ref.dtype)


def _mla_attn(qn, qpe, ckv_keys, kpe_keys, wuk, wuv, g, nb, seq_len, tq, pairs_per_step):
    t = qn.shape[0]
    nkeys = ckv_keys.shape[1]
    npairs = MLA_HEADS // 2
    ngrp = npairs // pairs_per_step
    nq = seq_len // tq
    wp = pairs_per_step * LANES
    if pairs_per_step % 2 == 0:
        wpe = wp // 2
        pe_map = lambda b, p, i: (b * nq + i, p)
    else:
        wpe = LANES
        pe_map = lambda b, p, i: (b * nq + i, p // 2)
    qmap = lambda b, p, i: (b * nq + i, p)
    return pl.pallas_call(
        functools.partial(_mla_attn_kernel, pairs_per_step=pairs_per_step),
        grid=(nb, ngrp, nq),
        in_specs=[pl.BlockSpec((tq, wp), qmap),
                  pl.BlockSpec((tq, wpe), pe_map),
                  pl.BlockSpec((None, nkeys, KV_LORA), lambda b, p, i: (b, 0, 0)),
                  pl.BlockSpec((None, nkeys, LANES), lambda b, p, i: (b, 0, 0)),
                  pl.BlockSpec((KV_LORA, wp), lambda b, p, i: (0, p)),
                  pl.BlockSpec((KV_LORA, wp), lambda b, p, i: (0, p)),
                  pl.BlockSpec((tq, wp), qmap)],
        out_specs=pl.BlockSpec((tq, wp), qmap),
        out_shape=jax.ShapeDtypeStruct((t, MLA_WIDTH), BF16),
        scratch_shapes=[pltpu.VMEM((pairs_per_step, nkeys, 2 * LANES), BF16),
                        pltpu.VMEM((pairs_per_step, nkeys, LANES), BF16)],
        compiler_params=_cparams("arbitrary", "arbitrary", "arbitrary"),
        name="mla_attention",
    )(qn, qpe, ckv_keys, kpe_keys, wuk, wuv, g)


def _mla_out_kernel(a_ref, w_ref, x_ref, gate_ref, fw_ref, o_ref):
    xn = x_ref[...] + gate_ref[...] * _dot(a_ref[...], w_ref[...])
    ms = jnp.mean(xn * xn, axis=-1, keepdims=True)
    o_ref[...] = (xn * lax.rsqrt(ms + EPS)) * fw_ref[...]


def _mla_out(a, w, x, gate, fw, seq_len, tm):
    t, d = x.shape
    per_seq = seq_len // tm if gate.shape[0] > 1 else None
    row = lambda i: (i, 0)
    const = lambda i: (0, 0)
    mod = (lambda i: (i // per_seq, 0, 0)) if per_seq else (lambda i: (0, 0, 0))
    return pl.pallas_call(
        _mla_out_kernel,
        grid=(t // tm,),
        in_specs=[pl.BlockSpec((tm, MLA_WIDTH), row), pl.BlockSpec(w.shape, const),
                  pl.BlockSpec((tm, d), row), pl.BlockSpec((None, 1, d), mod), pl.BlockSpec((1, d), const)],
        out_specs=pl.BlockSpec((tm, d), row),
        out_shape=jax.ShapeDtypeStruct((t, d), F32),
        compiler_params=_cparams("arbitrary"),
        name="mla_out_proj_final_norm",
    )(a, w, x, gate, fw)


def _rope_tables(length, dim):
    rows = length // GRID_W
    row = jnp.repeat(jnp.arange(rows), GRID_W).astype(F32)
    col = jnp.tile(jnp.arange(GRID_W), rows).astype(F32)
    nf = dim // 4
    inv = 1.0 / (ROPE_BASE ** (jnp.arange(nf, dtype=F32) / nf))
    ar = row[:, None] * inv[None, :]
    ac = col[:, None] * inv[None, :]
    ang = jnp.concatenate([ar, ar, ac, ac], axis=-1)
    sign = jnp.tile(jnp.concatenate([-jnp.ones((nf,), F32), jnp.ones((nf,), F32)]), 2)
    reps = LANES // dim
    return jnp.tile(jnp.cos(ang), (1, reps)), jnp.tile(jnp.sin(ang) * sign, (1, reps))


def _group(x, seq_len, conds_rows, mods, wts, tabs_a, tabs_c, ctx_cache):
    nb, _, d = x.shape
    t = nb * seq_len
    latent = ctx_cache is not None
    xf = x.reshape(t, d)
    tm = 512

    def mod_rows(layer):
        m = mods[layer][conds_rows]
        sh, sc, gt = jnp.split(m[:, None, :], 3, axis=-1)
        return sh, sc, gt

    sh, sc, gt = mod_rows(0)
    q, k, v, g, z, xbc, dt = _ab_in(xf, wts["norm_w"][0:1], sc, sh, wts["ab_w_in"], tabs_a if latent else None,
                                    seq_len, tm)
    xbc_c = _conv(xbc, wts["ab_conv_w"], wts["ab_conv_b"], seq_len, min(tm, seq_len))
    if latent:
        k_ctx, v_ctx, s_f0, s_b0, _, _ = ctx_cache
        attn = _attn_band(wts["ab_sink"], q, k, v, k_ctx, v_ctx, g, nb, seq_len)
    else:
        s_f0 = s_b0 = None
        attn = _attn_ctx(wts["ab_sink"], q, k, v, g, seq_len)
    yf, s_f = _ssd(xbc_c, dt, wts["ssd_par"], wts["ssd_dskip"], s_f0, nb, seq_len, 0)
    yb, s_b = _ssd(xbc_c, dt, wts["ssd_par"], wts["ssd_dskip"], s_b0, nb, seq_len, 1)
    x1 = _ab_out(attn, yf, yb, z, wts["ab_gnorm_w"], wts["ab_w_out"], xf, gt, seq_len, tm)

    sh, sc, gt = mod_rows(1)
    qn, qpe, ckv, kpe, g1, ckv32, kpe32 = _mla_in(
        x1, wts["norm_w"][1:2], sc, sh, wts["mla_w_in"], wts["mla_q_norm_w"], wts["mla_kv_norm_w"],
        wts["mla_w_uq"], tabs_c if latent else None, seq_len, tm)
    ckv_keys = ckv.reshape(nb, seq_len, KV_LORA)
    kpe_keys = kpe.reshape(nb, seq_len, LANES)
    if latent:
        ckv_x, kpe_x = ctx_cache[4], ctx_cache[5]
        ckv_keys = jnp.concatenate([ckv_keys, ckv_x], axis=1)
        kpe_keys = jnp.concatenate([kpe_keys, kpe_x], axis=1)
        attn1 = _mla_attn(qn, qpe, ckv_keys, kpe_keys, wts["mla_w_uk"], wts["mla_w_uv"], g1, nb, seq_len, 256, 1)
    else:
        attn1 = _mla_attn(qn, qpe, ckv_keys, kpe_keys, wts["mla_w_uk"], wts["mla_w_uv"], g1, nb, seq_len,
                          seq_len, MLA_HEADS // 2)
    y = _mla_out(attn1, wts["mla_w_out"], x1, gt, wts["final_norm_w"], seq_len, tm)
    return y.reshape(nb, seq_len, d), (k, v, s_f, s_b, ckv32, kpe32)


def kernel(x_prompt, x_sample, cache_a_k, cache_a_v, state_ssd_fwd, state_ssd_bwd, cache_mla_ckv, cache_mla_kpe,
           c, c_ctx, ada_w, ada_b, norm_w, ab_w_in, ab_sink, ab_conv_w, ab_conv_b, ab_dt_bias, ab_a_log,
           ab_d_skip, ab_gnorm_w, ab_w_out, mla_w_in, mla_q_norm_w, mla_kv_norm_w, mla_w_uq, mla_w_ukv,
           mla_w_out, final_norm_w):
    batch, seq, d = x_prompt.shape
    dec_batch, dec_seq, _ = x_sample.shape
    assert ada_w.shape[0] == 2 and ab_w_in.shape[0] == 1 and mla_w_in.shape[0] == 1
    assert dec_batch + 1 <= SUBLANES

    conds = jnp.concatenate([c_ctx[None, :], c, jnp.zeros((SUBLANES - 1 - dec_batch, d), F32)], axis=0)
    mods = _modulation(conds, ada_w, ada_b)

    w_ab = ab_w_in[0]
    w_ab = jnp.concatenate([w_ab, jnp.zeros((d, LANES - 2 * SSD_HEADS), F32)], axis=1).astype(BF16)
    mw = mla_w_in[0]
    o_kpe = Q_LORA + KV_LORA
    w_mla = jnp.concatenate([mw[:, :o_kpe], jnp.tile(mw[:, o_kpe:o_kpe + MLA_ROPE], (1, LANES // MLA_ROPE)),
                             mw[:, o_kpe + MLA_ROPE:]], axis=1).astype(BF16)
    wuq = mla_w_uq[0].reshape(Q_LORA, MLA_HEADS, MLA_NOPE + MLA_ROPE)
    wuq = jnp.concatenate([wuq[:, :, :MLA_NOPE].reshape(Q_LORA, -1), wuq[:, :, MLA_NOPE:].reshape(Q_LORA, -1)],
                          axis=1).astype(BF16)
    wukv = mla_w_ukv[0].reshape(KV_LORA, MLA_HEADS, MLA_NOPE + MLA_V)
    wuk = wukv[:, :, :MLA_NOPE].reshape(KV_LORA, -1).astype(BF16)
    wuv = wukv[:, :, MLA_NOPE:].reshape(KV_LORA, -1).astype(BF16)
    pad_lanes = lambda r: jnp.concatenate([r.reshape(1, -1), jnp.zeros((1, LANES - r.size), F32)], axis=1)
    ssd_par = jnp.concatenate([pad_lanes(ab_dt_bias[0]), pad_lanes(ab_a_log[0]),
                               jnp.zeros((SUBLANES - 2, LANES), F32)], axis=0)
    wts = dict(
        norm_w=norm_w, ab_w_in=w_ab, ab_sink=ab_sink[0],
        ab_conv_w=jnp.concatenate([ab_conv_w[0], jnp.zeros((SUBLANES - CONV_K, CONV_CH), F32)], axis=0),
        ab_conv_b=ab_conv_b[0][None, :], ssd_par=ssd_par,
        ssd_dskip=jnp.repeat(ab_d_skip[0], SSD_HEAD_DIM)[None, :],
        ab_gnorm_w=ab_gnorm_w[0][None, :], ab_w_out=ab_w_out[0].astype(BF16),
        mla_w_in=w_mla, mla_q_norm_w=mla_q_norm_w[0][None, :], mla_kv_norm_w=mla_kv_norm_w[0][None, :],
        mla_w_uq=wuq, mla_w_uk=wuk, mla_w_uv=wuv, mla_w_out=mla_w_out[0].astype(BF16),
        final_norm_w=final_norm_w[None, :],
    )
    tabs_a = _rope_tables(dec_seq, A_HEAD_DIM)
    tabs_c = _rope_tables(dec_seq, MLA_ROPE)

    y_prompt, (k, v, s_f, s_b, ckv32, kpe32) = _group(
        x_prompt, seq, jnp.zeros((1,), jnp.int32), mods, wts, tabs_a, tabs_c, None)

    past = cache_a_k.shape[2]
    ctx_cache = (cache_a_k[:, 0].reshape(dec_batch, past, A_KV_WIDTH),
                 cache_a_v[:, 0].reshape(dec_batch, past, A_KV_WIDTH),
                 state_ssd_fwd[:, 0], state_ssd_bwd[:, 0],
                 cache_mla_ckv[:, 0].astype(BF16),
                 jnp.tile(cache_mla_kpe[:, 0], (1, 1, LANES // MLA_ROPE)).astype(BF16))
    y_sample, _ = _group(x_sample, dec_seq, 1 + jnp.arange(dec_batch), mods, wts, tabs_a, tabs_c, ctx_cache)

    return (y_prompt, y_sample,
            k.reshape(batch, 1, seq, A_KV_HEADS, A_HEAD_DIM), v.reshape(batch, 1, seq, A_KV_HEADS, A_HEAD_DIM),
            s_f[:, None], s_b[:, None],
            ckv32.reshape(batch, 1, seq, KV_LORA), kpe32[:, :MLA_ROPE].reshape(batch, 1, seq, MLA_ROPE))
```

```python
import functools
import math

import jax
import jax.numpy as jnp
import numpy as np
from jax import lax
from jax.experimental import pallas as pl
from jax.experimental.pallas import tpu as pltpu

F32 = jnp.float32
BF16 = jnp.bfloat16

LANES = 128
SUBLANES = 8
VMEM_LIMIT_BYTES = 56 * 1024 * 1024

GRID_W = 64
ROPE_BASE = 10000.0
EPS = 1e-6
NEG_INF = -1e30
WINDOW_BLK = 128
CHUNK = 128
A_HEADS, A_KV_HEADS, A_HEAD_DIM = 8, 2, 64
A_WIDTH = A_HEADS * A_HEAD_DIM
A_KV_WIDTH = A_KV_HEADS * A_HEAD_DIM
A_SCALE = A_HEAD_DIM ** -0.5
SSD_HEADS, SSD_HEAD_DIM, SSD_GROUPS, SSD_STATE = 16, 64, 2, 64
SSD_INNER = SSD_HEADS * SSD_HEAD_DIM
CONV_K = 5
BC_WIDTH = SSD_GROUPS * SSD_STATE
CONV_CH = SSD_INNER + 2 * BC_WIDTH
MLA_HEADS, MLA_NOPE, MLA_ROPE, MLA_V = 16, 64, 32, 64
Q_LORA, KV_LORA = 256, 128
MLA_WIDTH = MLA_HEADS * MLA_V
MLA_SCALE = (MLA_NOPE + MLA_ROPE) ** -0.5
LOG2E = 1.4426950408889634


def _cparams(*sem):
    return pltpu.CompilerParams(dimension_semantics=sem, vmem_limit_bytes=VMEM_LIMIT_BYTES)


def _silu(x):
    return x * (1.0 / (1.0 + jnp.exp(-x)))


def _dot(a, b):
    return jnp.dot(a, b, preferred_element_type=F32, precision=lax.Precision.DEFAULT)


def _dot_nt(a, b):
    return lax.dot_general(a, b, (((1,), (1,)), ((), ())), preferred_element_type=F32,
                           precision=lax.Precision.DEFAULT)


def _lane_iota(shape):
    return lax.broadcasted_iota(jnp.int32, shape, len(shape) - 1)


def _row_iota(shape):
    return lax.broadcasted_iota(jnp.int32, shape, len(shape) - 2)


def _mod_kernel(cond_ref, w_ref, b_ref, o_ref):
    s = _silu(cond_ref[...])
    o_ref[...] = _dot(s.astype(BF16), w_ref[...].astype(BF16)) + b_ref[...]


def _modulation(conds, ada_w, ada_b):
    depth, d, d3 = ada_w.shape
    tn = 768
    return pl.pallas_call(
        _mod_kernel,
        grid=(depth, d3 // tn),
        in_specs=[pl.BlockSpec((SUBLANES, d), lambda l, j: (0, 0)),
                  pl.BlockSpec((None, d, tn), lambda l, j: (l, 0, j)),
                  pl.BlockSpec((None, 1, tn), lambda l, j: (l, 0, j))],
        out_specs=pl.BlockSpec((None, SUBLANES, tn), lambda l, j: (l, 0, j)),
        out_shape=jax.ShapeDtypeStruct((depth, SUBLANES, d3), F32),
        compiler_params=_cparams("arbitrary", "arbitrary"),
        name="modulation",
    )(conds, ada_w, ada_b.reshape(depth, 1, d3))


def _norm_mod(x, nw, scale, shift):
    ms = jnp.mean(x * x, axis=-1, keepdims=True)
    y = (x * lax.rsqrt(ms + EPS)) * nw
    return y * (1.0 + scale) + shift


def _rope(x, cos, sin_signed, half_period):
    outs = []
    first = (_lane_iota((x.shape[0], LANES)) & (2 * half_period - 1)) < half_period
    for j in range(x.shape[1] // LANES):
        xj = x[:, j * LANES:(j + 1) * LANES]
        up = pltpu.roll(xj, LANES - half_period, 1)
        dn = pltpu.roll(xj, half_period, 1)
        outs.append(xj * cos + jnp.where(first, up, dn) * sin_signed)
    return outs[0] if len(outs) == 1 else jnp.concatenate(outs, axis=1)


AB_COLS = (A_WIDTH, A_KV_WIDTH, A_KV_WIDTH, A_WIDTH, SSD_INNER, CONV_CH, LANES)


def _ab_in_kernel(*refs, rope):
    if rope:
        (x_ref, nw_ref, sc_ref, sh_ref, w_ref, cos_ref, sin_ref,
         q_ref, k_ref, v_ref, g_ref, z_ref, xbc_ref, dt_ref) = refs
    else:
        (x_ref, nw_ref, sc_ref, sh_ref, w_ref,
         q_ref, k_ref, v_ref, g_ref, z_ref, xbc_ref, dt_ref) = refs
    h = _norm_mod(x_ref[...], nw_ref[...], sc_ref[...], sh_ref[...]).astype(BF16)
    offs = np.concatenate([[0], np.cumsum(AB_COLS)])
    outs = (q_ref, k_ref, v_ref, g_ref, z_ref, xbc_ref, dt_ref)
    for i, o_ref in enumerate(outs):
        y = _dot(h, w_ref[:, int(offs[i]):int(offs[i + 1])])
        if rope and i in (0, 1):
            y = _rope(y, cos_ref[...], sin_ref[...], A_HEAD_DIM // 4)
        o_ref[...] = y.astype(o_ref.dtype)


def _ab_in(x, nw, scale, shift, w, rope_tabs, seq_len, tm):
    t, d = x.shape
    per_seq = seq_len // tm if scale.shape[0] > 1 else None
    row = lambda i: (i, 0)
    mod = (lambda i: (i // per_seq, 0, 0)) if per_seq else (lambda i: (0, 0, 0))
    const = lambda i: (0, 0)
    in_specs = [pl.BlockSpec((tm, d), row),
                pl.BlockSpec((1, d), const),
                pl.BlockSpec((None, 1, d), mod),
                pl.BlockSpec((None, 1, d), mod),
                pl.BlockSpec(w.shape, const)]
    args = [x, nw, scale, shift, w]
    if rope_tabs is not None:
        nt = seq_len // tm
        pos = lambda i: (i % nt, 0)
        in_specs += [pl.BlockSpec((tm, LANES), pos), pl.BlockSpec((tm, LANES), pos)]
        args += list(rope_tabs)
    dts = (BF16, F32, F32, BF16, BF16, BF16, F32)
    return pl.pallas_call(
        functools.partial(_ab_in_kernel, rope=rope_tabs is not None),
        grid=(t // tm,),
        in_specs=in_specs,
        out_specs=[pl.BlockSpec((tm, c), row) for c in AB_COLS],
        out_shape=[jax.ShapeDtypeStruct((t, c), dt) for c, dt in zip(AB_COLS, dts)],
        compiler_params=_cparams("arbitrary"),
        name="ab_in_proj",
    )(*args)


def _conv_kernel(prev_ref, main_ref, next_ref, w_ref, b_ref, o_ref, *, tiles_per_seq):
    i = pl.program_id(0) % tiles_per_seq
    tm = main_ref.shape[0]
    prev = jnp.where(i > 0, prev_ref[...].astype(F32), 0.0)
    nxt = jnp.where(i < tiles_per_seq - 1, next_ref[...].astype(F32), 0.0)
    ext = jnp.concatenate([prev, main_ref[...].astype(F32), nxt], axis=0)
    pad = CONV_K // 2
    acc = b_ref[...] + ext[SUBLANES - pad:SUBLANES - pad + tm] * w_ref[0:1, :]
    for j in range(1, CONV_K):
        acc = acc + ext[SUBLANES - pad + j:SUBLANES - pad + j + tm] * w_ref[j:j + 1, :]
    o_ref[...] = _silu(acc).astype(o_ref.dtype)


def _conv(xbc, w, b, seq_len, tm):
    t, c = xbc.shape
    tps = seq_len // tm
    hb = tm // SUBLANES
    nblk8 = t // SUBLANES
    return pl.pallas_call(
        functools.partial(_conv_kernel, tiles_per_seq=tps),
        grid=(t // tm,),
        in_specs=[pl.BlockSpec((SUBLANES, c), lambda i: (jnp.maximum(i * hb - 1, 0), 0)),
                  pl.BlockSpec((tm, c), lambda i: (i, 0)),
                  pl.BlockSpec((SUBLANES, c), lambda i: (jnp.minimum((i + 1) * hb, nblk8 - 1), 0)),
                  pl.BlockSpec((SUBLANES, c), lambda i: (0, 0)),
                  pl.BlockSpec((1, c), lambda i: (0, 0))],
        out_specs=pl.BlockSpec((tm, c), lambda i: (i, 0)),
        out_shape=jax.ShapeDtypeStruct((t, c), BF16),
        compiler_params=_cparams("arbitrary"),
        name="ssd_conv",
    )(xbc, xbc, xbc, w, b)


def _ssd_kernel(x_ref, b_ref, c_ref, dt_ref, par_ref, dskip_ref, h0_ref, y_ref, hout_ref, h_scr, *,
                direction, nc, has_h0):
    ci = pl.program_id(1)
    hp = SSD_HEADS // SSD_GROUPS
    lane = _lane_iota((CHUNK, LANES))
    rowi = _row_iota((CHUNK, LANES))
    lo_half = lane < SSD_HEAD_DIM

    @pl.when(ci == 0)
    def _init():
        if has_h0:
            for pr in range(SSD_HEADS // 2):
                both = jnp.concatenate([h0_ref[2 * pr], h0_ref[2 * pr + 1]], axis=0)
                both = jnp.concatenate([both, jnp.zeros_like(both)], axis=1)
                st = both.T[0:SSD_STATE]
                lo_st = _lane_iota(st.shape) < SSD_HEAD_DIM
                h_scr[2 * pr] = jnp.where(lo_st, st, 0.0)
                h_scr[2 * pr + 1] = jnp.where(lo_st, 0.0, st)
        else:
            h_scr[...] = jnp.zeros_like(h_scr)

    dtp_in = dt_ref[...] + par_ref[0:1, :]
    dtp = jnp.maximum(dtp_in, 0.0) + jnp.log1p(jnp.exp(-jnp.abs(dtp_in)))
    la = dtp * (-jnp.exp(par_ref[1:2, :]))
    tri = (rowi >= lane) if direction == 0 else (rowi <= lane)
    tri = jnp.where(tri, 1.0, 0.0).astype(BF16)
    p1 = la.astype(BF16)
    r1 = la - p1.astype(F32)
    p2 = r1.astype(BF16)
    p3 = (r1 - p2.astype(F32)).astype(BF16)
    cum = _dot(tri, p1) + _dot(tri, p2) + _dot(tri, p3)
    cum_t = cum.T
    dtp_t = dtp.T
    end = CHUNK - 1 if direction == 0 else 0
    w_t = jnp.exp(cum_t[:, end:end + 1] - cum_t) * dtp_t
    causal = (rowi >= lane) if direction == 0 else (rowi <= lane)

    bm = b_ref[...]
    cm = c_ref[...]
    bm_t = bm.astype(F32).T
    cb = []
    cmask = []
    for g in range(SSD_GROUPS):
        in_g = (lane >= g * SSD_STATE) & (lane < (g + 1) * SSD_STATE)
        cg = jnp.where(in_g, cm, jnp.zeros_like(cm))
        cmask.append(cg.astype(F32))
        cb.append(_dot_nt(cg, bm))

    for pr in range(SSD_HEADS // 2):
        xp = x_ref[:, pr * LANES:(pr + 1) * LANES]
        y_pair = None
        for e in range(2):
            h = 2 * pr + e
            g = h // hp
            col = direction * SSD_HEADS + h
            xe = jnp.where(lo_half if e == 0 else ~lo_half, xp, jnp.zeros_like(xp))
            cumcol = jnp.broadcast_to(cum[:, col:col + 1], (CHUNK, LANES))
            diff = cumcol - cum_t[col:col + 1, :]
            dec = jnp.exp(jnp.where(causal, diff, NEG_INF))
            sc = cb[g] * dec * dtp_t[col:col + 1, :]
            ce = cmask[g] * jnp.exp(cumcol)
            hs = h_scr[h]
            hsb = hs.astype(BF16)
            zpad = jnp.zeros_like(hsb)
            a = jnp.concatenate([sc.astype(BF16), ce.astype(BF16)], axis=1)
            wmat = jnp.concatenate([xe, hsb, zpad] if g == 0 else [xe, zpad, hsb], axis=0)
            ye = _dot(a, wmat)
            y_pair = ye if y_pair is None else y_pair + ye
            bw = (bm_t[g * SSD_STATE:(g + 1) * SSD_STATE, :] * w_t[col:col + 1, :]).astype(BF16)
            new = _dot(bw, xe)
            cdec = jnp.exp(cumcol[end:end + 1, :])
            h_scr[h] = hs * cdec + new
        if direction == 0:
            y_pair = y_pair + dskip_ref[:, pr * LANES:(pr + 1) * LANES] * xp.astype(F32)
        y_ref[:, pr * LANES:(pr + 1) * LANES] = y_pair.astype(y_ref.dtype)

    @pl.when(ci == nc - 1)
    def _final():
        for pr in range(SSD_HEADS // 2):
            st = h_scr[2 * pr] + h_scr[2 * pr + 1]
            st = jnp.concatenate([st, jnp.zeros_like(st)], axis=0).T
            hout_ref[2 * pr] = st[0:SSD_HEAD_DIM, 0:SSD_STATE]
            hout_ref[2 * pr + 1] = st[SSD_HEAD_DIM:2 * SSD_HEAD_DIM, 0:SSD_STATE]


def _ssd(xbc, dt, par, dskip, h0, nb, seq_len, direction):
    t = xbc.shape[0]
    nc = seq_len // CHUNK
    cidx = (lambda b, c: b * nc + c) if direction == 0 else (lambda b, c: b * nc + nc - 1 - c)
    xcols = SSD_INNER // LANES
    has_h0 = h0 is not None
    if not has_h0:
        h0 = jnp.zeros((1, SSD_HEADS, SSD_HEAD_DIM, SSD_STATE), F32)
    h0_map = (lambda b, c: (b, 0, 0, 0)) if has_h0 else (lambda b, c: (0, 0, 0, 0))
    st_shape = (None, SSD_HEADS, SSD_HEAD_DIM, SSD_STATE)
    return pl.pallas_call(
        functools.partial(_ssd_kernel, direction=direction, nc=nc, has_h0=has_h0),
        grid=(nb, nc),
        in_specs=[pl.BlockSpec((CHUNK, SSD_INNER), lambda b, c: (cidx(b, c), 0)),
                  pl.BlockSpec((CHUNK, LANES), lambda b, c: (cidx(b, c), xcols)),
                  pl.BlockSpec((CHUNK, LANES), lambda b, c: (cidx(b, c), xcols + 1)),
                  pl.BlockSpec((CHUNK, LANES), lambda b, c: (cidx(b, c), 0)),
                  pl.BlockSpec((SUBLANES, LANES), lambda b, c: (0, 0)),
                  pl.BlockSpec((1, SSD_INNER), lambda b, c: (0, 0)),
                  pl.BlockSpec(st_shape, h0_map)],
        out_specs=[pl.BlockSpec((CHUNK, SSD_INNER), lambda b, c: (cidx(b, c), 0)),
                   pl.BlockSpec(st_shape, lambda b, c: (b, 0, 0, 0)),
                   pl.BlockSpec((None, SSD_HEADS, SSD_STATE, LANES), lambda b, c: (b, 0, 0, 0))],
        out_shape=[jax.ShapeDtypeStruct((t, SSD_INNER), BF16),
                   jax.ShapeDtypeStruct((nb, SSD_HEADS, SSD_HEAD_DIM, SSD_STATE), F32),
                   jax.ShapeDtypeStruct((nb, SSD_HEADS, SSD_STATE, LANES), F32)],
        compiler_params=_cparams("arbitrary", "arbitrary"),
        name="ssd_scan_fwd" if direction == 0 else "ssd_scan_bwd",
    )(xbc, xbc, xbc, dt, par, dskip, h0)[:2]


def _kv_variants(k, v):
    lane = _lane_iota(k.shape)
    lo = lane < A_HEAD_DIM
    k_sw = pltpu.roll(k, A_HEAD_DIM, 1)
    v_sw = pltpu.roll(v, A_HEAD_DIM, 1)
    ks, vs = {}, {}
    for j in range(A_KV_HEADS):
        for e in range(2):
            src_k, src_v = (k, v) if j == e else (k_sw, v_sw)
            half = lo if e == 0 else ~lo
            ks[j, e] = jnp.where(half, src_k, 0.0).astype(BF16)
            vs[j, e] = src_v.astype(BF16)
    return ks, vs


def _gqa_heads(q_ref, g_ref, o_ref, sink_ref, ks, vs, valid):
    m = q_ref.shape[0]
    assert A_HEADS // A_KV_HEADS == 4
    lo = _lane_iota((m, LANES)) < A_HEAD_DIM
    upper_rows = _row_iota((2 * m, 1)) >= m
    scale = jnp.asarray(A_SCALE, BF16)
    outs = {}
    for j in range(A_KV_HEADS):
        rows = jnp.concatenate([q_ref[:, 2 * j * LANES:(2 * j + 1) * LANES],
                                q_ref[:, (2 * j + 1) * LANES:(2 * j + 2) * LANES]], axis=0) * scale
        for e in range(2):
            s = _dot_nt(rows, ks[j, e])
            if valid is not None:
                nloc = valid.shape[1]
                s = jnp.concatenate([jnp.where(valid, s[:, :nloc], NEG_INF), s[:, nloc:]], axis=1)
            sink = jnp.where(upper_rows, sink_ref[4 * j + 2 + e], sink_ref[4 * j + e])
            mx = jnp.maximum(jnp.max(s, axis=1, keepdims=True), sink)
            p = jnp.exp(s - mx)
            den = jnp.sum(p, axis=1, keepdims=True) + jnp.exp(sink - mx)
            outs[j, e] = _dot(p.astype(BF16), vs[j, e]) / den
    for pr in range(A_HEADS // 2):
        j, half = pr // 2, pr % 2
        r = slice(half * m, (half + 1) * m)
        attn = jnp.where(lo, outs[j, 0][r], outs[j, 1][r])
        gate = g_ref[:, pr * LANES:(pr + 1) * LANES].astype(F32)
        o_ref[:, pr * LANES:(pr + 1) * LANES] = (attn * _silu(gate)).astype(o_ref.dtype)


def _attn_ctx_kernel(sink_ref, q_ref, k_ref, v_ref, g_ref, o_ref):
    ks, vs = _kv_variants(k_ref[...], v_ref[...])
    _gqa_heads(q_ref, g_ref, o_ref, sink_ref, ks, vs, None)


def _attn_ctx(sink, q, k, v, g, seq_len):
    t = q.shape[0]
    row = lambda b: (b, 0)
    return pl.pallas_call(
        _attn_ctx_kernel,
        grid=(t // seq_len,),
        in_specs=[pl.BlockSpec(memory_space=pltpu.SMEM),
                  pl.BlockSpec((seq_len, A_WIDTH), row),
                  pl.BlockSpec((seq_len, A_KV_WIDTH), row),
                  pl.BlockSpec((seq_len, A_KV_WIDTH), row),
                  pl.BlockSpec((seq_len, A_WIDTH), row)],
        out_specs=pl.BlockSpec((seq_len, A_WIDTH), row),
        out_shape=jax.ShapeDtypeStruct((t, A_WIDTH), BF16),
        compiler_params=_cparams("arbitrary"),
        name="swa_context_attention",
    )(sink, q, k, v, g)


BAND_Q = 256


def _attn_band_kernel(sink_ref, q_ref, kp_ref, kc_ref, kn_ref, vp_ref, vc_ref, vn_ref, kx_ref, vx_ref, g_ref,
                      o_ref, *, nsteps):
    n = pl.program_id(1)
    tq = q_ref.shape[0]
    blk = WINDOW_BLK
    k = jnp.concatenate([kp_ref[...], kc_ref[...], kn_ref[...], kx_ref[...]], axis=0)
    v = jnp.concatenate([vp_ref[...], vc_ref[...], vn_ref[...], vx_ref[...]], axis=0)
    nloc = tq + 2 * blk
    qi = _row_iota((2 * tq, nloc)) & (tq - 1)
    col = _lane_iota((2 * tq, nloc))
    rel = col - blk - qi
    valid = ((rel >= -blk) & (rel <= blk) & ((col >= blk) | (n > 0)) & ((col < nloc - blk) | (n < nsteps - 1)))
    ks, vs = _kv_variants(k, v)
    _gqa_heads(q_ref, g_ref, o_ref, sink_ref, ks, vs, valid)


def _attn_band(sink, q, k, v, k_ctx, v_ctx, g, nb, seq_len):
    t = q.shape[0]
    blk = WINDOW_BLK
    tq = BAND_Q
    assert tq & (tq - 1) == 0 and tq % blk == 0 and seq_len % tq == 0
    per = tq // blk
    nblk = seq_len // blk
    nsteps = seq_len // tq
    nctx = k_ctx.shape[1]
    cur = lambda b, n: (b * nsteps + n, 0)
    prv = lambda b, n: (b * nblk + jnp.maximum(n * per - 1, 0), 0)
    nxt = lambda b, n: (b * nblk + jnp.minimum((n + 1) * per, nblk - 1), 0)
    ctx = lambda b, n: (b, 0, 0)
    edge = lambda f: pl.BlockSpec((blk, A_KV_WIDTH), f)
    mid = pl.BlockSpec((tq, A_KV_WIDTH), cur)
    return pl.pallas_call(
        functools.partial(_attn_band_kernel, nsteps=nsteps),
        grid=(nb, nsteps),
        in_specs=[pl.BlockSpec(memory_space=pltpu.SMEM),
                  pl.BlockSpec((tq, A_WIDTH), cur),
                  edge(prv), mid, edge(nxt),
                  edge(prv), mid, edge(nxt),
                  pl.BlockSpec((None, nctx, A_KV_WIDTH), ctx),
                  pl.BlockSpec((None, nctx, A_KV_WIDTH), ctx),
                  pl.BlockSpec((tq, A_WIDTH), cur)],
        out_specs=pl.BlockSpec((tq, A_WIDTH), cur),
        out_shape=jax.ShapeDtypeStruct((t, A_WIDTH), BF16),
        compiler_params=_cparams("arbitrary", "arbitrary"),
        name="swa_banded_attention",
    )(sink, q, k, k, k, v, v, v, k_ctx, v_ctx, g)


def _ab_out_kernel(a_ref, yf_ref, yb_ref, z_ref, gnw_ref, w_ref, x_ref, gate_ref, o_ref):
    y = (yf_ref[...].astype(F32) + yb_ref[...].astype(F32)) * _silu(z_ref[...].astype(F32))
    ms = jnp.mean(y * y, axis=-1, keepdims=True)
    s = ((y * lax.rsqrt(ms + EPS)) * gnw_ref[...]).astype(BF16)
    out = _dot(a_ref[...], w_ref[0:A_WIDTH, :]) + _dot(s, w_ref[A_WIDTH:, :])
    o_ref[...] = x_ref[...] + gate_ref[...] * out


def _ab_out(a, yf, yb, z, gnw, w, x, gate, seq_len, tm):
    t, d = x.shape
    per_seq = seq_len // tm if gate.shape[0] > 1 else None
    row = lambda i: (i, 0)
    const = lambda i: (0, 0)
    mod = (lambda i: (i // per_seq, 0, 0)) if per_seq else (lambda i: (0, 0, 0))
    return pl.pallas_call(
        _ab_out_kernel,
        grid=(t // tm,),
        in_specs=[pl.BlockSpec((tm, A_WIDTH), row),
                  pl.BlockSpec((tm, SSD_INNER), row),
                  pl.BlockSpec((tm, SSD_INNER), row),
                  pl.BlockSpec((tm, SSD_INNER), row),
                  pl.BlockSpec((1, SSD_INNER), const),
                  pl.BlockSpec(w.shape, const),
                  pl.BlockSpec((tm, d), row),
                  pl.BlockSpec((None, 1, d), mod)],
        out_specs=pl.BlockSpec((tm, d), row),
        out_shape=jax.ShapeDtypeStruct((t, d), F32),
        compiler_params=_cparams("arbitrary"),
        name="ab_out_proj",
    )(a, yf, yb, z, gnw, w, x, gate)


MLA_COLS = (Q_LORA, KV_LORA, LANES, MLA_WIDTH)


def _mla_in_kernel(*refs, rope):
    if rope:
        (x_ref, nw_ref, sc_ref, sh_ref, w_ref, qnw_ref, kvnw_ref, wuq_ref, cos_ref, sin_ref,
         qn_ref, qpe_ref, ckv_ref, kpe_ref, g_ref, ckv32_ref, kpe32_ref) = refs
    else:
        (x_ref, nw_ref, sc_ref, sh_ref, w_ref, qnw_ref, kvnw_ref, wuq_ref,
         qn_ref, qpe_ref, ckv_ref, kpe_ref, g_ref, ckv32_ref, kpe32_ref) = refs
    h = _norm_mod(x_ref[...], nw_ref[...], sc_ref[...], sh_ref[...]).astype(BF16)
    offs = [int(o) for o in np.concatenate([[0], np.cumsum(MLA_COLS)])]
    cq = _dot(h, w_ref[:, offs[0]:offs[1]])
    ckv = _dot(h, w_ref[:, offs[1]:offs[2]])
    kpe = _dot(h, w_ref[:, offs[2]:offs[3]])
    g_ref[...] = _dot(h, w_ref[:, offs[3]:offs[4]]).astype(g_ref.dtype)

    def rms(u, w):
        return (u * lax.rsqrt(jnp.mean(u * u, axis=-1, keepdims=True) + EPS)) * w

    cqn = rms(cq, qnw_ref[...]).astype(BF16)
    nope_w = MLA_HEADS * MLA_NOPE
    qn_ref[...] = _dot(cqn, wuq_ref[:, 0:nope_w]).astype(qn_ref.dtype)
    qpe = _dot(cqn, wuq_ref[:, nope_w:])
    if rope:
        qpe = _rope(qpe, cos_ref[...], sin_ref[...], MLA_ROPE // 4)
        kpe = _rope(kpe, cos_ref[...], sin_ref[...], MLA_ROPE // 4)
    qpe_ref[...] = qpe.astype(qpe_ref.dtype)
    ckvn = rms(ckv, kvnw_ref[...])
    ckv_ref[...] = ckvn.astype(ckv_ref.dtype)
    kpe_ref[...] = kpe.astype(kpe_ref.dtype)
    ckv32_ref[...] = ckvn
    kpe32_ref[...] = kpe


def _mla_in(x, nw, scale, shift, w, qnw, kvnw, wuq, rope_tabs, seq_len, tm):
    t, d = x.shape
    per_seq = seq_len // tm if scale.shape[0] > 1 else None
    row = lambda i: (i, 0)
    const = lambda i: (0, 0)
    mod = (lambda i: (i // per_seq, 0, 0)) if per_seq else (lambda i: (0, 0, 0))
    in_specs = [pl.BlockSpec((tm, d), row), pl.BlockSpec((1, d), const),
                pl.BlockSpec((None, 1, d), mod), pl.BlockSpec((None, 1, d), mod),
                pl.BlockSpec(w.shape, const), pl.BlockSpec((1, Q_LORA), const),
                pl.BlockSpec((1, KV_LORA), const), pl.BlockSpec(wuq.shape, const)]
    args = [x, nw, scale, shift, w, qnw, kvnw, wuq]
    if rope_tabs is not None:
        nt = seq_len // tm
        pos = lambda i: (i % nt, 0)
        in_specs += [pl.BlockSpec((tm, LANES), pos), pl.BlockSpec((tm, LANES), pos)]
        args += list(rope_tabs)
    widths = (MLA_HEADS * MLA_NOPE, MLA_HEADS * MLA_ROPE, KV_LORA, LANES, MLA_WIDTH, KV_LORA, LANES)
    dts = (BF16, BF16, BF16, BF16, BF16, F32, F32)
    return pl.pallas_call(
        functools.partial(_mla_in_kernel, rope=rope_tabs is not None),
        grid=(t // tm,),
        in_specs=in_specs,
        out_specs=[pl.BlockSpec((tm, c), row) for c in widths],
        out_shape=[jax.ShapeDtypeStruct((t, c), dt) for c, dt in zip(widths, dts)],
        compiler_params=_cparams("arbitrary"),
        name="mla_in_proj",
    )(*args)


MLA_UNIT_ROWS = 256
MLA_KEY_TILE = 256


def _mla_attn_kernel(qn_ref, qpe_ref, ckv_ref, kpe_ref, wukt_ref, wuv_ref, g_ref, o_ref, kcat_scr, v_scr, s_scr, *,
                     pairs_per_step):
    pg = pl.program_id(1)
    qb = pl.program_id(2)
    tq = qn_ref.shape[0]
    c_exp = MLA_SCALE * LOG2E

    @pl.when(qb == 0)
    def _expand():
        ckv = ckv_ref[...]
        eye = jnp.where(_row_iota((LANES, LANES)) == _lane_iota((LANES, LANES)), 1.0, 0.0).astype(BF16)
        kpe_t = _dot_nt(eye, kpe_ref[...]).astype(BF16)
        for i in range(pairs_per_step):
            kcat_scr[i, 0:LANES, :] = _dot_nt(wukt_ref[i * LANES:(i + 1) * LANES, :], ckv).astype(BF16)
            kcat_scr[i, LANES:2 * LANES, :] = kpe_t
            v_scr[i] = _dot(ckv, wuv_ref[:, i * LANES:(i + 1) * LANES]).astype(BF16)

    nkeys = kcat_scr.shape[-1]
    kt_w = min(MLA_KEY_TILE, nkeys)
    nkt = nkeys // kt_w
    ru = min(MLA_UNIT_ROWS, tq)
    lane = _lane_iota((ru, LANES))
    units = [(i, r, e) for i in range(pairs_per_step) for r in range(tq // ru) for e in range(2)]

    def lane_tiles(x):
        return [x[:, j * LANES:(j + 1) * LANES] for j in range(x.shape[1] // LANES)]

    def q_ext(u):
        i, r, e = units[u]
        rows = slice(r * ru, (r + 1) * ru)
        qn = qn_ref[rows, i * LANES:(i + 1) * LANES]
        qpe = qpe_ref[rows, (i // 2) * LANES:(i // 2 + 1) * LANES]
        slot = 2 * (i % 2) + e if pairs_per_step % 2 == 0 else 2 * ((pg * pairs_per_step + i) % 2) + e
        nope_half = (lane < MLA_NOPE) if e == 0 else (lane >= MLA_NOPE)
        return jnp.concatenate([jnp.where(nope_half, qn, jnp.zeros_like(qn)),
                                jnp.where((lane >> 5) == slot, qpe, jnp.zeros_like(qpe))], axis=1)

    def score_tile(u, st, kt):
        cols = slice(kt * kt_w, (kt + 1) * kt_w)
        s = _dot(st["q"], kcat_scr[units[u][0], :, cols])
        s_scr[u % 2, :, cols] = s
        for t in lane_tiles(s):
            st["mrun"] = t if st["mrun"] is None else jnp.maximum(st["mrun"], t)

    def value_tile(u, st, kt):
        cols = slice(kt * kt_w, (kt + 1) * kt_w)
        p = jnp.exp2((s_scr[u % 2, :, cols] - st["m"]) * c_exp)
        for t in lane_tiles(p):
            st["lrun"] = t if st["lrun"] is None else st["lrun"] + t
        pv = _dot(p.astype(BF16), v_scr[units[u][0], cols, :])
        st["acc"] = pv if st["acc"] is None else st["acc"] + pv

    def finish(u, st, done):
        i, r, e = units[u]
        den = jnp.sum(st["lrun"], axis=1, keepdims=True)
        done[e] = st["acc"] / den
        if e == 1:
            rows = slice(r * ru, (r + 1) * ru)
            attn = jnp.where(lane < MLA_V, done[0], done[1])
            gate = g_ref[rows, i * LANES:(i + 1) * LANES].astype(F32)
            o_ref[rows, i * LANES:(i + 1) * LANES] = (attn * _silu(gate)).astype(o_ref.dtype)

    done = {}
    cur = None
    for u in range(len(units) + 1):
        nxt = dict(q=q_ext(u), mrun=None) if u < len(units) else None
        for kt in range(nkt):
            if nxt is not None:
                score_tile(u, nxt, kt)
            if cur is not None:
                value_tile(u - 1, cur, kt)
        if cur is not None:
            finish(u - 1, cur, done)
        if nxt is not None:
            nxt.update(m=jnp.max(nxt["mrun"], axis=1, keepdims=True), lrun=None, acc=None)
        cur = nxt


def _mla_attn(qn, qpe, ckv_keys, kpe_keys, wukt, wuv, g, nb, seq_len, tq, pairs_per_step):
    t = qn.shape[0]
    nkeys = ckv_keys.shape[1]
    npairs = MLA_HEADS // 2
    ngrp = npairs // pairs_per_step
    nq = seq_len // tq
    wp = pairs_per_step * LANES
    if pairs_per_step % 2 == 0:
        wpe = wp // 2
        pe_map = lambda b, p, i: (b * nq + i, p)
    else:
        wpe = LANES
        pe_map = lambda b, p, i: (b * nq + i, p // 2)
    qmap = lambda b, p, i: (b * nq + i, p)
    return pl.pallas_call(
        functools.partial(_mla_attn_kernel, pairs_per_step=pairs_per_step),
        grid=(nb, ngrp, nq),
        in_specs=[pl.BlockSpec((tq, wp), qmap),
                  pl.BlockSpec((tq, wpe), pe_map),
                  pl.BlockSpec((None, nkeys, KV_LORA), lambda b, p, i: (b, 0, 0)),
                  pl.BlockSpec((None, nkeys, LANES), lambda b, p, i: (b, 0, 0)),
                  pl.BlockSpec((wp, KV_LORA), lambda b, p, i: (p, 0)),
                  pl.BlockSpec((KV_LORA, wp), lambda b, p, i: (0, p)),
                  pl.BlockSpec((tq, wp), qmap)],
        out_specs=pl.BlockSpec((tq, wp), qmap),
        out_shape=jax.ShapeDtypeStruct((t, MLA_WIDTH), BF16),
        scratch_shapes=[pltpu.VMEM((pairs_per_step, 2 * LANES, nkeys), BF16),
                        pltpu.VMEM((pairs_per_step, nkeys, LANES), BF16),
                        pltpu.VMEM((2, min(MLA_UNIT_ROWS, tq), nkeys), F32)],
        compiler_params=_cparams("arbitrary", "arbitrary", "arbitrary"),
        name="mla_attention",
    )(qn, qpe, ckv_keys, kpe_keys, wukt, wuv, g)


def _mla_out_kernel(a_ref, w_ref, x_ref, gate_ref, fw_ref, o_ref):
    xn = x_ref[...] + gate_ref[...] * _dot(a_ref[...], w_ref[...])
    ms = jnp.mean(xn * xn, axis=-1, keepdims=True)
    o_ref[...] = (xn * lax.rsqrt(ms + EPS)) * fw_ref[...]


def _mla_out(a, w, x, gate, fw, seq_len, tm):
    t, d = x.shape
    per_seq = seq_len // tm if gate.shape[0] > 1 else None
    row = lambda i: (i, 0)
    const = lambda i: (0, 0)
    mod = (lambda i: (i // per_seq, 0, 0)) if per_seq else (lambda i: (0, 0, 0))
    return pl.pallas_call(
        _mla_out_kernel,
        grid=(t // tm,),
        in_specs=[pl.BlockSpec((tm, MLA_WIDTH), row), pl.BlockSpec(w.shape, const),
                  pl.BlockSpec((tm, d), row), pl.BlockSpec((None, 1, d), mod), pl.BlockSpec((1, d), const)],
        out_specs=pl.BlockSpec((tm, d), row),
        out_shape=jax.ShapeDtypeStruct((t, d), F32),
        compiler_params=_cparams("arbitrary"),
        name="mla_out_proj_final_norm",
    )(a, w, x, gate, fw)


def _rope_tables(length, dim):
    rows = length // GRID_W
    row = jnp.repeat(jnp.arange(rows), GRID_W).astype(F32)
    col = jnp.tile(jnp.arange(GRID_W), rows).astype(F32)
    nf = dim // 4
    inv = 1.0 / (ROPE_BASE ** (jnp.arange(nf, dtype=F32) / nf))
    ar = row[:, None] * inv[None, :]
    ac = col[:, None] * inv[None, :]
    ang = jnp.concatenate([ar, ar, ac, ac], axis=-1)
    sign = jnp.tile(jnp.concatenate([-jnp.ones((nf,), F32), jnp.ones((nf,), F32)]), 2)
    reps = LANES // dim
    return jnp.tile(jnp.cos(ang), (1, reps)), jnp.tile(jnp.sin(ang) * sign, (1, reps))


def _group(x, seq_len, conds_rows, mods, wts, tabs_a, tabs_c, ctx_cache):
    nb, _, d = x.shape
    t = nb * seq_len
    latent = ctx_cache is not None
    xf = x.reshape(t, d)
    tm = 512

    def mod_rows(layer):
        m = mods[layer][conds_rows]
        sh, sc, gt = jnp.split(m[:, None, :], 3, axis=-1)
        return sh, sc, gt

    sh, sc, gt = mod_rows(0)
    q, k, v, g, z, xbc, dt = _ab_in(xf, wts["norm_w"][0:1], sc, sh, wts["ab_w_in"], tabs_a if latent else None,
                                    seq_len, tm)
    xbc_c = _conv(xbc, wts["ab_conv_w"], wts["ab_conv_b"], seq_len, min(tm, seq_len))
    if latent:
        k_ctx, v_ctx, s_f0, s_b0, _, _ = ctx_cache
        attn = _attn_band(wts["ab_sink"], q, k, v, k_ctx, v_ctx, g, nb, seq_len)
    else:
        s_f0 = s_b0 = None
        attn = _attn_ctx(wts["ab_sink"], q, k, v, g, seq_len)
    yf, s_f = _ssd(xbc_c, dt, wts["ssd_par"], wts["ssd_dskip"], s_f0, nb, seq_len, 0)
    yb, s_b = _ssd(xbc_c, dt, wts["ssd_par"], wts["ssd_dskip"], s_b0, nb, seq_len, 1)
    x1 = _ab_out(attn, yf, yb, z, wts["ab_gnorm_w"], wts["ab_w_out"], xf, gt, seq_len, tm)

    sh, sc, gt = mod_rows(1)
    qn, qpe, ckv, kpe, g1, ckv32, kpe32 = _mla_in(
        x1, wts["norm_w"][1:2], sc, sh, wts["mla_w_in"], wts["mla_q_norm_w"], wts["mla_kv_norm_w"],
        wts["mla_w_uq"], tabs_c if latent else None, seq_len, tm)
    ckv_keys = ckv.reshape(nb, seq_len, KV_LORA)
    kpe_keys = kpe.reshape(nb, seq_len, LANES)
    if latent:
        ckv_x, kpe_x = ctx_cache[4], ctx_cache[5]
        ckv_keys = jnp.concatenate([ckv_keys, ckv_x], axis=1)
        kpe_keys = jnp.concatenate([kpe_keys, kpe_x], axis=1)
        attn1 = _mla_attn(qn, qpe, ckv_keys, kpe_keys, wts["mla_w_ukt"], wts["mla_w_uv"], g1, nb, seq_len, 512, 1)
    else:
        attn1 = _mla_attn(qn, qpe, ckv_keys, kpe_keys, wts["mla_w_ukt"], wts["mla_w_uv"], g1, nb, seq_len,
                          seq_len, MLA_HEADS // 2)
    y = _mla_out(attn1, wts["mla_w_out"], x1, gt, wts["final_norm_w"], seq_len, tm)
    return y.reshape(nb, seq_len, d), (k, v, s_f, s_b, ckv32, kpe32)


def kernel(x_prompt, x_sample, cache_a_k, cache_a_v, state_ssd_fwd, state_ssd_bwd, cache_mla_ckv, cache_mla_kpe,
           c, c_ctx, ada_w, ada_b, norm_w, ab_w_in, ab_sink, ab_conv_w, ab_conv_b, ab_dt_bias, ab_a_log,
           ab_d_skip, ab_gnorm_w, ab_w_out, mla_w_in, mla_q_norm_w, mla_kv_norm_w, mla_w_uq, mla_w_ukv,
           mla_w_out, final_norm_w):
    batch, seq, d = x_prompt.shape
    dec_batch, dec_seq, _ = x_sample.shape
    assert ada_w.shape[0] == 2 and ab_w_in.shape[0] == 1 and mla_w_in.shape[0] == 1
    assert dec_batch + 1 <= SUBLANES

    conds = jnp.concatenate([c_ctx[None, :], c, jnp.zeros((SUBLANES - 1 - dec_batch, d), F32)], axis=0)
    mods = _modulation(conds, ada_w, ada_b)

    w_ab = ab_w_in[0]
    w_ab = jnp.concatenate([w_ab, jnp.zeros((d, LANES - 2 * SSD_HEADS), F32)], axis=1).astype(BF16)
    mw = mla_w_in[0]
    o_kpe = Q_LORA + KV_LORA
    w_mla = jnp.concatenate([mw[:, :o_kpe], jnp.tile(mw[:, o_kpe:o_kpe + MLA_ROPE], (1, LANES // MLA_ROPE)),
                             mw[:, o_kpe + MLA_ROPE:]], axis=1).astype(BF16)
    wuq = mla_w_uq[0].reshape(Q_LORA, MLA_HEADS, MLA_NOPE + MLA_ROPE)
    wuq = jnp.concatenate([wuq[:, :, :MLA_NOPE].reshape(Q_LORA, -1), wuq[:, :, MLA_NOPE:].reshape(Q_LORA, -1)],
                          axis=1).astype(BF16)
    wukv = mla_w_ukv[0].reshape(KV_LORA, MLA_HEADS, MLA_NOPE + MLA_V)
    wukt = wukv[:, :, :MLA_NOPE].reshape(KV_LORA, -1).T.astype(BF16)
    wuv = wukv[:, :, MLA_NOPE:].reshape(KV_LORA, -1).astype(BF16)
    pad_lanes = lambda r: jnp.concatenate([r.reshape(1, -1), jnp.zeros((1, LANES - r.size), F32)], axis=1)
    ssd_par = jnp.concatenate([pad_lanes(ab_dt_bias[0]), pad_lanes(ab_a_log[0]),
                               jnp.zeros((SUBLANES - 2, LANES), F32)], axis=0)
    wts = dict(
        norm_w=norm_w, ab_w_in=w_ab, ab_sink=ab_sink[0],
        ab_conv_w=jnp.concatenate([ab_conv_w[0], jnp.zeros((SUBLANES - CONV_K, CONV_CH), F32)], axis=0),
        ab_conv_b=ab_conv_b[0][None, :], ssd_par=ssd_par,
        ssd_dskip=jnp.repeat(ab_d_skip[0], SSD_HEAD_DIM)[None, :],
        ab_gnorm_w=ab_gnorm_w[0][None, :], ab_w_out=ab_w_out[0].astype(BF16),
        mla_w_in=w_mla, mla_q_norm_w=mla_q_norm_w[0][None, :], mla_kv_norm_w=mla_kv_norm_w[0][None, :],
        mla_w_uq=wuq, mla_w_ukt=wukt, mla_w_uv=wuv, mla_w_out=mla_w_out[0].astype(BF16),
        final_norm_w=final_norm_w[None, :],
    )
    tabs_a = _rope_tables(dec_seq, A_HEAD_DIM)
    tabs_c = _rope_tables(dec_seq, MLA_ROPE)

    y_prompt, (k, v, s_f, s_b, ckv32, kpe32) = _group(
        x_prompt, seq, jnp.zeros((1,), jnp.int32), mods, wts, tabs_a, tabs_c, None)

    past = cache_a_k.shape[2]
    ctx_cache = (cache_a_k[:, 0].reshape(dec_batch, past, A_KV_WIDTH),
                 cache_a_v[:, 0].reshape(dec_batch, past, A_KV_WIDTH),
                 state_ssd_fwd[:, 0], state_ssd_bwd[:, 0],
                 cache_mla_ckv[:, 0].astype(BF16),
                 jnp.tile(cache_mla_kpe[:, 0], (1, 1, LANES // MLA_ROPE)).astype(BF16))
    y_sample, _ = _group(x_sample, dec_seq, 1 + jnp.arange(dec_batch), mods, wts, tabs_a, tabs_c, ctx_cache)

    return (y_prompt, y_sample,
            k.reshape(batch, 1, seq, A_KV_HEADS, A_HEAD_DIM), v.reshape(batch, 1, seq, A_KV_HEADS, A_HEAD_DIM),
            s_f[:, None], s_b[:, None],
            ckv32.reshape(batch, 1, seq, KV_LORA), kpe32[:, :MLA_ROPE].reshape(batch, 1, seq, MLA_ROPE))
```

```python
import functools
import math

import jax
import jax.numpy as jnp
import numpy as np
from jax import lax
from jax.experimental import pallas as pl
from jax.experimental.pallas import tpu as pltpu

F32 = jnp.float32
BF16 = jnp.bfloat16

LANES = 128
SUBLANES = 8
VMEM_LIMIT_BYTES = 56 * 1024 * 1024

GRID_W = 64
ROPE_BASE = 10000.0
EPS = 1e-6
NEG_INF = -1e30
WINDOW_BLK = 128
CHUNK = 128
A_HEADS, A_KV_HEADS, A_HEAD_DIM = 8, 2, 64
A_WIDTH = A_HEADS * A_HEAD_DIM
A_KV_WIDTH = A_KV_HEADS * A_HEAD_DIM
A_SCALE = A_HEAD_DIM ** -0.5
SSD_HEADS, SSD_HEAD_DIM, SSD_GROUPS, SSD_STATE = 16, 64, 2, 64
SSD_INNER = SSD_HEADS * SSD_HEAD_DIM
CONV_K = 5
BC_WIDTH = SSD_GROUPS * SSD_STATE
CONV_CH = SSD_INNER + 2 * BC_WIDTH
MLA_HEADS, MLA_NOPE, MLA_ROPE, MLA_V = 16, 64, 32, 64
Q_LORA, KV_LORA = 256, 128
MLA_WIDTH = MLA_HEADS * MLA_V
MLA_SCALE = (MLA_NOPE + MLA_ROPE) ** -0.5
LOG2E = 1.4426950408889634


def _cparams(*sem):
    return pltpu.CompilerParams(dimension_semantics=sem, vmem_limit_bytes=VMEM_LIMIT_BYTES)


def _silu(x):
    return x * (1.0 / (1.0 + jnp.exp(-x)))


def _dot(a, b):
    return jnp.dot(a, b, preferred_element_type=F32, precision=lax.Precision.DEFAULT)


def _dot_nt(a, b):
    return lax.dot_general(a, b, (((1,), (1,)), ((), ())), preferred_element_type=F32,
                           precision=lax.Precision.DEFAULT)


def _lane_iota(shape):
    return lax.broadcasted_iota(jnp.int32, shape, len(shape) - 1)


def _row_iota(shape):
    return lax.broadcasted_iota(jnp.int32, shape, len(shape) - 2)


def _mod_kernel(cond_ref, w_ref, b_ref, o_ref):
    s = _silu(cond_ref[...])
    o_ref[...] = _dot(s.astype(BF16), w_ref[...].astype(BF16)) + b_ref[...]


def _modulation(conds, ada_w, ada_b):
    depth, d, d3 = ada_w.shape
    tn = 768
    return pl.pallas_call(
        _mod_kernel,
        grid=(depth, d3 // tn),
        in_specs=[pl.BlockSpec((SUBLANES, d), lambda l, j: (0, 0)),
                  pl.BlockSpec((None, d, tn), lambda l, j: (l, 0, j)),
                  pl.BlockSpec((None, 1, tn), lambda l, j: (l, 0, j))],
        out_specs=pl.BlockSpec((None, SUBLANES, tn), lambda l, j: (l, 0, j)),
        out_shape=jax.ShapeDtypeStruct((depth, SUBLANES, d3), F32),
        compiler_params=_cparams("arbitrary", "arbitrary"),
        name="modulation",
    )(conds, ada_w, ada_b.reshape(depth, 1, d3))


def _norm_mod(x, nw, scale, shift):
    ms = jnp.mean(x * x, axis=-1, keepdims=True)
    y = (x * lax.rsqrt(ms + EPS)) * nw
    return y * (1.0 + scale) + shift


def _rope(x, cos, sin_signed, half_period):
    outs = []
    first = (_lane_iota((x.shape[0], LANES)) & (2 * half_period - 1)) < half_period
    for j in range(x.shape[1] // LANES):
        xj = x[:, j * LANES:(j + 1) * LANES]
        up = pltpu.roll(xj, LANES - half_period, 1)
        dn = pltpu.roll(xj, half_period, 1)
        outs.append(xj * cos + jnp.where(first, up, dn) * sin_signed)
    return outs[0] if len(outs) == 1 else jnp.concatenate(outs, axis=1)


AB_COLS = (A_WIDTH, A_KV_WIDTH, A_KV_WIDTH, A_WIDTH, SSD_INNER, CONV_CH, LANES)


def _ab_in_kernel(*refs, rope):
    if rope:
        (x_ref, nw_ref, sc_ref, sh_ref, w_ref, cos_ref, sin_ref,
         q_ref, k_ref, v_ref, g_ref, z_ref, xbc_ref, dt_ref) = refs
    else:
        (x_ref, nw_ref, sc_ref, sh_ref, w_ref,
         q_ref, k_ref, v_ref, g_ref, z_ref, xbc_ref, dt_ref) = refs
    h = _norm_mod(x_ref[...], nw_ref[...], sc_ref[...], sh_ref[...]).astype(BF16)
    offs = np.concatenate([[0], np.cumsum(AB_COLS)])
    outs = (q_ref, k_ref, v_ref, g_ref, z_ref, xbc_ref, dt_ref)
    for i, o_ref in enumerate(outs):
        y = _dot(h, w_ref[:, int(offs[i]):int(offs[i + 1])])
        if rope and i in (0, 1):
            y = _rope(y, cos_ref[...], sin_ref[...], A_HEAD_DIM // 4)
        o_ref[...] = y.astype(o_ref.dtype)


def _ab_in(x, nw, scale, shift, w, rope_tabs, seq_len, tm):
    t, d = x.shape
    per_seq = seq_len // tm if scale.shape[0] > 1 else None
    row = lambda i: (i, 0)
    mod = (lambda i: (i // per_seq, 0, 0)) if per_seq else (lambda i: (0, 0, 0))
    const = lambda i: (0, 0)
    in_specs = [pl.BlockSpec((tm, d), row),
                pl.BlockSpec((1, d), const),
                pl.BlockSpec((None, 1, d), mod),
                pl.BlockSpec((None, 1, d), mod),
                pl.BlockSpec(w.shape, const)]
    args = [x, nw, scale, shift, w]
    if rope_tabs is not None:
        nt = seq_len // tm
        pos = lambda i: (i % nt, 0)
        in_specs += [pl.BlockSpec((tm, LANES), pos), pl.BlockSpec((tm, LANES), pos)]
        args += list(rope_tabs)
    dts = (BF16, F32, F32, BF16, BF16, BF16, F32)
    return pl.pallas_call(
        functools.partial(_ab_in_kernel, rope=rope_tabs is not None),
        grid=(t // tm,),
        in_specs=in_specs,
        out_specs=[pl.BlockSpec((tm, c), row) for c in AB_COLS],
        out_shape=[jax.ShapeDtypeStruct((t, c), dt) for c, dt in zip(AB_COLS, dts)],
        compiler_params=_cparams("arbitrary"),
        name="ab_in_proj",
    )(*args)


def _conv_kernel(prev_ref, main_ref, next_ref, w_ref, b_ref, o_ref, *, tiles_per_seq):
    i = pl.program_id(0) % tiles_per_seq
    tm = main_ref.shape[0]
    prev = jnp.where(i > 0, prev_ref[...].astype(F32), 0.0)
    nxt = jnp.where(i < tiles_per_seq - 1, next_ref[...].astype(F32), 0.0)
    ext = jnp.concatenate([prev, main_ref[...].astype(F32), nxt], axis=0)
    pad = CONV_K // 2
    acc = b_ref[...] + ext[SUBLANES - pad:SUBLANES - pad + tm] * w_ref[0:1, :]
    for j in range(1, CONV_K):
        acc = acc + ext[SUBLANES - pad + j:SUBLANES - pad + j + tm] * w_ref[j:j + 1, :]
    o_ref[...] = _silu(acc).astype(o_ref.dtype)


def _conv(xbc, w, b, seq_len, tm):
    t, c = xbc.shape
    tps = seq_len // tm
    hb = tm // SUBLANES
    nblk8 = t // SUBLANES
    return pl.pallas_call(
        functools.partial(_conv_kernel, tiles_per_seq=tps),
        grid=(t // tm,),
        in_specs=[pl.BlockSpec((SUBLANES, c), lambda i: (jnp.maximum(i * hb - 1, 0), 0)),
                  pl.BlockSpec((tm, c), lambda i: (i, 0)),
                  pl.BlockSpec((SUBLANES, c), lambda i: (jnp.minimum((i + 1) * hb, nblk8 - 1), 0)),
                  pl.BlockSpec((SUBLANES, c), lambda i: (0, 0)),
                  pl.BlockSpec((1, c), lambda i: (0, 0))],
        out_specs=pl.BlockSpec((tm, c), lambda i: (i, 0)),
        out_shape=jax.ShapeDtypeStruct((t, c), BF16),
        compiler_params=_cparams("arbitrary"),
        name="ssd_conv",
    )(xbc, xbc, xbc, w, b)


def _ssd_kernel(x_ref, b_ref, c_ref, dt_ref, par_ref, dskip_ref, h0_ref, y_ref, hout_ref, h_scr, *,
                direction, nc, has_h0):
    ci = pl.program_id(1)
    hp = SSD_HEADS // SSD_GROUPS
    lane = _lane_iota((CHUNK, LANES))
    rowi = _row_iota((CHUNK, LANES))
    lo_half = lane < SSD_HEAD_DIM

    @pl.when(ci == 0)
    def _init():
        if has_h0:
            for pr in range(SSD_HEADS // 2):
                both = jnp.concatenate([h0_ref[2 * pr], h0_ref[2 * pr + 1]], axis=0)
                both = jnp.concatenate([both, jnp.zeros_like(both)], axis=1)
                st = both.T[0:SSD_STATE]
                lo_st = _lane_iota(st.shape) < SSD_HEAD_DIM
                h_scr[2 * pr] = jnp.where(lo_st, st, 0.0)
                h_scr[2 * pr + 1] = jnp.where(lo_st, 0.0, st)
        else:
            h_scr[...] = jnp.zeros_like(h_scr)

    dtp_in = dt_ref[...] + par_ref[0:1, :]
    dtp = jnp.maximum(dtp_in, 0.0) + jnp.log1p(jnp.exp(-jnp.abs(dtp_in)))
    la = dtp * (-jnp.exp(par_ref[1:2, :]))
    tri = (rowi >= lane) if direction == 0 else (rowi <= lane)
    tri = jnp.where(tri, 1.0, 0.0).astype(BF16)
    p1 = la.astype(BF16)
    r1 = la - p1.astype(F32)
    p2 = r1.astype(BF16)
    p3 = (r1 - p2.astype(F32)).astype(BF16)
    cum = _dot(tri, p1) + _dot(tri, p2) + _dot(tri, p3)
    cum_t = cum.T
    dtp_t = dtp.T
    end = CHUNK - 1 if direction == 0 else 0
    w_t = jnp.exp(cum_t[:, end:end + 1] - cum_t) * dtp_t
    causal = (rowi >= lane) if direction == 0 else (rowi <= lane)

    bm = b_ref[...]
    cm = c_ref[...]
    bm_t = bm.astype(F32).T
    cb = []
    cmask = []
    for g in range(SSD_GROUPS):
        in_g = (lane >= g * SSD_STATE) & (lane < (g + 1) * SSD_STATE)
        cg = jnp.where(in_g, cm, jnp.zeros_like(cm))
        cmask.append(cg.astype(F32))
        cb.append(_dot_nt(cg, bm))

    for pr in range(SSD_HEADS // 2):
        xp = x_ref[:, pr * LANES:(pr + 1) * LANES]
        y_pair = None
        for e in range(2):
            h = 2 * pr + e
            g = h // hp
            col = direction * SSD_HEADS + h
            xe = jnp.where(lo_half if e == 0 else ~lo_half, xp, jnp.zeros_like(xp))
            cumcol = jnp.broadcast_to(cum[:, col:col + 1], (CHUNK, LANES))
            diff = cumcol - cum_t[col:col + 1, :]
            dec = jnp.exp(jnp.where(causal, diff, NEG_INF))
            sc = cb[g] * dec * dtp_t[col:col + 1, :]
            ce = cmask[g] * jnp.exp(cumcol)
            hs = h_scr[h]
            hsb = hs.astype(BF16)
            zpad = jnp.zeros_like(hsb)
            a = jnp.concatenate([sc.astype(BF16), ce.astype(BF16)], axis=1)
            wmat = jnp.concatenate([xe, hsb, zpad] if g == 0 else [xe, zpad, hsb], axis=0)
            ye = _dot(a, wmat)
            y_pair = ye if y_pair is None else y_pair + ye
            bw = (bm_t[g * SSD_STATE:(g + 1) * SSD_STATE, :] * w_t[col:col + 1, :]).astype(BF16)
            new = _dot(bw, xe)
            cdec = jnp.exp(cumcol[end:end + 1, :])
            h_scr[h] = hs * cdec + new
        if direction == 0:
            y_pair = y_pair + dskip_ref[:, pr * LANES:(pr + 1) * LANES] * xp.astype(F32)
        y_ref[:, pr * LANES:(pr + 1) * LANES] = y_pair.astype(y_ref.dtype)

    @pl.when(ci == nc - 1)
    def _final():
        for pr in range(SSD_HEADS // 2):
            st = h_scr[2 * pr] + h_scr[2 * pr + 1]
            st = jnp.concatenate([st, jnp.zeros_like(st)], axis=0).T
            hout_ref[2 * pr] = st[0:SSD_HEAD_DIM, 0:SSD_STATE]
            hout_ref[2 * pr + 1] = st[SSD_HEAD_DIM:2 * SSD_HEAD_DIM, 0:SSD_STATE]


def _ssd(xbc, dt, par, dskip, h0, nb, seq_len, direction):
    t = xbc.shape[0]
    nc = seq_len // CHUNK
    cidx = (lambda b, c: b * nc + c) if direction == 0 else (lambda b, c: b * nc + nc - 1 - c)
    xcols = SSD_INNER // LANES
    has_h0 = h0 is not None
    if not has_h0:
        h0 = jnp.zeros((1, SSD_HEADS, SSD_HEAD_DIM, SSD_STATE), F32)
    h0_map = (lambda b, c: (b, 0, 0, 0)) if has_h0 else (lambda b, c: (0, 0, 0, 0))
    st_shape = (None, SSD_HEADS, SSD_HEAD_DIM, SSD_STATE)
    return pl.pallas_call(
        functools.partial(_ssd_kernel, direction=direction, nc=nc, has_h0=has_h0),
        grid=(nb, nc),
        in_specs=[pl.BlockSpec((CHUNK, SSD_INNER), lambda b, c: (cidx(b, c), 0)),
                  pl.BlockSpec((CHUNK, LANES), lambda b, c: (cidx(b, c), xcols)),
                  pl.BlockSpec((CHUNK, LANES), lambda b, c: (cidx(b, c), xcols + 1)),
                  pl.BlockSpec((CHUNK, LANES), lambda b, c: (cidx(b, c), 0)),
                  pl.BlockSpec((SUBLANES, LANES), lambda b, c: (0, 0)),
                  pl.BlockSpec((1, SSD_INNER), lambda b, c: (0, 0)),
                  pl.BlockSpec(st_shape, h0_map)],
        out_specs=[pl.BlockSpec((CHUNK, SSD_INNER), lambda b, c: (cidx(b, c), 0)),
                   pl.BlockSpec(st_shape, lambda b, c: (b, 0, 0, 0)),
                   pl.BlockSpec((None, SSD_HEADS, SSD_STATE, LANES), lambda b, c: (b, 0, 0, 0))],
        out_shape=[jax.ShapeDtypeStruct((t, SSD_INNER), BF16),
                   jax.ShapeDtypeStruct((nb, SSD_HEADS, SSD_HEAD_DIM, SSD_STATE), F32),
                   jax.ShapeDtypeStruct((nb, SSD_HEADS, SSD_STATE, LANES), F32)],
        compiler_params=_cparams("arbitrary", "arbitrary"),
        name="ssd_scan_fwd" if direction == 0 else "ssd_scan_bwd",
    )(xbc, xbc, xbc, dt, par, dskip, h0)[:2]


def _kv_variants(k, v):
    lane = _lane_iota(k.shape)
    lo = lane < A_HEAD_DIM
    k_sw = pltpu.roll(k, A_HEAD_DIM, 1)
    v_sw = pltpu.roll(v, A_HEAD_DIM, 1)
    ks, vs = {}, {}
    for j in range(A_KV_HEADS):
        for e in range(2):
            src_k, src_v = (k, v) if j == e else (k_sw, v_sw)
            half = lo if e == 0 else ~lo
            ks[j, e] = jnp.where(half, src_k, 0.0).astype(BF16)
            vs[j, e] = src_v.astype(BF16)
    return ks, vs


def _gqa_heads(q_ref, g_ref, o_ref, sink_ref, ks, vs, valid):
    m = q_ref.shape[0]
    assert A_HEADS // A_KV_HEADS == 4
    lo = _lane_iota((m, LANES)) < A_HEAD_DIM
    upper_rows = _row_iota((2 * m, 1)) >= m
    scale = jnp.asarray(A_SCALE, BF16)
    outs = {}
    for j in range(A_KV_HEADS):
        rows = jnp.concatenate([q_ref[:, 2 * j * LANES:(2 * j + 1) * LANES],
                                q_ref[:, (2 * j + 1) * LANES:(2 * j + 2) * LANES]], axis=0) * scale
        for e in range(2):
            s = _dot_nt(rows, ks[j, e])
            if valid is not None:
                nloc = valid.shape[1]
                s = jnp.concatenate([jnp.where(valid, s[:, :nloc], NEG_INF), s[:, nloc:]], axis=1)
            sink = jnp.where(upper_rows, sink_ref[4 * j + 2 + e], sink_ref[4 * j + e])
            mx = jnp.maximum(jnp.max(s, axis=1, keepdims=True), sink)
            p = jnp.exp(s - mx)
            den = jnp.sum(p, axis=1, keepdims=True) + jnp.exp(sink - mx)
            outs[j, e] = _dot(p.astype(BF16), vs[j, e]) / den
    for pr in range(A_HEADS // 2):
        j, half = pr // 2, pr % 2
        r = slice(half * m, (half + 1) * m)
        attn = jnp.where(lo, outs[j, 0][r], outs[j, 1][r])
        gate = g_ref[:, pr * LANES:(pr + 1) * LANES].astype(F32)
        o_ref[:, pr * LANES:(pr + 1) * LANES] = (attn * _silu(gate)).astype(o_ref.dtype)


def _attn_ctx_kernel(sink_ref, q_ref, k_ref, v_ref, g_ref, o_ref):
    ks, vs = _kv_variants(k_ref[...], v_ref[...])
    _gqa_heads(q_ref, g_ref, o_ref, sink_ref, ks, vs, None)


def _attn_ctx(sink, q, k, v, g, seq_len):
    t = q.shape[0]
    row = lambda b: (b, 0)
    return pl.pallas_call(
        _attn_ctx_kernel,
        grid=(t // seq_len,),
        in_specs=[pl.BlockSpec(memory_space=pltpu.SMEM),
                  pl.BlockSpec((seq_len, A_WIDTH), row),
                  pl.BlockSpec((seq_len, A_KV_WIDTH), row),
                  pl.BlockSpec((seq_len, A_KV_WIDTH), row),
                  pl.BlockSpec((seq_len, A_WIDTH), row)],
        out_specs=pl.BlockSpec((seq_len, A_WIDTH), row),
        out_shape=jax.ShapeDtypeStruct((t, A_WIDTH), BF16),
        compiler_params=_cparams("arbitrary"),
        name="swa_context_attention",
    )(sink, q, k, v, g)


BAND_Q = 256


def _attn_band_kernel(sink_ref, q_ref, kp_ref, kc_ref, kn_ref, vp_ref, vc_ref, vn_ref, kx_ref, vx_ref, g_ref,
                      o_ref, *, nsteps):
    n = pl.program_id(1)
    tq = q_ref.shape[0]
    blk = WINDOW_BLK
    k = jnp.concatenate([kp_ref[...], kc_ref[...], kn_ref[...], kx_ref[...]], axis=0)
    v = jnp.concatenate([vp_ref[...], vc_ref[...], vn_ref[...], vx_ref[...]], axis=0)
    nloc = tq + 2 * blk
    qi = _row_iota((2 * tq, nloc)) & (tq - 1)
    col = _lane_iota((2 * tq, nloc))
    rel = col - blk - qi
    valid = ((rel >= -blk) & (rel <= blk) & ((col >= blk) | (n > 0)) & ((col < nloc - blk) | (n < nsteps - 1)))
    ks, vs = _kv_variants(k, v)
    _gqa_heads(q_ref, g_ref, o_ref, sink_ref, ks, vs, valid)


def _attn_band(sink, q, k, v, k_ctx, v_ctx, g, nb, seq_len):
    t = q.shape[0]
    blk = WINDOW_BLK
    tq = BAND_Q
    assert tq & (tq - 1) == 0 and tq % blk == 0 and seq_len % tq == 0
    per = tq // blk
    nblk = seq_len // blk
    nsteps = seq_len // tq
    nctx = k_ctx.shape[1]
    cur = lambda b, n: (b * nsteps + n, 0)
    prv = lambda b, n: (b * nblk + jnp.maximum(n * per - 1, 0), 0)
    nxt = lambda b, n: (b * nblk + jnp.minimum((n + 1) * per, nblk - 1), 0)
    ctx = lambda b, n: (b, 0, 0)
    edge = lambda f: pl.BlockSpec((blk, A_KV_WIDTH), f)
    mid = pl.BlockSpec((tq, A_KV_WIDTH), cur)
    return pl.pallas_call(
        functools.partial(_attn_band_kernel, nsteps=nsteps),
        grid=(nb, nsteps),
        in_specs=[pl.BlockSpec(memory_space=pltpu.SMEM),
                  pl.BlockSpec((tq, A_WIDTH), cur),
                  edge(prv), mid, edge(nxt),
                  edge(prv), mid, edge(nxt),
                  pl.BlockSpec((None, nctx, A_KV_WIDTH), ctx),
                  pl.BlockSpec((None, nctx, A_KV_WIDTH), ctx),
                  pl.BlockSpec((tq, A_WIDTH), cur)],
        out_specs=pl.BlockSpec((tq, A_WIDTH), cur),
        out_shape=jax.ShapeDtypeStruct((t, A_WIDTH), BF16),
        compiler_params=_cparams("arbitrary", "arbitrary"),
        name="swa_banded_attention",
    )(sink, q, k, k, k, v, v, v, k_ctx, v_ctx, g)


def _ab_out_kernel(a_ref, yf_ref, yb_ref, z_ref, gnw_ref, w_ref, x_ref, gate_ref, o_ref):
    y = (yf_ref[...].astype(F32) + yb_ref[...].astype(F32)) * _silu(z_ref[...].astype(F32))
    ms = jnp.mean(y * y, axis=-1, keepdims=True)
    s = ((y * lax.rsqrt(ms + EPS)) * gnw_ref[...]).astype(BF16)
    out = _dot(a_ref[...], w_ref[0:A_WIDTH, :]) + _dot(s, w_ref[A_WIDTH:, :])
    o_ref[...] = x_ref[...] + gate_ref[...] * out


def _ab_out(a, yf, yb, z, gnw, w, x, gate, seq_len, tm):
    t, d = x.shape
    per_seq = seq_len // tm if gate.shape[0] > 1 else None
    row = lambda i: (i, 0)
    const = lambda i: (0, 0)
    mod = (lambda i: (i // per_seq, 0, 0)) if per_seq else (lambda i: (0, 0, 0))
    return pl.pallas_call(
        _ab_out_kernel,
        grid=(t // tm,),
        in_specs=[pl.BlockSpec((tm, A_WIDTH), row),
                  pl.BlockSpec((tm, SSD_INNER), row),
                  pl.BlockSpec((tm, SSD_INNER), row),
                  pl.BlockSpec((tm, SSD_INNER), row),
                  pl.BlockSpec((1, SSD_INNER), const),
                  pl.BlockSpec(w.shape, const),
                  pl.BlockSpec((tm, d), row),
                  pl.BlockSpec((None, 1, d), mod)],
        out_specs=pl.BlockSpec((tm, d), row),
        out_shape=jax.ShapeDtypeStruct((t, d), F32),
        compiler_params=_cparams("arbitrary"),
        name="ab_out_proj",
    )(a, yf, yb, z, gnw, w, x, gate)


MLA_COLS = (Q_LORA, KV_LORA, LANES, MLA_WIDTH)


def _mla_in_kernel(*refs, rope):
    if rope:
        (x_ref, nw_ref, sc_ref, sh_ref, w_ref, qnw_ref, kvnw_ref, wuq_ref, cos_ref, sin_ref,
         qn_ref, qpe_ref, ckv_ref, kpe_ref, g_ref, ckv32_ref, kpe32_ref) = refs
    else:
        (x_ref, nw_ref, sc_ref, sh_ref, w_ref, qnw_ref, kvnw_ref, wuq_ref,
         qn_ref, qpe_ref, ckv_ref, kpe_ref, g_ref, ckv32_ref, kpe32_ref) = refs
    h = _norm_mod(x_ref[...], nw_ref[...], sc_ref[...], sh_ref[...]).astype(BF16)
    offs = [int(o) for o in np.concatenate([[0], np.cumsum(MLA_COLS)])]
    cq = _dot(h, w_ref[:, offs[0]:offs[1]])
    ckv = _dot(h, w_ref[:, offs[1]:offs[2]])
    kpe = _dot(h, w_ref[:, offs[2]:offs[3]])
    g_ref[...] = _dot(h, w_ref[:, offs[3]:offs[4]]).astype(g_ref.dtype)

    def rms(u, w):
        return (u * lax.rsqrt(jnp.mean(u * u, axis=-1, keepdims=True) + EPS)) * w

    cqn = rms(cq, qnw_ref[...]).astype(BF16)
    nope_w = MLA_HEADS * MLA_NOPE
    qn_ref[...] = _dot(cqn, wuq_ref[:, 0:nope_w]).astype(qn_ref.dtype)
    qpe = _dot(cqn, wuq_ref[:, nope_w:])
    if rope:
        qpe = _rope(qpe, cos_ref[...], sin_ref[...], MLA_ROPE // 4)
        kpe = _rope(kpe, cos_ref[...], sin_ref[...], MLA_ROPE // 4)
    qpe_ref[...] = qpe.astype(qpe_ref.dtype)
    ckvn = rms(ckv, kvnw_ref[...])
    ckv_ref[...] = ckvn.astype(ckv_ref.dtype)
    kpe_ref[...] = kpe.astype(kpe_ref.dtype)
    ckv32_ref[...] = ckvn
    kpe32_ref[...] = kpe


def _mla_in(x, nw, scale, shift, w, qnw, kvnw, wuq, rope_tabs, seq_len, tm):
    t, d = x.shape
    per_seq = seq_len // tm if scale.shape[0] > 1 else None
    row = lambda i: (i, 0)
    const = lambda i: (0, 0)
    mod = (lambda i: (i // per_seq, 0, 0)) if per_seq else (lambda i: (0, 0, 0))
    in_specs = [pl.BlockSpec((tm, d), row), pl.BlockSpec((1, d), const),
                pl.BlockSpec((None, 1, d), mod), pl.BlockSpec((None, 1, d), mod),
                pl.BlockSpec(w.shape, const), pl.BlockSpec((1, Q_LORA), const),
                pl.BlockSpec((1, KV_LORA), const), pl.BlockSpec(wuq.shape, const)]
    args = [x, nw, scale, shift, w, qnw, kvnw, wuq]
    if rope_tabs is not None:
        nt = seq_len // tm
        pos = lambda i: (i % nt, 0)
        in_specs += [pl.BlockSpec((tm, LANES), pos), pl.BlockSpec((tm, LANES), pos)]
        args += list(rope_tabs)
    widths = (MLA_HEADS * MLA_NOPE, MLA_HEADS * MLA_ROPE, KV_LORA, LANES, MLA_WIDTH, KV_LORA, LANES)
    dts = (BF16, BF16, BF16, BF16, BF16, F32, F32)
    return pl.pallas_call(
        functools.partial(_mla_in_kernel, rope=rope_tabs is not None),
        grid=(t // tm,),
        in_specs=in_specs,
        out_specs=[pl.BlockSpec((tm, c), row) for c in widths],
        out_shape=[jax.ShapeDtypeStruct((t, c), dt) for c, dt in zip(widths, dts)],
        compiler_params=_cparams("arbitrary"),
        name="mla_in_proj",
    )(*args)


MLA_UNIT_ROWS = 256
MLA_KEY_TILE = 256


def _mla_attn_kernel(qn_ref, qpe_ref, ckv_ref, kpe_ref, wukt_ref, wuv_ref, g_ref, o_ref, kcat_scr, v_scr, s_scr, *,
                     pairs_per_step):
    pg = pl.program_id(1)
    qb = pl.program_id(2)
    tq = qn_ref.shape[0]
    c_exp = MLA_SCALE * LOG2E

    @pl.when(qb == 0)
    def _expand():
        ckv = ckv_ref[...]
        eye = jnp.where(_row_iota((LANES, LANES)) == _lane_iota((LANES, LANES)), 1.0, 0.0).astype(BF16)
        kpe_t = _dot_nt(eye, kpe_ref[...]).astype(BF16)
        for i in range(pairs_per_step):
            kcat_scr[i, 0:LANES, :] = _dot_nt(wukt_ref[i * LANES:(i + 1) * LANES, :], ckv).astype(BF16)
            kcat_scr[i, LANES:2 * LANES, :] = kpe_t
            v_scr[i] = _dot(ckv, wuv_ref[:, i * LANES:(i + 1) * LANES]).astype(BF16)

    nkeys = kcat_scr.shape[-1]
    kt_w = min(MLA_KEY_TILE, nkeys)
    nkt = nkeys // kt_w
    ru = min(MLA_UNIT_ROWS, tq)
    lane = _lane_iota((ru, LANES))
    units = [(i, r, e) for i in range(pairs_per_step) for r in range(tq // ru) for e in range(2)]

    def lane_tiles(x):
        return [x[:, j * LANES:(j + 1) * LANES] for j in range(x.shape[1] // LANES)]

    def q_ext(u):
        i, r, e = units[u]
        rows = slice(r * ru, (r + 1) * ru)
        qn = qn_ref[rows, i * LANES:(i + 1) * LANES]
        qpe = qpe_ref[rows, (i // 2) * LANES:(i // 2 + 1) * LANES]
        slot = 2 * (i % 2) + e if pairs_per_step % 2 == 0 else 2 * ((pg * pairs_per_step + i) % 2) + e
        nope_half = (lane < MLA_NOPE) if e == 0 else (lane >= MLA_NOPE)
        return jnp.concatenate([jnp.where(nope_half, qn, jnp.zeros_like(qn)),
                                jnp.where((lane >> 5) == slot, qpe, jnp.zeros_like(qpe))], axis=1)

    def score_tile(u, st, kt):
        cols = slice(kt * kt_w, (kt + 1) * kt_w)
        s = _dot(st["q"], kcat_scr[units[u][0], :, cols])
        s_scr[u % 2, :, cols] = s
        for t in lane_tiles(s):
            st["mrun"] = t if st["mrun"] is None else jnp.maximum(st["mrun"], t)

    def value_tile(u, st, kt):
        cols = slice(kt * kt_w, (kt + 1) * kt_w)
        p = jnp.exp2((s_scr[u % 2, :, cols] - st["m"]) * c_exp)
        for t in lane_tiles(p):
            st["lrun"] = t if st["lrun"] is None else st["lrun"] + t
        pv = _dot(p.astype(BF16), v_scr[units[u][0], cols, :])
        st["acc"] = pv if st["acc"] is None else st["acc"] + pv

    def finish(u, st, done):
        i, r, e = units[u]
        den = jnp.sum(st["lrun"], axis=1, keepdims=True)
        done[e] = st["acc"] / den
        if e == 1:
            rows = slice(r * ru, (r + 1) * ru)
            attn = jnp.where(lane < MLA_V, done[0], done[1])
            gate = g_ref[rows, i * LANES:(i + 1) * LANES].astype(F32)
            o_ref[rows, i * LANES:(i + 1) * LANES] = (attn * _silu(gate)).astype(o_ref.dtype)

    done = {}
    cur = None
    for u in range(len(units) + 1):
        nxt = dict(q=q_ext(u), mrun=None) if u < len(units) else None
        for kt in range(nkt):
            if nxt is not None:
                score_tile(u, nxt, kt)
            if cur is not None:
                value_tile(u - 1, cur, kt)
        if cur is not None:
            finish(u - 1, cur, done)
        if nxt is not None:
            nxt.update(m=jnp.max(nxt["mrun"], axis=1, keepdims=True), lrun=None, acc=None)
        cur = nxt


def _mla_attn(qn, qpe, ckv_keys, kpe_keys, wukt, wuv, g, nb, seq_len, tq, pairs_per_step):
    t = qn.shape[0]
    nkeys = ckv_keys.shape[1]
    npairs = MLA_HEADS // 2
    ngrp = npairs // pairs_per_step
    nq = seq_len // tq
    wp = pairs_per_step * LANES
    if pairs_per_step % 2 == 0:
        wpe = wp // 2
        pe_map = lambda b, p, i: (b * nq + i, p)
    else:
        wpe = LANES
        pe_map = lambda b, p, i: (b * nq + i, p // 2)
    qmap = lambda b, p, i: (b * nq + i, p)
    return pl.pallas_call(
        functools.partial(_mla_attn_kernel, pairs_per_step=pairs_per_step),
        grid=(nb, ngrp, nq),
        in_specs=[pl.BlockSpec((tq, wp), qmap),
                  pl.BlockSpec((tq, wpe), pe_map),
                  pl.BlockSpec((None, nkeys, KV_LORA), lambda b, p, i: (b, 0, 0)),
                  pl.BlockSpec((None, nkeys, LANES), lambda b, p, i: (b, 0, 0)),
                  pl.BlockSpec((wp, KV_LORA), lambda b, p, i: (p, 0)),
                  pl.BlockSpec((KV_LORA, wp), lambda b, p, i: (0, p)),
                  pl.BlockSpec((tq, wp), qmap)],
        out_specs=pl.BlockSpec((tq, wp), qmap),
        out_shape=jax.ShapeDtypeStruct((t, MLA_WIDTH), BF16),
        scratch_shapes=[pltpu.VMEM((pairs_per_step, 2 * LANES, nkeys), BF16),
                        pltpu.VMEM((pairs_per_step, nkeys, LANES), BF16),
                        pltpu.VMEM((2, min(MLA_UNIT_ROWS, tq), nkeys), F32)],
        compiler_params=_cparams("arbitrary", "arbitrary", "arbitrary"),
        name="mla_attention",
    )(qn, qpe, ckv_keys, kpe_keys, wukt, wuv, g)


def _mla_attn_seq_kernel(qn_ref, qpe_ref, ckv_ref, kpe_ref, wukt_ref, wuv_ref, g_ref, o_ref,
                         kcat_scr, v_scr, s_scr, m_scr, half_scr):
    pair = pl.program_id(1)
    seq = qn_ref.shape[0]
    nkeys = kcat_scr.shape[-1]
    ru, kt_w = MLA_UNIT_ROWS, MLA_KEY_TILE
    nkt = nkeys // kt_w
    nrb = seq // ru
    c_exp = MLA_SCALE * LOG2E
    lane = _lane_iota((ru, LANES))

    ckv = ckv_ref[...]
    eye = jnp.where(_row_iota((LANES, LANES)) == _lane_iota((LANES, LANES)), 1.0, 0.0).astype(BF16)
    kcat_scr[0:LANES, :] = _dot_nt(wukt_ref[...], ckv).astype(BF16)
    kcat_scr[LANES:2 * LANES, :] = _dot_nt(eye, kpe_ref[...]).astype(BF16)
    v_scr[:, 0:LANES] = _dot(ckv, wuv_ref[...]).astype(BF16)
    v_scr[:, LANES:2 * LANES] = jnp.ones((nkeys, LANES), BF16)

    def rows_of(rb):
        return slice(rb * ru, (rb + 1) * ru) if isinstance(rb, int) else pl.ds(pl.multiple_of(rb * ru, ru), ru)

    def q_ext(rb, e):
        rows = rows_of(rb)
        qn = qn_ref[rows, :]
        qpe = qpe_ref[rows, :]
        nope_half = (lane < MLA_NOPE) if e == 0 else (lane >= MLA_NOPE)
        slot = 2 * (pair % 2) + e
        return jnp.concatenate([jnp.where(nope_half, qn, jnp.zeros_like(qn)),
                                jnp.where((lane >> 5) == slot, qpe, jnp.zeros_like(qpe))], axis=1)

    def phase(score, value):
        q = q_ext(*score) if score is not None else None
        mrun = acc = None
        for kt in range(nkt):
            cols = slice(kt * kt_w, (kt + 1) * kt_w)
            if score is not None:
                s = _dot(q, kcat_scr[:, cols])
                s_scr[score[1], :, cols] = s
                for j in range(kt_w // LANES):
                    t = s[:, j * LANES:(j + 1) * LANES]
                    mrun = t if mrun is None else jnp.maximum(mrun, t)
            if value is not None:
                mb = m_scr[value]
                p = jnp.concatenate(
                    [jnp.exp2((s_scr[value, :, kt * kt_w + j * LANES:kt * kt_w + (j + 1) * LANES] - mb) * c_exp)
                     for j in range(kt_w // LANES)], axis=1)
                pv = _dot(p.astype(BF16), v_scr[cols, :])
                acc = pv if acc is None else acc + pv
        if score is not None:
            m_scr[score[1]] = jnp.broadcast_to(jnp.max(mrun, axis=1, keepdims=True), (ru, LANES))
        return None if acc is None else acc[:, 0:LANES] / acc[:, LANES:2 * LANES]

    phase((0, 0), None)

    def row_block(rb, carry):
        half_scr[...] = phase((rb, 1), 0)
        out1 = phase((jnp.minimum(rb + 1, nrb - 1), 0), 1)
        rows = rows_of(rb)
        attn = jnp.where(lane < MLA_V, half_scr[...], out1)
        o_ref[rows, :] = (attn * _silu(g_ref[rows, :].astype(F32))).astype(o_ref.dtype)
        return carry

    lax.fori_loop(0, nrb, row_block, 0)


def _mla_attn_seq(qn, qpe, ckv_keys, kpe_keys, wukt, wuv, g, nb, seq_len):
    t = qn.shape[0]
    nkeys = ckv_keys.shape[1]
    assert seq_len % MLA_UNIT_ROWS == 0 and nkeys % MLA_KEY_TILE == 0
    qmap = lambda b, p: (b, p)
    return pl.pallas_call(
        _mla_attn_seq_kernel,
        grid=(nb, MLA_HEADS // 2),
        in_specs=[pl.BlockSpec((seq_len, LANES), qmap),
                  pl.BlockSpec((seq_len, LANES), lambda b, p: (b, p // 2)),
                  pl.BlockSpec((None, nkeys, KV_LORA), lambda b, p: (b, 0, 0)),
                  pl.BlockSpec((None, nkeys, LANES), lambda b, p: (b, 0, 0)),
                  pl.BlockSpec((LANES, KV_LORA), lambda b, p: (p, 0)),
                  pl.BlockSpec((KV_LORA, LANES), lambda b, p: (0, p)),
                  pl.BlockSpec((seq_len, LANES), qmap)],
        out_specs=pl.BlockSpec((seq_len, LANES), qmap),
        out_shape=jax.ShapeDtypeStruct((t, MLA_WIDTH), BF16),
        scratch_shapes=[pltpu.VMEM((2 * LANES, nkeys), BF16),
                        pltpu.VMEM((nkeys, 2 * LANES), BF16),
                        pltpu.VMEM((2, MLA_UNIT_ROWS, nkeys), F32),
                        pltpu.VMEM((2, MLA_UNIT_ROWS, LANES), F32),
                        pltpu.VMEM((MLA_UNIT_ROWS, LANES), F32)],
        compiler_params=_cparams("arbitrary", "arbitrary"),
        name="mla_attention_seq",
    )(qn, qpe, ckv_keys, kpe_keys, wukt, wuv, g)


def _mla_out_kernel(a_ref, w_ref, x_ref, gate_ref, fw_ref, o_ref):
    xn = x_ref[...] + gate_ref[...] * _dot(a_ref[...], w_ref[...])
    ms = jnp.mean(xn * xn, axis=-1, keepdims=True)
    o_ref[...] = (xn * lax.rsqrt(ms + EPS)) * fw_ref[...]


def _mla_out(a, w, x, gate, fw, seq_len, tm):
    t, d = x.shape
    per_seq = seq_len // tm if gate.shape[0] > 1 else None
    row = lambda i: (i, 0)
    const = lambda i: (0, 0)
    mod = (lambda i: (i // per_seq, 0, 0)) if per_seq else (lambda i: (0, 0, 0))
    return pl.pallas_call(
        _mla_out_kernel,
        grid=(t // tm,),
        in_specs=[pl.BlockSpec((tm, MLA_WIDTH), row), pl.BlockSpec(w.shape, const),
                  pl.BlockSpec((tm, d), row), pl.BlockSpec((None, 1, d), mod), pl.BlockSpec((1, d), const)],
        out_specs=pl.BlockSpec((tm, d), row),
        out_shape=jax.ShapeDtypeStruct((t, d), F32),
        compiler_params=_cparams("arbitrary"),
        name="mla_out_proj_final_norm",
    )(a, w, x, gate, fw)


def _rope_tables(length, dim):
    rows = length // GRID_W
    row = jnp.repeat(jnp.arange(rows), GRID_W).astype(F32)
    col = jnp.tile(jnp.arange(GRID_W), rows).astype(F32)
    nf = dim // 4
    inv = 1.0 / (ROPE_BASE ** (jnp.arange(nf, dtype=F32) / nf))
    ar = row[:, None] * inv[None, :]
    ac = col[:, None] * inv[None, :]
    ang = jnp.concatenate([ar, ar, ac, ac], axis=-1)
    sign = jnp.tile(jnp.concatenate([-jnp.ones((nf,), F32), jnp.ones((nf,), F32)]), 2)
    reps = LANES // dim
    return jnp.tile(jnp.cos(ang), (1, reps)), jnp.tile(jnp.sin(ang) * sign, (1, reps))


def _group(x, seq_len, conds_rows, mods, wts, tabs_a, tabs_c, ctx_cache):
    nb, _, d = x.shape
    t = nb * seq_len
    latent = ctx_cache is not None
    xf = x.reshape(t, d)
    tm = 512

    def mod_rows(layer):
        m = mods[layer][conds_rows]
        sh, sc, gt = jnp.split(m[:, None, :], 3, axis=-1)
        return sh, sc, gt

    sh, sc, gt = mod_rows(0)
    q, k, v, g, z, xbc, dt = _ab_in(xf, wts["norm_w"][0:1], sc, sh, wts["ab_w_in"], tabs_a if latent else None,
                                    seq_len, tm)
    xbc_c = _conv(xbc, wts["ab_conv_w"], wts["ab_conv_b"], seq_len, min(tm, seq_len))
    if latent:
        k_ctx, v_ctx, s_f0, s_b0, _, _ = ctx_cache
        attn = _attn_band(wts["ab_sink"], q, k, v, k_ctx, v_ctx, g, nb, seq_len)
    else:
        s_f0 = s_b0 = None
        attn = _attn_ctx(wts["ab_sink"], q, k, v, g, seq_len)
    yf, s_f = _ssd(xbc_c, dt, wts["ssd_par"], wts["ssd_dskip"], s_f0, nb, seq_len, 0)
    yb, s_b = _ssd(xbc_c, dt, wts["ssd_par"], wts["ssd_dskip"], s_b0, nb, seq_len, 1)
    x1 = _ab_out(attn, yf, yb, z, wts["ab_gnorm_w"], wts["ab_w_out"], xf, gt, seq_len, tm)

    sh, sc, gt = mod_rows(1)
    qn, qpe, ckv, kpe, g1, ckv32, kpe32 = _mla_in(
        x1, wts["norm_w"][1:2], sc, sh, wts["mla_w_in"], wts["mla_q_norm_w"], wts["mla_kv_norm_w"],
        wts["mla_w_uq"], tabs_c if latent else None, seq_len, tm)
    ckv_keys = ckv.reshape(nb, seq_len, KV_LORA)
    kpe_keys = kpe.reshape(nb, seq_len, LANES)
    if latent:
        ckv_x, kpe_x = ctx_cache[4], ctx_cache[5]
        ckv_keys = jnp.concatenate([ckv_keys, ckv_x], axis=1)
        kpe_keys = jnp.concatenate([kpe_keys, kpe_x], axis=1)
        attn1 = _mla_attn_seq(qn, qpe, ckv_keys, kpe_keys, wts["mla_w_ukt"], wts["mla_w_uv"], g1, nb, seq_len)
    else:
        attn1 = _mla_attn(qn, qpe, ckv_keys, kpe_keys, wts["mla_w_ukt"], wts["mla_w_uv"], g1, nb, seq_len,
                          seq_len, MLA_HEADS // 2)
    y = _mla_out(attn1, wts["mla_w_out"], x1, gt, wts["final_norm_w"], seq_len, tm)
    return y.reshape(nb, seq_len, d), (k, v, s_f, s_b, ckv32, kpe32)


def kernel(x_prompt, x_sample, cache_a_k, cache_a_v, state_ssd_fwd, state_ssd_bwd, cache_mla_ckv, cache_mla_kpe,
           c, c_ctx, ada_w, ada_b, norm_w, ab_w_in, ab_sink, ab_conv_w, ab_conv_b, ab_dt_bias, ab_a_log,
           ab_d_skip, ab_gnorm_w, ab_w_out, mla_w_in, mla_q_norm_w, mla_kv_norm_w, mla_w_uq, mla_w_ukv,
           mla_w_out, final_norm_w):
    batch, seq, d = x_prompt.shape
    dec_batch, dec_seq, _ = x_sample.shape
    assert ada_w.shape[0] == 2 and ab_w_in.shape[0] == 1 and mla_w_in.shape[0] == 1
    assert dec_batch + 1 <= SUBLANES

    conds = jnp.concatenate([c_ctx[None, :], c, jnp.zeros((SUBLANES - 1 - dec_batch, d), F32)], axis=0)
    mods = _modulation(conds, ada_w, ada_b)

    w_ab = ab_w_in[0]
    w_ab = jnp.concatenate([w_ab, jnp.zeros((d, LANES - 2 * SSD_HEADS), F32)], axis=1).astype(BF16)
    mw = mla_w_in[0]
    o_kpe = Q_LORA + KV_LORA
    w_mla = jnp.concatenate([mw[:, :o_kpe], jnp.tile(mw[:, o_kpe:o_kpe + MLA_ROPE], (1, LANES // MLA_ROPE)),
                             mw[:, o_kpe + MLA_ROPE:]], axis=1).astype(BF16)
    wuq = mla_w_uq[0].reshape(Q_LORA, MLA_HEADS, MLA_NOPE + MLA_ROPE)
    wuq = jnp.concatenate([wuq[:, :, :MLA_NOPE].reshape(Q_LORA, -1), wuq[:, :, MLA_NOPE:].reshape(Q_LORA, -1)],
                          axis=1).astype(BF16)
    wukv = mla_w_ukv[0].reshape(KV_LORA, MLA_HEADS, MLA_NOPE + MLA_V)
    wukt = wukv[:, :, :MLA_NOPE].reshape(KV_LORA, -1).T.astype(BF16)
    wuv = wukv[:, :, MLA_NOPE:].reshape(KV_LORA, -1).astype(BF16)
    pad_lanes = lambda r: jnp.concatenate([r.reshape(1, -1), jnp.zeros((1, LANES - r.size), F32)], axis=1)
    ssd_par = jnp.concatenate([pad_lanes(ab_dt_bias[0]), pad_lanes(ab_a_log[0]),
                               jnp.zeros((SUBLANES - 2, LANES), F32)], axis=0)
    wts = dict(
        norm_w=norm_w, ab_w_in=w_ab, ab_sink=ab_sink[0],
        ab_conv_w=jnp.concatenate([ab_conv_w[0], jnp.zeros((SUBLANES - CONV_K, CONV_CH), F32)], axis=0),
        ab_conv_b=ab_conv_b[0][None, :], ssd_par=ssd_par,
        ssd_dskip=jnp.repeat(ab_d_skip[0], SSD_HEAD_DIM)[None, :],
        ab_gnorm_w=ab_gnorm_w[0][None, :], ab_w_out=ab_w_out[0].astype(BF16),
        mla_w_in=w_mla, mla_q_norm_w=mla_q_norm_w[0][None, :], mla_kv_norm_w=mla_kv_norm_w[0][None, :],
        mla_w_uq=wuq, mla_w_ukt=wukt, mla_w_uv=wuv, mla_w_out=mla_w_out[0].astype(BF16),
        final_norm_w=final_norm_w[None, :],
    )
    tabs_a = _rope_tables(dec_seq, A_HEAD_DIM)
    tabs_c = _rope_tables(dec_seq, MLA_ROPE)

    y_prompt, (k, v, s_f, s_b, ckv32, kpe32) = _group(
        x_prompt, seq, jnp.zeros((1,), jnp.int32), mods, wts, tabs_a, tabs_c, None)

    past = cache_a_k.shape[2]
    ctx_cache = (cache_a_k[:, 0].reshape(dec_batch, past, A_KV_WIDTH),
                 cache_a_v[:, 0].reshape(dec_batch, past, A_KV_WIDTH),
                 state_ssd_fwd[:, 0], state_ssd_bwd[:, 0],
                 cache_mla_ckv[:, 0].astype(BF16),
                 jnp.tile(cache_mla_kpe[:, 0], (1, 1, LANES // MLA_ROPE)).astype(BF16))
    y_sample, _ = _group(x_sample, dec_seq, 1 + jnp.arange(dec_batch), mods, wts, tabs_a, tabs_c, ctx_cache)

    return (y_prompt, y_sample,
            k.reshape(batch, 1, seq, A_KV_HEADS, A_HEAD_DIM), v.reshape(batch, 1, seq, A_KV_HEADS, A_HEAD_DIM),
            s_f[:, None], s_b[:, None],
            ckv32.reshape(batch, 1, seq, KV_LORA), kpe32[:, :MLA_ROPE].reshape(batch, 1, seq, MLA_ROPE))
```

```python
import functools
import math

import jax
import jax.numpy as jnp
import numpy as np
from jax import lax
from jax.experimental import pallas as pl
from jax.experimental.pallas import tpu as pltpu

F32 = jnp.float32
BF16 = jnp.bfloat16

LANES = 128
SUBLANES = 8
VMEM_LIMIT_BYTES = 56 * 1024 * 1024

GRID_W = 64
ROPE_BASE = 10000.0
EPS = 1e-6
NEG_INF = -1e30
WINDOW_BLK = 128
CHUNK = 128
A_HEADS, A_KV_HEADS, A_HEAD_DIM = 8, 2, 64
A_WIDTH = A_HEADS * A_HEAD_DIM
A_KV_WIDTH = A_KV_HEADS * A_HEAD_DIM
A_SCALE = A_HEAD_DIM ** -0.5
SSD_HEADS, SSD_HEAD_DIM, SSD_GROUPS, SSD_STATE = 16, 64, 2, 64
SSD_INNER = SSD_HEADS * SSD_HEAD_DIM
CONV_K = 5
BC_WIDTH = SSD_GROUPS * SSD_STATE
CONV_CH = SSD_INNER + 2 * BC_WIDTH
MLA_HEADS, MLA_NOPE, MLA_ROPE, MLA_V = 16, 64, 32, 64
Q_LORA, KV_LORA = 256, 128
MLA_WIDTH = MLA_HEADS * MLA_V
MLA_SCALE = (MLA_NOPE + MLA_ROPE) ** -0.5
LOG2E = 1.4426950408889634


def _cparams(*sem):
    return pltpu.CompilerParams(dimension_semantics=sem, vmem_limit_bytes=VMEM_LIMIT_BYTES)


def _silu(x):
    return x * (1.0 / (1.0 + jnp.exp(-x)))


def _dot(a, b):
    return jnp.dot(a, b, preferred_element_type=F32, precision=lax.Precision.DEFAULT)


def _dot_nt(a, b):
    return lax.dot_general(a, b, (((1,), (1,)), ((), ())), preferred_element_type=F32,
                           precision=lax.Precision.DEFAULT)


def _row_splits(m, parts=2):
    step = m // parts
    return [slice(r * step, (r + 1) * step) for r in range(parts)]


def _lane_iota(shape):
    return lax.broadcasted_iota(jnp.int32, shape, len(shape) - 1)


def _row_iota(shape):
    return lax.broadcasted_iota(jnp.int32, shape, len(shape) - 2)


def _mod_kernel(cond_ref, w_ref, b_ref, o_ref):
    s = _silu(cond_ref[...])
    o_ref[...] = _dot(s.astype(BF16), w_ref[...].astype(BF16)) + b_ref[...]


def _modulation(conds, ada_w, ada_b):
    depth, d, d3 = ada_w.shape
    tn = 768
    return pl.pallas_call(
        _mod_kernel,
        grid=(depth, d3 // tn),
        in_specs=[pl.BlockSpec((SUBLANES, d), lambda l, j: (0, 0)),
                  pl.BlockSpec((None, d, tn), lambda l, j: (l, 0, j)),
                  pl.BlockSpec((None, 1, tn), lambda l, j: (l, 0, j))],
        out_specs=pl.BlockSpec((None, SUBLANES, tn), lambda l, j: (l, 0, j)),
        out_shape=jax.ShapeDtypeStruct((depth, SUBLANES, d3), F32),
        compiler_params=_cparams("arbitrary", "arbitrary"),
        name="modulation",
    )(conds, ada_w, ada_b.reshape(depth, 1, d3))


def _norm_mod(x, nw, scale, shift):
    ms = jnp.mean(x * x, axis=-1, keepdims=True)
    y = (x * lax.rsqrt(ms + EPS)) * nw
    return y * (1.0 + scale) + shift


def _rope(x, cos, sin_signed, half_period):
    outs = []
    first = (_lane_iota((x.shape[0], LANES)) & (2 * half_period - 1)) < half_period
    for j in range(x.shape[1] // LANES):
        xj = x[:, j * LANES:(j + 1) * LANES]
        up = pltpu.roll(xj, LANES - half_period, 1)
        dn = pltpu.roll(xj, half_period, 1)
        outs.append(xj * cos + jnp.where(first, up, dn) * sin_signed)
    return outs[0] if len(outs) == 1 else jnp.concatenate(outs, axis=1)


AB_COLS = (A_WIDTH, A_KV_WIDTH, A_KV_WIDTH, A_WIDTH, SSD_INNER, CONV_CH, LANES)


def _ab_in_kernel(*refs, rope, seq_len):
    if rope:
        (x_ref, nw_ref, sc_ref, sh_ref, w_ref, cos_ref, sin_ref,
         q_ref, k_ref, v_ref, g_ref, z_ref, xbc_ref, dt_ref) = refs
        kvt_refs = {}
    else:
        (x_ref, nw_ref, sc_ref, sh_ref, w_ref,
         q_ref, k_ref, v_ref, g_ref, z_ref, xbc_ref, dt_ref, kt_ref, vt_ref) = refs
        kvt_refs = {1: kt_ref, 2: vt_ref}
    offs = np.concatenate([[0], np.cumsum(AB_COLS)])
    outs = (q_ref, k_ref, v_ref, g_ref, z_ref, xbc_ref, dt_ref)
    for rows in _row_splits(x_ref.shape[0]):
        h = _norm_mod(x_ref[rows, :], nw_ref[...], sc_ref[...], sh_ref[...]).astype(BF16)
        for i, o_ref in enumerate(outs):
            y = _dot(h, w_ref[:, int(offs[i]):int(offs[i + 1])])
            if rope and i in (0, 1):
                y = _rope(y, cos_ref[rows, :], sin_ref[rows, :], A_HEAD_DIM // 4)
            o_ref[rows, :] = y.astype(o_ref.dtype)
            if i in kvt_refs:
                for s in range(rows.start // seq_len, rows.stop // seq_len):
                    kvt_refs[i][s] = y[s * seq_len - rows.start:(s + 1) * seq_len - rows.start, :].T


def _ab_in(x, nw, scale, shift, w, rope_tabs, seq_len, tm):
    t, d = x.shape
    per_seq = seq_len // tm if scale.shape[0] > 1 else None
    row = lambda i: (i, 0)
    mod = (lambda i: (i // per_seq, 0, 0)) if per_seq else (lambda i: (0, 0, 0))
    const = lambda i: (0, 0)
    in_specs = [pl.BlockSpec((tm, d), row),
                pl.BlockSpec((1, d), const),
                pl.BlockSpec((None, 1, d), mod),
                pl.BlockSpec((None, 1, d), mod),
                pl.BlockSpec(w.shape, const)]
    args = [x, nw, scale, shift, w]
    if rope_tabs is not None:
        nt = seq_len // tm
        pos = lambda i: (i % nt, 0)
        in_specs += [pl.BlockSpec((tm, LANES), pos), pl.BlockSpec((tm, LANES), pos)]
        args += list(rope_tabs)
    dts = (BF16, F32, F32, BF16, BF16, BF16, F32)
    out_specs = [pl.BlockSpec((tm, c), row) for c in AB_COLS]
    out_shape = [jax.ShapeDtypeStruct((t, c), dt) for c, dt in zip(AB_COLS, dts)]
    if rope_tabs is None:
        assert (tm // 2) % seq_len == 0
        spt = tm // seq_len
        out_specs += [pl.BlockSpec((spt, A_KV_WIDTH, seq_len), lambda i: (i, 0, 0))] * 2
        out_shape += [jax.ShapeDtypeStruct((t // seq_len, A_KV_WIDTH, seq_len), F32)] * 2
    return pl.pallas_call(
        functools.partial(_ab_in_kernel, rope=rope_tabs is not None, seq_len=seq_len),
        grid=(t // tm,),
        in_specs=in_specs,
        out_specs=out_specs,
        out_shape=out_shape,
        compiler_params=_cparams("arbitrary"),
        name="ab_in_proj",
    )(*args)


def _conv_kernel(prev_ref, main_ref, next_ref, w_ref, b_ref, o_ref, *, tiles_per_seq):
    i = pl.program_id(0) % tiles_per_seq
    tm = main_ref.shape[0]
    prev = jnp.where(i > 0, prev_ref[...].astype(F32), 0.0)
    nxt = jnp.where(i < tiles_per_seq - 1, next_ref[...].astype(F32), 0.0)
    ext = jnp.concatenate([prev, main_ref[...].astype(F32), nxt], axis=0)
    first = SUBLANES - CONV_K // 2
    acc = b_ref[...] + ext[first:first + tm] * w_ref[0:1, :]
    for j in range(1, CONV_K):
        acc = acc + ext[first + j:first + j + tm] * w_ref[j:j + 1, :]
    o_ref[...] = _silu(acc).astype(o_ref.dtype)


def _conv(xbc, w, b, seq_len, tm):
    t, c = xbc.shape
    tps = seq_len // tm
    hb = tm // SUBLANES
    nblk8 = t // SUBLANES
    return pl.pallas_call(
        functools.partial(_conv_kernel, tiles_per_seq=tps),
        grid=(t // tm,),
        in_specs=[pl.BlockSpec((SUBLANES, c), lambda i: (jnp.maximum(i * hb - 1, 0), 0)),
                  pl.BlockSpec((tm, c), lambda i: (i, 0)),
                  pl.BlockSpec((SUBLANES, c), lambda i: (jnp.minimum((i + 1) * hb, nblk8 - 1), 0)),
                  pl.BlockSpec((SUBLANES, c), lambda i: (0, 0)),
                  pl.BlockSpec((1, c), lambda i: (0, 0))],
        out_specs=pl.BlockSpec((tm, c), lambda i: (i, 0)),
        out_shape=jax.ShapeDtypeStruct((t, c), BF16),
        compiler_params=_cparams("arbitrary"),
        name="ssd_conv",
    )(xbc, xbc, xbc, w, b)


def _ssd_init(h0_ref, h_scr, has_h0):
    if not has_h0:
        h_scr[...] = jnp.zeros_like(h_scr)
        return
    for pr in range(SSD_HEADS // 2):
        both = jnp.concatenate([h0_ref[2 * pr], h0_ref[2 * pr + 1]], axis=0)
        both = jnp.concatenate([both, jnp.zeros_like(both)], axis=1)
        st = both.T[0:SSD_STATE]
        lo_st = _lane_iota(st.shape) < SSD_HEAD_DIM
        h_scr[2 * pr] = jnp.where(lo_st, st, 0.0)
        h_scr[2 * pr + 1] = jnp.where(lo_st, 0.0, st)


def _ssd_final(h_scr, hout_ref):
    for pr in range(SSD_HEADS // 2):
        st = h_scr[2 * pr] + h_scr[2 * pr + 1]
        st = jnp.concatenate([st, jnp.zeros_like(st)], axis=0).T
        hout_ref[2 * pr] = st[0:SSD_HEAD_DIM, 0:SSD_STATE]
        hout_ref[2 * pr + 1] = st[SSD_HEAD_DIM:2 * SSD_HEAD_DIM, 0:SSD_STATE]


def _ssd_prelude(b_ref, c_ref, dt_ref, par_ref, direction):
    lane = _lane_iota((CHUNK, LANES))
    rowi = _row_iota((CHUNK, LANES))
    dtp_in = dt_ref[...] + par_ref[0:1, :]
    dtp = jnp.maximum(dtp_in, 0.0) + jnp.log(1.0 + jnp.exp(-jnp.abs(dtp_in)))
    la = dtp * (-jnp.exp(par_ref[1:2, :]))
    causal = (rowi >= lane) if direction == 0 else (rowi <= lane)
    tri = jnp.where(causal, 1.0, 0.0).astype(BF16)
    p1 = la.astype(BF16)
    r1 = la - p1.astype(F32)
    p2 = r1.astype(BF16)
    p3 = (r1 - p2.astype(F32)).astype(BF16)
    cum = _dot(tri, p1) + _dot(tri, p2) + _dot(tri, p3)
    cum_t = cum.T
    dtp_t = dtp.T
    end = CHUNK - 1 if direction == 0 else 0
    w_t = jnp.exp(cum_t[:, end:end + 1] - cum_t) * dtp_t
    bm = b_ref[...]
    cm = c_ref[...]
    cb, cmask = [], []
    for g in range(SSD_GROUPS):
        in_g = (lane >= g * SSD_STATE) & (lane < (g + 1) * SSD_STATE)
        cg = jnp.where(in_g, cm, jnp.zeros_like(cm))
        cmask.append(cg.astype(F32))
        cb.append(_dot_nt(cg, bm))
    return dict(cum=cum, cum_t=cum_t, dtp_t=dtp_t, w_t=w_t, causal=causal, end=end,
                bm_t=bm.astype(F32).T, cb=cb, cmask=cmask, lo_half=lane < SSD_HEAD_DIM)


def _ssd_head_pair(pre, pr, x_ref, dskip_ref, y_ref, h_scr, direction):
    hp = SSD_HEADS // SSD_GROUPS
    cum, cum_t, dtp_t, w_t, end = pre["cum"], pre["cum_t"], pre["dtp_t"], pre["w_t"], pre["end"]
    lo_half = pre["lo_half"]
    xp = x_ref[:, pr * LANES:(pr + 1) * LANES]
    y_pair = None
    for e in range(2):
        h = 2 * pr + e
        g = h // hp
        col = direction * SSD_HEADS + h
        xe = jnp.where(lo_half if e == 0 else ~lo_half, xp, jnp.zeros_like(xp))
        cumcol = jnp.broadcast_to(cum[:, col:col + 1], (CHUNK, LANES))
        diff = cumcol - cum_t[col:col + 1, :]
        dec = jnp.exp(jnp.where(pre["causal"], diff, NEG_INF))
        sc = pre["cb"][g] * dec * dtp_t[col:col + 1, :]
        ce = pre["cmask"][g] * jnp.exp(cumcol)
        hs = h_scr[h]
        hsb = hs.astype(BF16)
        zpad = jnp.zeros_like(hsb)
        a = jnp.concatenate([sc.astype(BF16), ce.astype(BF16)], axis=1)
        wmat = jnp.concatenate([xe, hsb, zpad] if g == 0 else [xe, zpad, hsb], axis=0)
        ye = _dot(a, wmat)
        y_pair = ye if y_pair is None else y_pair + ye
        bw = (pre["bm_t"][g * SSD_STATE:(g + 1) * SSD_STATE, :] * w_t[col:col + 1, :]).astype(BF16)
        new = _dot(bw, xe)
        cdec = jnp.exp(cumcol[end:end + 1, :])
        h_scr[h] = hs * cdec + new
    if direction == 0:
        y_pair = y_pair + dskip_ref[:, pr * LANES:(pr + 1) * LANES] * xp.astype(F32)
    y_ref[:, pr * LANES:(pr + 1) * LANES] = y_pair.astype(y_ref.dtype)


def _ssd_kernel(xf_ref, bf_ref, cf_ref, dtf_ref, xb_ref, bb_ref, cb_ref, dtb_ref, par_ref, dskip_ref,
                h0f_ref, h0b_ref, yf_ref, yb_ref, houtf_ref, houtb_ref, hf_scr, hb_scr, *, nc, has_h0):
    ci = pl.program_id(1)

    @pl.when(ci == 0)
    def _init():
        _ssd_init(h0f_ref, hf_scr, has_h0)
        _ssd_init(h0b_ref, hb_scr, has_h0)

    pre_f = _ssd_prelude(bf_ref, cf_ref, dtf_ref, par_ref, 0)
    pre_b = _ssd_prelude(bb_ref, cb_ref, dtb_ref, par_ref, 1)
    for pr in range(SSD_HEADS // 2):
        _ssd_head_pair(pre_f, pr, xf_ref, dskip_ref, yf_ref, hf_scr, 0)
        _ssd_head_pair(pre_b, pr, xb_ref, dskip_ref, yb_ref, hb_scr, 1)

    @pl.when(ci == nc - 1)
    def _final():
        _ssd_final(hf_scr, houtf_ref)
        _ssd_final(hb_scr, houtb_ref)


def _ssd(xbc, dt, par, dskip, h0f, h0b, nb, seq_len):
    t = xbc.shape[0]
    nc = seq_len // CHUNK
    fwd = lambda b, c: b * nc + c
    bwd = lambda b, c: b * nc + nc - 1 - c
    xcols = SSD_INNER // LANES
    has_h0 = h0f is not None
    if not has_h0:
        h0f = h0b = jnp.zeros((1, SSD_HEADS, SSD_HEAD_DIM, SSD_STATE), F32)
    h0_map = (lambda b, c: (b, 0, 0, 0)) if has_h0 else (lambda b, c: (0, 0, 0, 0))
    st_shape = (None, SSD_HEADS, SSD_HEAD_DIM, SSD_STATE)
    carry_shape = (None, SSD_HEADS, SSD_STATE, LANES)
    per_b = lambda b, c: (b, 0, 0, 0)

    def chunk_specs(cidx):
        return [pl.BlockSpec((CHUNK, SSD_INNER), lambda b, c: (cidx(b, c), 0)),
                pl.BlockSpec((CHUNK, LANES), lambda b, c: (cidx(b, c), xcols)),
                pl.BlockSpec((CHUNK, LANES), lambda b, c: (cidx(b, c), xcols + 1)),
                pl.BlockSpec((CHUNK, LANES), lambda b, c: (cidx(b, c), 0))]

    outs = pl.pallas_call(
        functools.partial(_ssd_kernel, nc=nc, has_h0=has_h0),
        grid=(nb, nc),
        in_specs=chunk_specs(fwd) + chunk_specs(bwd) + [
            pl.BlockSpec((SUBLANES, LANES), lambda b, c: (0, 0)),
            pl.BlockSpec((1, SSD_INNER), lambda b, c: (0, 0)),
            pl.BlockSpec(st_shape, h0_map), pl.BlockSpec(st_shape, h0_map)],
        out_specs=[pl.BlockSpec((CHUNK, SSD_INNER), lambda b, c: (fwd(b, c), 0)),
                   pl.BlockSpec((CHUNK, SSD_INNER), lambda b, c: (bwd(b, c), 0)),
                   pl.BlockSpec(st_shape, per_b), pl.BlockSpec(st_shape, per_b),
                   pl.BlockSpec(carry_shape, per_b), pl.BlockSpec(carry_shape, per_b)],
        out_shape=[jax.ShapeDtypeStruct((t, SSD_INNER), BF16), jax.ShapeDtypeStruct((t, SSD_INNER), BF16),
                   jax.ShapeDtypeStruct((nb, SSD_HEADS, SSD_HEAD_DIM, SSD_STATE), F32),
                   jax.ShapeDtypeStruct((nb, SSD_HEADS, SSD_HEAD_DIM, SSD_STATE), F32),
                   jax.ShapeDtypeStruct((nb, SSD_HEADS, SSD_STATE, LANES), F32),
                   jax.ShapeDtypeStruct((nb, SSD_HEADS, SSD_STATE, LANES), F32)],
        compiler_params=_cparams("arbitrary", "arbitrary"),
        name="ssd_scan_bidir",
    )(xbc, xbc, xbc, dt, xbc, xbc, xbc, dt, par, dskip, h0f, h0b)
    return outs[:4]


def _kv_variants(k, v):
    lane = _lane_iota(k.shape)
    lo = lane < A_HEAD_DIM
    k_sw = pltpu.roll(k, A_HEAD_DIM, 1)
    v_sw = pltpu.roll(v, A_HEAD_DIM, 1)
    ks, vs = {}, {}
    for j in range(A_KV_HEADS):
        for e in range(2):
            src_k, src_v = (k, v) if j == e else (k_sw, v_sw)
            half = lo if e == 0 else ~lo
            ks[j, e] = jnp.where(half, src_k, 0.0).astype(BF16)
            vs[j, e] = src_v.astype(BF16)
    return ks, vs


def _gqa_heads(q_ref, g_ref, o_ref, sink_ref, ks, vs, valid):
    m = q_ref.shape[0]
    assert A_HEADS // A_KV_HEADS == 4
    lo = _lane_iota((m, LANES)) < A_HEAD_DIM
    upper_rows = _row_iota((2 * m, 1)) >= m
    scale = jnp.asarray(A_SCALE, BF16)
    outs = {}
    for j in range(A_KV_HEADS):
        rows = jnp.concatenate([q_ref[:, 2 * j * LANES:(2 * j + 1) * LANES],
                                q_ref[:, (2 * j + 1) * LANES:(2 * j + 2) * LANES]], axis=0) * scale
        for e in range(2):
            s = _dot_nt(rows, ks[j, e])
            if valid is not None:
                nloc = valid.shape[1]
                s = jnp.concatenate([jnp.where(valid, s[:, :nloc], NEG_INF), s[:, nloc:]], axis=1)
            sink = jnp.where(upper_rows, sink_ref[4 * j + 2 + e], sink_ref[4 * j + e])
            mx = jnp.maximum(jnp.max(s, axis=1, keepdims=True), sink)
            p = jnp.exp(s - mx)
            den = jnp.sum(p, axis=1, keepdims=True) + jnp.exp(sink - mx)
            outs[j, e] = _dot(p.astype(BF16), vs[j, e]) / den
    for pr in range(A_HEADS // 2):
        j, half = pr // 2, pr % 2
        r = slice(half * m, (half + 1) * m)
        attn = jnp.where(lo, outs[j, 0][r], outs[j, 1][r])
        gate = g_ref[:, pr * LANES:(pr + 1) * LANES].astype(F32)
        o_ref[:, pr * LANES:(pr + 1) * LANES] = (attn * _silu(gate)).astype(o_ref.dtype)


def _attn_ctx_kernel(sink_ref, q_ref, k_ref, v_ref, g_ref, o_ref):
    ks, vs = _kv_variants(k_ref[...], v_ref[...])
    _gqa_heads(q_ref, g_ref, o_ref, sink_ref, ks, vs, None)


def _attn_ctx(sink, q, k, v, g, seq_len):
    t = q.shape[0]
    row = lambda b: (b, 0)
    return pl.pallas_call(
        _attn_ctx_kernel,
        grid=(t // seq_len,),
        in_specs=[pl.BlockSpec(memory_space=pltpu.SMEM),
                  pl.BlockSpec((seq_len, A_WIDTH), row),
                  pl.BlockSpec((seq_len, A_KV_WIDTH), row),
                  pl.BlockSpec((seq_len, A_KV_WIDTH), row),
                  pl.BlockSpec((seq_len, A_WIDTH), row)],
        out_specs=pl.BlockSpec((seq_len, A_WIDTH), row),
        out_shape=jax.ShapeDtypeStruct((t, A_WIDTH), BF16),
        compiler_params=_cparams("arbitrary"),
        name="swa_context_attention",
    )(sink, q, k, v, g)


BAND_Q = 256


def _attn_band_kernel(sink_ref, q_ref, kp_ref, kc_ref, kn_ref, vp_ref, vc_ref, vn_ref, kx_ref, vx_ref, g_ref,
                      o_ref, *, nsteps):
    n = pl.program_id(1)
    tq = q_ref.shape[0]
    blk = WINDOW_BLK
    k = jnp.concatenate([kp_ref[...], kc_ref[...], kn_ref[...], kx_ref[...]], axis=0)
    v = jnp.concatenate([vp_ref[...], vc_ref[...], vn_ref[...], vx_ref[...]], axis=0)
    nloc = tq + 2 * blk
    qi = _row_iota((2 * tq, nloc)) & (tq - 1)
    col = _lane_iota((2 * tq, nloc))
    rel = col - blk - qi
    valid = ((rel >= -blk) & (rel <= blk) & ((col >= blk) | (n > 0)) & ((col < nloc - blk) | (n < nsteps - 1)))
    ks, vs = _kv_variants(k, v)
    _gqa_heads(q_ref, g_ref, o_ref, sink_ref, ks, vs, valid)


def _attn_band(sink, q, k, v, k_ctx, v_ctx, g, nb, seq_len):
    t = q.shape[0]
    blk = WINDOW_BLK
    tq = BAND_Q
    assert tq & (tq - 1) == 0 and tq % blk == 0 and seq_len % tq == 0
    per = tq // blk
    nblk = seq_len // blk
    nsteps = seq_len // tq
    nctx = k_ctx.shape[1]
    cur = lambda b, n: (b * nsteps + n, 0)
    prv = lambda b, n: (b * nblk + jnp.maximum(n * per - 1, 0), 0)
    nxt = lambda b, n: (b * nblk + jnp.minimum((n + 1) * per, nblk - 1), 0)
    ctx = lambda b, n: (b, 0, 0)
    edge = lambda f: pl.BlockSpec((blk, A_KV_WIDTH), f)
    mid = pl.BlockSpec((tq, A_KV_WIDTH), cur)
    return pl.pallas_call(
        functools.partial(_attn_band_kernel, nsteps=nsteps),
        grid=(nb, nsteps),
        in_specs=[pl.BlockSpec(memory_space=pltpu.SMEM),
                  pl.BlockSpec((tq, A_WIDTH), cur),
                  edge(prv), mid, edge(nxt),
                  edge(prv), mid, edge(nxt),
                  pl.BlockSpec((None, nctx, A_KV_WIDTH), ctx),
                  pl.BlockSpec((None, nctx, A_KV_WIDTH), ctx),
                  pl.BlockSpec((tq, A_WIDTH), cur)],
        out_specs=pl.BlockSpec((tq, A_WIDTH), cur),
        out_shape=jax.ShapeDtypeStruct((t, A_WIDTH), BF16),
        compiler_params=_cparams("arbitrary", "arbitrary"),
        name="swa_banded_attention",
    )(sink, q, k, k, k, v, v, v, k_ctx, v_ctx, g)


def _ab_out_kernel(a_ref, yf_ref, yb_ref, z_ref, gnw_ref, w_ref, x_ref, gate_ref, o_ref):
    for rows in _row_splits(x_ref.shape[0]):
        y = (yf_ref[rows, :].astype(F32) + yb_ref[rows, :].astype(F32)) * _silu(z_ref[rows, :].astype(F32))
        ms = jnp.mean(y * y, axis=-1, keepdims=True)
        s = ((y * lax.rsqrt(ms + EPS)) * gnw_ref[...]).astype(BF16)
        out = _dot(a_ref[rows, :], w_ref[0:A_WIDTH, :]) + _dot(s, w_ref[A_WIDTH:, :])
        o_ref[rows, :] = x_ref[rows, :] + gate_ref[...] * out


def _ab_out(a, yf, yb, z, gnw, w, x, gate, seq_len, tm):
    t, d = x.shape
    per_seq = seq_len // tm if gate.shape[0] > 1 else None
    row = lambda i: (i, 0)
    const = lambda i: (0, 0)
    mod = (lambda i: (i // per_seq, 0, 0)) if per_seq else (lambda i: (0, 0, 0))
    return pl.pallas_call(
        _ab_out_kernel,
        grid=(t // tm,),
        in_specs=[pl.BlockSpec((tm, A_WIDTH), row),
                  pl.BlockSpec((tm, SSD_INNER), row),
                  pl.BlockSpec((tm, SSD_INNER), row),
                  pl.BlockSpec((tm, SSD_INNER), row),
                  pl.BlockSpec((1, SSD_INNER), const),
                  pl.BlockSpec(w.shape, const),
                  pl.BlockSpec((tm, d), row),
                  pl.BlockSpec((None, 1, d), mod)],
        out_specs=pl.BlockSpec((tm, d), row),
        out_shape=jax.ShapeDtypeStruct((t, d), F32),
        compiler_params=_cparams("arbitrary"),
        name="ab_out_proj",
    )(a, yf, yb, z, gnw, w, x, gate)


MLA_COLS = (Q_LORA, KV_LORA, LANES, MLA_WIDTH)


def _mla_in_kernel(*refs, rope):
    if rope:
        (x_ref, nw_ref, sc_ref, sh_ref, w_ref, qnw_ref, kvnw_ref, wuq_ref, cos_ref, sin_ref,
         qn_ref, qpe_ref, ckv_ref, kpe_ref, g_ref, ckv32_ref, kpe32_ref) = refs
    else:
        (x_ref, nw_ref, sc_ref, sh_ref, w_ref, qnw_ref, kvnw_ref, wuq_ref,
         qn_ref, qpe_ref, ckv_ref, kpe_ref, g_ref, ckv32_ref, kpe32_ref) = refs
    offs = [int(o) for o in np.concatenate([[0], np.cumsum(MLA_COLS)])]
    nope_w = MLA_HEADS * MLA_NOPE

    def rms(u, w):
        return (u * lax.rsqrt(jnp.mean(u * u, axis=-1, keepdims=True) + EPS)) * w

    for rows in _row_splits(x_ref.shape[0]):
        h = _norm_mod(x_ref[rows, :], nw_ref[...], sc_ref[...], sh_ref[...]).astype(BF16)
        cq = _dot(h, w_ref[:, offs[0]:offs[1]])
        ckv = _dot(h, w_ref[:, offs[1]:offs[2]])
        kpe = _dot(h, w_ref[:, offs[2]:offs[3]])
        g_ref[rows, :] = _dot(h, w_ref[:, offs[3]:offs[4]]).astype(g_ref.dtype)
        cqn = rms(cq, qnw_ref[...]).astype(BF16)
        qn_ref[rows, :] = _dot(cqn, wuq_ref[:, 0:nope_w]).astype(qn_ref.dtype)
        qpe = _dot(cqn, wuq_ref[:, nope_w:])
        if rope:
            qpe = _rope(qpe, cos_ref[rows, :], sin_ref[rows, :], MLA_ROPE // 4)
            kpe = _rope(kpe, cos_ref[rows, :], sin_ref[rows, :], MLA_ROPE // 4)
        qpe_ref[rows, :] = qpe.astype(qpe_ref.dtype)
        ckvn = rms(ckv, kvnw_ref[...])
        ckv_ref[rows, :] = ckvn.astype(ckv_ref.dtype)
        kpe_ref[rows, :] = kpe.astype(kpe_ref.dtype)
        ckv32_ref[rows, :] = ckvn
        kpe32_ref[rows, :] = kpe


def _mla_in(x, nw, scale, shift, w, qnw, kvnw, wuq, rope_tabs, seq_len, tm):
    t, d = x.shape
    per_seq = seq_len // tm if scale.shape[0] > 1 else None
    row = lambda i: (i, 0)
    const = lambda i: (0, 0)
    mod = (lambda i: (i // per_seq, 0, 0)) if per_seq else (lambda i: (0, 0, 0))
    in_specs = [pl.BlockSpec((tm, d), row), pl.BlockSpec((1, d), const),
                pl.BlockSpec((None, 1, d), mod), pl.BlockSpec((None, 1, d), mod),
                pl.BlockSpec(w.shape, const), pl.BlockSpec((1, Q_LORA), const),
                pl.BlockSpec((1, KV_LORA), const), pl.BlockSpec(wuq.shape, const)]
    args = [x, nw, scale, shift, w, qnw, kvnw, wuq]
    if rope_tabs is not None:
        nt = seq_len // tm
        pos = lambda i: (i % nt, 0)
        in_specs += [pl.BlockSpec((tm, LANES), pos), pl.BlockSpec((tm, LANES), pos)]
        args += list(rope_tabs)
    widths = (MLA_HEADS * MLA_NOPE, MLA_HEADS * MLA_ROPE, KV_LORA, LANES, MLA_WIDTH, KV_LORA, LANES)
    dts = (BF16, BF16, BF16, BF16, BF16, F32, F32)
    return pl.pallas_call(
        functools.partial(_mla_in_kernel, rope=rope_tabs is not None),
        grid=(t // tm,),
        in_specs=in_specs,
        out_specs=[pl.BlockSpec((tm, c), row) for c in widths],
        out_shape=[jax.ShapeDtypeStruct((t, c), dt) for c, dt in zip(widths, dts)],
        compiler_params=_cparams("arbitrary"),
        name="mla_in_proj",
    )(*args)


MLA_UNIT_ROWS = 256
MLA_KEY_TILE = 256


def _mla_attn_kernel(qn_ref, qpe_ref, ckv_ref, kpe_ref, wukt_ref, wuv_ref, g_ref, o_ref, kcat_scr, v_scr, s_scr, *,
                     pairs_per_step):
    pg = pl.program_id(1)
    qb = pl.program_id(2)
    tq = qn_ref.shape[0]
    c_exp = MLA_SCALE * LOG2E

    @pl.when(qb == 0)
    def _expand():
        ckv = ckv_ref[...]
        eye = jnp.where(_row_iota((LANES, LANES)) == _lane_iota((LANES, LANES)), 1.0, 0.0).astype(BF16)
        kpe_t = _dot_nt(eye, kpe_ref[...]).astype(BF16)
        for i in range(pairs_per_step):
            kcat_scr[i, 0:LANES, :] = _dot_nt(wukt_ref[i * LANES:(i + 1) * LANES, :], ckv).astype(BF16)
            kcat_scr[i, LANES:2 * LANES, :] = kpe_t
            v_scr[i] = _dot(ckv, wuv_ref[:, i * LANES:(i + 1) * LANES]).astype(BF16)

    nkeys = kcat_scr.shape[-1]
    kt_w = min(MLA_KEY_TILE, nkeys)
    nkt = nkeys // kt_w
    ru = min(MLA_UNIT_ROWS, tq)
    lane = _lane_iota((ru, LANES))
    units = [(i, r, e) for i in range(pairs_per_step) for r in range(tq // ru) for e in range(2)]

    def lane_tiles(x):
        return [x[:, j * LANES:(j + 1) * LANES] for j in range(x.shape[1] // LANES)]

    def q_ext(u):
        i, r, e = units[u]
        rows = slice(r * ru, (r + 1) * ru)
        qn = qn_ref[rows, i * LANES:(i + 1) * LANES]
        qpe = qpe_ref[rows, (i // 2) * LANES:(i // 2 + 1) * LANES]
        slot = 2 * (i % 2) + e if pairs_per_step % 2 == 0 else 2 * ((pg * pairs_per_step + i) % 2) + e
        nope_half = (lane < MLA_NOPE) if e == 0 else (lane >= MLA_NOPE)
        return jnp.concatenate([jnp.where(nope_half, qn, jnp.zeros_like(qn)),
                                jnp.where((lane >> 5) == slot, qpe, jnp.zeros_like(qpe))], axis=1)

    def score_tile(u, st, kt):
        cols = slice(kt * kt_w, (kt + 1) * kt_w)
        s = _dot(st["q"], kcat_scr[units[u][0], :, cols])
        s_scr[u % 2, :, cols] = s
        for t in lane_tiles(s):
            st["mrun"] = t if st["mrun"] is None else jnp.maximum(st["mrun"], t)

    def value_tile(u, st, kt):
        cols = slice(kt * kt_w, (kt + 1) * kt_w)
        p = jnp.exp2((s_scr[u % 2, :, cols] - st["m"]) * c_exp)
        for t in lane_tiles(p):
            st["lrun"] = t if st["lrun"] is None else st["lrun"] + t
        pv = _dot(p.astype(BF16), v_scr[units[u][0], cols, :])
        st["acc"] = pv if st["acc"] is None else st["acc"] + pv

    def finish(u, st, done):
        i, r, e = units[u]
        den = jnp.sum(st["lrun"], axis=1, keepdims=True)
        done[e] = st["acc"] / den
        if e == 1:
            rows = slice(r * ru, (r + 1) * ru)
            attn = jnp.where(lane < MLA_V, done[0], done[1])
            gate = g_ref[rows, i * LANES:(i + 1) * LANES].astype(F32)
            o_ref[rows, i * LANES:(i + 1) * LANES] = (attn * _silu(gate)).astype(o_ref.dtype)

    done = {}
    cur = None
    for u in range(len(units) + 1):
        nxt = dict(q=q_ext(u), mrun=None) if u < len(units) else None
        for kt in range(nkt):
            if nxt is not None:
                score_tile(u, nxt, kt)
            if cur is not None:
                value_tile(u - 1, cur, kt)
        if cur is not None:
            finish(u - 1, cur, done)
        if nxt is not None:
            nxt.update(m=jnp.max(nxt["mrun"], axis=1, keepdims=True), lrun=None, acc=None)
        cur = nxt


def _mla_attn(qn, qpe, ckv_keys, kpe_keys, wukt, wuv, g, nb, seq_len, tq, pairs_per_step):
    t = qn.shape[0]
    nkeys = ckv_keys.shape[1]
    npairs = MLA_HEADS // 2
    ngrp = npairs // pairs_per_step
    nq = seq_len // tq
    wp = pairs_per_step * LANES
    if pairs_per_step % 2 == 0:
        wpe = wp // 2
        pe_map = lambda b, p, i: (b * nq + i, p)
    else:
        wpe = LANES
        pe_map = lambda b, p, i: (b * nq + i, p // 2)
    qmap = lambda b, p, i: (b * nq + i, p)
    return pl.pallas_call(
        functools.partial(_mla_attn_kernel, pairs_per_step=pairs_per_step),
        grid=(nb, ngrp, nq),
        in_specs=[pl.BlockSpec((tq, wp), qmap),
                  pl.BlockSpec((tq, wpe), pe_map),
                  pl.BlockSpec((None, nkeys, KV_LORA), lambda b, p, i: (b, 0, 0)),
                  pl.BlockSpec((None, nkeys, LANES), lambda b, p, i: (b, 0, 0)),
                  pl.BlockSpec((wp, KV_LORA), lambda b, p, i: (p, 0)),
                  pl.BlockSpec((KV_LORA, wp), lambda b, p, i: (0, p)),
                  pl.BlockSpec((tq, wp), qmap)],
        out_specs=pl.BlockSpec((tq, wp), qmap),
        out_shape=jax.ShapeDtypeStruct((t, MLA_WIDTH), BF16),
        scratch_shapes=[pltpu.VMEM((pairs_per_step, 2 * LANES, nkeys), BF16),
                        pltpu.VMEM((pairs_per_step, nkeys, LANES), BF16),
                        pltpu.VMEM((2, min(MLA_UNIT_ROWS, tq), nkeys), F32)],
        compiler_params=_cparams("arbitrary", "arbitrary", "arbitrary"),
        name="mla_attention",
    )(qn, qpe, ckv_keys, kpe_keys, wukt, wuv, g)


def _mla_attn_seq_kernel(qn_ref, qpe_ref, ckv_ref, kpe_ref, wukt_ref, wuv_ref, g_ref, o_ref,
                         kcat_scr, v_scr, s_scr, m_scr, half_scr):
    pair = pl.program_id(1)
    seq = qn_ref.shape[0]
    nkeys = kcat_scr.shape[-1]
    ru, kt_w = MLA_UNIT_ROWS, MLA_KEY_TILE
    nkt = nkeys // kt_w
    nrb = seq // ru
    c_exp = MLA_SCALE * LOG2E
    lane = _lane_iota((ru, LANES))

    ckv = ckv_ref[...]
    eye = jnp.where(_row_iota((LANES, LANES)) == _lane_iota((LANES, LANES)), 1.0, 0.0).astype(BF16)
    kcat_scr[0:LANES, :] = _dot_nt(wukt_ref[...], ckv).astype(BF16)
    kcat_scr[LANES:2 * LANES, :] = _dot_nt(eye, kpe_ref[...]).astype(BF16)
    v_scr[:, 0:LANES] = _dot(ckv, wuv_ref[...]).astype(BF16)
    v_scr[:, LANES:2 * LANES] = jnp.ones((nkeys, LANES), BF16)

    def rows_of(rb):
        return slice(rb * ru, (rb + 1) * ru) if isinstance(rb, int) else pl.ds(pl.multiple_of(rb * ru, ru), ru)

    def q_ext(rb, e):
        rows = rows_of(rb)
        qn = qn_ref[rows, :]
        qpe = qpe_ref[rows, :]
        nope_half = (lane < MLA_NOPE) if e == 0 else (lane >= MLA_NOPE)
        slot = 2 * (pair % 2) + e
        return jnp.concatenate([jnp.where(nope_half, qn, jnp.zeros_like(qn)),
                                jnp.where((lane >> 5) == slot, qpe, jnp.zeros_like(qpe))], axis=1)

    def phase(score, value):
        q = q_ext(*score) if score is not None else None
        mrun = acc = None
        for kt in range(nkt):
            cols = slice(kt * kt_w, (kt + 1) * kt_w)
            if score is not None:
                s = _dot(q, kcat_scr[:, cols])
                s_scr[score[1], :, cols] = s
                for j in range(kt_w // LANES):
                    t = s[:, j * LANES:(j + 1) * LANES]
                    mrun = t if mrun is None else jnp.maximum(mrun, t)
            if value is not None:
                mb = m_scr[value]
                p = jnp.concatenate(
                    [jnp.exp2((s_scr[value, :, kt * kt_w + j * LANES:kt * kt_w + (j + 1) * LANES] - mb) * c_exp)
                     for j in range(kt_w // LANES)], axis=1)
                pv = _dot(p.astype(BF16), v_scr[cols, :])
                acc = pv if acc is None else acc + pv
        if score is not None:
            m_scr[score[1]] = jnp.broadcast_to(jnp.max(mrun, axis=1, keepdims=True), (ru, LANES))
        return None if acc is None else acc[:, 0:LANES] / acc[:, LANES:2 * LANES]

    phase((0, 0), None)

    def row_block(rb, carry):
        half_scr[...] = phase((rb, 1), 0)
        out1 = phase((jnp.minimum(rb + 1, nrb - 1), 0), 1)
        rows = rows_of(rb)
        attn = jnp.where(lane < MLA_V, half_scr[...], out1)
        o_ref[rows, :] = (attn * _silu(g_ref[rows, :].astype(F32))).astype(o_ref.dtype)
        return carry

    lax.fori_loop(0, nrb, row_block, 0)


def _mla_attn_seq(qn, qpe, ckv_keys, kpe_keys, wukt, wuv, g, nb, seq_len):
    t = qn.shape[0]
    nkeys = ckv_keys.shape[1]
    assert seq_len % MLA_UNIT_ROWS == 0 and nkeys % MLA_KEY_TILE == 0
    qmap = lambda b, p: (b, p)
    return pl.pallas_call(
        _mla_attn_seq_kernel,
        grid=(nb, MLA_HEADS // 2),
        in_specs=[pl.BlockSpec((seq_len, LANES), qmap),
                  pl.BlockSpec((seq_len, LANES), lambda b, p: (b, p // 2)),
                  pl.BlockSpec((None, nkeys, KV_LORA), lambda b, p: (b, 0, 0)),
                  pl.BlockSpec((None, nkeys, LANES), lambda b, p: (b, 0, 0)),
                  pl.BlockSpec((LANES, KV_LORA), lambda b, p: (p, 0)),
                  pl.BlockSpec((KV_LORA, LANES), lambda b, p: (0, p)),
                  pl.BlockSpec((seq_len, LANES), qmap)],
        out_specs=pl.BlockSpec((seq_len, LANES), qmap),
        out_shape=jax.ShapeDtypeStruct((t, MLA_WIDTH), BF16),
        scratch_shapes=[pltpu.VMEM((2 * LANES, nkeys), BF16),
                        pltpu.VMEM((nkeys, 2 * LANES), BF16),
                        pltpu.VMEM((2, MLA_UNIT_ROWS, nkeys), F32),
                        pltpu.VMEM((2, MLA_UNIT_ROWS, LANES), F32),
                        pltpu.VMEM((MLA_UNIT_ROWS, LANES), F32)],
        compiler_params=_cparams("arbitrary", "arbitrary"),
        name="mla_attention_seq",
    )(qn, qpe, ckv_keys, kpe_keys, wukt, wuv, g)


def _mla_out_kernel(a_ref, w_ref, x_ref, gate_ref, fw_ref, o_ref):
    for rows in _row_splits(x_ref.shape[0]):
        xn = x_ref[rows, :] + gate_ref[...] * _dot(a_ref[rows, :], w_ref[...])
        ms = jnp.mean(xn * xn, axis=-1, keepdims=True)
        o_ref[rows, :] = (xn * lax.rsqrt(ms + EPS)) * fw_ref[...]


def _mla_out(a, w, x, gate, fw, seq_len, tm):
    t, d = x.shape
    per_seq = seq_len // tm if gate.shape[0] > 1 else None
    row = lambda i: (i, 0)
    const = lambda i: (0, 0)
    mod = (lambda i: (i // per_seq, 0, 0)) if per_seq else (lambda i: (0, 0, 0))
    return pl.pallas_call(
        _mla_out_kernel,
        grid=(t // tm,),
        in_specs=[pl.BlockSpec((tm, MLA_WIDTH), row), pl.BlockSpec(w.shape, const),
                  pl.BlockSpec((tm, d), row), pl.BlockSpec((None, 1, d), mod), pl.BlockSpec((1, d), const)],
        out_specs=pl.BlockSpec((tm, d), row),
        out_shape=jax.ShapeDtypeStruct((t, d), F32),
        compiler_params=_cparams("arbitrary"),
        name="mla_out_proj_final_norm",
    )(a, w, x, gate, fw)


def _rope_tables(length, dim):
    rows = length // GRID_W
    row = jnp.repeat(jnp.arange(rows), GRID_W).astype(F32)
    col = jnp.tile(jnp.arange(GRID_W), rows).astype(F32)
    nf = dim // 4
    inv = 1.0 / (ROPE_BASE ** (jnp.arange(nf, dtype=F32) / nf))
    ar = row[:, None] * inv[None, :]
    ac = col[:, None] * inv[None, :]
    ang = jnp.concatenate([ar, ar, ac, ac], axis=-1)
    sign = jnp.tile(jnp.concatenate([-jnp.ones((nf,), F32), jnp.ones((nf,), F32)]), 2)
    reps = LANES // dim
    return jnp.tile(jnp.cos(ang), (1, reps)), jnp.tile(jnp.sin(ang) * sign, (1, reps))


def _group(x, seq_len, conds_rows, mods, wts, tabs_a, tabs_c, ctx_cache):
    nb, _, d = x.shape
    t = nb * seq_len
    latent = ctx_cache is not None
    xf = x.reshape(t, d)
    tm = 512

    def mod_rows(layer):
        m = mods[layer][conds_rows]
        sh, sc, gt = jnp.split(m[:, None, :], 3, axis=-1)
        return sh, sc, gt

    sh, sc, gt = mod_rows(0)
    q, k, v, g, z, xbc, dt, *kv_t = _ab_in(xf, wts["norm_w"][0:1], sc, sh, wts["ab_w_in"],
                                           tabs_a if latent else None, seq_len, tm)
    xbc_c = _conv(xbc, wts["ab_conv_w"], wts["ab_conv_b"], seq_len, min(tm, seq_len))
    if latent:
        k_ctx, v_ctx, s_f0, s_b0, _, _ = ctx_cache
        attn = _attn_band(wts["ab_sink"], q, k, v, k_ctx, v_ctx, g, nb, seq_len)
    else:
        s_f0 = s_b0 = None
        attn = _attn_ctx(wts["ab_sink"], q, k, v, g, seq_len)
    yf, yb, s_f, s_b = _ssd(xbc_c, dt, wts["ssd_par"], wts["ssd_dskip"], s_f0, s_b0, nb, seq_len)
    x1 = _ab_out(attn, yf, yb, z, wts["ab_gnorm_w"], wts["ab_w_out"], xf, gt, seq_len, tm)

    sh, sc, gt = mod_rows(1)
    qn, qpe, ckv, kpe, g1, ckv32, kpe32 = _mla_in(
        x1, wts["norm_w"][1:2], sc, sh, wts["mla_w_in"], wts["mla_q_norm_w"], wts["mla_kv_norm_w"],
        wts["mla_w_uq"], tabs_c if latent else None, seq_len, tm)
    ckv_keys = ckv.reshape(nb, seq_len, KV_LORA)
    kpe_keys = kpe.reshape(nb, seq_len, LANES)
    if latent:
        ckv_x, kpe_x = ctx_cache[4], ctx_cache[5]
        ckv_keys = jnp.concatenate([ckv_keys, ckv_x], axis=1)
        kpe_keys = jnp.concatenate([kpe_keys, kpe_x], axis=1)
        attn1 = _mla_attn_seq(qn, qpe, ckv_keys, kpe_keys, wts["mla_w_ukt"], wts["mla_w_uv"], g1, nb, seq_len)
    else:
        attn1 = _mla_attn(qn, qpe, ckv_keys, kpe_keys, wts["mla_w_ukt"], wts["mla_w_uv"], g1, nb, seq_len,
                          seq_len, MLA_HEADS // 2)
    y = _mla_out(attn1, wts["mla_w_out"], x1, gt, wts["final_norm_w"], seq_len, tm)
    return y.reshape(nb, seq_len, d), (kv_t, s_f, s_b, ckv32, kpe32)


def kernel(x_prompt, x_sample, cache_a_k, cache_a_v, state_ssd_fwd, state_ssd_bwd, cache_mla_ckv, cache_mla_kpe,
           c, c_ctx, ada_w, ada_b, norm_w, ab_w_in, ab_sink, ab_conv_w, ab_conv_b, ab_dt_bias, ab_a_log,
           ab_d_skip, ab_gnorm_w, ab_w_out, mla_w_in, mla_q_norm_w, mla_kv_norm_w, mla_w_uq, mla_w_ukv,
           mla_w_out, final_norm_w):
    batch, seq, d = x_prompt.shape
    dec_batch, dec_seq, _ = x_sample.shape
    assert ada_w.shape[0] == 2 and ab_w_in.shape[0] == 1 and mla_w_in.shape[0] == 1
    assert dec_batch + 1 <= SUBLANES

    conds = jnp.concatenate([c_ctx[None, :], c, jnp.zeros((SUBLANES - 1 - dec_batch, d), F32)], axis=0)
    mods = _modulation(conds, ada_w, ada_b)

    w_ab = ab_w_in[0]
    w_ab = jnp.concatenate([w_ab, jnp.zeros((d, LANES - 2 * SSD_HEADS), F32)], axis=1).astype(BF16)
    mw = mla_w_in[0]
    o_kpe = Q_LORA + KV_LORA
    w_mla = jnp.concatenate([mw[:, :o_kpe], jnp.tile(mw[:, o_kpe:o_kpe + MLA_ROPE], (1, LANES // MLA_ROPE)),
                             mw[:, o_kpe + MLA_ROPE:]], axis=1).astype(BF16)
    wuq = mla_w_uq[0].reshape(Q_LORA, MLA_HEADS, MLA_NOPE + MLA_ROPE)
    wuq = jnp.concatenate([wuq[:, :, :MLA_NOPE].reshape(Q_LORA, -1), wuq[:, :, MLA_NOPE:].reshape(Q_LORA, -1)],
                          axis=1).astype(BF16)
    wukv = mla_w_ukv[0].reshape(KV_LORA, MLA_HEADS, MLA_NOPE + MLA_V)
    wukt = wukv[:, :, :MLA_NOPE].reshape(KV_LORA, -1).T.astype(BF16)
    wuv = wukv[:, :, MLA_NOPE:].reshape(KV_LORA, -1).astype(BF16)
    pad_lanes = lambda r: jnp.concatenate([r.reshape(1, -1), jnp.zeros((1, LANES - r.size), F32)], axis=1)
    ssd_par = jnp.concatenate([pad_lanes(ab_dt_bias[0]), pad_lanes(ab_a_log[0]),
                               jnp.zeros((SUBLANES - 2, LANES), F32)], axis=0)
    wts = dict(
        norm_w=norm_w, ab_w_in=w_ab, ab_sink=ab_sink[0],
        ab_conv_w=jnp.concatenate([ab_conv_w[0], jnp.zeros((SUBLANES - CONV_K, CONV_CH), F32)], axis=0),
        ab_conv_b=ab_conv_b[0][None, :], ssd_par=ssd_par,
        ssd_dskip=jnp.repeat(ab_d_skip[0], SSD_HEAD_DIM)[None, :],
        ab_gnorm_w=ab_gnorm_w[0][None, :], ab_w_out=ab_w_out[0].astype(BF16),
        mla_w_in=w_mla, mla_q_norm_w=mla_q_norm_w[0][None, :], mla_kv_norm_w=mla_kv_norm_w[0][None, :],
        mla_w_uq=wuq, mla_w_ukt=wukt, mla_w_uv=wuv, mla_w_out=mla_w_out[0].astype(BF16),
        final_norm_w=final_norm_w[None, :],
    )
    tabs_a = _rope_tables(dec_seq, A_HEAD_DIM)
    tabs_c = _rope_tables(dec_seq, MLA_ROPE)

    y_prompt, ((k_t, v_t), s_f, s_b, ckv32, kpe32) = _group(
        x_prompt, seq, jnp.zeros((1,), jnp.int32), mods, wts, tabs_a, tabs_c, None)

    past = cache_a_k.shape[2]
    ctx_cache = (cache_a_k[:, 0].reshape(dec_batch, past, A_KV_WIDTH),
                 cache_a_v[:, 0].reshape(dec_batch, past, A_KV_WIDTH),
                 state_ssd_fwd[:, 0], state_ssd_bwd[:, 0],
                 cache_mla_ckv[:, 0].astype(BF16),
                 jnp.tile(cache_mla_kpe[:, 0], (1, 1, LANES // MLA_ROPE)).astype(BF16))
    y_sample, _ = _group(x_sample, dec_seq, 1 + jnp.arange(dec_batch), mods, wts, tabs_a, tabs_c, ctx_cache)

    cache_layout = lambda u: u.reshape(batch, 1, A_KV_HEADS, A_HEAD_DIM, seq).transpose(0, 1, 4, 2, 3)
    return (y_prompt, y_sample,
            cache_layout(k_t), cache_layout(v_t),
            s_f[:, None], s_b[:, None],
            ckv32.reshape(batch, 1, seq, KV_LORA), kpe32[:, :MLA_ROPE].reshape(batch, 1, seq, MLA_ROPE))
```

```python
import functools
import math

import jax
import jax.numpy as jnp
import numpy as np
from jax import lax
from jax.experimental import pallas as pl
from jax.experimental.pallas import tpu as pltpu

F32 = jnp.float32
BF16 = jnp.bfloat16

LANES = 128
SUBLANES = 8
VMEM_LIMIT_BYTES = 56 * 1024 * 1024

GRID_W = 64
ROPE_BASE = 10000.0
EPS = 1e-6
NEG_INF = -1e30
WINDOW_BLK = 128
CHUNK = 128
A_HEADS, A_KV_HEADS, A_HEAD_DIM = 8, 2, 64
A_WIDTH = A_HEADS * A_HEAD_DIM
A_KV_WIDTH = A_KV_HEADS * A_HEAD_DIM
A_SCALE = A_HEAD_DIM ** -0.5
SSD_HEADS, SSD_HEAD_DIM, SSD_GROUPS, SSD_STATE = 16, 64, 2, 64
SSD_INNER = SSD_HEADS * SSD_HEAD_DIM
CONV_K = 5
BC_WIDTH = SSD_GROUPS * SSD_STATE
CONV_CH = SSD_INNER + 2 * BC_WIDTH
MLA_HEADS, MLA_NOPE, MLA_ROPE, MLA_V = 16, 64, 32, 64
Q_LORA, KV_LORA = 256, 128
MLA_WIDTH = MLA_HEADS * MLA_V
MLA_SCALE = (MLA_NOPE + MLA_ROPE) ** -0.5
LOG2E = 1.4426950408889634


def _cparams(*sem):
    return pltpu.CompilerParams(dimension_semantics=sem, vmem_limit_bytes=VMEM_LIMIT_BYTES)


def _silu(x):
    return x * (1.0 / (1.0 + jnp.exp(-x)))


def _dot(a, b):
    return jnp.dot(a, b, preferred_element_type=F32, precision=lax.Precision.DEFAULT)


def _dot_nt(a, b):
    return lax.dot_general(a, b, (((1,), (1,)), ((), ())), preferred_element_type=F32,
                           precision=lax.Precision.DEFAULT)


def _row_splits(m, parts=2):
    step = m // parts
    return [slice(r * step, (r + 1) * step) for r in range(parts)]


def _lane_iota(shape):
    return lax.broadcasted_iota(jnp.int32, shape, len(shape) - 1)


def _row_iota(shape):
    return lax.broadcasted_iota(jnp.int32, shape, len(shape) - 2)


def _mod_kernel(cond_ref, w_ref, b_ref, o_ref):
    s = _silu(cond_ref[...])
    o_ref[...] = _dot(s.astype(BF16), w_ref[...].astype(BF16)) + b_ref[...]


def _modulation(conds, ada_w, ada_b):
    depth, d, d3 = ada_w.shape
    tn = 768
    return pl.pallas_call(
        _mod_kernel,
        grid=(depth, d3 // tn),
        in_specs=[pl.BlockSpec((SUBLANES, d), lambda l, j: (0, 0)),
                  pl.BlockSpec((None, d, tn), lambda l, j: (l, 0, j)),
                  pl.BlockSpec((None, 1, tn), lambda l, j: (l, 0, j))],
        out_specs=pl.BlockSpec((None, SUBLANES, tn), lambda l, j: (l, 0, j)),
        out_shape=jax.ShapeDtypeStruct((depth, SUBLANES, d3), F32),
        compiler_params=_cparams("arbitrary", "arbitrary"),
        name="modulation",
    )(conds, ada_w, ada_b.reshape(depth, 1, d3))


def _norm_mod(x, nw, scale, shift):
    ms = jnp.mean(x * x, axis=-1, keepdims=True)
    y = (x * lax.rsqrt(ms + EPS)) * nw
    return y * (1.0 + scale) + shift


def _rope(x, cos, sin_signed, half_period):
    outs = []
    first = (_lane_iota((x.shape[0], LANES)) & (2 * half_period - 1)) < half_period
    for j in range(x.shape[1] // LANES):
        xj = x[:, j * LANES:(j + 1) * LANES]
        up = pltpu.roll(xj, LANES - half_period, 1)
        dn = pltpu.roll(xj, half_period, 1)
        outs.append(xj * cos + jnp.where(first, up, dn) * sin_signed)
    return outs[0] if len(outs) == 1 else jnp.concatenate(outs, axis=1)


AB_COLS = (A_WIDTH, A_KV_WIDTH, A_KV_WIDTH, A_WIDTH, SSD_INNER, CONV_CH, LANES)


def _ab_in_kernel(*refs, rope, seq_len):
    if rope:
        (x_ref, nw_ref, sc_ref, sh_ref, w_ref, cos_ref, sin_ref,
         q_ref, k_ref, v_ref, g_ref, z_ref, xbc_ref, dt_ref) = refs
        kvt_refs = {}
    else:
        (x_ref, nw_ref, sc_ref, sh_ref, w_ref,
         q_ref, k_ref, v_ref, g_ref, z_ref, xbc_ref, dt_ref, kt_ref, vt_ref) = refs
        kvt_refs = {1: kt_ref, 2: vt_ref}
    offs = np.concatenate([[0], np.cumsum(AB_COLS)])
    outs = (q_ref, k_ref, v_ref, g_ref, z_ref, xbc_ref, dt_ref)
    for rows in _row_splits(x_ref.shape[0]):
        h = _norm_mod(x_ref[rows, :], nw_ref[...], sc_ref[...], sh_ref[...]).astype(BF16)
        for i, o_ref in enumerate(outs):
            y = _dot(h, w_ref[:, int(offs[i]):int(offs[i + 1])])
            if rope and i in (0, 1):
                y = _rope(y, cos_ref[rows, :], sin_ref[rows, :], A_HEAD_DIM // 4)
            o_ref[rows, :] = y.astype(o_ref.dtype)
            if i in kvt_refs:
                for s in range(rows.start // seq_len, rows.stop // seq_len):
                    kvt_refs[i][s] = y[s * seq_len - rows.start:(s + 1) * seq_len - rows.start, :].T


def _ab_in(x, nw, scale, shift, w, rope_tabs, seq_len, tm):
    t, d = x.shape
    per_seq = seq_len // tm if scale.shape[0] > 1 else None
    row = lambda i: (i, 0)
    mod = (lambda i: (i // per_seq, 0, 0)) if per_seq else (lambda i: (0, 0, 0))
    const = lambda i: (0, 0)
    in_specs = [pl.BlockSpec((tm, d), row),
                pl.BlockSpec((1, d), const),
                pl.BlockSpec((None, 1, d), mod),
                pl.BlockSpec((None, 1, d), mod),
                pl.BlockSpec(w.shape, const)]
    args = [x, nw, scale, shift, w]
    if rope_tabs is not None:
        nt = seq_len // tm
        pos = lambda i: (i % nt, 0)
        in_specs += [pl.BlockSpec((tm, LANES), pos), pl.BlockSpec((tm, LANES), pos)]
        args += list(rope_tabs)
    dts = (BF16, F32, F32, BF16, BF16, BF16, F32)
    out_specs = [pl.BlockSpec((tm, c), row) for c in AB_COLS]
    out_shape = [jax.ShapeDtypeStruct((t, c), dt) for c, dt in zip(AB_COLS, dts)]
    if rope_tabs is None:
        assert (tm // 2) % seq_len == 0
        spt = tm // seq_len
        out_specs += [pl.BlockSpec((spt, A_KV_WIDTH, seq_len), lambda i: (i, 0, 0))] * 2
        out_shape += [jax.ShapeDtypeStruct((t // seq_len, A_KV_WIDTH, seq_len), F32)] * 2
    return pl.pallas_call(
        functools.partial(_ab_in_kernel, rope=rope_tabs is not None, seq_len=seq_len),
        grid=(t // tm,),
        in_specs=in_specs,
        out_specs=out_specs,
        out_shape=out_shape,
        compiler_params=_cparams("arbitrary"),
        name="ab_in_proj",
    )(*args)


def _conv_kernel(prev_ref, main_ref, next_ref, w_ref, b_ref, o_ref, *, tiles_per_seq):
    i = pl.program_id(0) % tiles_per_seq
    tm = main_ref.shape[0]
    prev = jnp.where(i > 0, prev_ref[...].astype(F32), 0.0)
    nxt = jnp.where(i < tiles_per_seq - 1, next_ref[...].astype(F32), 0.0)
    ext = jnp.concatenate([prev, main_ref[...].astype(F32), nxt], axis=0)
    first = SUBLANES - CONV_K // 2
    acc = b_ref[...] + ext[first:first + tm] * w_ref[0:1, :]
    for j in range(1, CONV_K):
        acc = acc + ext[first + j:first + j + tm] * w_ref[j:j + 1, :]
    o_ref[...] = _silu(acc).astype(o_ref.dtype)


def _conv(xbc, w, b, seq_len, tm):
    t, c = xbc.shape
    tps = seq_len // tm
    hb = tm // SUBLANES
    nblk8 = t // SUBLANES
    return pl.pallas_call(
        functools.partial(_conv_kernel, tiles_per_seq=tps),
        grid=(t // tm,),
        in_specs=[pl.BlockSpec((SUBLANES, c), lambda i: (jnp.maximum(i * hb - 1, 0), 0)),
                  pl.BlockSpec((tm, c), lambda i: (i, 0)),
                  pl.BlockSpec((SUBLANES, c), lambda i: (jnp.minimum((i + 1) * hb, nblk8 - 1), 0)),
                  pl.BlockSpec((SUBLANES, c), lambda i: (0, 0)),
                  pl.BlockSpec((1, c), lambda i: (0, 0))],
        out_specs=pl.BlockSpec((tm, c), lambda i: (i, 0)),
        out_shape=jax.ShapeDtypeStruct((t, c), BF16),
        compiler_params=_cparams("arbitrary"),
        name="ssd_conv",
    )(xbc, xbc, xbc, w, b)


def _ssd_init(h0_ref, h_scr, has_h0):
    if not has_h0:
        h_scr[...] = jnp.zeros_like(h_scr)
        return
    for pr in range(SSD_HEADS // 2):
        both = jnp.concatenate([h0_ref[2 * pr], h0_ref[2 * pr + 1]], axis=0)
        both = jnp.concatenate([both, jnp.zeros_like(both)], axis=1)
        st = both.T[0:SSD_STATE]
        lo_st = _lane_iota(st.shape) < SSD_HEAD_DIM
        h_scr[2 * pr] = jnp.where(lo_st, st, 0.0)
        h_scr[2 * pr + 1] = jnp.where(lo_st, 0.0, st)


def _ssd_final(h_scr, hout_ref):
    for pr in range(SSD_HEADS // 2):
        st = h_scr[2 * pr] + h_scr[2 * pr + 1]
        st = jnp.concatenate([st, jnp.zeros_like(st)], axis=0).T
        hout_ref[2 * pr] = st[0:SSD_HEAD_DIM, 0:SSD_STATE]
        hout_ref[2 * pr + 1] = st[SSD_HEAD_DIM:2 * SSD_HEAD_DIM, 0:SSD_STATE]


def _ssd_prelude(b_ref, c_ref, dt_ref, par_ref, direction):
    lane = _lane_iota((CHUNK, LANES))
    rowi = _row_iota((CHUNK, LANES))
    dtp_in = dt_ref[...] + par_ref[0:1, :]
    dtp = jnp.maximum(dtp_in, 0.0) + jnp.log(1.0 + jnp.exp(-jnp.abs(dtp_in)))
    la = dtp * (-jnp.exp(par_ref[1:2, :]))
    causal = (rowi >= lane) if direction == 0 else (rowi <= lane)
    tri = jnp.where(causal, 1.0, 0.0).astype(BF16)
    p1 = la.astype(BF16)
    r1 = la - p1.astype(F32)
    p2 = r1.astype(BF16)
    p3 = (r1 - p2.astype(F32)).astype(BF16)
    cum = _dot(tri, p1) + _dot(tri, p2) + _dot(tri, p3)
    cum_t = cum.T
    dtp_t = dtp.T
    end = CHUNK - 1 if direction == 0 else 0
    w_t = jnp.exp(cum_t[:, end:end + 1] - cum_t) * dtp_t
    lrow_t = cum_t - jnp.log(dtp_t)
    bm = b_ref[...]
    cm = c_ref[...]
    cb, cmask = [], []
    for g in range(SSD_GROUPS):
        in_g = (lane >= g * SSD_STATE) & (lane < (g + 1) * SSD_STATE)
        cg = jnp.where(in_g, cm, jnp.zeros_like(cm))
        cmask.append(cg.astype(F32))
        cb.append(_dot_nt(cg, bm))
    return dict(cum=cum, lrow_t=lrow_t, w_t=w_t, causal=causal, end=end,
                bm_t=bm.astype(F32).T, cb=cb, cmask=cmask, lo_half=lane < SSD_HEAD_DIM)


def _ssd_head_pair(pre, pr, x_ref, dskip_ref, y_ref, h_scr, direction):
    hp = SSD_HEADS // SSD_GROUPS
    cum, lrow_t, w_t, end = pre["cum"], pre["lrow_t"], pre["w_t"], pre["end"]
    lo_half = pre["lo_half"]
    xp = x_ref[:, pr * LANES:(pr + 1) * LANES]
    y_pair = None
    for e in range(2):
        h = 2 * pr + e
        g = h // hp
        col = direction * SSD_HEADS + h
        xe = jnp.where(lo_half if e == 0 else ~lo_half, xp, jnp.zeros_like(xp))
        cumcol = jnp.broadcast_to(cum[:, col:col + 1], (CHUNK, LANES))
        diff = cumcol - lrow_t[col:col + 1, :]
        sc = pre["cb"][g] * jnp.exp(jnp.where(pre["causal"], diff, NEG_INF))
        ce = pre["cmask"][g] * jnp.exp(cumcol)
        hs = h_scr[h]
        hsb = hs.astype(BF16)
        zpad = jnp.zeros_like(hsb)
        bw = (pre["bm_t"][g * SSD_STATE:(g + 1) * SSD_STATE, :] * w_t[col:col + 1, :]).astype(BF16)
        a = jnp.concatenate([jnp.concatenate([sc.astype(BF16), ce.astype(BF16)], axis=1),
                             jnp.concatenate([bw, jnp.zeros_like(bw)], axis=1)], axis=0)
        wmat = jnp.concatenate([xe, hsb, zpad] if g == 0 else [xe, zpad, hsb], axis=0)
        res = _dot(a, wmat)
        ye = res[0:CHUNK]
        y_pair = ye if y_pair is None else y_pair + ye
        cdec = jnp.exp(cumcol[end:end + 1, :])
        h_scr[h] = hs * cdec + res[CHUNK:CHUNK + SSD_STATE]
    if direction == 0:
        y_pair = y_pair + dskip_ref[:, pr * LANES:(pr + 1) * LANES] * xp.astype(F32)
    y_ref[:, pr * LANES:(pr + 1) * LANES] = y_pair.astype(y_ref.dtype)


def _ssd_kernel(xf_ref, bf_ref, cf_ref, dtf_ref, xb_ref, bb_ref, cb_ref, dtb_ref, par_ref, dskip_ref,
                h0f_ref, h0b_ref, yf_ref, yb_ref, houtf_ref, houtb_ref, hf_scr, hb_scr, *, nc, has_h0):
    ci = pl.program_id(1)

    @pl.when(ci == 0)
    def _init():
        _ssd_init(h0f_ref, hf_scr, has_h0)
        _ssd_init(h0b_ref, hb_scr, has_h0)

    pre_f = _ssd_prelude(bf_ref, cf_ref, dtf_ref, par_ref, 0)
    pre_b = _ssd_prelude(bb_ref, cb_ref, dtb_ref, par_ref, 1)
    for pr in range(SSD_HEADS // 2):
        _ssd_head_pair(pre_f, pr, xf_ref, dskip_ref, yf_ref, hf_scr, 0)
        _ssd_head_pair(pre_b, pr, xb_ref, dskip_ref, yb_ref, hb_scr, 1)

    @pl.when(ci == nc - 1)
    def _final():
        _ssd_final(hf_scr, houtf_ref)
        _ssd_final(hb_scr, houtb_ref)


def _ssd(xbc, dt, par, dskip, h0f, h0b, nb, seq_len):
    t = xbc.shape[0]
    nc = seq_len // CHUNK
    fwd = lambda b, c: b * nc + c
    bwd = lambda b, c: b * nc + nc - 1 - c
    xcols = SSD_INNER // LANES
    has_h0 = h0f is not None
    if not has_h0:
        h0f = h0b = jnp.zeros((1, SSD_HEADS, SSD_HEAD_DIM, SSD_STATE), F32)
    h0_map = (lambda b, c: (b, 0, 0, 0)) if has_h0 else (lambda b, c: (0, 0, 0, 0))
    st_shape = (None, SSD_HEADS, SSD_HEAD_DIM, SSD_STATE)
    carry_shape = (None, SSD_HEADS, SSD_STATE, LANES)
    per_b = lambda b, c: (b, 0, 0, 0)

    def chunk_specs(cidx):
        return [pl.BlockSpec((CHUNK, SSD_INNER), lambda b, c: (cidx(b, c), 0)),
                pl.BlockSpec((CHUNK, LANES), lambda b, c: (cidx(b, c), xcols)),
                pl.BlockSpec((CHUNK, LANES), lambda b, c: (cidx(b, c), xcols + 1)),
                pl.BlockSpec((CHUNK, LANES), lambda b, c: (cidx(b, c), 0))]

    outs = pl.pallas_call(
        functools.partial(_ssd_kernel, nc=nc, has_h0=has_h0),
        grid=(nb, nc),
        in_specs=chunk_specs(fwd) + chunk_specs(bwd) + [
            pl.BlockSpec((SUBLANES, LANES), lambda b, c: (0, 0)),
            pl.BlockSpec((1, SSD_INNER), lambda b, c: (0, 0)),
            pl.BlockSpec(st_shape, h0_map), pl.BlockSpec(st_shape, h0_map)],
        out_specs=[pl.BlockSpec((CHUNK, SSD_INNER), lambda b, c: (fwd(b, c), 0)),
                   pl.BlockSpec((CHUNK, SSD_INNER), lambda b, c: (bwd(b, c), 0)),
                   pl.BlockSpec(st_shape, per_b), pl.BlockSpec(st_shape, per_b),
                   pl.BlockSpec(carry_shape, per_b), pl.BlockSpec(carry_shape, per_b)],
        out_shape=[jax.ShapeDtypeStruct((t, SSD_INNER), BF16), jax.ShapeDtypeStruct((t, SSD_INNER), BF16),
                   jax.ShapeDtypeStruct((nb, SSD_HEADS, SSD_HEAD_DIM, SSD_STATE), F32),
                   jax.ShapeDtypeStruct((nb, SSD_HEADS, SSD_HEAD_DIM, SSD_STATE), F32),
                   jax.ShapeDtypeStruct((nb, SSD_HEADS, SSD_STATE, LANES), F32),
                   jax.ShapeDtypeStruct((nb, SSD_HEADS, SSD_STATE, LANES), F32)],
        compiler_params=_cparams("arbitrary", "arbitrary"),
        name="ssd_scan_bidir",
    )(xbc, xbc, xbc, dt, xbc, xbc, xbc, dt, par, dskip, h0f, h0b)
    return outs[:4]


def _kv_variants(k, v):
    lane = _lane_iota(k.shape)
    lo = lane < A_HEAD_DIM
    k_sw = pltpu.roll(k, A_HEAD_DIM, 1)
    v_sw = pltpu.roll(v, A_HEAD_DIM, 1)
    ks, vs = {}, {}
    for j in range(A_KV_HEADS):
        for e in range(2):
            src_k, src_v = (k, v) if j == e else (k_sw, v_sw)
            half = lo if e == 0 else ~lo
            ks[j, e] = jnp.where(half, src_k, 0.0).astype(BF16)
            vs[j, e] = src_v.astype(BF16)
    return ks, vs


def _gqa_heads(q_ref, g_ref, o_ref, sink_ref, ks, vs, valid):
    m = q_ref.shape[0]
    assert A_HEADS // A_KV_HEADS == 4
    lo = _lane_iota((m, LANES)) < A_HEAD_DIM
    upper_rows = _row_iota((2 * m, 1)) >= m
    scale = jnp.asarray(A_SCALE, BF16)
    outs = {}
    for j in range(A_KV_HEADS):
        rows = jnp.concatenate([q_ref[:, 2 * j * LANES:(2 * j + 1) * LANES],
                                q_ref[:, (2 * j + 1) * LANES:(2 * j + 2) * LANES]], axis=0) * scale
        for e in range(2):
            s = _dot_nt(rows, ks[j, e])
            if valid is not None:
                nloc = valid.shape[1]
                s = jnp.concatenate([jnp.where(valid, s[:, :nloc], NEG_INF), s[:, nloc:]], axis=1)
            sink = jnp.where(upper_rows, sink_ref[4 * j + 2 + e], sink_ref[4 * j + e])
            mx = jnp.maximum(jnp.max(s, axis=1, keepdims=True), sink)
            p = jnp.exp(s - mx)
            den = jnp.sum(p, axis=1, keepdims=True) + jnp.exp(sink - mx)
            outs[j, e] = _dot(p.astype(BF16), vs[j, e]) / den
    for pr in range(A_HEADS // 2):
        j, half = pr // 2, pr % 2
        r = slice(half * m, (half + 1) * m)
        attn = jnp.where(lo, outs[j, 0][r], outs[j, 1][r])
        gate = g_ref[:, pr * LANES:(pr + 1) * LANES].astype(F32)
        o_ref[:, pr * LANES:(pr + 1) * LANES] = (attn * _silu(gate)).astype(o_ref.dtype)


def _attn_ctx_kernel(sink_ref, q_ref, k_ref, v_ref, g_ref, o_ref):
    ks, vs = _kv_variants(k_ref[...], v_ref[...])
    _gqa_heads(q_ref, g_ref, o_ref, sink_ref, ks, vs, None)


def _attn_ctx(sink, q, k, v, g, seq_len):
    t = q.shape[0]
    row = lambda b: (b, 0)
    return pl.pallas_call(
        _attn_ctx_kernel,
        grid=(t // seq_len,),
        in_specs=[pl.BlockSpec(memory_space=pltpu.SMEM),
                  pl.BlockSpec((seq_len, A_WIDTH), row),
                  pl.BlockSpec((seq_len, A_KV_WIDTH), row),
                  pl.BlockSpec((seq_len, A_KV_WIDTH), row),
                  pl.BlockSpec((seq_len, A_WIDTH), row)],
        out_specs=pl.BlockSpec((seq_len, A_WIDTH), row),
        out_shape=jax.ShapeDtypeStruct((t, A_WIDTH), BF16),
        compiler_params=_cparams("arbitrary"),
        name="swa_context_attention",
    )(sink, q, k, v, g)


BAND_Q = 256


def _attn_band_kernel(sink_ref, q_ref, kp_ref, kc_ref, kn_ref, vp_ref, vc_ref, vn_ref, kx_ref, vx_ref, g_ref,
                      o_ref, *, nsteps):
    n = pl.program_id(1)
    tq = q_ref.shape[0]
    blk = WINDOW_BLK
    k = jnp.concatenate([kp_ref[...], kc_ref[...], kn_ref[...], kx_ref[...]], axis=0)
    v = jnp.concatenate([vp_ref[...], vc_ref[...], vn_ref[...], vx_ref[...]], axis=0)
    nloc = tq + 2 * blk
    qi = _row_iota((2 * tq, nloc)) & (tq - 1)
    col = _lane_iota((2 * tq, nloc))
    rel = col - blk - qi
    valid = ((rel >= -blk) & (rel <= blk) & ((col >= blk) | (n > 0)) & ((col < nloc - blk) | (n < nsteps - 1)))
    ks, vs = _kv_variants(k, v)
    _gqa_heads(q_ref, g_ref, o_ref, sink_ref, ks, vs, valid)


def _attn_band(sink, q, k, v, k_ctx, v_ctx, g, nb, seq_len):
    t = q.shape[0]
    blk = WINDOW_BLK
    tq = BAND_Q
    assert tq & (tq - 1) == 0 and tq % blk == 0 and seq_len % tq == 0
    per = tq // blk
    nblk = seq_len // blk
    nsteps = seq_len // tq
    nctx = k_ctx.shape[1]
    cur = lambda b, n: (b * nsteps + n, 0)
    prv = lambda b, n: (b * nblk + jnp.maximum(n * per - 1, 0), 0)
    nxt = lambda b, n: (b * nblk + jnp.minimum((n + 1) * per, nblk - 1), 0)
    ctx = lambda b, n: (b, 0, 0)
    edge = lambda f: pl.BlockSpec((blk, A_KV_WIDTH), f)
    mid = pl.BlockSpec((tq, A_KV_WIDTH), cur)
    return pl.pallas_call(
        functools.partial(_attn_band_kernel, nsteps=nsteps),
        grid=(nb, nsteps),
        in_specs=[pl.BlockSpec(memory_space=pltpu.SMEM),
                  pl.BlockSpec((tq, A_WIDTH), cur),
                  edge(prv), mid, edge(nxt),
                  edge(prv), mid, edge(nxt),
                  pl.BlockSpec((None, nctx, A_KV_WIDTH), ctx),
                  pl.BlockSpec((None, nctx, A_KV_WIDTH), ctx),
                  pl.BlockSpec((tq, A_WIDTH), cur)],
        out_specs=pl.BlockSpec((tq, A_WIDTH), cur),
        out_shape=jax.ShapeDtypeStruct((t, A_WIDTH), BF16),
        compiler_params=_cparams("arbitrary", "arbitrary"),
        name="swa_banded_attention",
    )(sink, q, k, k, k, v, v, v, k_ctx, v_ctx, g)


def _ab_out_kernel(a_ref, yf_ref, yb_ref, z_ref, gnw_ref, w_ref, x_ref, gate_ref, o_ref):
    for rows in _row_splits(x_ref.shape[0]):
        y = (yf_ref[rows, :].astype(F32) + yb_ref[rows, :].astype(F32)) * _silu(z_ref[rows, :].astype(F32))
        ms = jnp.mean(y * y, axis=-1, keepdims=True)
        s = ((y * lax.rsqrt(ms + EPS)) * gnw_ref[...]).astype(BF16)
        out = _dot(a_ref[rows, :], w_ref[0:A_WIDTH, :]) + _dot(s, w_ref[A_WIDTH:, :])
        o_ref[rows, :] = x_ref[rows, :] + gate_ref[...] * out


def _ab_out(a, yf, yb, z, gnw, w, x, gate, seq_len, tm):
    t, d = x.shape
    per_seq = seq_len // tm if gate.shape[0] > 1 else None
    row = lambda i: (i, 0)
    const = lambda i: (0, 0)
    mod = (lambda i: (i // per_seq, 0, 0)) if per_seq else (lambda i: (0, 0, 0))
    return pl.pallas_call(
        _ab_out_kernel,
        grid=(t // tm,),
        in_specs=[pl.BlockSpec((tm, A_WIDTH), row),
                  pl.BlockSpec((tm, SSD_INNER), row),
                  pl.BlockSpec((tm, SSD_INNER), row),
                  pl.BlockSpec((tm, SSD_INNER), row),
                  pl.BlockSpec((1, SSD_INNER), const),
                  pl.BlockSpec(w.shape, const),
                  pl.BlockSpec((tm, d), row),
                  pl.BlockSpec((None, 1, d), mod)],
        out_specs=pl.BlockSpec((tm, d), row),
        out_shape=jax.ShapeDtypeStruct((t, d), F32),
        compiler_params=_cparams("arbitrary"),
        name="ab_out_proj",
    )(a, yf, yb, z, gnw, w, x, gate)


MLA_COLS = (Q_LORA, KV_LORA, LANES, MLA_WIDTH)


def _mla_in_kernel(*refs, rope):
    if rope:
        (x_ref, nw_ref, sc_ref, sh_ref, w_ref, qnw_ref, kvnw_ref, wuq_ref, cos_ref, sin_ref,
         qn_ref, qpe_ref, ckv_ref, kpe_ref, g_ref, ckv32_ref, kpe32_ref) = refs
    else:
        (x_ref, nw_ref, sc_ref, sh_ref, w_ref, qnw_ref, kvnw_ref, wuq_ref,
         qn_ref, qpe_ref, ckv_ref, kpe_ref, g_ref, ckv32_ref, kpe32_ref) = refs
    offs = [int(o) for o in np.concatenate([[0], np.cumsum(MLA_COLS)])]
    nope_w = MLA_HEADS * MLA_NOPE

    def rms(u, w):
        return (u * lax.rsqrt(jnp.mean(u * u, axis=-1, keepdims=True) + EPS)) * w

    for rows in _row_splits(x_ref.shape[0]):
        h = _norm_mod(x_ref[rows, :], nw_ref[...], sc_ref[...], sh_ref[...]).astype(BF16)
        cq = _dot(h, w_ref[:, offs[0]:offs[1]])
        ckv = _dot(h, w_ref[:, offs[1]:offs[2]])
        kpe = _dot(h, w_ref[:, offs[2]:offs[3]])
        g_ref[rows, :] = _dot(h, w_ref[:, offs[3]:offs[4]]).astype(g_ref.dtype)
        cqn = rms(cq, qnw_ref[...]).astype(BF16)
        qn_ref[rows, :] = _dot(cqn, wuq_ref[:, 0:nope_w]).astype(qn_ref.dtype)
        qpe = _dot(cqn, wuq_ref[:, nope_w:])
        if rope:
            qpe = _rope(qpe, cos_ref[rows, :], sin_ref[rows, :], MLA_ROPE // 4)
            kpe = _rope(kpe, cos_ref[rows, :], sin_ref[rows, :], MLA_ROPE // 4)
        qpe_ref[rows, :] = qpe.astype(qpe_ref.dtype)
        ckvn = rms(ckv, kvnw_ref[...])
        ckv_ref[rows, :] = ckvn.astype(ckv_ref.dtype)
        kpe_ref[rows, :] = kpe.astype(kpe_ref.dtype)
        ckv32_ref[rows, :] = ckvn
        kpe32_ref[rows, :] = kpe


def _mla_in(x, nw, scale, shift, w, qnw, kvnw, wuq, rope_tabs, seq_len, tm):
    t, d = x.shape
    per_seq = seq_len // tm if scale.shape[0] > 1 else None
    row = lambda i: (i, 0)
    const = lambda i: (0, 0)
    mod = (lambda i: (i // per_seq, 0, 0)) if per_seq else (lambda i: (0, 0, 0))
    in_specs = [pl.BlockSpec((tm, d), row), pl.BlockSpec((1, d), const),
                pl.BlockSpec((None, 1, d), mod), pl.BlockSpec((None, 1, d), mod),
                pl.BlockSpec(w.shape, const), pl.BlockSpec((1, Q_LORA), const),
                pl.BlockSpec((1, KV_LORA), const), pl.BlockSpec(wuq.shape, const)]
    args = [x, nw, scale, shift, w, qnw, kvnw, wuq]
    if rope_tabs is not None:
        nt = seq_len // tm
        pos = lambda i: (i % nt, 0)
        in_specs += [pl.BlockSpec((tm, LANES), pos), pl.BlockSpec((tm, LANES), pos)]
        args += list(rope_tabs)
    widths = (MLA_HEADS * MLA_NOPE, MLA_HEADS * MLA_ROPE, KV_LORA, LANES, MLA_WIDTH, KV_LORA, LANES)
    dts = (BF16, BF16, BF16, BF16, BF16, F32, F32)
    return pl.pallas_call(
        functools.partial(_mla_in_kernel, rope=rope_tabs is not None),
        grid=(t // tm,),
        in_specs=in_specs,
        out_specs=[pl.BlockSpec((tm, c), row) for c in widths],
        out_shape=[jax.ShapeDtypeStruct((t, c), dt) for c, dt in zip(widths, dts)],
        compiler_params=_cparams("arbitrary"),
        name="mla_in_proj",
    )(*args)


MLA_UNIT_ROWS = 512
MLA_KEY_TILE = 256


def _mla_attn_kernel(qn_ref, qpe_ref, ckv_ref, kpe_ref, wukt_ref, wuv_ref, g_ref, o_ref, kcat_scr, v_scr, s_scr, *,
                     pairs_per_step):
    pg = pl.program_id(1)
    qb = pl.program_id(2)
    tq = qn_ref.shape[0]
    c_exp = MLA_SCALE * LOG2E

    @pl.when(qb == 0)
    def _expand():
        ckv = ckv_ref[...]
        eye = jnp.where(_row_iota((LANES, LANES)) == _lane_iota((LANES, LANES)), 1.0, 0.0).astype(BF16)
        kpe_t = _dot_nt(eye, kpe_ref[...]).astype(BF16)
        for i in range(pairs_per_step):
            kcat_scr[i, 0:LANES, :] = _dot_nt(wukt_ref[i * LANES:(i + 1) * LANES, :], ckv).astype(BF16)
            kcat_scr[i, LANES:2 * LANES, :] = kpe_t
            v_scr[i] = _dot(ckv, wuv_ref[:, i * LANES:(i + 1) * LANES]).astype(BF16)

    nkeys = kcat_scr.shape[-1]
    kt_w = min(MLA_KEY_TILE, nkeys)
    nkt = nkeys // kt_w
    ru = min(MLA_UNIT_ROWS, tq)
    lane = _lane_iota((ru, LANES))
    units = [(i, r, e) for i in range(pairs_per_step) for r in range(tq // ru) for e in range(2)]

    def lane_tiles(x):
        return [x[:, j * LANES:(j + 1) * LANES] for j in range(x.shape[1] // LANES)]

    def q_ext(u):
        i, r, e = units[u]
        rows = slice(r * ru, (r + 1) * ru)
        qn = qn_ref[rows, i * LANES:(i + 1) * LANES]
        qpe = qpe_ref[rows, (i // 2) * LANES:(i // 2 + 1) * LANES]
        slot = 2 * (i % 2) + e if pairs_per_step % 2 == 0 else 2 * ((pg * pairs_per_step + i) % 2) + e
        nope_half = (lane < MLA_NOPE) if e == 0 else (lane >= MLA_NOPE)
        return jnp.concatenate([jnp.where(nope_half, qn, jnp.zeros_like(qn)),
                                jnp.where((lane >> 5) == slot, qpe, jnp.zeros_like(qpe))], axis=1)

    def score_tile(u, st, kt):
        cols = slice(kt * kt_w, (kt + 1) * kt_w)
        s = _dot(st["q"], kcat_scr[units[u][0], :, cols])
        s_scr[u % 2, :, cols] = s
        for t in lane_tiles(s):
            st["mrun"] = t if st["mrun"] is None else jnp.maximum(st["mrun"], t)

    def value_tile(u, st, kt):
        cols = slice(kt * kt_w, (kt + 1) * kt_w)
        p = jnp.exp2((s_scr[u % 2, :, cols] - st["m"]) * c_exp)
        for t in lane_tiles(p):
            st["lrun"] = t if st["lrun"] is None else st["lrun"] + t
        pv = _dot(p.astype(BF16), v_scr[units[u][0], cols, :])
        st["acc"] = pv if st["acc"] is None else st["acc"] + pv

    def finish(u, st, done):
        i, r, e = units[u]
        den = jnp.sum(st["lrun"], axis=1, keepdims=True)
        done[e] = st["acc"] / den
        if e == 1:
            rows = slice(r * ru, (r + 1) * ru)
            attn = jnp.where(lane < MLA_V, done[0], done[1])
            gate = g_ref[rows, i * LANES:(i + 1) * LANES].astype(F32)
            o_ref[rows, i * LANES:(i + 1) * LANES] = (attn * _silu(gate)).astype(o_ref.dtype)

    done = {}
    cur = None
    for u in range(len(units) + 1):
        nxt = dict(q=q_ext(u), mrun=None) if u < len(units) else None
        for kt in range(nkt):
            if nxt is not None:
                score_tile(u, nxt, kt)
            if cur is not None:
                value_tile(u - 1, cur, kt)
        if cur is not None:
            finish(u - 1, cur, done)
        if nxt is not None:
            nxt.update(m=jnp.max(nxt["mrun"], axis=1, keepdims=True), lrun=None, acc=None)
        cur = nxt


def _mla_attn(qn, qpe, ckv_keys, kpe_keys, wukt, wuv, g, nb, seq_len, tq, pairs_per_step):
    t = qn.shape[0]
    nkeys = ckv_keys.shape[1]
    npairs = MLA_HEADS // 2
    ngrp = npairs // pairs_per_step
    nq = seq_len // tq
    wp = pairs_per_step * LANES
    if pairs_per_step % 2 == 0:
        wpe = wp // 2
        pe_map = lambda b, p, i: (b * nq + i, p)
    else:
        wpe = LANES
        pe_map = lambda b, p, i: (b * nq + i, p // 2)
    qmap = lambda b, p, i: (b * nq + i, p)
    return pl.pallas_call(
        functools.partial(_mla_attn_kernel, pairs_per_step=pairs_per_step),
        grid=(nb, ngrp, nq),
        in_specs=[pl.BlockSpec((tq, wp), qmap),
                  pl.BlockSpec((tq, wpe), pe_map),
                  pl.BlockSpec((None, nkeys, KV_LORA), lambda b, p, i: (b, 0, 0)),
                  pl.BlockSpec((None, nkeys, LANES), lambda b, p, i: (b, 0, 0)),
                  pl.BlockSpec((wp, KV_LORA), lambda b, p, i: (p, 0)),
                  pl.BlockSpec((KV_LORA, wp), lambda b, p, i: (0, p)),
                  pl.BlockSpec((tq, wp), qmap)],
        out_specs=pl.BlockSpec((tq, wp), qmap),
        out_shape=jax.ShapeDtypeStruct((t, MLA_WIDTH), BF16),
        scratch_shapes=[pltpu.VMEM((pairs_per_step, 2 * LANES, nkeys), BF16),
                        pltpu.VMEM((pairs_per_step, nkeys, LANES), BF16),
                        pltpu.VMEM((2, min(MLA_UNIT_ROWS, tq), nkeys), F32)],
        compiler_params=_cparams("arbitrary", "arbitrary", "arbitrary"),
        name="mla_attention",
    )(qn, qpe, ckv_keys, kpe_keys, wukt, wuv, g)


def _mla_attn_seq_kernel(qn_ref, qpe_ref, ckv_ref, kpe_ref, wukt_ref, wuv_ref, g_ref, o_ref,
                         kcat_scr, v_scr, s_scr, m_scr, half_scr):
    pair = pl.program_id(1)
    seq = qn_ref.shape[0]
    nkeys = kcat_scr.shape[-1]
    ru, kt_w = MLA_UNIT_ROWS, MLA_KEY_TILE
    nkt = nkeys // kt_w
    nrb = seq // ru
    c_exp = MLA_SCALE * LOG2E
    lane = _lane_iota((ru, LANES))

    ckv = ckv_ref[...]
    eye = jnp.where(_row_iota((LANES, LANES)) == _lane_iota((LANES, LANES)), 1.0, 0.0).astype(BF16)
    kcat_scr[0:LANES, :] = _dot_nt(wukt_ref[...], ckv).astype(BF16)
    kcat_scr[LANES:2 * LANES, :] = _dot_nt(eye, kpe_ref[...]).astype(BF16)
    v_scr[:, 0:LANES] = _dot(ckv, wuv_ref[...]).astype(BF16)
    v_scr[:, LANES:2 * LANES] = jnp.ones((nkeys, LANES), BF16)

    def rows_of(rb):
        return slice(rb * ru, (rb + 1) * ru) if isinstance(rb, int) else pl.ds(pl.multiple_of(rb * ru, ru), ru)

    def q_ext(rb, e):
        rows = rows_of(rb)
        qn = qn_ref[rows, :]
        qpe = qpe_ref[rows, :]
        nope_half = (lane < MLA_NOPE) if e == 0 else (lane >= MLA_NOPE)
        slot = 2 * (pair % 2) + e
        return jnp.concatenate([jnp.where(nope_half, qn, jnp.zeros_like(qn)),
                                jnp.where((lane >> 5) == slot, qpe, jnp.zeros_like(qpe))], axis=1)

    def phase(score, value):
        q = q_ext(*score) if score is not None else None
        mrun = acc = None
        for kt in range(nkt):
            cols = slice(kt * kt_w, (kt + 1) * kt_w)
            if score is not None:
                s = _dot(q, kcat_scr[:, cols])
                s_scr[score[1], :, cols] = s
                for j in range(kt_w // LANES):
                    t = s[:, j * LANES:(j + 1) * LANES]
                    mrun = t if mrun is None else jnp.maximum(mrun, t)
            if value is not None:
                mb = m_scr[value]
                p = jnp.concatenate(
                    [jnp.exp2((s_scr[value, :, kt * kt_w + j * LANES:kt * kt_w + (j + 1) * LANES] - mb) * c_exp)
                     for j in range(kt_w // LANES)], axis=1)
                pv = _dot(p.astype(BF16), v_scr[cols, :])
                acc = pv if acc is None else acc + pv
        if score is not None:
            m_scr[score[1]] = jnp.broadcast_to(jnp.max(mrun, axis=1, keepdims=True), (ru, LANES))
        return None if acc is None else acc[:, 0:LANES] / acc[:, LANES:2 * LANES]

    phase((0, 0), None)

    def row_block(rb, carry):
        half_scr[...] = phase((rb, 1), 0)
        out1 = phase((jnp.minimum(rb + 1, nrb - 1), 0), 1)
        rows = rows_of(rb)
        attn = jnp.where(lane < MLA_V, half_scr[...], out1)
        o_ref[rows, :] = (attn * _silu(g_ref[rows, :].astype(F32))).astype(o_ref.dtype)
        return carry

    lax.fori_loop(0, nrb, row_block, 0)


def _mla_attn_seq(qn, qpe, ckv_keys, kpe_keys, wukt, wuv, g, nb, seq_len):
    t = qn.shape[0]
    nkeys = ckv_keys.shape[1]
    assert seq_len % MLA_UNIT_ROWS == 0 and nkeys % MLA_KEY_TILE == 0
    qmap = lambda b, p: (b, p)
    return pl.pallas_call(
        _mla_attn_seq_kernel,
        grid=(nb, MLA_HEADS // 2),
        in_specs=[pl.BlockSpec((seq_len, LANES), qmap),
                  pl.BlockSpec((seq_len, LANES), lambda b, p: (b, p // 2)),
                  pl.BlockSpec((None, nkeys, KV_LORA), lambda b, p: (b, 0, 0)),
                  pl.BlockSpec((None, nkeys, LANES), lambda b, p: (b, 0, 0)),
                  pl.BlockSpec((LANES, KV_LORA), lambda b, p: (p, 0)),
                  pl.BlockSpec((KV_LORA, LANES), lambda b, p: (0, p)),
                  pl.BlockSpec((seq_len, LANES), qmap)],
        out_specs=pl.BlockSpec((seq_len, LANES), qmap),
        out_shape=jax.ShapeDtypeStruct((t, MLA_WIDTH), BF16),
        scratch_shapes=[pltpu.VMEM((2 * LANES, nkeys), BF16),
                        pltpu.VMEM((nkeys, 2 * LANES), BF16),
                        pltpu.VMEM((2, MLA_UNIT_ROWS, nkeys), F32),
                        pltpu.VMEM((2, MLA_UNIT_ROWS, LANES), F32),
                        pltpu.VMEM((MLA_UNIT_ROWS, LANES), F32)],
        compiler_params=_cparams("arbitrary", "arbitrary"),
        name="mla_attention_seq",
    )(qn, qpe, ckv_keys, kpe_keys, wukt, wuv, g)


def _mla_out_kernel(a_ref, w_ref, x_ref, gate_ref, fw_ref, o_ref):
    for rows in _row_splits(x_ref.shape[0]):
        xn = x_ref[rows, :] + gate_ref[...] * _dot(a_ref[rows, :], w_ref[...])
        ms = jnp.mean(xn * xn, axis=-1, keepdims=True)
        o_ref[rows, :] = (xn * lax.rsqrt(ms + EPS)) * fw_ref[...]


def _mla_out(a, w, x, gate, fw, seq_len, tm):
    t, d = x.shape
    per_seq = seq_len // tm if gate.shape[0] > 1 else None
    row = lambda i: (i, 0)
    const = lambda i: (0, 0)
    mod = (lambda i: (i // per_seq, 0, 0)) if per_seq else (lambda i: (0, 0, 0))
    return pl.pallas_call(
        _mla_out_kernel,
        grid=(t // tm,),
        in_specs=[pl.BlockSpec((tm, MLA_WIDTH), row), pl.BlockSpec(w.shape, const),
                  pl.BlockSpec((tm, d), row), pl.BlockSpec((None, 1, d), mod), pl.BlockSpec((1, d), const)],
        out_specs=pl.BlockSpec((tm, d), row),
        out_shape=jax.ShapeDtypeStruct((t, d), F32),
        compiler_params=_cparams("arbitrary"),
        name="mla_out_proj_final_norm",
    )(a, w, x, gate, fw)


def _rope_tables(length, dim):
    rows = length // GRID_W
    row = jnp.repeat(jnp.arange(rows), GRID_W).astype(F32)
    col = jnp.tile(jnp.arange(GRID_W), rows).astype(F32)
    nf = dim // 4
    inv = 1.0 / (ROPE_BASE ** (jnp.arange(nf, dtype=F32) / nf))
    ar = row[:, None] * inv[None, :]
    ac = col[:, None] * inv[None, :]
    ang = jnp.concatenate([ar, ar, ac, ac], axis=-1)
    sign = jnp.tile(jnp.concatenate([-jnp.ones((nf,), F32), jnp.ones((nf,), F32)]), 2)
    reps = LANES // dim
    return jnp.tile(jnp.cos(ang), (1, reps)), jnp.tile(jnp.sin(ang) * sign, (1, reps))


def _group(x, seq_len, conds_rows, mods, wts, tabs_a, tabs_c, ctx_cache):
    nb, _, d = x.shape
    t = nb * seq_len
    latent = ctx_cache is not None
    xf = x.reshape(t, d)
    tm = 512

    def mod_rows(layer):
        m = mods[layer][conds_rows]
        sh, sc, gt = jnp.split(m[:, None, :], 3, axis=-1)
        return sh, sc, gt

    sh, sc, gt = mod_rows(0)
    q, k, v, g, z, xbc, dt, *kv_t = _ab_in(xf, wts["norm_w"][0:1], sc, sh, wts["ab_w_in"],
                                           tabs_a if latent else None, seq_len, tm)
    xbc_c = _conv(xbc, wts["ab_conv_w"], wts["ab_conv_b"], seq_len, min(tm, seq_len))
    if latent:
        k_ctx, v_ctx, s_f0, s_b0, _, _ = ctx_cache
        attn = _attn_band(wts["ab_sink"], q, k, v, k_ctx, v_ctx, g, nb, seq_len)
    else:
        s_f0 = s_b0 = None
        attn = _attn_ctx(wts["ab_sink"], q, k, v, g, seq_len)
    yf, yb, s_f, s_b = _ssd(xbc_c, dt, wts["ssd_par"], wts["ssd_dskip"], s_f0, s_b0, nb, seq_len)
    x1 = _ab_out(attn, yf, yb, z, wts["ab_gnorm_w"], wts["ab_w_out"], xf, gt, seq_len, tm)

    sh, sc, gt = mod_rows(1)
    qn, qpe, ckv, kpe, g1, ckv32, kpe32 = _mla_in(
        x1, wts["norm_w"][1:2], sc, sh, wts["mla_w_in"], wts["mla_q_norm_w"], wts["mla_kv_norm_w"],
        wts["mla_w_uq"], tabs_c if latent else None, seq_len, tm)
    ckv_keys = ckv.reshape(nb, seq_len, KV_LORA)
    kpe_keys = kpe.reshape(nb, seq_len, LANES)
    if latent:
        ckv_x, kpe_x = ctx_cache[4], ctx_cache[5]
        ckv_keys = jnp.concatenate([ckv_keys, ckv_x], axis=1)
        kpe_keys = jnp.concatenate([kpe_keys, kpe_x], axis=1)
        attn1 = _mla_attn_seq(qn, qpe, ckv_keys, kpe_keys, wts["mla_w_ukt"], wts["mla_w_uv"], g1, nb, seq_len)
    else:
        attn1 = _mla_attn(qn, qpe, ckv_keys, kpe_keys, wts["mla_w_ukt"], wts["mla_w_uv"], g1, nb, seq_len,
                          seq_len, MLA_HEADS // 2)
    y = _mla_out(attn1, wts["mla_w_out"], x1, gt, wts["final_norm_w"], seq_len, tm)
    return y.reshape(nb, seq_len, d), (kv_t, s_f, s_b, ckv32, kpe32)


def kernel(x_prompt, x_sample, cache_a_k, cache_a_v, state_ssd_fwd, state_ssd_bwd, cache_mla_ckv, cache_mla_kpe,
           c, c_ctx, ada_w, ada_b, norm_w, ab_w_in, ab_sink, ab_conv_w, ab_conv_b, ab_dt_bias, ab_a_log,
           ab_d_skip, ab_gnorm_w, ab_w_out, mla_w_in, mla_q_norm_w, mla_kv_norm_w, mla_w_uq, mla_w_ukv,
           mla_w_out, final_norm_w):
    batch, seq, d = x_prompt.shape
    dec_batch, dec_seq, _ = x_sample.shape
    assert ada_w.shape[0] == 2 and ab_w_in.shape[0] == 1 and mla_w_in.shape[0] == 1
    assert dec_batch + 1 <= SUBLANES

    conds = jnp.concatenate([c_ctx[None, :], c, jnp.zeros((SUBLANES - 1 - dec_batch, d), F32)], axis=0)
    mods = _modulation(conds, ada_w, ada_b)

    w_ab = ab_w_in[0]
    w_ab = jnp.concatenate([w_ab, jnp.zeros((d, LANES - 2 * SSD_HEADS), F32)], axis=1).astype(BF16)
    mw = mla_w_in[0]
    o_kpe = Q_LORA + KV_LORA
    w_mla = jnp.concatenate([mw[:, :o_kpe], jnp.tile(mw[:, o_kpe:o_kpe + MLA_ROPE], (1, LANES // MLA_ROPE)),
                             mw[:, o_kpe + MLA_ROPE:]], axis=1).astype(BF16)
    wuq = mla_w_uq[0].reshape(Q_LORA, MLA_HEADS, MLA_NOPE + MLA_ROPE)
    wuq = jnp.concatenate([wuq[:, :, :MLA_NOPE].reshape(Q_LORA, -1), wuq[:, :, MLA_NOPE:].reshape(Q_LORA, -1)],
                          axis=1).astype(BF16)
    wukv = mla_w_ukv[0].reshape(KV_LORA, MLA_HEADS, MLA_NOPE + MLA_V)
    wukt = wukv[:, :, :MLA_NOPE].reshape(KV_LORA, -1).T.astype(BF16)
    wuv = wukv[:, :, MLA_NOPE:].reshape(KV_LORA, -1).astype(BF16)
    pad_lanes = lambda r: jnp.concatenate([r.reshape(1, -1), jnp.zeros((1, LANES - r.size), F32)], axis=1)
    ssd_par = jnp.concatenate([pad_lanes(ab_dt_bias[0]), pad_lanes(ab_a_log[0]),
                               jnp.zeros((SUBLANES - 2, LANES), F32)], axis=0)
    wts = dict(
        norm_w=norm_w, ab_w_in=w_ab, ab_sink=ab_sink[0],
        ab_conv_w=jnp.concatenate([ab_conv_w[0], jnp.zeros((SUBLANES - CONV_K, CONV_CH), F32)], axis=0),
        ab_conv_b=ab_conv_b[0][None, :], ssd_par=ssd_par,
        ssd_dskip=jnp.repeat(ab_d_skip[0], SSD_HEAD_DIM)[None, :],
        ab_gnorm_w=ab_gnorm_w[0][None, :], ab_w_out=ab_w_out[0].astype(BF16),
        mla_w_in=w_mla, mla_q_norm_w=mla_q_norm_w[0][None, :], mla_kv_norm_w=mla_kv_norm_w[0][None, :],
        mla_w_uq=wuq, mla_w_ukt=wukt, mla_w_uv=wuv, mla_w_out=mla_w_out[0].astype(BF16),
        final_norm_w=final_norm_w[None, :],
    )
    tabs_a = _rope_tables(dec_seq, A_HEAD_DIM)
    tabs_c = _rope_tables(dec_seq, MLA_ROPE)

    y_prompt, ((k_t, v_t), s_f, s_b, ckv32, kpe32) = _group(
        x_prompt, seq, jnp.zeros((1,), jnp.int32), mods, wts, tabs_a, tabs_c, None)

    past = cache_a_k.shape[2]
    ctx_cache = (cache_a_k[:, 0].reshape(dec_batch, past, A_KV_WIDTH),
                 cache_a_v[:, 0].reshape(dec_batch, past, A_KV_WIDTH),
                 state_ssd_fwd[:, 0], state_ssd_bwd[:, 0],
                 cache_mla_ckv[:, 0].astype(BF16),
                 jnp.tile(cache_mla_kpe[:, 0], (1, 1, LANES // MLA_ROPE)).astype(BF16))
    y_sample, _ = _group(x_sample, dec_seq, 1 + jnp.arange(dec_batch), mods, wts, tabs_a, tabs_c, ctx_cache)

    cache_layout = lambda u: u.reshape(batch, 1, A_KV_HEADS, A_HEAD_DIM, seq).transpose(0, 1, 4, 2, 3)
    return (y_prompt, y_sample,
            cache_layout(k_t), cache_layout(v_t),
            s_f[:, None], s_b[:, None],
            ckv32.reshape(batch, 1, seq, KV_LORA), kpe32[:, :MLA_ROPE].reshape(batch, 1, seq, MLA_ROPE))
```

```python
import functools
import math

import jax
import jax.numpy as jnp
import numpy as np
from jax import lax
from jax.experimental import pallas as pl
from jax.experimental.pallas import tpu as pltpu

F32 = jnp.float32
BF16 = jnp.bfloat16

LANES = 128
SUBLANES = 8
VMEM_LIMIT_BYTES = 56 * 1024 * 1024

GRID_W = 64
ROPE_BASE = 10000.0
EPS = 1e-6
NEG_INF = -1e30
WINDOW_BLK = 128
CHUNK = 128
A_HEADS, A_KV_HEADS, A_HEAD_DIM = 8, 2, 64
A_WIDTH = A_HEADS * A_HEAD_DIM
A_KV_WIDTH = A_KV_HEADS * A_HEAD_DIM
A_SCALE = A_HEAD_DIM ** -0.5
SSD_HEADS, SSD_HEAD_DIM, SSD_GROUPS, SSD_STATE = 16, 64, 2, 64
SSD_INNER = SSD_HEADS * SSD_HEAD_DIM
CONV_K = 5
BC_WIDTH = SSD_GROUPS * SSD_STATE
CONV_CH = SSD_INNER + 2 * BC_WIDTH
MLA_HEADS, MLA_NOPE, MLA_ROPE, MLA_V = 16, 64, 32, 64
Q_LORA, KV_LORA = 256, 128
MLA_WIDTH = MLA_HEADS * MLA_V
MLA_SCALE = (MLA_NOPE + MLA_ROPE) ** -0.5
LOG2E = 1.4426950408889634


def _cparams(*sem):
    return pltpu.CompilerParams(dimension_semantics=sem, vmem_limit_bytes=VMEM_LIMIT_BYTES)


def _silu(x):
    return x * (1.0 / (1.0 + jnp.exp(-x)))


def _dot(a, b):
    return jnp.dot(a, b, preferred_element_type=F32, precision=lax.Precision.DEFAULT)


def _dot_nt(a, b):
    return lax.dot_general(a, b, (((1,), (1,)), ((), ())), preferred_element_type=F32,
                           precision=lax.Precision.DEFAULT)


def _row_splits(m, parts=2):
    step = m // parts
    return [slice(r * step, (r + 1) * step) for r in range(parts)]


def _lane_iota(shape):
    return lax.broadcasted_iota(jnp.int32, shape, len(shape) - 1)


def _row_iota(shape):
    return lax.broadcasted_iota(jnp.int32, shape, len(shape) - 2)


def _mod_kernel(cond_ref, w_ref, b_ref, o_ref):
    s = _silu(cond_ref[...])
    o_ref[...] = _dot(s.astype(BF16), w_ref[...].astype(BF16)) + b_ref[...]


def _modulation(conds, ada_w, ada_b):
    depth, d, d3 = ada_w.shape
    tn = 768
    return pl.pallas_call(
        _mod_kernel,
        grid=(depth, d3 // tn),
        in_specs=[pl.BlockSpec((SUBLANES, d), lambda l, j: (0, 0)),
                  pl.BlockSpec((None, d, tn), lambda l, j: (l, 0, j)),
                  pl.BlockSpec((None, 1, tn), lambda l, j: (l, 0, j))],
        out_specs=pl.BlockSpec((None, SUBLANES, tn), lambda l, j: (l, 0, j)),
        out_shape=jax.ShapeDtypeStruct((depth, SUBLANES, d3), F32),
        compiler_params=_cparams("arbitrary", "arbitrary"),
        name="modulation",
    )(conds, ada_w, ada_b.reshape(depth, 1, d3))


def _norm_mod(x, nw, scale, shift):
    ms = jnp.mean(x * x, axis=-1, keepdims=True)
    y = (x * lax.rsqrt(ms + EPS)) * nw
    return y * (1.0 + scale) + shift


def _rope(x, cos, sin_signed, half_period):
    outs = []
    first = (_lane_iota((x.shape[0], LANES)) & (2 * half_period - 1)) < half_period
    for j in range(x.shape[1] // LANES):
        xj = x[:, j * LANES:(j + 1) * LANES]
        up = pltpu.roll(xj, LANES - half_period, 1)
        dn = pltpu.roll(xj, half_period, 1)
        outs.append(xj * cos + jnp.where(first, up, dn) * sin_signed)
    return outs[0] if len(outs) == 1 else jnp.concatenate(outs, axis=1)


AB_COLS = (A_WIDTH, A_KV_WIDTH, A_KV_WIDTH, A_WIDTH, SSD_INNER, CONV_CH, LANES)


def _ab_in_kernel(*refs, rope, seq_len):
    if rope:
        (x_ref, nw_ref, sc_ref, sh_ref, w_ref, wdt_ref, cos_ref, sin_ref,
         q_ref, k_ref, v_ref, g_ref, z_ref, xbc_ref, dt_ref) = refs
        kvt_refs = {}
    else:
        (x_ref, nw_ref, sc_ref, sh_ref, w_ref, wdt_ref,
         q_ref, k_ref, v_ref, g_ref, z_ref, xbc_ref, dt_ref, kt_ref, vt_ref) = refs
        kvt_refs = {1: kt_ref, 2: vt_ref}
    offs = np.concatenate([[0], np.cumsum(AB_COLS)])
    outs = (q_ref, k_ref, v_ref, g_ref, z_ref, xbc_ref, dt_ref)
    for rows in _row_splits(x_ref.shape[0]):
        h = _norm_mod(x_ref[rows, :], nw_ref[...], sc_ref[...], sh_ref[...]).astype(BF16)
        for i, o_ref in enumerate(outs):
            last = i == len(outs) - 1
            y = _dot(h, wdt_ref[...] if last else w_ref[:, int(offs[i]):int(offs[i + 1])])
            if rope and i in (0, 1):
                y = _rope(y, cos_ref[rows, :], sin_ref[rows, :], A_HEAD_DIM // 4)
            o_ref[rows, :] = y.astype(o_ref.dtype)
            if i in kvt_refs:
                for s in range(rows.start // seq_len, rows.stop // seq_len):
                    kvt_refs[i][s] = y[s * seq_len - rows.start:(s + 1) * seq_len - rows.start, :].T


def _ab_in(x, nw, scale, shift, w, wdt, rope_tabs, seq_len, tm):
    t, d = x.shape
    per_seq = seq_len // tm if scale.shape[0] > 1 else None
    row = lambda i: (i, 0)
    mod = (lambda i: (i // per_seq, 0, 0)) if per_seq else (lambda i: (0, 0, 0))
    const = lambda i: (0, 0)
    in_specs = [pl.BlockSpec((tm, d), row),
                pl.BlockSpec((1, d), const),
                pl.BlockSpec((None, 1, d), mod),
                pl.BlockSpec((None, 1, d), mod),
                pl.BlockSpec(w.shape, const),
                pl.BlockSpec(wdt.shape, const)]
    args = [x, nw, scale, shift, w, wdt]
    if rope_tabs is not None:
        nt = seq_len // tm
        pos = lambda i: (i % nt, 0)
        in_specs += [pl.BlockSpec((tm, LANES), pos), pl.BlockSpec((tm, LANES), pos)]
        args += list(rope_tabs)
    dts = (BF16, F32, F32, BF16, BF16, BF16, F32)
    out_specs = [pl.BlockSpec((tm, c), row) for c in AB_COLS]
    out_shape = [jax.ShapeDtypeStruct((t, c), dt) for c, dt in zip(AB_COLS, dts)]
    if rope_tabs is None:
        assert (tm // 2) % seq_len == 0
        spt = tm // seq_len
        out_specs += [pl.BlockSpec((spt, A_KV_WIDTH, seq_len), lambda i: (i, 0, 0))] * 2
        out_shape += [jax.ShapeDtypeStruct((t // seq_len, A_KV_WIDTH, seq_len), F32)] * 2
    return pl.pallas_call(
        functools.partial(_ab_in_kernel, rope=rope_tabs is not None, seq_len=seq_len),
        grid=(t // tm,),
        in_specs=in_specs,
        out_specs=out_specs,
        out_shape=out_shape,
        compiler_params=_cparams("arbitrary"),
        name="ab_in_proj",
    )(*args)


def _conv_kernel(prev_ref, main_ref, next_ref, w_ref, b_ref, o_ref, *, tiles_per_seq):
    i = pl.program_id(0) % tiles_per_seq
    tm = main_ref.shape[0]
    prev = jnp.where(i > 0, prev_ref[...].astype(F32), 0.0)
    nxt = jnp.where(i < tiles_per_seq - 1, next_ref[...].astype(F32), 0.0)
    ext = jnp.concatenate([prev, main_ref[...].astype(F32), nxt], axis=0)
    first = SUBLANES - CONV_K // 2
    acc = b_ref[...] + ext[first:first + tm] * w_ref[0:1, :]
    for j in range(1, CONV_K):
        acc = acc + ext[first + j:first + j + tm] * w_ref[j:j + 1, :]
    o_ref[...] = _silu(acc).astype(o_ref.dtype)


def _conv(xbc, w, b, seq_len, tm):
    t, c = xbc.shape
    tps = seq_len // tm
    hb = tm // SUBLANES
    nblk8 = t // SUBLANES
    return pl.pallas_call(
        functools.partial(_conv_kernel, tiles_per_seq=tps),
        grid=(t // tm,),
        in_specs=[pl.BlockSpec((SUBLANES, c), lambda i: (jnp.maximum(i * hb - 1, 0), 0)),
                  pl.BlockSpec((tm, c), lambda i: (i, 0)),
                  pl.BlockSpec((SUBLANES, c), lambda i: (jnp.minimum((i + 1) * hb, nblk8 - 1), 0)),
                  pl.BlockSpec((SUBLANES, c), lambda i: (0, 0)),
                  pl.BlockSpec((1, c), lambda i: (0, 0))],
        out_specs=pl.BlockSpec((tm, c), lambda i: (i, 0)),
        out_shape=jax.ShapeDtypeStruct((t, c), BF16),
        compiler_params=_cparams("arbitrary"),
        name="ssd_conv",
    )(xbc, xbc, xbc, w, b)


def _ssd_init(h0_ref, h_scr, has_h0):
    if not has_h0:
        h_scr[...] = jnp.zeros_like(h_scr)
        return
    for pr in range(SSD_HEADS // 2):
        both = jnp.concatenate([h0_ref[2 * pr], h0_ref[2 * pr + 1]], axis=0)
        both = jnp.concatenate([both, jnp.zeros_like(both)], axis=1)
        st = both.T[0:SSD_STATE]
        lo_st = _lane_iota(st.shape) < SSD_HEAD_DIM
        h_scr[2 * pr] = jnp.where(lo_st, st, 0.0)
        h_scr[2 * pr + 1] = jnp.where(lo_st, 0.0, st)


def _ssd_final(h_scr, hout_ref):
    for pr in range(SSD_HEADS // 2):
        st = h_scr[2 * pr] + h_scr[2 * pr + 1]
        st = jnp.concatenate([st, jnp.zeros_like(st)], axis=0).T
        hout_ref[2 * pr] = st[0:SSD_HEAD_DIM, 0:SSD_STATE]
        hout_ref[2 * pr + 1] = st[SSD_HEAD_DIM:2 * SSD_HEAD_DIM, 0:SSD_STATE]


def _ssd_prelude(b_ref, c_ref, dt_ref, par_ref, direction):
    lane = _lane_iota((CHUNK, LANES))
    rowi = _row_iota((CHUNK, LANES))
    dtp_in = dt_ref[...] + par_ref[0:1, :]
    dtp = jnp.maximum(dtp_in, 0.0) + jnp.log(1.0 + jnp.exp(-jnp.abs(dtp_in)))
    la = dtp * (-jnp.exp(par_ref[1:2, :]))
    causal = (rowi >= lane) if direction == 0 else (rowi <= lane)
    tri = jnp.where(causal, 1.0, 0.0).astype(BF16)
    p1 = la.astype(BF16)
    r1 = la - p1.astype(F32)
    p2 = r1.astype(BF16)
    p3 = (r1 - p2.astype(F32)).astype(BF16)
    cum = _dot(tri, p1) + _dot(tri, p2) + _dot(tri, p3)
    cum_t = cum.T
    dtp_t = dtp.T
    end = CHUNK - 1 if direction == 0 else 0
    w_t = jnp.exp(cum_t[:, end:end + 1] - cum_t) * dtp_t
    lrow_t = cum_t - jnp.log(dtp_t)
    bm = b_ref[...]
    cm = c_ref[...]
    cb, cmask = [], []
    for g in range(SSD_GROUPS):
        in_g = (lane >= g * SSD_STATE) & (lane < (g + 1) * SSD_STATE)
        cg = jnp.where(in_g, cm, jnp.zeros_like(cm))
        cmask.append(cg.astype(F32))
        cb.append(_dot_nt(cg, bm))
    return dict(cum=cum, lrow_t=lrow_t, w_t=w_t, causal=causal, end=end,
                bm_t=bm.astype(F32).T, cb=cb, cmask=cmask, lo_half=lane < SSD_HEAD_DIM)


def _ssd_head_pair(pre, pr, x_ref, dskip_ref, y_ref, h_scr, direction):
    hp = SSD_HEADS // SSD_GROUPS
    cum, lrow_t, w_t, end = pre["cum"], pre["lrow_t"], pre["w_t"], pre["end"]
    lo_half = pre["lo_half"]
    xp = x_ref[:, pr * LANES:(pr + 1) * LANES]
    y_pair = None
    for e in range(2):
        h = 2 * pr + e
        g = h // hp
        col = direction * SSD_HEADS + h
        xe = jnp.where(lo_half if e == 0 else ~lo_half, xp, jnp.zeros_like(xp))
        cumcol = jnp.broadcast_to(cum[:, col:col + 1], (CHUNK, LANES))
        diff = cumcol - lrow_t[col:col + 1, :]
        sc = pre["cb"][g] * jnp.exp(jnp.where(pre["causal"], diff, NEG_INF))
        ce = pre["cmask"][g] * jnp.exp(cumcol)
        hs = h_scr[h]
        hsb = hs.astype(BF16)
        zpad = jnp.zeros_like(hsb)
        bw = (pre["bm_t"][g * SSD_STATE:(g + 1) * SSD_STATE, :] * w_t[col:col + 1, :]).astype(BF16)
        a = jnp.concatenate([jnp.concatenate([sc.astype(BF16), ce.astype(BF16)], axis=1),
                             jnp.concatenate([bw, jnp.zeros_like(bw)], axis=1)], axis=0)
        wmat = jnp.concatenate([xe, hsb, zpad] if g == 0 else [xe, zpad, hsb], axis=0)
        res = _dot(a, wmat)
        ye = res[0:CHUNK]
        y_pair = ye if y_pair is None else y_pair + ye
        cdec = jnp.exp(cumcol[end:end + 1, :])
        h_scr[h] = hs * cdec + res[CHUNK:CHUNK + SSD_STATE]
    if direction == 0:
        y_pair = y_pair + dskip_ref[:, pr * LANES:(pr + 1) * LANES] * xp.astype(F32)
    y_ref[:, pr * LANES:(pr + 1) * LANES] = y_pair.astype(y_ref.dtype)


def _ssd_kernel(xf_ref, bf_ref, cf_ref, dtf_ref, xb_ref, bb_ref, cb_ref, dtb_ref, par_ref, dskip_ref,
                h0f_ref, h0b_ref, yf_ref, yb_ref, houtf_ref, houtb_ref, hf_scr, hb_scr, *, nc, has_h0):
    ci = pl.program_id(1)

    @pl.when(ci == 0)
    def _init():
        _ssd_init(h0f_ref, hf_scr, has_h0)
        _ssd_init(h0b_ref, hb_scr, has_h0)

    pre_f = _ssd_prelude(bf_ref, cf_ref, dtf_ref, par_ref, 0)
    pre_b = _ssd_prelude(bb_ref, cb_ref, dtb_ref, par_ref, 1)
    for pr in range(SSD_HEADS // 2):
        _ssd_head_pair(pre_f, pr, xf_ref, dskip_ref, yf_ref, hf_scr, 0)
        _ssd_head_pair(pre_b, pr, xb_ref, dskip_ref, yb_ref, hb_scr, 1)

    @pl.when(ci == nc - 1)
    def _final():
        _ssd_final(hf_scr, houtf_ref)
        _ssd_final(hb_scr, houtb_ref)


def _ssd(xbc, dt, par, dskip, h0f, h0b, nb, seq_len):
    t = xbc.shape[0]
    nc = seq_len // CHUNK
    fwd = lambda b, c: b * nc + c
    bwd = lambda b, c: b * nc + nc - 1 - c
    xcols = SSD_INNER // LANES
    has_h0 = h0f is not None
    if not has_h0:
        h0f = h0b = jnp.zeros((1, SSD_HEADS, SSD_HEAD_DIM, SSD_STATE), F32)
    h0_map = (lambda b, c: (b, 0, 0, 0)) if has_h0 else (lambda b, c: (0, 0, 0, 0))
    st_shape = (None, SSD_HEADS, SSD_HEAD_DIM, SSD_STATE)
    carry_shape = (None, SSD_HEADS, SSD_STATE, LANES)
    per_b = lambda b, c: (b, 0, 0, 0)

    def chunk_specs(cidx):
        return [pl.BlockSpec((CHUNK, SSD_INNER), lambda b, c: (cidx(b, c), 0)),
                pl.BlockSpec((CHUNK, LANES), lambda b, c: (cidx(b, c), xcols)),
                pl.BlockSpec((CHUNK, LANES), lambda b, c: (cidx(b, c), xcols + 1)),
                pl.BlockSpec((CHUNK, LANES), lambda b, c: (cidx(b, c), 0))]

    outs = pl.pallas_call(
        functools.partial(_ssd_kernel, nc=nc, has_h0=has_h0),
        grid=(nb, nc),
        in_specs=chunk_specs(fwd) + chunk_specs(bwd) + [
            pl.BlockSpec((SUBLANES, LANES), lambda b, c: (0, 0)),
            pl.BlockSpec((1, SSD_INNER), lambda b, c: (0, 0)),
            pl.BlockSpec(st_shape, h0_map), pl.BlockSpec(st_shape, h0_map)],
        out_specs=[pl.BlockSpec((CHUNK, SSD_INNER), lambda b, c: (fwd(b, c), 0)),
                   pl.BlockSpec((CHUNK, SSD_INNER), lambda b, c: (bwd(b, c), 0)),
                   pl.BlockSpec(st_shape, per_b), pl.BlockSpec(st_shape, per_b),
                   pl.BlockSpec(carry_shape, per_b), pl.BlockSpec(carry_shape, per_b)],
        out_shape=[jax.ShapeDtypeStruct((t, SSD_INNER), BF16), jax.ShapeDtypeStruct((t, SSD_INNER), BF16),
                   jax.ShapeDtypeStruct((nb, SSD_HEADS, SSD_HEAD_DIM, SSD_STATE), F32),
                   jax.ShapeDtypeStruct((nb, SSD_HEADS, SSD_HEAD_DIM, SSD_STATE), F32),
                   jax.ShapeDtypeStruct((nb, SSD_HEADS, SSD_STATE, LANES), F32),
                   jax.ShapeDtypeStruct((nb, SSD_HEADS, SSD_STATE, LANES), F32)],
        compiler_params=_cparams("arbitrary", "arbitrary"),
        name="ssd_scan_bidir",
    )(xbc, xbc, xbc, dt, xbc, xbc, xbc, dt, par, dskip, h0f, h0b)
    return outs[:4]


def _kv_variants(k, v):
    lane = _lane_iota(k.shape)
    lo = lane < A_HEAD_DIM
    k_sw = pltpu.roll(k, A_HEAD_DIM, 1)
    v_sw = pltpu.roll(v, A_HEAD_DIM, 1)
    ks, vs = {}, {}
    for j in range(A_KV_HEADS):
        for e in range(2):
            src_k, src_v = (k, v) if j == e else (k_sw, v_sw)
            half = lo if e == 0 else ~lo
            ks[j, e] = jnp.where(half, src_k, 0.0).astype(BF16)
            vs[j, e] = src_v.astype(BF16)
    return ks, vs


GQA_KEY_TILE = 256


def _gqa_heads(q_ref, g_ref, o_ref, sink_ref, ks, vs, valid_tile, s_scr):
    m = q_ref.shape[0]
    assert A_HEADS // A_KV_HEADS == 4
    lo = _lane_iota((m, LANES)) < A_HEAD_DIM
    upper_rows = _row_iota((2 * m, 1)) >= m
    scale = jnp.asarray(A_SCALE, BF16)
    units = [(j, e) for j in range(A_KV_HEADS) for e in range(2)]
    nkeys = ks[0, 0].shape[0]
    kt_w = GQA_KEY_TILE
    nkt = nkeys // kt_w
    rows_of = {}

    def q_rows(j):
        if j not in rows_of:
            rows_of[j] = jnp.concatenate([q_ref[:, 2 * j * LANES:(2 * j + 1) * LANES],
                                          q_ref[:, (2 * j + 1) * LANES:(2 * j + 2) * LANES]], axis=0) * scale
        return rows_of[j]

    def score_tile(u, st, kt):
        j, e = units[u]
        cols = slice(kt * kt_w, (kt + 1) * kt_w)
        s = _dot_nt(q_rows(j), ks[j, e][cols, :])
        ok = valid_tile(kt)
        if ok is not None:
            s = jnp.where(ok, s, NEG_INF)
        s_scr[u % 2, :, cols] = s
        for c in range(kt_w // LANES):
            t = s[:, c * LANES:(c + 1) * LANES]
            st["mrun"] = t if st["mrun"] is None else jnp.maximum(st["mrun"], t)

    def value_tile(u, st, kt):
        j, e = units[u]
        cols = slice(kt * kt_w, (kt + 1) * kt_w)
        p = jnp.exp(s_scr[u % 2, :, cols] - st["m"])
        for c in range(kt_w // LANES):
            t = p[:, c * LANES:(c + 1) * LANES]
            st["lrun"] = t if st["lrun"] is None else st["lrun"] + t
        pv = _dot(p.astype(BF16), vs[j, e][cols, :])
        st["acc"] = pv if st["acc"] is None else st["acc"] + pv

    outs = {}
    cur = None
    for u in range(len(units) + 1):
        nxt = dict(mrun=None) if u < len(units) else None
        for kt in range(nkt):
            if nxt is not None:
                score_tile(u, nxt, kt)
            if cur is not None:
                value_tile(u - 1, cur, kt)
        if cur is not None:
            den = jnp.sum(cur["lrun"], axis=1, keepdims=True) + jnp.exp(cur["sink"] - cur["m"])
            outs[units[u - 1]] = cur["acc"] / den
        if nxt is not None:
            j, e = units[u]
            sink = jnp.where(upper_rows, sink_ref[4 * j + 2 + e], sink_ref[4 * j + e])
            nxt.update(sink=sink, m=jnp.maximum(jnp.max(nxt["mrun"], axis=1, keepdims=True), sink),
                       lrun=None, acc=None)
        cur = nxt
    for pr in range(A_HEADS // 2):
        j, half = pr // 2, pr % 2
        r = slice(half * m, (half + 1) * m)
        attn = jnp.where(lo, outs[j, 0][r], outs[j, 1][r])
        gate = g_ref[:, pr * LANES:(pr + 1) * LANES].astype(F32)
        o_ref[:, pr * LANES:(pr + 1) * LANES] = (attn * _silu(gate)).astype(o_ref.dtype)


def _attn_ctx_kernel(sink_ref, q_ref, k_ref, v_ref, g_ref, o_ref, s_scr):
    ks, vs = _kv_variants(k_ref[...], v_ref[...])
    _gqa_heads(q_ref, g_ref, o_ref, sink_ref, ks, vs, lambda kt: None, s_scr)


def _attn_ctx(sink, q, k, v, g, seq_len):
    t = q.shape[0]
    row = lambda b: (b, 0)
    return pl.pallas_call(
        _attn_ctx_kernel,
        grid=(t // seq_len,),
        in_specs=[pl.BlockSpec(memory_space=pltpu.SMEM),
                  pl.BlockSpec((seq_len, A_WIDTH), row),
                  pl.BlockSpec((seq_len, A_KV_WIDTH), row),
                  pl.BlockSpec((seq_len, A_KV_WIDTH), row),
                  pl.BlockSpec((seq_len, A_WIDTH), row)],
        out_specs=pl.BlockSpec((seq_len, A_WIDTH), row),
        out_shape=jax.ShapeDtypeStruct((t, A_WIDTH), BF16),
        scratch_shapes=[pltpu.VMEM((2, 2 * seq_len, seq_len), F32)],
        compiler_params=_cparams("arbitrary"),
        name="swa_context_attention",
    )(sink, q, k, v, g)


BAND_Q = 256


def _attn_band_kernel(sink_ref, q_ref, kp_ref, kc_ref, kn_ref, vp_ref, vc_ref, vn_ref, kx_ref, vx_ref, g_ref,
                      o_ref, s_scr, *, nsteps):
    n = pl.program_id(1)
    tq = q_ref.shape[0]
    blk = WINDOW_BLK
    k = jnp.concatenate([kp_ref[...], kc_ref[...], kn_ref[...], kx_ref[...]], axis=0)
    v = jnp.concatenate([vp_ref[...], vc_ref[...], vn_ref[...], vx_ref[...]], axis=0)
    nloc = tq + 2 * blk
    qi = _row_iota((2 * tq, GQA_KEY_TILE)) & (tq - 1)

    def valid_tile(kt):
        if kt * GQA_KEY_TILE >= nloc:
            return None
        col = _lane_iota((2 * tq, GQA_KEY_TILE)) + kt * GQA_KEY_TILE
        rel = col - blk - qi
        return ((rel >= -blk) & (rel <= blk) & ((col >= blk) | (n > 0)) & ((col < nloc - blk) | (n < nsteps - 1)))

    assert nloc % GQA_KEY_TILE == 0
    ks, vs = _kv_variants(k, v)
    _gqa_heads(q_ref, g_ref, o_ref, sink_ref, ks, vs, valid_tile, s_scr)


def _attn_band(sink, q, k, v, k_ctx, v_ctx, g, nb, seq_len):
    t = q.shape[0]
    blk = WINDOW_BLK
    tq = BAND_Q
    assert tq & (tq - 1) == 0 and tq % blk == 0 and seq_len % tq == 0
    per = tq // blk
    nblk = seq_len // blk
    nsteps = seq_len // tq
    nctx = k_ctx.shape[1]
    cur = lambda b, n: (b * nsteps + n, 0)
    prv = lambda b, n: (b * nblk + jnp.maximum(n * per - 1, 0), 0)
    nxt = lambda b, n: (b * nblk + jnp.minimum((n + 1) * per, nblk - 1), 0)
    ctx = lambda b, n: (b, 0, 0)
    edge = lambda f: pl.BlockSpec((blk, A_KV_WIDTH), f)
    mid = pl.BlockSpec((tq, A_KV_WIDTH), cur)
    return pl.pallas_call(
        functools.partial(_attn_band_kernel, nsteps=nsteps),
        grid=(nb, nsteps),
        in_specs=[pl.BlockSpec(memory_space=pltpu.SMEM),
                  pl.BlockSpec((tq, A_WIDTH), cur),
                  edge(prv), mid, edge(nxt),
                  edge(prv), mid, edge(nxt),
                  pl.BlockSpec((None, nctx, A_KV_WIDTH), ctx),
                  pl.BlockSpec((None, nctx, A_KV_WIDTH), ctx),
                  pl.BlockSpec((tq, A_WIDTH), cur)],
        out_specs=pl.BlockSpec((tq, A_WIDTH), cur),
        out_shape=jax.ShapeDtypeStruct((t, A_WIDTH), BF16),
        scratch_shapes=[pltpu.VMEM((2, 2 * tq, tq + 2 * blk + nctx), F32)],
        compiler_params=_cparams("arbitrary", "arbitrary"),
        name="swa_banded_attention",
    )(sink, q, k, k, k, v, v, v, k_ctx, v_ctx, g)


def _ab_out_kernel(a_ref, yf_ref, yb_ref, z_ref, gnw_ref, w_ref, x_ref, gate_ref, o_ref):
    for rows in _row_splits(x_ref.shape[0]):
        y = (yf_ref[rows, :].astype(F32) + yb_ref[rows, :].astype(F32)) * _silu(z_ref[rows, :].astype(F32))
        ms = jnp.mean(y * y, axis=-1, keepdims=True)
        s = ((y * lax.rsqrt(ms + EPS)) * gnw_ref[...]).astype(BF16)
        out = _dot(a_ref[rows, :], w_ref[0:A_WIDTH, :]) + _dot(s, w_ref[A_WIDTH:, :])
        o_ref[rows, :] = x_ref[rows, :] + gate_ref[...] * out


def _ab_out(a, yf, yb, z, gnw, w, x, gate, seq_len, tm):
    t, d = x.shape
    per_seq = seq_len // tm if gate.shape[0] > 1 else None
    row = lambda i: (i, 0)
    const = lambda i: (0, 0)
    mod = (lambda i: (i // per_seq, 0, 0)) if per_seq else (lambda i: (0, 0, 0))
    return pl.pallas_call(
        _ab_out_kernel,
        grid=(t // tm,),
        in_specs=[pl.BlockSpec((tm, A_WIDTH), row),
                  pl.BlockSpec((tm, SSD_INNER), row),
                  pl.BlockSpec((tm, SSD_INNER), row),
                  pl.BlockSpec((tm, SSD_INNER), row),
                  pl.BlockSpec((1, SSD_INNER), const),
                  pl.BlockSpec(w.shape, const),
                  pl.BlockSpec((tm, d), row),
                  pl.BlockSpec((None, 1, d), mod)],
        out_specs=pl.BlockSpec((tm, d), row),
        out_shape=jax.ShapeDtypeStruct((t, d), F32),
        compiler_params=_cparams("arbitrary"),
        name="ab_out_proj",
    )(a, yf, yb, z, gnw, w, x, gate)


MLA_COLS = (Q_LORA, KV_LORA, LANES, MLA_WIDTH)


def _mla_in_kernel(*refs, rope):
    if rope:
        (x_ref, nw_ref, sc_ref, sh_ref, w_ref, qnw_ref, kvnw_ref, wuq_ref, cos_ref, sin_ref,
         qn_ref, qpe_ref, ckv_ref, kpe_ref, g_ref, ckv32_ref, kpe32_ref) = refs
    else:
        (x_ref, nw_ref, sc_ref, sh_ref, w_ref, qnw_ref, kvnw_ref, wuq_ref,
         qn_ref, qpe_ref, ckv_ref, kpe_ref, g_ref, ckv32_ref, kpe32_ref) = refs
    offs = [int(o) for o in np.concatenate([[0], np.cumsum(MLA_COLS)])]
    nope_w = MLA_HEADS * MLA_NOPE

    def rms(u, w):
        return (u * lax.rsqrt(jnp.mean(u * u, axis=-1, keepdims=True) + EPS)) * w

    for rows in _row_splits(x_ref.shape[0]):
        h = _norm_mod(x_ref[rows, :], nw_ref[...], sc_ref[...], sh_ref[...]).astype(BF16)
        cq = _dot(h, w_ref[:, offs[0]:offs[1]])
        ckv = _dot(h, w_ref[:, offs[1]:offs[2]])
        kpe = _dot(h, w_ref[:, offs[2]:offs[3]])
        g_ref[rows, :] = _dot(h, w_ref[:, offs[3]:offs[4]]).astype(g_ref.dtype)
        cqn = rms(cq, qnw_ref[...]).astype(BF16)
        qn_ref[rows, :] = _dot(cqn, wuq_ref[:, 0:nope_w]).astype(qn_ref.dtype)
        qpe = _dot(cqn, wuq_ref[:, nope_w:])
        if rope:
            qpe = _rope(qpe, cos_ref[rows, :], sin_ref[rows, :], MLA_ROPE // 4)
            kpe = _rope(kpe, cos_ref[rows, :], sin_ref[rows, :], MLA_ROPE // 4)
        qpe_ref[rows, :] = qpe.astype(qpe_ref.dtype)
        ckvn = rms(ckv, kvnw_ref[...])
        ckv_ref[rows, :] = ckvn.astype(ckv_ref.dtype)
        kpe_ref[rows, :] = kpe.astype(kpe_ref.dtype)
        ckv32_ref[rows, :] = ckvn
        kpe32_ref[rows, :] = kpe


def _mla_in(x, nw, scale, shift, w, qnw, kvnw, wuq, rope_tabs, seq_len, tm):
    t, d = x.shape
    per_seq = seq_len // tm if scale.shape[0] > 1 else None
    row = lambda i: (i, 0)
    const = lambda i: (0, 0)
    mod = (lambda i: (i // per_seq, 0, 0)) if per_seq else (lambda i: (0, 0, 0))
    in_specs = [pl.BlockSpec((tm, d), row), pl.BlockSpec((1, d), const),
                pl.BlockSpec((None, 1, d), mod), pl.BlockSpec((None, 1, d), mod),
                pl.BlockSpec(w.shape, const), pl.BlockSpec((1, Q_LORA), const),
                pl.BlockSpec((1, KV_LORA), const), pl.BlockSpec(wuq.shape, const)]
    args = [x, nw, scale, shift, w, qnw, kvnw, wuq]
    if rope_tabs is not None:
        nt = seq_len // tm
        pos = lambda i: (i % nt, 0)
        in_specs += [pl.BlockSpec((tm, LANES), pos), pl.BlockSpec((tm, LANES), pos)]
        args += list(rope_tabs)
    widths = (MLA_HEADS * MLA_NOPE, MLA_HEADS * MLA_ROPE, KV_LORA, LANES, MLA_WIDTH, KV_LORA, LANES)
    dts = (BF16, BF16, BF16, BF16, BF16, F32, F32)
    return pl.pallas_call(
        functools.partial(_mla_in_kernel, rope=rope_tabs is not None),
        grid=(t // tm,),
        in_specs=in_specs,
        out_specs=[pl.BlockSpec((tm, c), row) for c in widths],
        out_shape=[jax.ShapeDtypeStruct((t, c), dt) for c, dt in zip(widths, dts)],
        compiler_params=_cparams("arbitrary"),
        name="mla_in_proj",
    )(*args)


MLA_UNIT_ROWS = 512
MLA_KEY_TILE = 256


def _mla_attn_kernel(qn_ref, qpe_ref, ckv_ref, kpe_ref, wukt_ref, wuv_ref, g_ref, o_ref, kcat_scr, v_scr, s_scr, *,
                     pairs_per_step):
    pg = pl.program_id(1)
    qb = pl.program_id(2)
    tq = qn_ref.shape[0]
    c_exp = MLA_SCALE * LOG2E

    @pl.when(qb == 0)
    def _expand():
        ckv = ckv_ref[...]
        eye = jnp.where(_row_iota((LANES, LANES)) == _lane_iota((LANES, LANES)), 1.0, 0.0).astype(BF16)
        kpe_t = _dot_nt(eye, kpe_ref[...]).astype(BF16)
        for i in range(pairs_per_step):
            kcat_scr[i, 0:LANES, :] = _dot_nt(wukt_ref[i * LANES:(i + 1) * LANES, :], ckv).astype(BF16)
            kcat_scr[i, LANES:2 * LANES, :] = kpe_t
            v_scr[i] = _dot(ckv, wuv_ref[:, i * LANES:(i + 1) * LANES]).astype(BF16)

    nkeys = kcat_scr.shape[-1]
    kt_w = min(MLA_KEY_TILE, nkeys)
    nkt = nkeys // kt_w
    ru = min(MLA_UNIT_ROWS, tq)
    lane = _lane_iota((ru, LANES))
    units = [(i, r, e) for i in range(pairs_per_step) for r in range(tq // ru) for e in range(2)]

    def lane_tiles(x):
        return [x[:, j * LANES:(j + 1) * LANES] for j in range(x.shape[1] // LANES)]

    def q_ext(u):
        i, r, e = units[u]
        rows = slice(r * ru, (r + 1) * ru)
        qn = qn_ref[rows, i * LANES:(i + 1) * LANES]
        qpe = qpe_ref[rows, (i // 2) * LANES:(i // 2 + 1) * LANES]
        slot = 2 * (i % 2) + e if pairs_per_step % 2 == 0 else 2 * ((pg * pairs_per_step + i) % 2) + e
        nope_half = (lane < MLA_NOPE) if e == 0 else (lane >= MLA_NOPE)
        return jnp.concatenate([jnp.where(nope_half, qn, jnp.zeros_like(qn)),
                                jnp.where((lane >> 5) == slot, qpe, jnp.zeros_like(qpe))], axis=1)

    def score_tile(u, st, kt):
        cols = slice(kt * kt_w, (kt + 1) * kt_w)
        s = _dot(st["q"], kcat_scr[units[u][0], :, cols])
        s_scr[u % 2, :, cols] = s
        for t in lane_tiles(s):
            st["mrun"] = t if st["mrun"] is None else jnp.maximum(st["mrun"], t)

    def value_tile(u, st, kt):
        cols = slice(kt * kt_w, (kt + 1) * kt_w)
        p = jnp.exp2((s_scr[u % 2, :, cols] - st["m"]) * c_exp)
        for t in lane_tiles(p):
            st["lrun"] = t if st["lrun"] is None else st["lrun"] + t
        pv = _dot(p.astype(BF16), v_scr[units[u][0], cols, :])
        st["acc"] = pv if st["acc"] is None else st["acc"] + pv

    def finish(u, st, done):
        i, r, e = units[u]
        den = jnp.sum(st["lrun"], axis=1, keepdims=True)
        done[e] = st["acc"] / den
        if e == 1:
            rows = slice(r * ru, (r + 1) * ru)
            attn = jnp.where(lane < MLA_V, done[0], done[1])
            gate = g_ref[rows, i * LANES:(i + 1) * LANES].astype(F32)
            o_ref[rows, i * LANES:(i + 1) * LANES] = (attn * _silu(gate)).astype(o_ref.dtype)

    done = {}
    cur = None
    for u in range(len(units) + 1):
        nxt = dict(q=q_ext(u), mrun=None) if u < len(units) else None
        for kt in range(nkt):
            if nxt is not None:
                score_tile(u, nxt, kt)
            if cur is not None:
                value_tile(u - 1, cur, kt)
        if cur is not None:
            finish(u - 1, cur, done)
        if nxt is not None:
            nxt.update(m=jnp.max(nxt["mrun"], axis=1, keepdims=True), lrun=None, acc=None)
        cur = nxt


def _mla_attn(qn, qpe, ckv_keys, kpe_keys, wukt, wuv, g, nb, seq_len, tq, pairs_per_step):
    t = qn.shape[0]
    nkeys = ckv_keys.shape[1]
    npairs = MLA_HEADS // 2
    ngrp = npairs // pairs_per_step
    nq = seq_len // tq
    wp = pairs_per_step * LANES
    if pairs_per_step % 2 == 0:
        wpe = wp // 2
        pe_map = lambda b, p, i: (b * nq + i, p)
    else:
        wpe = LANES
        pe_map = lambda b, p, i: (b * nq + i, p // 2)
    qmap = lambda b, p, i: (b * nq + i, p)
    return pl.pallas_call(
        functools.partial(_mla_attn_kernel, pairs_per_step=pairs_per_step),
        grid=(nb, ngrp, nq),
        in_specs=[pl.BlockSpec((tq, wp), qmap),
                  pl.BlockSpec((tq, wpe), pe_map),
                  pl.BlockSpec((None, nkeys, KV_LORA), lambda b, p, i: (b, 0, 0)),
                  pl.BlockSpec((None, nkeys, LANES), lambda b, p, i: (b, 0, 0)),
                  pl.BlockSpec((wp, KV_LORA), lambda b, p, i: (p, 0)),
                  pl.BlockSpec((KV_LORA, wp), lambda b, p, i: (0, p)),
                  pl.BlockSpec((tq, wp), qmap)],
        out_specs=pl.BlockSpec((tq, wp), qmap),
        out_shape=jax.ShapeDtypeStruct((t, MLA_WIDTH), BF16),
        scratch_shapes=[pltpu.VMEM((pairs_per_step, 2 * LANES, nkeys), BF16),
                        pltpu.VMEM((pairs_per_step, nkeys, LANES), BF16),
                        pltpu.VMEM((2, min(MLA_UNIT_ROWS, tq), nkeys), F32)],
        compiler_params=_cparams("arbitrary", "arbitrary", "arbitrary"),
        name="mla_attention",
    )(qn, qpe, ckv_keys, kpe_keys, wukt, wuv, g)


def _mla_attn_seq_kernel(qn_ref, qpe_ref, ckv_ref, kpe_ref, wukt_ref, wuv_ref, g_ref, o_ref,
                         kcat_scr, v_scr, s_scr, m_scr, half_scr):
    pair = pl.program_id(1)
    seq = qn_ref.shape[0]
    nkeys = kcat_scr.shape[-1]
    ru, kt_w = MLA_UNIT_ROWS, MLA_KEY_TILE
    nkt = nkeys // kt_w
    nrb = seq // ru
    c_exp = MLA_SCALE * LOG2E
    lane = _lane_iota((ru, LANES))

    ckv = ckv_ref[...]
    eye = jnp.where(_row_iota((LANES, LANES)) == _lane_iota((LANES, LANES)), 1.0, 0.0).astype(BF16)
    kcat_scr[0:LANES, :] = _dot_nt(wukt_ref[...], ckv).astype(BF16)
    kcat_scr[LANES:2 * LANES, :] = _dot_nt(eye, kpe_ref[...]).astype(BF16)
    v_scr[:, 0:LANES] = _dot(ckv, wuv_ref[...]).astype(BF16)
    v_scr[:, LANES:2 * LANES] = jnp.ones((nkeys, LANES), BF16)

    def rows_of(rb):
        return slice(rb * ru, (rb + 1) * ru) if isinstance(rb, int) else pl.ds(pl.multiple_of(rb * ru, ru), ru)

    def q_ext(rb, e):
        rows = rows_of(rb)
        qn = qn_ref[rows, :]
        qpe = qpe_ref[rows, :]
        nope_half = (lane < MLA_NOPE) if e == 0 else (lane >= MLA_NOPE)
        slot = 2 * (pair % 2) + e
        return jnp.concatenate([jnp.where(nope_half, qn, jnp.zeros_like(qn)),
                                jnp.where((lane >> 5) == slot, qpe, jnp.zeros_like(qpe))], axis=1)

    def phase(score, value):
        q = q_ext(*score) if score is not None else None
        mrun = acc = None
        for kt in range(nkt):
            cols = slice(kt * kt_w, (kt + 1) * kt_w)
            if score is not None:
                s = _dot(q, kcat_scr[:, cols])
                s_scr[score[1], :, cols] = s
                for j in range(kt_w // LANES):
                    t = s[:, j * LANES:(j + 1) * LANES]
                    mrun = t if mrun is None else jnp.maximum(mrun, t)
            if value is not None:
                mb = m_scr[value]
                p = jnp.concatenate(
                    [jnp.exp2((s_scr[value, :, kt * kt_w + j * LANES:kt * kt_w + (j + 1) * LANES] - mb) * c_exp)
                     for j in range(kt_w // LANES)], axis=1)
                pv = _dot(p.astype(BF16), v_scr[cols, :])
                acc = pv if acc is None else acc + pv
        if score is not None:
            m_scr[score[1]] = jnp.broadcast_to(jnp.max(mrun, axis=1, keepdims=True), (ru, LANES))
        return None if acc is None else acc[:, 0:LANES] / acc[:, LANES:2 * LANES]

    phase((0, 0), None)

    def row_block(rb, carry):
        half_scr[...] = phase((rb, 1), 0)
        out1 = phase((jnp.minimum(rb + 1, nrb - 1), 0), 1)
        rows = rows_of(rb)
        attn = jnp.where(lane < MLA_V, half_scr[...], out1)
        o_ref[rows, :] = (attn * _silu(g_ref[rows, :].astype(F32))).astype(o_ref.dtype)
        return carry

    lax.fori_loop(0, nrb, row_block, 0)


def _mla_attn_seq(qn, qpe, ckv_keys, kpe_keys, wukt, wuv, g, nb, seq_len):
    t = qn.shape[0]
    nkeys = ckv_keys.shape[1]
    assert seq_len % MLA_UNIT_ROWS == 0 and nkeys % MLA_KEY_TILE == 0
    qmap = lambda b, p: (b, p)
    return pl.pallas_call(
        _mla_attn_seq_kernel,
        grid=(nb, MLA_HEADS // 2),
        in_specs=[pl.BlockSpec((seq_len, LANES), qmap),
                  pl.BlockSpec((seq_len, LANES), lambda b, p: (b, p // 2)),
                  pl.BlockSpec((None, nkeys, KV_LORA), lambda b, p: (b, 0, 0)),
                  pl.BlockSpec((None, nkeys, LANES), lambda b, p: (b, 0, 0)),
                  pl.BlockSpec((LANES, KV_LORA), lambda b, p: (p, 0)),
                  pl.BlockSpec((KV_LORA, LANES), lambda b, p: (0, p)),
                  pl.BlockSpec((seq_len, LANES), qmap)],
        out_specs=pl.BlockSpec((seq_len, LANES), qmap),
        out_shape=jax.ShapeDtypeStruct((t, MLA_WIDTH), BF16),
        scratch_shapes=[pltpu.VMEM((2 * LANES, nkeys), BF16),
                        pltpu.VMEM((nkeys, 2 * LANES), BF16),
                        pltpu.VMEM((2, MLA_UNIT_ROWS, nkeys), F32),
                        pltpu.VMEM((2, MLA_UNIT_ROWS, LANES), F32),
                        pltpu.VMEM((MLA_UNIT_ROWS, LANES), F32)],
        compiler_params=_cparams("arbitrary", "arbitrary"),
        name="mla_attention_seq",
    )(qn, qpe, ckv_keys, kpe_keys, wukt, wuv, g)


def _mla_out_kernel(a_ref, w_ref, x_ref, gate_ref, fw_ref, o_ref):
    for rows in _row_splits(x_ref.shape[0]):
        xn = x_ref[rows, :] + gate_ref[...] * _dot(a_ref[rows, :], w_ref[...])
        ms = jnp.mean(xn * xn, axis=-1, keepdims=True)
        o_ref[rows, :] = (xn * lax.rsqrt(ms + EPS)) * fw_ref[...]


def _mla_out(a, w, x, gate, fw, seq_len, tm):
    t, d = x.shape
    per_seq = seq_len // tm if gate.shape[0] > 1 else None
    row = lambda i: (i, 0)
    const = lambda i: (0, 0)
    mod = (lambda i: (i // per_seq, 0, 0)) if per_seq else (lambda i: (0, 0, 0))
    return pl.pallas_call(
        _mla_out_kernel,
        grid=(t // tm,),
        in_specs=[pl.BlockSpec((tm, MLA_WIDTH), row), pl.BlockSpec(w.shape, const),
                  pl.BlockSpec((tm, d), row), pl.BlockSpec((None, 1, d), mod), pl.BlockSpec((1, d), const)],
        out_specs=pl.BlockSpec((tm, d), row),
        out_shape=jax.ShapeDtypeStruct((t, d), F32),
        compiler_params=_cparams("arbitrary"),
        name="mla_out_proj_final_norm",
    )(a, w, x, gate, fw)


def _rope_tables(length, dim):
    rows = length // GRID_W
    f32 = np.float32
    row = np.repeat(np.arange(rows), GRID_W).astype(f32)
    col = np.tile(np.arange(GRID_W), rows).astype(f32)
    nf = dim // 4
    inv = (f32(1.0) / np.power(f32(ROPE_BASE), np.arange(nf, dtype=f32) / f32(nf))).astype(f32)
    ar = row[:, None] * inv[None, :]
    ac = col[:, None] * inv[None, :]
    ang = np.concatenate([ar, ar, ac, ac], axis=-1).astype(f32)
    sign = np.tile(np.concatenate([-np.ones((nf,), f32), np.ones((nf,), f32)]), 2)
    reps = LANES // dim
    return (jnp.asarray(np.tile(np.cos(ang).astype(f32), (1, reps))),
            jnp.asarray(np.tile((np.sin(ang) * sign).astype(f32), (1, reps))))


PROJ_TM = 1024
CONV_TM = 512


def _group(x, seq_len, conds_rows, mods, wts, tabs_a, tabs_c, ctx_cache):
    nb, _, d = x.shape
    t = nb * seq_len
    latent = ctx_cache is not None
    xf = x.reshape(t, d)
    tm = PROJ_TM

    def mod_rows(layer):
        m = mods[layer][conds_rows]
        sh, sc, gt = jnp.split(m[:, None, :], 3, axis=-1)
        return sh, sc, gt

    sh, sc, gt = mod_rows(0)
    q, k, v, g, z, xbc, dt, *kv_t = _ab_in(xf, wts["norm_w"][0:1], sc, sh, wts["ab_w_in"], wts["ab_w_dt"],
                                           tabs_a if latent else None, seq_len, tm)
    xbc_c = _conv(xbc, wts["ab_conv_w"], wts["ab_conv_b"], seq_len, min(CONV_TM, seq_len))
    if latent:
        k_ctx, v_ctx, s_f0, s_b0, _, _ = ctx_cache
        attn = _attn_band(wts["ab_sink"], q, k, v, k_ctx, v_ctx, g, nb, seq_len)
    else:
        s_f0 = s_b0 = None
        attn = _attn_ctx(wts["ab_sink"], q, k, v, g, seq_len)
    yf, yb, s_f, s_b = _ssd(xbc_c, dt, wts["ssd_par"], wts["ssd_dskip"], s_f0, s_b0, nb, seq_len)
    x1 = _ab_out(attn, yf, yb, z, wts["ab_gnorm_w"], wts["ab_w_out"], xf, gt, seq_len, tm)

    sh, sc, gt = mod_rows(1)
    qn, qpe, ckv, kpe, g1, ckv32, kpe32 = _mla_in(
        x1, wts["norm_w"][1:2], sc, sh, wts["mla_w_in"], wts["mla_q_norm_w"], wts["mla_kv_norm_w"],
        wts["mla_w_uq"], tabs_c if latent else None, seq_len, tm)
    ckv_keys = ckv.reshape(nb, seq_len, KV_LORA)
    kpe_keys = kpe.reshape(nb, seq_len, LANES)
    if latent:
        ckv_x, kpe_x = ctx_cache[4], ctx_cache[5]
        ckv_keys = jnp.concatenate([ckv_keys, ckv_x], axis=1)
        kpe_keys = jnp.concatenate([kpe_keys, kpe_x], axis=1)
        attn1 = _mla_attn_seq(qn, qpe, ckv_keys, kpe_keys, wts["mla_w_ukt"], wts["mla_w_uv"], g1, nb, seq_len)
    else:
        attn1 = _mla_attn(qn, qpe, ckv_keys, kpe_keys, wts["mla_w_ukt"], wts["mla_w_uv"], g1, nb, seq_len,
                          seq_len, MLA_HEADS // 2)
    y = _mla_out(attn1, wts["mla_w_out"], x1, gt, wts["final_norm_w"], seq_len, tm)
    return y.reshape(nb, seq_len, d), (kv_t, s_f, s_b, ckv32, kpe32)


def kernel(x_prompt, x_sample, cache_a_k, cache_a_v, state_ssd_fwd, state_ssd_bwd, cache_mla_ckv, cache_mla_kpe,
           c, c_ctx, ada_w, ada_b, norm_w, ab_w_in, ab_sink, ab_conv_w, ab_conv_b, ab_dt_bias, ab_a_log,
           ab_d_skip, ab_gnorm_w, ab_w_out, mla_w_in, mla_q_norm_w, mla_kv_norm_w, mla_w_uq, mla_w_ukv,
           mla_w_out, final_norm_w):
    batch, seq, d = x_prompt.shape
    dec_batch, dec_seq, _ = x_sample.shape
    assert ada_w.shape[0] == 2 and ab_w_in.shape[0] == 1 and mla_w_in.shape[0] == 1
    assert dec_batch + 1 <= SUBLANES

    conds = jnp.concatenate([c_ctx[None, :], c, jnp.zeros((SUBLANES - 1 - dec_batch, d), F32)], axis=0)
    mods = _modulation(conds, ada_w, ada_b)

    n_main = sum(AB_COLS[:-1])
    w_ab = ab_w_in[0][:, :n_main].astype(BF16)
    w_dt = jnp.concatenate([ab_w_in[0][:, n_main:], jnp.zeros((d, LANES - 2 * SSD_HEADS), F32)], axis=1).astype(BF16)
    mw = mla_w_in[0]
    o_kpe = Q_LORA + KV_LORA
    w_mla = jnp.concatenate([mw[:, :o_kpe], jnp.tile(mw[:, o_kpe:o_kpe + MLA_ROPE], (1, LANES // MLA_ROPE)),
                             mw[:, o_kpe + MLA_ROPE:]], axis=1).astype(BF16)
    wuq = mla_w_uq[0].reshape(Q_LORA, MLA_HEADS, MLA_NOPE + MLA_ROPE)
    wuq = jnp.concatenate([wuq[:, :, :MLA_NOPE].reshape(Q_LORA, -1), wuq[:, :, MLA_NOPE:].reshape(Q_LORA, -1)],
                          axis=1).astype(BF16)
    wukv = mla_w_ukv[0].reshape(KV_LORA, MLA_HEADS, MLA_NOPE + MLA_V)
    wukt = wukv[:, :, :MLA_NOPE].reshape(KV_LORA, -1).T.astype(BF16)
    wuv = wukv[:, :, MLA_NOPE:].reshape(KV_LORA, -1).astype(BF16)
    pad_lanes = lambda r: jnp.concatenate([r.reshape(1, -1), jnp.zeros((1, LANES - r.size), F32)], axis=1)
    ssd_par = jnp.concatenate([pad_lanes(ab_dt_bias[0]), pad_lanes(ab_a_log[0]),
                               jnp.zeros((SUBLANES - 2, LANES), F32)], axis=0)
    wts = dict(
        norm_w=norm_w, ab_w_in=w_ab, ab_w_dt=w_dt, ab_sink=ab_sink[0],
        ab_conv_w=jnp.concatenate([ab_conv_w[0], jnp.zeros((SUBLANES - CONV_K, CONV_CH), F32)], axis=0),
        ab_conv_b=ab_conv_b[0][None, :], ssd_par=ssd_par,
        ssd_dskip=jnp.repeat(ab_d_skip[0], SSD_HEAD_DIM)[None, :],
        ab_gnorm_w=ab_gnorm_w[0][None, :], ab_w_out=ab_w_out[0].astype(BF16),
        mla_w_in=w_mla, mla_q_norm_w=mla_q_norm_w[0][None, :], mla_kv_norm_w=mla_kv_norm_w[0][None, :],
        mla_w_uq=wuq, mla_w_ukt=wukt, mla_w_uv=wuv, mla_w_out=mla_w_out[0].astype(BF16),
        final_norm_w=final_norm_w[None, :],
    )
    tabs_a = _rope_tables(dec_seq, A_HEAD_DIM)
    tabs_c = _rope_tables(dec_seq, MLA_ROPE)

    y_prompt, ((k_t, v_t), s_f, s_b, ckv32, kpe32) = _group(
        x_prompt, seq, jnp.zeros((1,), jnp.int32), mods, wts, tabs_a, tabs_c, None)

    past = cache_a_k.shape[2]
    ctx_cache = (cache_a_k[:, 0].reshape(dec_batch, past, A_KV_WIDTH),
                 cache_a_v[:, 0].reshape(dec_batch, past, A_KV_WIDTH),
                 state_ssd_fwd[:, 0], state_ssd_bwd[:, 0],
                 cache_mla_ckv[:, 0].astype(BF16),
                 jnp.tile(cache_mla_kpe[:, 0], (1, 1, LANES // MLA_ROPE)).astype(BF16))
    y_sample, _ = _group(x_sample, dec_seq, 1 + jnp.arange(dec_batch), mods, wts, tabs_a, tabs_c, ctx_cache)

    cache_layout = lambda u: u.reshape(batch, 1, A_KV_HEADS, A_HEAD_DIM, seq).transpose(0, 1, 4, 2, 3)
    return (y_prompt, y_sample,
            cache_layout(k_t), cache_layout(v_t),
            s_f[:, None], s_b[:, None],
            ckv32.reshape(batch, 1, seq, KV_LORA), kpe32[:, :MLA_ROPE].reshape(batch, 1, seq, MLA_ROPE))
```

```python
import functools
import math

import jax
import jax.numpy as jnp
import numpy as np
from jax import lax
from jax.experimental import pallas as pl
from jax.experimental.pallas import tpu as pltpu

F32 = jnp.float32
BF16 = jnp.bfloat16

LANES = 128
SUBLANES = 8
VMEM_LIMIT_BYTES = 56 * 1024 * 1024

GRID_W = 64
ROPE_BASE = 10000.0
EPS = 1e-6
NEG_INF = -1e30
WINDOW_BLK = 128
CHUNK = 128
A_HEADS, A_KV_HEADS, A_HEAD_DIM = 8, 2, 64
A_WIDTH = A_HEADS * A_HEAD_DIM
A_KV_WIDTH = A_KV_HEADS * A_HEAD_DIM
A_SCALE = A_HEAD_DIM ** -0.5
SSD_HEADS, SSD_HEAD_DIM, SSD_GROUPS, SSD_STATE = 16, 64, 2, 64
SSD_INNER = SSD_HEADS * SSD_HEAD_DIM
CONV_K = 5
BC_WIDTH = SSD_GROUPS * SSD_STATE
CONV_CH = SSD_INNER + 2 * BC_WIDTH
MLA_HEADS, MLA_NOPE, MLA_ROPE, MLA_V = 16, 64, 32, 64
Q_LORA, KV_LORA = 256, 128
MLA_WIDTH = MLA_HEADS * MLA_V
MLA_SCALE = (MLA_NOPE + MLA_ROPE) ** -0.5
LOG2E = 1.4426950408889634


def _cparams(*sem):
    return pltpu.CompilerParams(dimension_semantics=sem, vmem_limit_bytes=VMEM_LIMIT_BYTES)


def _silu(x):
    return x * (1.0 / (1.0 + jnp.exp(-x)))


def _dot(a, b):
    return jnp.dot(a, b, preferred_element_type=F32, precision=lax.Precision.DEFAULT)


def _dot_nt(a, b):
    return lax.dot_general(a, b, (((1,), (1,)), ((), ())), preferred_element_type=F32,
                           precision=lax.Precision.DEFAULT)


def _row_splits(m, parts=2):
    step = m // parts
    return [slice(r * step, (r + 1) * step) for r in range(parts)]


def _lane_iota(shape):
    return lax.broadcasted_iota(jnp.int32, shape, len(shape) - 1)


def _row_iota(shape):
    return lax.broadcasted_iota(jnp.int32, shape, len(shape) - 2)


def _mod_kernel(cond_ref, w_ref, b_ref, o_ref):
    s = _silu(cond_ref[...])
    o_ref[...] = _dot(s.astype(BF16), w_ref[...].astype(BF16)) + b_ref[...]


def _modulation(conds, ada_w, ada_b):
    depth, d, d3 = ada_w.shape
    tn = 768
    return pl.pallas_call(
        _mod_kernel,
        grid=(depth, d3 // tn),
        in_specs=[pl.BlockSpec((SUBLANES, d), lambda l, j: (0, 0)),
                  pl.BlockSpec((None, d, tn), lambda l, j: (l, 0, j)),
                  pl.BlockSpec((None, 1, tn), lambda l, j: (l, 0, j))],
        out_specs=pl.BlockSpec((None, SUBLANES, tn), lambda l, j: (l, 0, j)),
        out_shape=jax.ShapeDtypeStruct((depth, SUBLANES, d3), F32),
        compiler_params=_cparams("arbitrary", "arbitrary"),
        name="modulation",
    )(conds, ada_w, ada_b.reshape(depth, 1, d3))


def _norm_mod(x, nw, scale, shift):
    ms = jnp.mean(x * x, axis=-1, keepdims=True)
    y = (x * lax.rsqrt(ms + EPS)) * nw
    return y * (1.0 + scale) + shift


def _rope(x, cos, sin_signed, half_period):
    outs = []
    first = (_lane_iota((x.shape[0], LANES)) & (2 * half_period - 1)) < half_period
    for j in range(x.shape[1] // LANES):
        xj = x[:, j * LANES:(j + 1) * LANES]
        up = pltpu.roll(xj, LANES - half_period, 1)
        dn = pltpu.roll(xj, half_period, 1)
        outs.append(xj * cos + jnp.where(first, up, dn) * sin_signed)
    return outs[0] if len(outs) == 1 else jnp.concatenate(outs, axis=1)


AB_COLS = (A_WIDTH, A_KV_WIDTH, A_KV_WIDTH, A_WIDTH, SSD_INNER, CONV_CH, LANES)


def _ab_in_kernel(*refs, rope, seq_len):
    if rope:
        (x_ref, nw_ref, sc_ref, sh_ref, w_ref, wdt_ref, cos_ref, sin_ref,
         q_ref, k_ref, v_ref, g_ref, z_ref, xbc_ref, dt_ref) = refs
        kvt_refs = {}
    else:
        (x_ref, nw_ref, sc_ref, sh_ref, w_ref, wdt_ref,
         q_ref, k_ref, v_ref, g_ref, z_ref, xbc_ref, dt_ref, kt_ref, vt_ref) = refs
        kvt_refs = {1: kt_ref, 2: vt_ref}
    offs = np.concatenate([[0], np.cumsum(AB_COLS)])
    outs = (q_ref, k_ref, v_ref, g_ref, z_ref, xbc_ref, dt_ref)
    for rows in _row_splits(x_ref.shape[0]):
        h = _norm_mod(x_ref[rows, :], nw_ref[...], sc_ref[...], sh_ref[...]).astype(BF16)
        for i, o_ref in enumerate(outs):
            last = i == len(outs) - 1
            y = _dot(h, wdt_ref[...] if last else w_ref[:, int(offs[i]):int(offs[i + 1])])
            if rope and i in (0, 1):
                y = _rope(y, cos_ref[rows, :], sin_ref[rows, :], A_HEAD_DIM // 4)
            o_ref[rows, :] = y.astype(o_ref.dtype)
            if i in kvt_refs:
                for s in range(rows.start // seq_len, rows.stop // seq_len):
                    kvt_refs[i][s] = y[s * seq_len - rows.start:(s + 1) * seq_len - rows.start, :].T


def _ab_in(x, nw, scale, shift, w, wdt, rope_tabs, seq_len, tm):
    t, d = x.shape
    per_seq = seq_len // tm if scale.shape[0] > 1 else None
    row = lambda i: (i, 0)
    mod = (lambda i: (i // per_seq, 0, 0)) if per_seq else (lambda i: (0, 0, 0))
    const = lambda i: (0, 0)
    in_specs = [pl.BlockSpec((tm, d), row),
                pl.BlockSpec((1, d), const),
                pl.BlockSpec((None, 1, d), mod),
                pl.BlockSpec((None, 1, d), mod),
                pl.BlockSpec(w.shape, const),
                pl.BlockSpec(wdt.shape, const)]
    args = [x, nw, scale, shift, w, wdt]
    if rope_tabs is not None:
        nt = seq_len // tm
        pos = lambda i: (i % nt, 0)
        in_specs += [pl.BlockSpec((tm, LANES), pos), pl.BlockSpec((tm, LANES), pos)]
        args += list(rope_tabs)
    dts = (BF16, F32, F32, BF16, BF16, BF16, F32)
    out_specs = [pl.BlockSpec((tm, c), row) for c in AB_COLS]
    out_shape = [jax.ShapeDtypeStruct((t, c), dt) for c, dt in zip(AB_COLS, dts)]
    if rope_tabs is None:
        assert (tm // 2) % seq_len == 0
        spt = tm // seq_len
        out_specs += [pl.BlockSpec((spt, A_KV_WIDTH, seq_len), lambda i: (i, 0, 0))] * 2
        out_shape += [jax.ShapeDtypeStruct((t // seq_len, A_KV_WIDTH, seq_len), F32)] * 2
    return pl.pallas_call(
        functools.partial(_ab_in_kernel, rope=rope_tabs is not None, seq_len=seq_len),
        grid=(t // tm,),
        in_specs=in_specs,
        out_specs=out_specs,
        out_shape=out_shape,
        compiler_params=_cparams("arbitrary"),
        name="ab_in_proj",
    )(*args)


def _conv_kernel(prev_ref, main_ref, next_ref, w_ref, b_ref, o_ref, *, tiles_per_seq):
    i = pl.program_id(0) % tiles_per_seq
    tm = main_ref.shape[0]
    prev = jnp.where(i > 0, prev_ref[...].astype(F32), 0.0)
    nxt = jnp.where(i < tiles_per_seq - 1, next_ref[...].astype(F32), 0.0)
    ext = jnp.concatenate([prev, main_ref[...].astype(F32), nxt], axis=0)
    first = SUBLANES - CONV_K // 2
    acc = b_ref[...] + ext[first:first + tm] * w_ref[0:1, :]
    for j in range(1, CONV_K):
        acc = acc + ext[first + j:first + j + tm] * w_ref[j:j + 1, :]
    o_ref[...] = _silu(acc).astype(o_ref.dtype)


def _conv(xbc, w, b, seq_len, tm):
    t, c = xbc.shape
    tps = seq_len // tm
    hb = tm // SUBLANES
    nblk8 = t // SUBLANES
    return pl.pallas_call(
        functools.partial(_conv_kernel, tiles_per_seq=tps),
        grid=(t // tm,),
        in_specs=[pl.BlockSpec((SUBLANES, c), lambda i: (jnp.maximum(i * hb - 1, 0), 0)),
                  pl.BlockSpec((tm, c), lambda i: (i, 0)),
                  pl.BlockSpec((SUBLANES, c), lambda i: (jnp.minimum((i + 1) * hb, nblk8 - 1), 0)),
                  pl.BlockSpec((SUBLANES, c), lambda i: (0, 0)),
                  pl.BlockSpec((1, c), lambda i: (0, 0))],
        out_specs=pl.BlockSpec((tm, c), lambda i: (i, 0)),
        out_shape=jax.ShapeDtypeStruct((t, c), BF16),
        compiler_params=_cparams("arbitrary"),
        name="ssd_conv",
    )(xbc, xbc, xbc, w, b)


def _ssd_init(h0_ref, h_scr, has_h0):
    if not has_h0:
        h_scr[...] = jnp.zeros_like(h_scr)
        return
    for pr in range(SSD_HEADS // 2):
        both = jnp.concatenate([h0_ref[2 * pr], h0_ref[2 * pr + 1]], axis=0)
        both = jnp.concatenate([both, jnp.zeros_like(both)], axis=1)
        st = both.T[0:SSD_STATE]
        lo_st = _lane_iota(st.shape) < SSD_HEAD_DIM
        h_scr[2 * pr] = jnp.where(lo_st, st, 0.0)
        h_scr[2 * pr + 1] = jnp.where(lo_st, 0.0, st)


def _ssd_final(h_scr, hout_ref):
    for pr in range(SSD_HEADS // 2):
        st = h_scr[2 * pr] + h_scr[2 * pr + 1]
        st = jnp.concatenate([st, jnp.zeros_like(st)], axis=0).T
        hout_ref[2 * pr] = st[0:SSD_HEAD_DIM, 0:SSD_STATE]
        hout_ref[2 * pr + 1] = st[SSD_HEAD_DIM:2 * SSD_HEAD_DIM, 0:SSD_STATE]


def _ssd_prelude(b_ref, c_ref, dt_ref, par_ref, direction):
    lane = _lane_iota((CHUNK, LANES))
    rowi = _row_iota((CHUNK, LANES))
    dtp_in = dt_ref[...] + par_ref[0:1, :]
    dtp = jnp.maximum(dtp_in, 0.0) + jnp.log(1.0 + jnp.exp(-jnp.abs(dtp_in)))
    la = dtp * (-jnp.exp(par_ref[1:2, :]))
    causal = (rowi >= lane) if direction == 0 else (rowi <= lane)
    tri = jnp.where(causal, 1.0, 0.0).astype(BF16)
    p1 = la.astype(BF16)
    r1 = la - p1.astype(F32)
    p2 = r1.astype(BF16)
    p3 = (r1 - p2.astype(F32)).astype(BF16)
    cum = _dot(tri, p1) + _dot(tri, p2) + _dot(tri, p3)
    cum_t = cum.T
    dtp_t = dtp.T
    end = CHUNK - 1 if direction == 0 else 0
    w_t = jnp.exp(cum_t[:, end:end + 1] - cum_t) * dtp_t
    lrow_t = cum_t - jnp.log(dtp_t)
    bm = b_ref[...]
    cm = c_ref[...]
    cb, cmask = [], []
    for g in range(SSD_GROUPS):
        in_g = (lane >= g * SSD_STATE) & (lane < (g + 1) * SSD_STATE)
        cg = jnp.where(in_g, cm, jnp.zeros_like(cm))
        cmask.append(cg.astype(F32))
        cb.append(_dot_nt(cg, bm))
    return dict(cum=cum, lrow_t=lrow_t, w_t=w_t, causal=causal, end=end,
                bm_t=bm.astype(F32).T, cb=cb, cmask=cmask, lo_half=lane < SSD_HEAD_DIM)


def _ssd_head_pair(pre, pr, x_ref, dskip_ref, y_ref, h_scr, direction):
    hp = SSD_HEADS // SSD_GROUPS
    cum, lrow_t, w_t, end = pre["cum"], pre["lrow_t"], pre["w_t"], pre["end"]
    lo_half = pre["lo_half"]
    xp = x_ref[:, pr * LANES:(pr + 1) * LANES]
    y_pair = None
    for e in range(2):
        h = 2 * pr + e
        g = h // hp
        col = direction * SSD_HEADS + h
        xe = jnp.where(lo_half if e == 0 else ~lo_half, xp, jnp.zeros_like(xp))
        cumcol = jnp.broadcast_to(cum[:, col:col + 1], (CHUNK, LANES))
        diff = cumcol - lrow_t[col:col + 1, :]
        sc = pre["cb"][g] * jnp.exp(jnp.where(pre["causal"], diff, NEG_INF))
        ce = pre["cmask"][g] * jnp.exp(cumcol)
        hs = h_scr[h]
        hsb = hs.astype(BF16)
        zpad = jnp.zeros_like(hsb)
        bw = (pre["bm_t"][g * SSD_STATE:(g + 1) * SSD_STATE, :] * w_t[col:col + 1, :]).astype(BF16)
        a = jnp.concatenate([jnp.concatenate([sc.astype(BF16), ce.astype(BF16)], axis=1),
                             jnp.concatenate([bw, jnp.zeros_like(bw)], axis=1)], axis=0)
        wmat = jnp.concatenate([xe, hsb, zpad] if g == 0 else [xe, zpad, hsb], axis=0)
        res = _dot(a, wmat)
        ye = res[0:CHUNK]
        y_pair = ye if y_pair is None else y_pair + ye
        cdec = jnp.exp(cumcol[end:end + 1, :])
        h_scr[h] = hs * cdec + res[CHUNK:CHUNK + SSD_STATE]
    if direction == 0:
        y_pair = y_pair + dskip_ref[:, pr * LANES:(pr + 1) * LANES] * xp.astype(F32)
    y_ref[:, pr * LANES:(pr + 1) * LANES] = y_pair.astype(y_ref.dtype)


def _ssd_kernel(xf_ref, bf_ref, cf_ref, dtf_ref, xb_ref, bb_ref, cb_ref, dtb_ref, par_ref, dskip_ref,
                h0f_ref, h0b_ref, yf_ref, yb_ref, houtf_ref, houtb_ref, hf_scr, hb_scr, *, nc, has_h0):
    ci = pl.program_id(1)

    @pl.when(ci == 0)
    def _init():
        _ssd_init(h0f_ref, hf_scr, has_h0)
        _ssd_init(h0b_ref, hb_scr, has_h0)

    pre_f = _ssd_prelude(bf_ref, cf_ref, dtf_ref, par_ref, 0)
    pre_b = _ssd_prelude(bb_ref, cb_ref, dtb_ref, par_ref, 1)
    for pr in range(SSD_HEADS // 2):
        _ssd_head_pair(pre_f, pr, xf_ref, dskip_ref, yf_ref, hf_scr, 0)
        _ssd_head_pair(pre_b, pr, xb_ref, dskip_ref, yb_ref, hb_scr, 1)

    @pl.when(ci == nc - 1)
    def _final():
        _ssd_final(hf_scr, houtf_ref)
        _ssd_final(hb_scr, houtb_ref)


def _ssd(xbc, dt, par, dskip, h0f, h0b, nb, seq_len):
    t = xbc.shape[0]
    nc = seq_len // CHUNK
    fwd = lambda b, c: b * nc + c
    bwd = lambda b, c: b * nc + nc - 1 - c
    xcols = SSD_INNER // LANES
    has_h0 = h0f is not None
    if not has_h0:
        h0f = h0b = jnp.zeros((1, SSD_HEADS, SSD_HEAD_DIM, SSD_STATE), F32)
    h0_map = (lambda b, c: (b, 0, 0, 0)) if has_h0 else (lambda b, c: (0, 0, 0, 0))
    st_shape = (None, SSD_HEADS, SSD_HEAD_DIM, SSD_STATE)
    carry_shape = (None, SSD_HEADS, SSD_STATE, LANES)
    per_b = lambda b, c: (b, 0, 0, 0)

    def chunk_specs(cidx):
        return [pl.BlockSpec((CHUNK, SSD_INNER), lambda b, c: (cidx(b, c), 0)),
                pl.BlockSpec((CHUNK, LANES), lambda b, c: (cidx(b, c), xcols)),
                pl.BlockSpec((CHUNK, LANES), lambda b, c: (cidx(b, c), xcols + 1)),
                pl.BlockSpec((CHUNK, LANES), lambda b, c: (cidx(b, c), 0))]

    outs = pl.pallas_call(
        functools.partial(_ssd_kernel, nc=nc, has_h0=has_h0),
        grid=(nb, nc),
        in_specs=chunk_specs(fwd) + chunk_specs(bwd) + [
            pl.BlockSpec((SUBLANES, LANES), lambda b, c: (0, 0)),
            pl.BlockSpec((1, SSD_INNER), lambda b, c: (0, 0)),
            pl.BlockSpec(st_shape, h0_map), pl.BlockSpec(st_shape, h0_map)],
        out_specs=[pl.BlockSpec((CHUNK, SSD_INNER), lambda b, c: (fwd(b, c), 0)),
                   pl.BlockSpec((CHUNK, SSD_INNER), lambda b, c: (bwd(b, c), 0)),
                   pl.BlockSpec(st_shape, per_b), pl.BlockSpec(st_shape, per_b),
                   pl.BlockSpec(carry_shape, per_b), pl.BlockSpec(carry_shape, per_b)],
        out_shape=[jax.ShapeDtypeStruct((t, SSD_INNER), BF16), jax.ShapeDtypeStruct((t, SSD_INNER), BF16),
                   jax.ShapeDtypeStruct((nb, SSD_HEADS, SSD_HEAD_DIM, SSD_STATE), F32),
                   jax.ShapeDtypeStruct((nb, SSD_HEADS, SSD_HEAD_DIM, SSD_STATE), F32),
                   jax.ShapeDtypeStruct((nb, SSD_HEADS, SSD_STATE, LANES), F32),
                   jax.ShapeDtypeStruct((nb, SSD_HEADS, SSD_STATE, LANES), F32)],
        compiler_params=_cparams("arbitrary", "arbitrary"),
        name="ssd_scan_bidir",
    )(xbc, xbc, xbc, dt, xbc, xbc, xbc, dt, par, dskip, h0f, h0b)
    return outs[:4]


def _kv_variants(k, v):
    lane = _lane_iota(k.shape)
    lo = lane < A_HEAD_DIM
    k_sw = pltpu.roll(k, A_HEAD_DIM, 1)
    v_sw = pltpu.roll(v, A_HEAD_DIM, 1)
    ks, vs = {}, {}
    for j in range(A_KV_HEADS):
        for e in range(2):
            src_k, src_v = (k, v) if j == e else (k_sw, v_sw)
            half = lo if e == 0 else ~lo
            ks[j, e] = jnp.where(half, src_k, 0.0).astype(BF16)
            vs[j, e] = src_v.astype(BF16)
    return ks, vs


GQA_KEY_TILE = 256


def _gqa_heads(q_ref, g_ref, o_ref, sink_ref, ks, vs, valid_tile, s_scr):
    m = q_ref.shape[0]
    assert A_HEADS // A_KV_HEADS == 4
    lo = _lane_iota((m, LANES)) < A_HEAD_DIM
    upper_rows = _row_iota((2 * m, 1)) >= m
    scale = jnp.asarray(A_SCALE, BF16)
    units = [(j, e) for j in range(A_KV_HEADS) for e in range(2)]
    nkeys = ks[0, 0].shape[0]
    kt_w = GQA_KEY_TILE
    nkt = nkeys // kt_w
    rows_of = {}

    def q_rows(j):
        if j not in rows_of:
            rows_of[j] = jnp.concatenate([q_ref[:, 2 * j * LANES:(2 * j + 1) * LANES],
                                          q_ref[:, (2 * j + 1) * LANES:(2 * j + 2) * LANES]], axis=0) * scale
        return rows_of[j]

    def score_tile(u, st, kt):
        j, e = units[u]
        cols = slice(kt * kt_w, (kt + 1) * kt_w)
        s = _dot_nt(q_rows(j), ks[j, e][cols, :])
        ok = valid_tile(kt)
        if ok is not None:
            s = jnp.where(ok, s, NEG_INF)
        s_scr[u % 2, :, cols] = s
        for c in range(kt_w // LANES):
            t = s[:, c * LANES:(c + 1) * LANES]
            st["mrun"] = t if st["mrun"] is None else jnp.maximum(st["mrun"], t)

    def value_tile(u, st, kt):
        j, e = units[u]
        cols = slice(kt * kt_w, (kt + 1) * kt_w)
        p = jnp.exp(s_scr[u % 2, :, cols] - st["m"])
        for c in range(kt_w // LANES):
            t = p[:, c * LANES:(c + 1) * LANES]
            st["lrun"] = t if st["lrun"] is None else st["lrun"] + t
        pv = _dot(p.astype(BF16), vs[j, e][cols, :])
        st["acc"] = pv if st["acc"] is None else st["acc"] + pv

    outs = {}
    cur = None
    for u in range(len(units) + 1):
        nxt = dict(mrun=None) if u < len(units) else None
        for kt in range(nkt):
            if nxt is not None:
                score_tile(u, nxt, kt)
            if cur is not None:
                value_tile(u - 1, cur, kt)
        if cur is not None:
            den = jnp.sum(cur["lrun"], axis=1, keepdims=True) + jnp.exp(cur["sink"] - cur["m"])
            outs[units[u - 1]] = cur["acc"] / den
        if nxt is not None:
            j, e = units[u]
            sink = jnp.where(upper_rows, sink_ref[4 * j + 2 + e], sink_ref[4 * j + e])
            nxt.update(sink=sink, m=jnp.maximum(jnp.max(nxt["mrun"], axis=1, keepdims=True), sink),
                       lrun=None, acc=None)
        cur = nxt
    for pr in range(A_HEADS // 2):
        j, half = pr // 2, pr % 2
        r = slice(half * m, (half + 1) * m)
        attn = jnp.where(lo, outs[j, 0][r], outs[j, 1][r])
        gate = g_ref[:, pr * LANES:(pr + 1) * LANES].astype(F32)
        o_ref[:, pr * LANES:(pr + 1) * LANES] = (attn * _silu(gate)).astype(o_ref.dtype)


def _attn_ctx_kernel(sink_ref, q_ref, k_ref, v_ref, g_ref, o_ref, s_scr):
    ks, vs = _kv_variants(k_ref[...], v_ref[...])
    _gqa_heads(q_ref, g_ref, o_ref, sink_ref, ks, vs, lambda kt: None, s_scr)


def _attn_ctx(sink, q, k, v, g, seq_len):
    t = q.shape[0]
    row = lambda b: (b, 0)
    return pl.pallas_call(
        _attn_ctx_kernel,
        grid=(t // seq_len,),
        in_specs=[pl.BlockSpec(memory_space=pltpu.SMEM),
                  pl.BlockSpec((seq_len, A_WIDTH), row),
                  pl.BlockSpec((seq_len, A_KV_WIDTH), row),
                  pl.BlockSpec((seq_len, A_KV_WIDTH), row),
                  pl.BlockSpec((seq_len, A_WIDTH), row)],
        out_specs=pl.BlockSpec((seq_len, A_WIDTH), row),
        out_shape=jax.ShapeDtypeStruct((t, A_WIDTH), BF16),
        scratch_shapes=[pltpu.VMEM((2, 2 * seq_len, seq_len), F32)],
        compiler_params=_cparams("arbitrary"),
        name="swa_context_attention",
    )(sink, q, k, v, g)


BAND_Q = 256


def _attn_band_kernel(sink_ref, q_ref, kp_ref, kc_ref, kn_ref, vp_ref, vc_ref, vn_ref, kx_ref, vx_ref, g_ref,
                      o_ref, s_scr, *, nsteps):
    n = pl.program_id(1)
    tq = q_ref.shape[0]
    blk = WINDOW_BLK
    k = jnp.concatenate([kp_ref[...], kc_ref[...], kn_ref[...], kx_ref[...]], axis=0)
    v = jnp.concatenate([vp_ref[...], vc_ref[...], vn_ref[...], vx_ref[...]], axis=0)
    nloc = tq + 2 * blk
    qi = _row_iota((2 * tq, GQA_KEY_TILE)) & (tq - 1)

    def valid_tile(kt):
        if kt * GQA_KEY_TILE >= nloc:
            return None
        col = _lane_iota((2 * tq, GQA_KEY_TILE)) + kt * GQA_KEY_TILE
        rel = col - blk - qi
        return ((rel >= -blk) & (rel <= blk) & ((col >= blk) | (n > 0)) & ((col < nloc - blk) | (n < nsteps - 1)))

    assert nloc % GQA_KEY_TILE == 0
    ks, vs = _kv_variants(k, v)
    _gqa_heads(q_ref, g_ref, o_ref, sink_ref, ks, vs, valid_tile, s_scr)


def _attn_band(sink, q, k, v, k_ctx, v_ctx, g, nb, seq_len):
    t = q.shape[0]
    blk = WINDOW_BLK
    tq = BAND_Q
    assert tq & (tq - 1) == 0 and tq % blk == 0 and seq_len % tq == 0
    per = tq // blk
    nblk = seq_len // blk
    nsteps = seq_len // tq
    nctx = k_ctx.shape[1]
    cur = lambda b, n: (b * nsteps + n, 0)
    prv = lambda b, n: (b * nblk + jnp.maximum(n * per - 1, 0), 0)
    nxt = lambda b, n: (b * nblk + jnp.minimum((n + 1) * per, nblk - 1), 0)
    ctx = lambda b, n: (b, 0, 0)
    edge = lambda f: pl.BlockSpec((blk, A_KV_WIDTH), f)
    mid = pl.BlockSpec((tq, A_KV_WIDTH), cur)
    return pl.pallas_call(
        functools.partial(_attn_band_kernel, nsteps=nsteps),
        grid=(nb, nsteps),
        in_specs=[pl.BlockSpec(memory_space=pltpu.SMEM),
                  pl.BlockSpec((tq, A_WIDTH), cur),
                  edge(prv), mid, edge(nxt),
                  edge(prv), mid, edge(nxt),
                  pl.BlockSpec((None, nctx, A_KV_WIDTH), ctx),
                  pl.BlockSpec((None, nctx, A_KV_WIDTH), ctx),
                  pl.BlockSpec((tq, A_WIDTH), cur)],
        out_specs=pl.BlockSpec((tq, A_WIDTH), cur),
        out_shape=jax.ShapeDtypeStruct((t, A_WIDTH), BF16),
        scratch_shapes=[pltpu.VMEM((2, 2 * tq, tq + 2 * blk + nctx), F32)],
        compiler_params=_cparams("arbitrary", "arbitrary"),
        name="swa_banded_attention",
    )(sink, q, k, k, k, v, v, v, k_ctx, v_ctx, g)


MLA_COLS = (Q_LORA, KV_LORA, LANES, MLA_WIDTH)


def _ab_out_mla_in_kernel(*refs, rope):
    (a_ref, yf_ref, yb_ref, z_ref, gnw_ref, wout_ref, x_ref, gate_ref), refs = refs[:8], refs[8:]
    if rope:
        (nw_ref, sc_ref, sh_ref, w_ref, qnw_ref, kvnw_ref, wuq_ref, cos_ref, sin_ref,
         x1_ref, qn_ref, qpe_ref, ckv_ref, kpe_ref, g_ref, ckv32_ref, kpe32_ref) = refs
    else:
        (nw_ref, sc_ref, sh_ref, w_ref, qnw_ref, kvnw_ref, wuq_ref,
         x1_ref, qn_ref, qpe_ref, ckv_ref, kpe_ref, g_ref, ckv32_ref, kpe32_ref) = refs
    offs = [int(o) for o in np.concatenate([[0], np.cumsum(MLA_COLS)])]
    nope_w = MLA_HEADS * MLA_NOPE

    def rms(u, w):
        return (u * lax.rsqrt(jnp.mean(u * u, axis=-1, keepdims=True) + EPS)) * w

    for rows in _row_splits(x_ref.shape[0]):
        y = (yf_ref[rows, :].astype(F32) + yb_ref[rows, :].astype(F32)) * _silu(z_ref[rows, :].astype(F32))
        ms = jnp.mean(y * y, axis=-1, keepdims=True)
        s = ((y * lax.rsqrt(ms + EPS)) * gnw_ref[...]).astype(BF16)
        out = _dot(a_ref[rows, :], wout_ref[0:A_WIDTH, :]) + _dot(s, wout_ref[A_WIDTH:, :])
        x1 = x_ref[rows, :] + gate_ref[...] * out
        x1_ref[rows, :] = x1
        h = _norm_mod(x1, nw_ref[...], sc_ref[...], sh_ref[...]).astype(BF16)
        cq = _dot(h, w_ref[:, offs[0]:offs[1]])
        ckv = _dot(h, w_ref[:, offs[1]:offs[2]])
        kpe = _dot(h, w_ref[:, offs[2]:offs[3]])
        g_ref[rows, :] = _dot(h, w_ref[:, offs[3]:offs[4]]).astype(g_ref.dtype)
        cqn = rms(cq, qnw_ref[...]).astype(BF16)
        qn_ref[rows, :] = _dot(cqn, wuq_ref[:, 0:nope_w]).astype(qn_ref.dtype)
        qpe = _dot(cqn, wuq_ref[:, nope_w:])
        if rope:
            qpe = _rope(qpe, cos_ref[rows, :], sin_ref[rows, :], MLA_ROPE // 4)
            kpe = _rope(kpe, cos_ref[rows, :], sin_ref[rows, :], MLA_ROPE // 4)
        qpe_ref[rows, :] = qpe.astype(qpe_ref.dtype)
        ckvn = rms(ckv, kvnw_ref[...])
        ckv_ref[rows, :] = ckvn.astype(ckv_ref.dtype)
        kpe_ref[rows, :] = kpe.astype(kpe_ref.dtype)
        ckv32_ref[rows, :] = ckvn
        kpe32_ref[rows, :] = kpe


def _ab_out_mla_in(a, yf, yb, z, gnw, wout, x, gate, nw, scale, shift, w, qnw, kvnw, wuq, rope_tabs, seq_len, tm):
    t, d = x.shape
    per_seq = seq_len // tm if scale.shape[0] > 1 else None
    row = lambda i: (i, 0)
    const = lambda i: (0, 0)
    mod = (lambda i: (i // per_seq, 0, 0)) if per_seq else (lambda i: (0, 0, 0))
    in_specs = [pl.BlockSpec((tm, A_WIDTH), row), pl.BlockSpec((tm, SSD_INNER), row),
                pl.BlockSpec((tm, SSD_INNER), row), pl.BlockSpec((tm, SSD_INNER), row),
                pl.BlockSpec((1, SSD_INNER), const), pl.BlockSpec(wout.shape, const),
                pl.BlockSpec((tm, d), row), pl.BlockSpec((None, 1, d), mod),
                pl.BlockSpec((1, d), const),
                pl.BlockSpec((None, 1, d), mod), pl.BlockSpec((None, 1, d), mod),
                pl.BlockSpec(w.shape, const), pl.BlockSpec((1, Q_LORA), const),
                pl.BlockSpec((1, KV_LORA), const), pl.BlockSpec(wuq.shape, const)]
    args = [a, yf, yb, z, gnw, wout, x, gate, nw, scale, shift, w, qnw, kvnw, wuq]
    if rope_tabs is not None:
        nt = seq_len // tm
        pos = lambda i: (i % nt, 0)
        in_specs += [pl.BlockSpec((tm, LANES), pos), pl.BlockSpec((tm, LANES), pos)]
        args += list(rope_tabs)
    widths = (d, MLA_HEADS * MLA_NOPE, MLA_HEADS * MLA_ROPE, KV_LORA, LANES, MLA_WIDTH, KV_LORA, LANES)
    dts = (F32, BF16, BF16, BF16, BF16, BF16, F32, F32)
    return pl.pallas_call(
        functools.partial(_ab_out_mla_in_kernel, rope=rope_tabs is not None),
        grid=(t // tm,),
        in_specs=in_specs,
        out_specs=[pl.BlockSpec((tm, c), row) for c in widths],
        out_shape=[jax.ShapeDtypeStruct((t, c), dt) for c, dt in zip(widths, dts)],
        compiler_params=_cparams("arbitrary"),
        name="ab_out_mla_in_proj",
    )(*args)


MLA_UNIT_ROWS = 512
MLA_KEY_TILE = 256


def _mla_attn_kernel(qn_ref, qpe_ref, ckv_ref, kpe_ref, wukt_ref, wuv_ref, g_ref, o_ref, kcat_scr, v_scr, s_scr, *,
                     pairs_per_step):
    pg = pl.program_id(1)
    qb = pl.program_id(2)
    tq = qn_ref.shape[0]
    c_exp = MLA_SCALE * LOG2E

    @pl.when(qb == 0)
    def _expand():
        ckv = ckv_ref[...]
        eye = jnp.where(_row_iota((LANES, LANES)) == _lane_iota((LANES, LANES)), 1.0, 0.0).astype(BF16)
        kpe_t = _dot_nt(eye, kpe_ref[...]).astype(BF16)
        for i in range(pairs_per_step):
            kcat_scr[i, 0:LANES, :] = _dot_nt(wukt_ref[i * LANES:(i + 1) * LANES, :], ckv).astype(BF16)
            kcat_scr[i, LANES:2 * LANES, :] = kpe_t
            v_scr[i] = _dot(ckv, wuv_ref[:, i * LANES:(i + 1) * LANES]).astype(BF16)

    nkeys = kcat_scr.shape[-1]
    kt_w = min(MLA_KEY_TILE, nkeys)
    nkt = nkeys // kt_w
    ru = min(MLA_UNIT_ROWS, tq)
    lane = _lane_iota((ru, LANES))
    units = [(i, r, e) for i in range(pairs_per_step) for r in range(tq // ru) for e in range(2)]

    def lane_tiles(x):
        return [x[:, j * LANES:(j + 1) * LANES] for j in range(x.shape[1] // LANES)]

    def q_ext(u):
        i, r, e = units[u]
        rows = slice(r * ru, (r + 1) * ru)
        qn = qn_ref[rows, i * LANES:(i + 1) * LANES]
        qpe = qpe_ref[rows, (i // 2) * LANES:(i // 2 + 1) * LANES]
        slot = 2 * (i % 2) + e if pairs_per_step % 2 == 0 else 2 * ((pg * pairs_per_step + i) % 2) + e
        nope_half = (lane < MLA_NOPE) if e == 0 else (lane >= MLA_NOPE)
        return jnp.concatenate([jnp.where(nope_half, qn, jnp.zeros_like(qn)),
                                jnp.where((lane >> 5) == slot, qpe, jnp.zeros_like(qpe))], axis=1)

    def score_tile(u, st, kt):
        cols = slice(kt * kt_w, (kt + 1) * kt_w)
        s = _dot(st["q"], kcat_scr[units[u][0], :, cols])
        s_scr[u % 2, :, cols] = s
        for t in lane_tiles(s):
            st["mrun"] = t if st["mrun"] is None else jnp.maximum(st["mrun"], t)

    def value_tile(u, st, kt):
        cols = slice(kt * kt_w, (kt + 1) * kt_w)
        p = jnp.exp2((s_scr[u % 2, :, cols] - st["m"]) * c_exp)
        for t in lane_tiles(p):
            st["lrun"] = t if st["lrun"] is None else st["lrun"] + t
        pv = _dot(p.astype(BF16), v_scr[units[u][0], cols, :])
        st["acc"] = pv if st["acc"] is None else st["acc"] + pv

    def finish(u, st, done):
        i, r, e = units[u]
        den = jnp.sum(st["lrun"], axis=1, keepdims=True)
        done[e] = st["acc"] / den
        if e == 1:
            rows = slice(r * ru, (r + 1) * ru)
            attn = jnp.where(lane < MLA_V, done[0], done[1])
            gate = g_ref[rows, i * LANES:(i + 1) * LANES].astype(F32)
            o_ref[rows, i * LANES:(i + 1) * LANES] = (attn * _silu(gate)).astype(o_ref.dtype)

    done = {}
    cur = None
    for u in range(len(units) + 1):
        nxt = dict(q=q_ext(u), mrun=None) if u < len(units) else None
        for kt in range(nkt):
            if nxt is not None:
                score_tile(u, nxt, kt)
            if cur is not None:
                value_tile(u - 1, cur, kt)
        if cur is not None:
            finish(u - 1, cur, done)
        if nxt is not None:
            nxt.update(m=jnp.max(nxt["mrun"], axis=1, keepdims=True), lrun=None, acc=None)
        cur = nxt


def _mla_attn(qn, qpe, ckv_keys, kpe_keys, wukt, wuv, g, nb, seq_len, tq, pairs_per_step):
    t = qn.shape[0]
    nkeys = ckv_keys.shape[1]
    npairs = MLA_HEADS // 2
    ngrp = npairs // pairs_per_step
    nq = seq_len // tq
    wp = pairs_per_step * LANES
    if pairs_per_step % 2 == 0:
        wpe = wp // 2
        pe_map = lambda b, p, i: (b * nq + i, p)
    else:
        wpe = LANES
        pe_map = lambda b, p, i: (b * nq + i, p // 2)
    qmap = lambda b, p, i: (b * nq + i, p)
    return pl.pallas_call(
        functools.partial(_mla_attn_kernel, pairs_per_step=pairs_per_step),
        grid=(nb, ngrp, nq),
        in_specs=[pl.BlockSpec((tq, wp), qmap),
                  pl.BlockSpec((tq, wpe), pe_map),
                  pl.BlockSpec((None, nkeys, KV_LORA), lambda b, p, i: (b, 0, 0)),
                  pl.BlockSpec((None, nkeys, LANES), lambda b, p, i: (b, 0, 0)),
                  pl.BlockSpec((wp, KV_LORA), lambda b, p, i: (p, 0)),
                  pl.BlockSpec((KV_LORA, wp), lambda b, p, i: (0, p)),
                  pl.BlockSpec((tq, wp), qmap)],
        out_specs=pl.BlockSpec((tq, wp), qmap),
        out_shape=jax.ShapeDtypeStruct((t, MLA_WIDTH), BF16),
        scratch_shapes=[pltpu.VMEM((pairs_per_step, 2 * LANES, nkeys), BF16),
                        pltpu.VMEM((pairs_per_step, nkeys, LANES), BF16),
                        pltpu.VMEM((2, min(MLA_UNIT_ROWS, tq), nkeys), F32)],
        compiler_params=_cparams("arbitrary", "arbitrary", "arbitrary"),
        name="mla_attention",
    )(qn, qpe, ckv_keys, kpe_keys, wukt, wuv, g)


def _mla_attn_seq_kernel(qn_ref, qpe_ref, ckv_ref, kpe_ref, wukt_ref, wuv_ref, g_ref, o_ref,
                         kcat_scr, v_scr, s_scr, m_scr, half_scr):
    pair = pl.program_id(1)
    seq = qn_ref.shape[0]
    nkeys = kcat_scr.shape[-1]
    ru, kt_w = MLA_UNIT_ROWS, MLA_KEY_TILE
    nkt = nkeys // kt_w
    nrb = seq // ru
    c_exp = MLA_SCALE * LOG2E
    lane = _lane_iota((ru, LANES))

    ckv = ckv_ref[...]
    eye = jnp.where(_row_iota((LANES, LANES)) == _lane_iota((LANES, LANES)), 1.0, 0.0).astype(BF16)
    kcat_scr[0:LANES, :] = _dot_nt(wukt_ref[...], ckv).astype(BF16)
    kcat_scr[LANES:2 * LANES, :] = _dot_nt(eye, kpe_ref[...]).astype(BF16)
    v_scr[:, 0:LANES] = _dot(ckv, wuv_ref[...]).astype(BF16)
    v_scr[:, LANES:2 * LANES] = jnp.ones((nkeys, LANES), BF16)

    def rows_of(rb):
        return slice(rb * ru, (rb + 1) * ru) if isinstance(rb, int) else pl.ds(pl.multiple_of(rb * ru, ru), ru)

    def q_ext(rb, e):
        rows = rows_of(rb)
        qn = qn_ref[rows, :]
        qpe = qpe_ref[rows, :]
        nope_half = (lane < MLA_NOPE) if e == 0 else (lane >= MLA_NOPE)
        slot = 2 * (pair % 2) + e
        return jnp.concatenate([jnp.where(nope_half, qn, jnp.zeros_like(qn)),
                                jnp.where((lane >> 5) == slot, qpe, jnp.zeros_like(qpe))], axis=1)

    def phase(score, value):
        q = q_ext(*score) if score is not None else None
        mrun = acc = None
        for kt in range(nkt):
            cols = slice(kt * kt_w, (kt + 1) * kt_w)
            if score is not None:
                s = _dot(q, kcat_scr[:, cols])
                s_scr[score[1], :, cols] = s
                for j in range(kt_w // LANES):
                    t = s[:, j * LANES:(j + 1) * LANES]
                    mrun = t if mrun is None else jnp.maximum(mrun, t)
            if value is not None:
                mb = m_scr[value]
                p = jnp.concatenate(
                    [jnp.exp2((s_scr[value, :, kt * kt_w + j * LANES:kt * kt_w + (j + 1) * LANES] - mb) * c_exp)
                     for j in range(kt_w // LANES)], axis=1)
                pv = _dot(p.astype(BF16), v_scr[cols, :])
                acc = pv if acc is None else acc + pv
        if score is not None:
            m_scr[score[1]] = jnp.broadcast_to(jnp.max(mrun, axis=1, keepdims=True), (ru, LANES))
        return None if acc is None else acc[:, 0:LANES] / acc[:, LANES:2 * LANES]

    phase((0, 0), None)

    def row_block(rb, carry):
        half_scr[...] = phase((rb, 1), 0)
        out1 = phase((jnp.minimum(rb + 1, nrb - 1), 0), 1)
        rows = rows_of(rb)
        attn = jnp.where(lane < MLA_V, half_scr[...], out1)
        o_ref[rows, :] = (attn * _silu(g_ref[rows, :].astype(F32))).astype(o_ref.dtype)
        return carry

    lax.fori_loop(0, nrb, row_block, 0)


def _mla_attn_seq(qn, qpe, ckv_keys, kpe_keys, wukt, wuv, g, nb, seq_len):
    t = qn.shape[0]
    nkeys = ckv_keys.shape[1]
    assert seq_len % MLA_UNIT_ROWS == 0 and nkeys % MLA_KEY_TILE == 0
    qmap = lambda b, p: (b, p)
    return pl.pallas_call(
        _mla_attn_seq_kernel,
        grid=(nb, MLA_HEADS // 2),
        in_specs=[pl.BlockSpec((seq_len, LANES), qmap),
                  pl.BlockSpec((seq_len, LANES), lambda b, p: (b, p // 2)),
                  pl.BlockSpec((None, nkeys, KV_LORA), lambda b, p: (b, 0, 0)),
                  pl.BlockSpec((None, nkeys, LANES), lambda b, p: (b, 0, 0)),
                  pl.BlockSpec((LANES, KV_LORA), lambda b, p: (p, 0)),
                  pl.BlockSpec((KV_LORA, LANES), lambda b, p: (0, p)),
                  pl.BlockSpec((seq_len, LANES), qmap)],
        out_specs=pl.BlockSpec((seq_len, LANES), qmap),
        out_shape=jax.ShapeDtypeStruct((t, MLA_WIDTH), BF16),
        scratch_shapes=[pltpu.VMEM((2 * LANES, nkeys), BF16),
                        pltpu.VMEM((nkeys, 2 * LANES), BF16),
                        pltpu.VMEM((2, MLA_UNIT_ROWS, nkeys), F32),
                        pltpu.VMEM((2, MLA_UNIT_ROWS, LANES), F32),
                        pltpu.VMEM((MLA_UNIT_ROWS, LANES), F32)],
        compiler_params=_cparams("arbitrary", "arbitrary"),
        name="mla_attention_seq",
    )(qn, qpe, ckv_keys, kpe_keys, wukt, wuv, g)


def _mla_out_kernel(a_ref, w_ref, x_ref, gate_ref, fw_ref, o_ref):
    for rows in _row_splits(x_ref.shape[0]):
        xn = x_ref[rows, :] + gate_ref[...] * _dot(a_ref[rows, :], w_ref[...])
        ms = jnp.mean(xn * xn, axis=-1, keepdims=True)
        o_ref[rows, :] = (xn * lax.rsqrt(ms + EPS)) * fw_ref[...]


def _mla_out(a, w, x, gate, fw, seq_len, tm):
    t, d = x.shape
    per_seq = seq_len // tm if gate.shape[0] > 1 else None
    row = lambda i: (i, 0)
    const = lambda i: (0, 0)
    mod = (lambda i: (i // per_seq, 0, 0)) if per_seq else (lambda i: (0, 0, 0))
    return pl.pallas_call(
        _mla_out_kernel,
        grid=(t // tm,),
        in_specs=[pl.BlockSpec((tm, MLA_WIDTH), row), pl.BlockSpec(w.shape, const),
                  pl.BlockSpec((tm, d), row), pl.BlockSpec((None, 1, d), mod), pl.BlockSpec((1, d), const)],
        out_specs=pl.BlockSpec((tm, d), row),
        out_shape=jax.ShapeDtypeStruct((t, d), F32),
        compiler_params=_cparams("arbitrary"),
        name="mla_out_proj_final_norm",
    )(a, w, x, gate, fw)


def _rope_tables(length, dim):
    rows = length // GRID_W
    f32 = np.float32
    row = np.repeat(np.arange(rows), GRID_W).astype(f32)
    col = np.tile(np.arange(GRID_W), rows).astype(f32)
    nf = dim // 4
    inv = (f32(1.0) / np.power(f32(ROPE_BASE), np.arange(nf, dtype=f32) / f32(nf))).astype(f32)
    ar = row[:, None] * inv[None, :]
    ac = col[:, None] * inv[None, :]
    ang = np.concatenate([ar, ar, ac, ac], axis=-1).astype(f32)
    sign = np.tile(np.concatenate([-np.ones((nf,), f32), np.ones((nf,), f32)]), 2)
    reps = LANES // dim
    return (jnp.asarray(np.tile(np.cos(ang).astype(f32), (1, reps))),
            jnp.asarray(np.tile((np.sin(ang) * sign).astype(f32), (1, reps))))


PROJ_TM = 1024
CONV_TM = 512
FUSED_TM = 512


def _group(x, seq_len, conds_rows, mods, wts, tabs_a, tabs_c, ctx_cache):
    nb, _, d = x.shape
    t = nb * seq_len
    latent = ctx_cache is not None
    xf = x.reshape(t, d)
    tm = PROJ_TM

    def mod_rows(layer):
        m = mods[layer][conds_rows]
        sh, sc, gt = jnp.split(m[:, None, :], 3, axis=-1)
        return sh, sc, gt

    sh, sc, gt = mod_rows(0)
    q, k, v, g, z, xbc, dt, *kv_t = _ab_in(xf, wts["norm_w"][0:1], sc, sh, wts["ab_w_in"], wts["ab_w_dt"],
                                           tabs_a if latent else None, seq_len, tm)
    xbc_c = _conv(xbc, wts["ab_conv_w"], wts["ab_conv_b"], seq_len, min(CONV_TM, seq_len))
    if latent:
        k_ctx, v_ctx, s_f0, s_b0, _, _ = ctx_cache
        attn = _attn_band(wts["ab_sink"], q, k, v, k_ctx, v_ctx, g, nb, seq_len)
    else:
        s_f0 = s_b0 = None
        attn = _attn_ctx(wts["ab_sink"], q, k, v, g, seq_len)
    yf, yb, s_f, s_b = _ssd(xbc_c, dt, wts["ssd_par"], wts["ssd_dskip"], s_f0, s_b0, nb, seq_len)

    gt0 = gt
    sh, sc, gt = mod_rows(1)
    x1, qn, qpe, ckv, kpe, g1, ckv32, kpe32 = _ab_out_mla_in(
        attn, yf, yb, z, wts["ab_gnorm_w"], wts["ab_w_out"], xf, gt0,
        wts["norm_w"][1:2], sc, sh, wts["mla_w_in"], wts["mla_q_norm_w"], wts["mla_kv_norm_w"],
        wts["mla_w_uq"], tabs_c if latent else None, seq_len, FUSED_TM)
    ckv_keys = ckv.reshape(nb, seq_len, KV_LORA)
    kpe_keys = kpe.reshape(nb, seq_len, LANES)
    if latent:
        ckv_x, kpe_x = ctx_cache[4], ctx_cache[5]
        ckv_keys = jnp.concatenate([ckv_keys, ckv_x], axis=1)
        kpe_keys = jnp.concatenate([kpe_keys, kpe_x], axis=1)
        attn1 = _mla_attn_seq(qn, qpe, ckv_keys, kpe_keys, wts["mla_w_ukt"], wts["mla_w_uv"], g1, nb, seq_len)
    else:
        attn1 = _mla_attn(qn, qpe, ckv_keys, kpe_keys, wts["mla_w_ukt"], wts["mla_w_uv"], g1, nb, seq_len,
                          seq_len, MLA_HEADS // 2)
    y = _mla_out(attn1, wts["mla_w_out"], x1, gt, wts["final_norm_w"], seq_len, tm)
    return y.reshape(nb, seq_len, d), (kv_t, s_f, s_b, ckv32, kpe32)


def kernel(x_prompt, x_sample, cache_a_k, cache_a_v, state_ssd_fwd, state_ssd_bwd, cache_mla_ckv, cache_mla_kpe,
           c, c_ctx, ada_w, ada_b, norm_w, ab_w_in, ab_sink, ab_conv_w, ab_conv_b, ab_dt_bias, ab_a_log,
           ab_d_skip, ab_gnorm_w, ab_w_out, mla_w_in, mla_q_norm_w, mla_kv_norm_w, mla_w_uq, mla_w_ukv,
           mla_w_out, final_norm_w):
    batch, seq, d = x_prompt.shape
    dec_batch, dec_seq, _ = x_sample.shape
    assert ada_w.shape[0] == 2 and ab_w_in.shape[0] == 1 and mla_w_in.shape[0] == 1
    assert dec_batch + 1 <= SUBLANES

    conds = jnp.concatenate([c_ctx[None, :], c, jnp.zeros((SUBLANES - 1 - dec_batch, d), F32)], axis=0)
    mods = _modulation(conds, ada_w, ada_b)

    n_main = sum(AB_COLS[:-1])
    w_ab = ab_w_in[0][:, :n_main].astype(BF16)
    w_dt = jnp.concatenate([ab_w_in[0][:, n_main:], jnp.zeros((d, LANES - 2 * SSD_HEADS), F32)], axis=1).astype(BF16)
    mw = mla_w_in[0]
    o_kpe = Q_LORA + KV_LORA
    w_mla = jnp.concatenate([mw[:, :o_kpe], jnp.tile(mw[:, o_kpe:o_kpe + MLA_ROPE], (1, LANES // MLA_ROPE)),
                             mw[:, o_kpe + MLA_ROPE:]], axis=1).astype(BF16)
    wuq = mla_w_uq[0].reshape(Q_LORA, MLA_HEADS, MLA_NOPE + MLA_ROPE)
    wuq = jnp.concatenate([wuq[:, :, :MLA_NOPE].reshape(Q_LORA, -1), wuq[:, :, MLA_NOPE:].reshape(Q_LORA, -1)],
                          axis=1).astype(BF16)
    wukv = mla_w_ukv[0].reshape(KV_LORA, MLA_HEADS, MLA_NOPE + MLA_V)
    wukt = wukv[:, :, :MLA_NOPE].reshape(KV_LORA, -1).T.astype(BF16)
    wuv = wukv[:, :, MLA_NOPE:].reshape(KV_LORA, -1).astype(BF16)
    pad_lanes = lambda r: jnp.concatenate([r.reshape(1, -1), jnp.zeros((1, LANES - r.size), F32)], axis=1)
    ssd_par = jnp.concatenate([pad_lanes(ab_dt_bias[0]), pad_lanes(ab_a_log[0]),
                               jnp.zeros((SUBLANES - 2, LANES), F32)], axis=0)
    wts = dict(
        norm_w=norm_w, ab_w_in=w_ab, ab_w_dt=w_dt, ab_sink=ab_sink[0],
        ab_conv_w=jnp.concatenate([ab_conv_w[0], jnp.zeros((SUBLANES - CONV_K, CONV_CH), F32)], axis=0),
        ab_conv_b=ab_conv_b[0][None, :], ssd_par=ssd_par,
        ssd_dskip=jnp.repeat(ab_d_skip[0], SSD_HEAD_DIM)[None, :],
        ab_gnorm_w=ab_gnorm_w[0][None, :], ab_w_out=ab_w_out[0].astype(BF16),
        mla_w_in=w_mla, mla_q_norm_w=mla_q_norm_w[0][None, :], mla_kv_norm_w=mla_kv_norm_w[0][None, :],
        mla_w_uq=wuq, mla_w_ukt=wukt, mla_w_uv=wuv, mla_w_out=mla_w_out[0].astype(BF16),
        final_norm_w=final_norm_w[None, :],
    )
    tabs_a = _rope_tables(dec_seq, A_HEAD_DIM)
    tabs_c = _rope_tables(dec_seq, MLA_ROPE)

    y_prompt, ((k_t, v_t), s_f, s_b, ckv32, kpe32) = _group(
        x_prompt, seq, jnp.zeros((1,), jnp.int32), mods, wts, tabs_a, tabs_c, None)

    past = cache_a_k.shape[2]
    ctx_cache = (cache_a_k[:, 0].reshape(dec_batch, past, A_KV_WIDTH),
                 cache_a_v[:, 0].reshape(dec_batch, past, A_KV_WIDTH),
                 state_ssd_fwd[:, 0], state_ssd_bwd[:, 0],
                 cache_mla_ckv[:, 0].astype(BF16),
                 jnp.tile(cache_mla_kpe[:, 0], (1, 1, LANES // MLA_ROPE)).astype(BF16))
    y_sample, _ = _group(x_sample, dec_seq, 1 + jnp.arange(dec_batch), mods, wts, tabs_a, tabs_c, ctx_cache)

    cache_layout = lambda u: u.reshape(batch, 1, A_KV_HEADS, A_HEAD_DIM, seq).transpose(0, 1, 4, 2, 3)
    return (y_prompt, y_sample,
            cache_layout(k_t), cache_layout(v_t),
            s_f[:, None], s_b[:, None],
            ckv32.reshape(batch, 1, seq, KV_LORA), kpe32[:, :MLA_ROPE].reshape(batch, 1, seq, MLA_ROPE))
```

```python
import functools
import math

import jax
import jax.numpy as jnp
import numpy as np
from jax import lax
from jax.experimental import pallas as pl
from jax.experimental.pallas import tpu as pltpu

F32 = jnp.float32
BF16 = jnp.bfloat16

LANES = 128
SUBLANES = 8
VMEM_LIMIT_BYTES = 56 * 1024 * 1024

GRID_W = 64
ROPE_BASE = 10000.0
EPS = 1e-6
NEG_INF = -1e30
WINDOW_BLK = 128
CHUNK = 128
A_HEADS, A_KV_HEADS, A_HEAD_DIM = 8, 2, 64
A_WIDTH = A_HEADS * A_HEAD_DIM
A_KV_WIDTH = A_KV_HEADS * A_HEAD_DIM
A_SCALE = A_HEAD_DIM ** -0.5
SSD_HEADS, SSD_HEAD_DIM, SSD_GROUPS, SSD_STATE = 16, 64, 2, 64
SSD_INNER = SSD_HEADS * SSD_HEAD_DIM
CONV_K = 5
BC_WIDTH = SSD_GROUPS * SSD_STATE
CONV_CH = SSD_INNER + 2 * BC_WIDTH
MLA_HEADS, MLA_NOPE, MLA_ROPE, MLA_V = 16, 64, 32, 64
Q_LORA, KV_LORA = 256, 128
MLA_WIDTH = MLA_HEADS * MLA_V
MLA_SCALE = (MLA_NOPE + MLA_ROPE) ** -0.5
LOG2E = 1.4426950408889634


def _cparams(*sem):
    return pltpu.CompilerParams(dimension_semantics=sem, vmem_limit_bytes=VMEM_LIMIT_BYTES)


def _silu(x):
    return x * (1.0 / (1.0 + jnp.exp(-x)))


def _dot(a, b):
    return jnp.dot(a, b, preferred_element_type=F32, precision=lax.Precision.DEFAULT)


def _dot_nt(a, b):
    return lax.dot_general(a, b, (((1,), (1,)), ((), ())), preferred_element_type=F32,
                           precision=lax.Precision.DEFAULT)


def _row_splits(m, parts=2):
    step = m // parts
    return [slice(r * step, (r + 1) * step) for r in range(parts)]


def _lane_iota(shape):
    return lax.broadcasted_iota(jnp.int32, shape, len(shape) - 1)


def _row_iota(shape):
    return lax.broadcasted_iota(jnp.int32, shape, len(shape) - 2)


def _mod_kernel(cond_ref, w_ref, b_ref, o_ref):
    s = _silu(cond_ref[...])
    o_ref[...] = _dot(s.astype(BF16), w_ref[...].astype(BF16)) + b_ref[...]


def _modulation(conds, ada_w, ada_b):
    depth, d, d3 = ada_w.shape
    tn = 768
    return pl.pallas_call(
        _mod_kernel,
        grid=(depth, d3 // tn),
        in_specs=[pl.BlockSpec((SUBLANES, d), lambda l, j: (0, 0)),
                  pl.BlockSpec((None, d, tn), lambda l, j: (l, 0, j)),
                  pl.BlockSpec((None, 1, tn), lambda l, j: (l, 0, j))],
        out_specs=pl.BlockSpec((None, SUBLANES, tn), lambda l, j: (l, 0, j)),
        out_shape=jax.ShapeDtypeStruct((depth, SUBLANES, d3), F32),
        compiler_params=_cparams("arbitrary", "arbitrary"),
        name="modulation",
    )(conds, ada_w, ada_b.reshape(depth, 1, d3))


def _norm_mod(x, nw, scale, shift):
    ms = jnp.mean(x * x, axis=-1, keepdims=True)
    y = (x * lax.rsqrt(ms + EPS)) * nw
    return y * (1.0 + scale) + shift


def _rope(x, cos, sin_signed, half_period):
    outs = []
    first = (_lane_iota((x.shape[0], LANES)) & (2 * half_period - 1)) < half_period
    for j in range(x.shape[1] // LANES):
        xj = x[:, j * LANES:(j + 1) * LANES]
        up = pltpu.roll(xj, LANES - half_period, 1)
        dn = pltpu.roll(xj, half_period, 1)
        outs.append(xj * cos + jnp.where(first, up, dn) * sin_signed)
    return outs[0] if len(outs) == 1 else jnp.concatenate(outs, axis=1)


AB_COLS = (A_WIDTH, A_KV_WIDTH, A_KV_WIDTH, A_WIDTH, SSD_INNER, CONV_CH, LANES)


def _ab_in_kernel(*refs, rope, seq_len):
    if rope:
        (x_ref, nw_ref, sc_ref, sh_ref, w_ref, wdt_ref, cos_ref, sin_ref,
         q_ref, k_ref, v_ref, g_ref, z_ref, xbc_ref, dt_ref) = refs
        kvt_refs = {}
    else:
        (x_ref, nw_ref, sc_ref, sh_ref, w_ref, wdt_ref,
         q_ref, k_ref, v_ref, g_ref, z_ref, xbc_ref, dt_ref, kt_ref, vt_ref) = refs
        kvt_refs = {1: kt_ref, 2: vt_ref}
    offs = np.concatenate([[0], np.cumsum(AB_COLS)])
    outs = (q_ref, k_ref, v_ref, g_ref, z_ref, xbc_ref, dt_ref)
    for rows in _row_splits(x_ref.shape[0]):
        h = _norm_mod(x_ref[rows, :], nw_ref[...], sc_ref[...], sh_ref[...]).astype(BF16)
        for i, o_ref in enumerate(outs):
            last = i == len(outs) - 1
            y = _dot(h, wdt_ref[...] if last else w_ref[:, int(offs[i]):int(offs[i + 1])])
            if rope and i in (0, 1):
                y = _rope(y, cos_ref[rows, :], sin_ref[rows, :], A_HEAD_DIM // 4)
            o_ref[rows, :] = y.astype(o_ref.dtype)
            if i in kvt_refs:
                for s in range(rows.start // seq_len, rows.stop // seq_len):
                    kvt_refs[i][s] = y[s * seq_len - rows.start:(s + 1) * seq_len - rows.start, :].T


def _ab_in(x, nw, scale, shift, w, wdt, rope_tabs, seq_len, tm):
    t, d = x.shape
    per_seq = seq_len // tm if scale.shape[0] > 1 else None
    row = lambda i: (i, 0)
    mod = (lambda i: (i // per_seq, 0, 0)) if per_seq else (lambda i: (0, 0, 0))
    const = lambda i: (0, 0)
    in_specs = [pl.BlockSpec((tm, d), row),
                pl.BlockSpec((1, d), const),
                pl.BlockSpec((None, 1, d), mod),
                pl.BlockSpec((None, 1, d), mod),
                pl.BlockSpec(w.shape, const),
                pl.BlockSpec(wdt.shape, const)]
    args = [x, nw, scale, shift, w, wdt]
    if rope_tabs is not None:
        nt = seq_len // tm
        pos = lambda i: (i % nt, 0)
        in_specs += [pl.BlockSpec((tm, LANES), pos), pl.BlockSpec((tm, LANES), pos)]
        args += list(rope_tabs)
    dts = (BF16, F32, F32, BF16, BF16, BF16, F32)
    out_specs = [pl.BlockSpec((tm, c), row) for c in AB_COLS]
    out_shape = [jax.ShapeDtypeStruct((t, c), dt) for c, dt in zip(AB_COLS, dts)]
    if rope_tabs is None:
        assert (tm // 2) % seq_len == 0
        spt = tm // seq_len
        out_specs += [pl.BlockSpec((spt, A_KV_WIDTH, seq_len), lambda i: (i, 0, 0))] * 2
        out_shape += [jax.ShapeDtypeStruct((t // seq_len, A_KV_WIDTH, seq_len), F32)] * 2
    return pl.pallas_call(
        functools.partial(_ab_in_kernel, rope=rope_tabs is not None, seq_len=seq_len),
        grid=(t // tm,),
        in_specs=in_specs,
        out_specs=out_specs,
        out_shape=out_shape,
        compiler_params=_cparams("arbitrary"),
        name="ab_in_proj",
    )(*args)


def _conv_kernel(prev_ref, main_ref, next_ref, w_ref, b_ref, o_ref, *, tiles_per_seq):
    i = pl.program_id(0) % tiles_per_seq
    tm = main_ref.shape[0]
    prev = jnp.where(i > 0, prev_ref[...].astype(F32), 0.0)
    nxt = jnp.where(i < tiles_per_seq - 1, next_ref[...].astype(F32), 0.0)
    ext = jnp.concatenate([prev, main_ref[...].astype(F32), nxt], axis=0)
    first = SUBLANES - CONV_K // 2
    acc = b_ref[...] + ext[first:first + tm] * w_ref[0:1, :]
    for j in range(1, CONV_K):
        acc = acc + ext[first + j:first + j + tm] * w_ref[j:j + 1, :]
    o_ref[...] = _silu(acc).astype(o_ref.dtype)


def _conv(xbc, w, b, seq_len, tm):
    t, c = xbc.shape
    tps = seq_len // tm
    hb = tm // SUBLANES
    nblk8 = t // SUBLANES
    return pl.pallas_call(
        functools.partial(_conv_kernel, tiles_per_seq=tps),
        grid=(t // tm,),
        in_specs=[pl.BlockSpec((SUBLANES, c), lambda i: (jnp.maximum(i * hb - 1, 0), 0)),
                  pl.BlockSpec((tm, c), lambda i: (i, 0)),
                  pl.BlockSpec((SUBLANES, c), lambda i: (jnp.minimum((i + 1) * hb, nblk8 - 1), 0)),
                  pl.BlockSpec((SUBLANES, c), lambda i: (0, 0)),
                  pl.BlockSpec((1, c), lambda i: (0, 0))],
        out_specs=pl.BlockSpec((tm, c), lambda i: (i, 0)),
        out_shape=jax.ShapeDtypeStruct((t, c), BF16),
        compiler_params=_cparams("arbitrary"),
        name="ssd_conv",
    )(xbc, xbc, xbc, w, b)


def _ssd_init(h0_ref, h_scr, has_h0):
    if not has_h0:
        h_scr[...] = jnp.zeros_like(h_scr)
        return
    for pr in range(SSD_HEADS // 2):
        both = jnp.concatenate([h0_ref[2 * pr], h0_ref[2 * pr + 1]], axis=0)
        both = jnp.concatenate([both, jnp.zeros_like(both)], axis=1)
        st = both.T[0:SSD_STATE]
        lo_st = _lane_iota(st.shape) < SSD_HEAD_DIM
        h_scr[2 * pr] = jnp.where(lo_st, st, 0.0)
        h_scr[2 * pr + 1] = jnp.where(lo_st, 0.0, st)


def _ssd_final(h_scr, hout_ref):
    for pr in range(SSD_HEADS // 2):
        st = h_scr[2 * pr] + h_scr[2 * pr + 1]
        st = jnp.concatenate([st, jnp.zeros_like(st)], axis=0).T
        hout_ref[2 * pr] = st[0:SSD_HEAD_DIM, 0:SSD_STATE]
        hout_ref[2 * pr + 1] = st[SSD_HEAD_DIM:2 * SSD_HEAD_DIM, 0:SSD_STATE]


def _ssd_prelude(b_ref, c_ref, dt_ref, par_ref, direction):
    lane = _lane_iota((CHUNK, LANES))
    rowi = _row_iota((CHUNK, LANES))
    dtp_in = dt_ref[...] + par_ref[0:1, :]
    dtp = jnp.maximum(dtp_in, 0.0) + jnp.log(1.0 + jnp.exp(-jnp.abs(dtp_in)))
    la = dtp * (-jnp.exp(par_ref[1:2, :]))
    causal = (rowi >= lane) if direction == 0 else (rowi <= lane)
    tri = jnp.where(causal, 1.0, 0.0).astype(BF16)
    p1 = la.astype(BF16)
    r1 = la - p1.astype(F32)
    p2 = r1.astype(BF16)
    p3 = (r1 - p2.astype(F32)).astype(BF16)
    cum = _dot(tri, p1) + _dot(tri, p2) + _dot(tri, p3)
    cum_t = cum.T
    dtp_t = dtp.T
    end = CHUNK - 1 if direction == 0 else 0
    w_t = jnp.exp(cum_t[:, end:end + 1] - cum_t) * dtp_t
    lrow_t = cum_t - jnp.log(dtp_t)
    bm = b_ref[...]
    cm = c_ref[...]
    cb, cmask = [], []
    for g in range(SSD_GROUPS):
        in_g = (lane >= g * SSD_STATE) & (lane < (g + 1) * SSD_STATE)
        cg = jnp.where(in_g, cm, jnp.zeros_like(cm))
        cmask.append(cg.astype(F32))
        cb.append(_dot_nt(cg, bm))
    return dict(cum=cum, lrow_t=lrow_t, w_t=w_t, causal=causal, end=end,
                bm_t=bm.astype(F32).T, cb=cb, cmask=cmask, lo_half=lane < SSD_HEAD_DIM)


def _ssd_head_pair(pre, pr, x_ref, dskip_ref, y_ref, h_scr, direction):
    hp = SSD_HEADS // SSD_GROUPS
    cum, lrow_t, w_t, end = pre["cum"], pre["lrow_t"], pre["w_t"], pre["end"]
    lo_half = pre["lo_half"]
    xp = x_ref[:, pr * LANES:(pr + 1) * LANES]
    y_pair = None
    for e in range(2):
        h = 2 * pr + e
        g = h // hp
        col = direction * SSD_HEADS + h
        xe = jnp.where(lo_half if e == 0 else ~lo_half, xp, jnp.zeros_like(xp))
        cumcol = jnp.broadcast_to(cum[:, col:col + 1], (CHUNK, LANES))
        diff = cumcol - lrow_t[col:col + 1, :]
        sc = pre["cb"][g] * jnp.exp(jnp.where(pre["causal"], diff, NEG_INF))
        ce = pre["cmask"][g] * jnp.exp(cumcol)
        hs = h_scr[h]
        hsb = hs.astype(BF16)
        zpad = jnp.zeros_like(hsb)
        bw = (pre["bm_t"][g * SSD_STATE:(g + 1) * SSD_STATE, :] * w_t[col:col + 1, :]).astype(BF16)
        a = jnp.concatenate([jnp.concatenate([sc.astype(BF16), ce.astype(BF16)], axis=1),
                             jnp.concatenate([bw, jnp.zeros_like(bw)], axis=1)], axis=0)
        wmat = jnp.concatenate([xe, hsb, zpad] if g == 0 else [xe, zpad, hsb], axis=0)
        res = _dot(a, wmat)
        ye = res[0:CHUNK]
        y_pair = ye if y_pair is None else y_pair + ye
        cdec = jnp.exp(cumcol[end:end + 1, :])
        h_scr[h] = hs * cdec + res[CHUNK:CHUNK + SSD_STATE]
    if direction == 0:
        y_pair = y_pair + dskip_ref[:, pr * LANES:(pr + 1) * LANES] * xp.astype(F32)
    y_ref[:, pr * LANES:(pr + 1) * LANES] = y_pair.astype(y_ref.dtype)


def _ssd_kernel(xf_ref, bf_ref, cf_ref, dtf_ref, xb_ref, bb_ref, cb_ref, dtb_ref, par_ref, dskip_ref,
                h0f_ref, h0b_ref, yf_ref, yb_ref, houtf_ref, houtb_ref, hf_scr, hb_scr, *, nc, has_h0):
    ci = pl.program_id(1)

    @pl.when(ci == 0)
    def _init():
        _ssd_init(h0f_ref, hf_scr, has_h0)
        _ssd_init(h0b_ref, hb_scr, has_h0)

    pre_f = _ssd_prelude(bf_ref, cf_ref, dtf_ref, par_ref, 0)
    pre_b = _ssd_prelude(bb_ref, cb_ref, dtb_ref, par_ref, 1)
    for pr in range(SSD_HEADS // 2):
        _ssd_head_pair(pre_f, pr, xf_ref, dskip_ref, yf_ref, hf_scr, 0)
        _ssd_head_pair(pre_b, pr, xb_ref, dskip_ref, yb_ref, hb_scr, 1)

    @pl.when(ci == nc - 1)
    def _final():
        _ssd_final(hf_scr, houtf_ref)
        _ssd_final(hb_scr, houtb_ref)


def _ssd(xbc, dt, par, dskip, h0f, h0b, nb, seq_len):
    t = xbc.shape[0]
    nc = seq_len // CHUNK
    fwd = lambda b, c: b * nc + c
    bwd = lambda b, c: b * nc + nc - 1 - c
    xcols = SSD_INNER // LANES
    has_h0 = h0f is not None
    if not has_h0:
        h0f = h0b = jnp.zeros((1, SSD_HEADS, SSD_HEAD_DIM, SSD_STATE), F32)
    h0_map = (lambda b, c: (b, 0, 0, 0)) if has_h0 else (lambda b, c: (0, 0, 0, 0))
    st_shape = (None, SSD_HEADS, SSD_HEAD_DIM, SSD_STATE)
    carry_shape = (None, SSD_HEADS, SSD_STATE, LANES)
    per_b = lambda b, c: (b, 0, 0, 0)
    one_block = lambda b, c: (0, 0, 0, 0)

    def chunk_specs(cidx):
        return [pl.BlockSpec((CHUNK, SSD_INNER), lambda b, c: (cidx(b, c), 0)),
                pl.BlockSpec((CHUNK, LANES), lambda b, c: (cidx(b, c), xcols)),
                pl.BlockSpec((CHUNK, LANES), lambda b, c: (cidx(b, c), xcols + 1)),
                pl.BlockSpec((CHUNK, LANES), lambda b, c: (cidx(b, c), 0))]

    outs = pl.pallas_call(
        functools.partial(_ssd_kernel, nc=nc, has_h0=has_h0),
        grid=(nb, nc),
        in_specs=chunk_specs(fwd) + chunk_specs(bwd) + [
            pl.BlockSpec((SUBLANES, LANES), lambda b, c: (0, 0)),
            pl.BlockSpec((1, SSD_INNER), lambda b, c: (0, 0)),
            pl.BlockSpec(st_shape, h0_map), pl.BlockSpec(st_shape, h0_map)],
        out_specs=[pl.BlockSpec((CHUNK, SSD_INNER), lambda b, c: (fwd(b, c), 0)),
                   pl.BlockSpec((CHUNK, SSD_INNER), lambda b, c: (bwd(b, c), 0)),
                   pl.BlockSpec(st_shape, per_b), pl.BlockSpec(st_shape, per_b),
                   pl.BlockSpec(carry_shape, one_block), pl.BlockSpec(carry_shape, one_block)],
        out_shape=[jax.ShapeDtypeStruct((t, SSD_INNER), BF16), jax.ShapeDtypeStruct((t, SSD_INNER), BF16),
                   jax.ShapeDtypeStruct((nb, SSD_HEADS, SSD_HEAD_DIM, SSD_STATE), F32),
                   jax.ShapeDtypeStruct((nb, SSD_HEADS, SSD_HEAD_DIM, SSD_STATE), F32),
                   jax.ShapeDtypeStruct((1, SSD_HEADS, SSD_STATE, LANES), F32),
                   jax.ShapeDtypeStruct((1, SSD_HEADS, SSD_STATE, LANES), F32)],
        compiler_params=_cparams("arbitrary", "arbitrary"),
        name="ssd_scan_bidir",
    )(xbc, xbc, xbc, dt, xbc, xbc, xbc, dt, par, dskip, h0f, h0b)
    return outs[:4]


def _kv_variants(k, v):
    lane = _lane_iota(k.shape)
    lo = lane < A_HEAD_DIM
    k_sw = pltpu.roll(k, A_HEAD_DIM, 1)
    v_sw = pltpu.roll(v, A_HEAD_DIM, 1)
    ks, vs = {}, {}
    for j in range(A_KV_HEADS):
        for e in range(2):
            src_k, src_v = (k, v) if j == e else (k_sw, v_sw)
            half = lo if e == 0 else ~lo
            ks[j, e] = jnp.where(half, src_k, 0.0).astype(BF16)
            vs[j, e] = src_v.astype(BF16)
    return ks, vs


GQA_KEY_TILE = 256


def _gqa_heads(q_ref, g_ref, o_ref, sink_ref, ks, vs, valid_tile, s_scr):
    m = q_ref.shape[0]
    assert A_HEADS // A_KV_HEADS == 4
    lo = _lane_iota((m, LANES)) < A_HEAD_DIM
    upper_rows = _row_iota((2 * m, 1)) >= m
    scale = jnp.asarray(A_SCALE, BF16)
    units = [(j, e) for j in range(A_KV_HEADS) for e in range(2)]
    nkeys = ks[0, 0].shape[0]
    kt_w = GQA_KEY_TILE
    nkt = nkeys // kt_w
    rows_of = {}

    def q_rows(j):
        if j not in rows_of:
            rows_of[j] = jnp.concatenate([q_ref[:, 2 * j * LANES:(2 * j + 1) * LANES],
                                          q_ref[:, (2 * j + 1) * LANES:(2 * j + 2) * LANES]], axis=0) * scale
        return rows_of[j]

    def score_tile(u, st, kt):
        j, e = units[u]
        cols = slice(kt * kt_w, (kt + 1) * kt_w)
        s = _dot_nt(q_rows(j), ks[j, e][cols, :])
        ok = valid_tile(kt)
        if ok is not None:
            s = jnp.where(ok, s, NEG_INF)
        s_scr[u % 2, :, cols] = s
        for c in range(kt_w // LANES):
            t = s[:, c * LANES:(c + 1) * LANES]
            st["mrun"] = t if st["mrun"] is None else jnp.maximum(st["mrun"], t)

    def value_tile(u, st, kt):
        j, e = units[u]
        cols = slice(kt * kt_w, (kt + 1) * kt_w)
        p = jnp.exp(s_scr[u % 2, :, cols] - st["m"])
        for c in range(kt_w // LANES):
            t = p[:, c * LANES:(c + 1) * LANES]
            st["lrun"] = t if st["lrun"] is None else st["lrun"] + t
        pv = _dot(p.astype(BF16), vs[j, e][cols, :])
        st["acc"] = pv if st["acc"] is None else st["acc"] + pv

    outs = {}
    cur = None
    for u in range(len(units) + 1):
        nxt = dict(mrun=None) if u < len(units) else None
        for kt in range(nkt):
            if nxt is not None:
                score_tile(u, nxt, kt)
            if cur is not None:
                value_tile(u - 1, cur, kt)
        if cur is not None:
            den = jnp.sum(cur["lrun"], axis=1, keepdims=True) + jnp.exp(cur["sink"] - cur["m"])
            outs[units[u - 1]] = cur["acc"] / den
        if nxt is not None:
            j, e = units[u]
            sink = jnp.where(upper_rows, sink_ref[4 * j + 2 + e], sink_ref[4 * j + e])
            nxt.update(sink=sink, m=jnp.maximum(jnp.max(nxt["mrun"], axis=1, keepdims=True), sink),
                       lrun=None, acc=None)
        cur = nxt
    for pr in range(A_HEADS // 2):
        j, half = pr // 2, pr % 2
        r = slice(half * m, (half + 1) * m)
        attn = jnp.where(lo, outs[j, 0][r], outs[j, 1][r])
        gate = g_ref[:, pr * LANES:(pr + 1) * LANES].astype(F32)
        o_ref[:, pr * LANES:(pr + 1) * LANES] = (attn * _silu(gate)).astype(o_ref.dtype)


def _attn_ctx_kernel(sink_ref, q_ref, k_ref, v_ref, g_ref, o_ref, s_scr):
    ks, vs = _kv_variants(k_ref[...], v_ref[...])
    _gqa_heads(q_ref, g_ref, o_ref, sink_ref, ks, vs, lambda kt: None, s_scr)


def _attn_ctx(sink, q, k, v, g, seq_len):
    t = q.shape[0]
    row = lambda b: (b, 0)
    return pl.pallas_call(
        _attn_ctx_kernel,
        grid=(t // seq_len,),
        in_specs=[pl.BlockSpec(memory_space=pltpu.SMEM),
                  pl.BlockSpec((seq_len, A_WIDTH), row),
                  pl.BlockSpec((seq_len, A_KV_WIDTH), row),
                  pl.BlockSpec((seq_len, A_KV_WIDTH), row),
                  pl.BlockSpec((seq_len, A_WIDTH), row)],
        out_specs=pl.BlockSpec((seq_len, A_WIDTH), row),
        out_shape=jax.ShapeDtypeStruct((t, A_WIDTH), BF16),
        scratch_shapes=[pltpu.VMEM((2, 2 * seq_len, seq_len), F32)],
        compiler_params=_cparams("arbitrary"),
        name="swa_context_attention",
    )(sink, q, k, v, g)


BAND_Q = 256


def _attn_band_kernel(sink_ref, q_ref, kp_ref, kc_ref, kn_ref, vp_ref, vc_ref, vn_ref, kx_ref, vx_ref, g_ref,
                      o_ref, s_scr, *, nsteps):
    n = pl.program_id(1)
    tq = q_ref.shape[0]
    blk = WINDOW_BLK
    k = jnp.concatenate([kp_ref[...], kc_ref[...], kn_ref[...], kx_ref[...]], axis=0)
    v = jnp.concatenate([vp_ref[...], vc_ref[...], vn_ref[...], vx_ref[...]], axis=0)
    nloc = tq + 2 * blk
    qi = _row_iota((2 * tq, GQA_KEY_TILE)) & (tq - 1)

    def valid_tile(kt):
        if kt * GQA_KEY_TILE >= nloc:
            return None
        col = _lane_iota((2 * tq, GQA_KEY_TILE)) + kt * GQA_KEY_TILE
        rel = col - blk - qi
        return ((rel >= -blk) & (rel <= blk) & ((col >= blk) | (n > 0)) & ((col < nloc - blk) | (n < nsteps - 1)))

    assert nloc % GQA_KEY_TILE == 0
    ks, vs = _kv_variants(k, v)
    _gqa_heads(q_ref, g_ref, o_ref, sink_ref, ks, vs, valid_tile, s_scr)


def _attn_band(sink, q, k, v, k_ctx, v_ctx, g, nb, seq_len):
    t = q.shape[0]
    blk = WINDOW_BLK
    tq = BAND_Q
    assert tq & (tq - 1) == 0 and tq % blk == 0 and seq_len % tq == 0
    per = tq // blk
    nblk = seq_len // blk
    nsteps = seq_len // tq
    nctx = k_ctx.shape[1]
    cur = lambda b, n: (b * nsteps + n, 0)
    prv = lambda b, n: (b * nblk + jnp.maximum(n * per - 1, 0), 0)
    nxt = lambda b, n: (b * nblk + jnp.minimum((n + 1) * per, nblk - 1), 0)
    ctx = lambda b, n: (b, 0, 0)
    edge = lambda f: pl.BlockSpec((blk, A_KV_WIDTH), f)
    mid = pl.BlockSpec((tq, A_KV_WIDTH), cur)
    return pl.pallas_call(
        functools.partial(_attn_band_kernel, nsteps=nsteps),
        grid=(nb, nsteps),
        in_specs=[pl.BlockSpec(memory_space=pltpu.SMEM),
                  pl.BlockSpec((tq, A_WIDTH), cur),
                  edge(prv), mid, edge(nxt),
                  edge(prv), mid, edge(nxt),
                  pl.BlockSpec((None, nctx, A_KV_WIDTH), ctx),
                  pl.BlockSpec((None, nctx, A_KV_WIDTH), ctx),
                  pl.BlockSpec((tq, A_WIDTH), cur)],
        out_specs=pl.BlockSpec((tq, A_WIDTH), cur),
        out_shape=jax.ShapeDtypeStruct((t, A_WIDTH), BF16),
        scratch_shapes=[pltpu.VMEM((2, 2 * tq, tq + 2 * blk + nctx), F32)],
        compiler_params=_cparams("arbitrary", "arbitrary"),
        name="swa_banded_attention",
    )(sink, q, k, k, k, v, v, v, k_ctx, v_ctx, g)


MLA_COLS = (Q_LORA, KV_LORA, LANES, MLA_WIDTH)


def _ab_out_mla_in_kernel(*refs, rope):
    (a_ref, yf_ref, yb_ref, z_ref, gnw_ref, wout_ref, x_ref, gate_ref), refs = refs[:8], refs[8:]
    if rope:
        (nw_ref, sc_ref, sh_ref, w_ref, qnw_ref, kvnw_ref, wuq_ref, cos_ref, sin_ref,
         x1_ref, qn_ref, qpe_ref, ckv_ref, kpe_ref, g_ref, ckv32_ref, kpe32_ref) = refs
    else:
        (nw_ref, sc_ref, sh_ref, w_ref, qnw_ref, kvnw_ref, wuq_ref,
         x1_ref, qn_ref, qpe_ref, ckv_ref, kpe_ref, g_ref, ckv32_ref, kpe32_ref) = refs
    offs = [int(o) for o in np.concatenate([[0], np.cumsum(MLA_COLS)])]
    nope_w = MLA_HEADS * MLA_NOPE

    def rms(u, w):
        return (u * lax.rsqrt(jnp.mean(u * u, axis=-1, keepdims=True) + EPS)) * w

    for rows in _row_splits(x_ref.shape[0]):
        y = (yf_ref[rows, :].astype(F32) + yb_ref[rows, :].astype(F32)) * _silu(z_ref[rows, :].astype(F32))
        ms = jnp.mean(y * y, axis=-1, keepdims=True)
        s = ((y * lax.rsqrt(ms + EPS)) * gnw_ref[...]).astype(BF16)
        out = _dot(a_ref[rows, :], wout_ref[0:A_WIDTH, :]) + _dot(s, wout_ref[A_WIDTH:, :])
        x1 = x_ref[rows, :] + gate_ref[...] * out
        x1_ref[rows, :] = x1
        h = _norm_mod(x1, nw_ref[...], sc_ref[...], sh_ref[...]).astype(BF16)
        cq = _dot(h, w_ref[:, offs[0]:offs[1]])
        ckv = _dot(h, w_ref[:, offs[1]:offs[2]])
        kpe = _dot(h, w_ref[:, offs[2]:offs[3]])
        g_ref[rows, :] = _dot(h, w_ref[:, offs[3]:offs[4]]).astype(g_ref.dtype)
        cqn = rms(cq, qnw_ref[...]).astype(BF16)
        qn_ref[rows, :] = _dot(cqn, wuq_ref[:, 0:nope_w]).astype(qn_ref.dtype)
        qpe = _dot(cqn, wuq_ref[:, nope_w:])
        if rope:
            qpe = _rope(qpe, cos_ref[rows, :], sin_ref[rows, :], MLA_ROPE // 4)
            kpe = _rope(kpe, cos_ref[rows, :], sin_ref[rows, :], MLA_ROPE // 4)
        qpe_ref[rows, :] = qpe.astype(qpe_ref.dtype)
        ckvn = rms(ckv, kvnw_ref[...])
        ckv_ref[rows, :] = ckvn.astype(ckv_ref.dtype)
        kpe_ref[rows, :] = kpe.astype(kpe_ref.dtype)
        ckv32_ref[rows, :] = ckvn
        kpe32_ref[rows, :] = kpe


def _ab_out_mla_in(a, yf, yb, z, gnw, wout, x, gate, nw, scale, shift, w, qnw, kvnw, wuq, rope_tabs, seq_len, tm):
    t, d = x.shape
    per_seq = seq_len // tm if scale.shape[0] > 1 else None
    row = lambda i: (i, 0)
    const = lambda i: (0, 0)
    mod = (lambda i: (i // per_seq, 0, 0)) if per_seq else (lambda i: (0, 0, 0))
    in_specs = [pl.BlockSpec((tm, A_WIDTH), row), pl.BlockSpec((tm, SSD_INNER), row),
                pl.BlockSpec((tm, SSD_INNER), row), pl.BlockSpec((tm, SSD_INNER), row),
                pl.BlockSpec((1, SSD_INNER), const), pl.BlockSpec(wout.shape, const),
                pl.BlockSpec((tm, d), row), pl.BlockSpec((None, 1, d), mod),
                pl.BlockSpec((1, d), const),
                pl.BlockSpec((None, 1, d), mod), pl.BlockSpec((None, 1, d), mod),
                pl.BlockSpec(w.shape, const), pl.BlockSpec((1, Q_LORA), const),
                pl.BlockSpec((1, KV_LORA), const), pl.BlockSpec(wuq.shape, const)]
    args = [a, yf, yb, z, gnw, wout, x, gate, nw, scale, shift, w, qnw, kvnw, wuq]
    if rope_tabs is not None:
        nt = seq_len // tm
        pos = lambda i: (i % nt, 0)
        in_specs += [pl.BlockSpec((tm, LANES), pos), pl.BlockSpec((tm, LANES), pos)]
        args += list(rope_tabs)
    widths = (d, MLA_HEADS * MLA_NOPE, MLA_HEADS * MLA_ROPE, KV_LORA, LANES, MLA_WIDTH, KV_LORA, LANES)
    dts = (F32, BF16, BF16, BF16, BF16, BF16, F32, F32)
    return pl.pallas_call(
        functools.partial(_ab_out_mla_in_kernel, rope=rope_tabs is not None),
        grid=(t // tm,),
        in_specs=in_specs,
        out_specs=[pl.BlockSpec((tm, c), row) for c in widths],
        out_shape=[jax.ShapeDtypeStruct((t, c), dt) for c, dt in zip(widths, dts)],
        compiler_params=_cparams("arbitrary"),
        name="ab_out_mla_in_proj",
    )(*args)


MLA_UNIT_ROWS = 512
MLA_KEY_TILE = 256


def _mla_attn_kernel(qn_ref, qpe_ref, ckv_ref, kpe_ref, wukt_ref, wuv_ref, g_ref, o_ref, kcat_scr, v_scr, s_scr, *,
                     pairs_per_step):
    pg = pl.program_id(1)
    qb = pl.program_id(2)
    tq = qn_ref.shape[0]
    c_exp = MLA_SCALE * LOG2E

    @pl.when(qb == 0)
    def _expand():
        ckv = ckv_ref[...]
        eye = jnp.where(_row_iota((LANES, LANES)) == _lane_iota((LANES, LANES)), 1.0, 0.0).astype(BF16)
        kpe_t = _dot_nt(eye, kpe_ref[...]).astype(BF16)
        for i in range(pairs_per_step):
            kcat_scr[i, 0:LANES, :] = _dot_nt(wukt_ref[i * LANES:(i + 1) * LANES, :], ckv).astype(BF16)
            kcat_scr[i, LANES:2 * LANES, :] = kpe_t
            v_scr[i] = _dot(ckv, wuv_ref[:, i * LANES:(i + 1) * LANES]).astype(BF16)

    nkeys = kcat_scr.shape[-1]
    kt_w = min(MLA_KEY_TILE, nkeys)
    nkt = nkeys // kt_w
    ru = min(MLA_UNIT_ROWS, tq)
    lane = _lane_iota((ru, LANES))
    units = [(i, r, e) for i in range(pairs_per_step) for r in range(tq // ru) for e in range(2)]

    def lane_tiles(x):
        return [x[:, j * LANES:(j + 1) * LANES] for j in range(x.shape[1] // LANES)]

    def q_ext(u):
        i, r, e = units[u]
        rows = slice(r * ru, (r + 1) * ru)
        qn = qn_ref[rows, i * LANES:(i + 1) * LANES]
        qpe = qpe_ref[rows, (i // 2) * LANES:(i // 2 + 1) * LANES]
        slot = 2 * (i % 2) + e if pairs_per_step % 2 == 0 else 2 * ((pg * pairs_per_step + i) % 2) + e
        nope_half = (lane < MLA_NOPE) if e == 0 else (lane >= MLA_NOPE)
        return jnp.concatenate([jnp.where(nope_half, qn, jnp.zeros_like(qn)),
                                jnp.where((lane >> 5) == slot, qpe, jnp.zeros_like(qpe))], axis=1)

    def score_tile(u, st, kt):
        cols = slice(kt * kt_w, (kt + 1) * kt_w)
        s = _dot(st["q"], kcat_scr[units[u][0], :, cols])
        s_scr[u % 2, :, cols] = s
        for t in lane_tiles(s):
            st["mrun"] = t if st["mrun"] is None else jnp.maximum(st["mrun"], t)

    def value_tile(u, st, kt):
        cols = slice(kt * kt_w, (kt + 1) * kt_w)
        p = jnp.exp2((s_scr[u % 2, :, cols] - st["m"]) * c_exp)
        for t in lane_tiles(p):
            st["lrun"] = t if st["lrun"] is None else st["lrun"] + t
        pv = _dot(p.astype(BF16), v_scr[units[u][0], cols, :])
        st["acc"] = pv if st["acc"] is None else st["acc"] + pv

    def finish(u, st, done):
        i, r, e = units[u]
        den = jnp.sum(st["lrun"], axis=1, keepdims=True)
        done[e] = st["acc"] / den
        if e == 1:
            rows = slice(r * ru, (r + 1) * ru)
            attn = jnp.where(lane < MLA_V, done[0], done[1])
            gate = g_ref[rows, i * LANES:(i + 1) * LANES].astype(F32)
            o_ref[rows, i * LANES:(i + 1) * LANES] = (attn * _silu(gate)).astype(o_ref.dtype)

    done = {}
    cur = None
    for u in range(len(units) + 1):
        nxt = dict(q=q_ext(u), mrun=None) if u < len(units) else None
        for kt in range(nkt):
            if nxt is not None:
                score_tile(u, nxt, kt)
            if cur is not None:
                value_tile(u - 1, cur, kt)
        if cur is not None:
            finish(u - 1, cur, done)
        if nxt is not None:
            nxt.update(m=jnp.max(nxt["mrun"], axis=1, keepdims=True), lrun=None, acc=None)
        cur = nxt


def _mla_attn(qn, qpe, ckv_keys, kpe_keys, wukt, wuv, g, nb, seq_len, tq, pairs_per_step):
    t = qn.shape[0]
    nkeys = ckv_keys.shape[1]
    npairs = MLA_HEADS // 2
    ngrp = npairs // pairs_per_step
    nq = seq_len // tq
    wp = pairs_per_step * LANES
    if pairs_per_step % 2 == 0:
        wpe = wp // 2
        pe_map = lambda b, p, i: (b * nq + i, p)
    else:
        wpe = LANES
        pe_map = lambda b, p, i: (b * nq + i, p // 2)
    qmap = lambda b, p, i: (b * nq + i, p)
    return pl.pallas_call(
        functools.partial(_mla_attn_kernel, pairs_per_step=pairs_per_step),
        grid=(nb, ngrp, nq),
        in_specs=[pl.BlockSpec((tq, wp), qmap),
                  pl.BlockSpec((tq, wpe), pe_map),
                  pl.BlockSpec((None, nkeys, KV_LORA), lambda b, p, i: (b, 0, 0)),
                  pl.BlockSpec((None, nkeys, LANES), lambda b, p, i: (b, 0, 0)),
                  pl.BlockSpec((wp, KV_LORA), lambda b, p, i: (p, 0)),
                  pl.BlockSpec((KV_LORA, wp), lambda b, p, i: (0, p)),
                  pl.BlockSpec((tq, wp), qmap)],
        out_specs=pl.BlockSpec((tq, wp), qmap),
        out_shape=jax.ShapeDtypeStruct((t, MLA_WIDTH), BF16),
        scratch_shapes=[pltpu.VMEM((pairs_per_step, 2 * LANES, nkeys), BF16),
                        pltpu.VMEM((pairs_per_step, nkeys, LANES), BF16),
                        pltpu.VMEM((2, min(MLA_UNIT_ROWS, tq), nkeys), F32)],
        compiler_params=_cparams("arbitrary", "arbitrary", "arbitrary"),
        name="mla_attention",
    )(qn, qpe, ckv_keys, kpe_keys, wukt, wuv, g)


def _mla_attn_seq_kernel(qn_ref, qpe_ref, ckv_ref, kpe_ref, wukt_ref, wuv_ref, g_ref, o_ref,
                         kcat_scr, v_scr, s_scr, m_scr, half_scr):
    pair = pl.program_id(1)
    seq = qn_ref.shape[0]
    nkeys = kcat_scr.shape[-1]
    ru, kt_w = MLA_UNIT_ROWS, MLA_KEY_TILE
    nkt = nkeys // kt_w
    nrb = seq // ru
    c_exp = MLA_SCALE * LOG2E
    lane = _lane_iota((ru, LANES))

    ckv = ckv_ref[...]
    eye = jnp.where(_row_iota((LANES, LANES)) == _lane_iota((LANES, LANES)), 1.0, 0.0).astype(BF16)
    kcat_scr[0:LANES, :] = _dot_nt(wukt_ref[...], ckv).astype(BF16)
    kcat_scr[LANES:2 * LANES, :] = _dot_nt(eye, kpe_ref[...]).astype(BF16)
    v_scr[:, 0:LANES] = _dot(ckv, wuv_ref[...]).astype(BF16)
    v_scr[:, LANES:2 * LANES] = jnp.ones((nkeys, LANES), BF16)

    def rows_of(rb):
        return slice(rb * ru, (rb + 1) * ru) if isinstance(rb, int) else pl.ds(pl.multiple_of(rb * ru, ru), ru)

    def q_ext(rb, e):
        rows = rows_of(rb)
        qn = qn_ref[rows, :]
        qpe = qpe_ref[rows, :]
        nope_half = (lane < MLA_NOPE) if e == 0 else (lane >= MLA_NOPE)
        slot = 2 * (pair % 2) + e
        return jnp.concatenate([jnp.where(nope_half, qn, jnp.zeros_like(qn)),
                                jnp.where((lane >> 5) == slot, qpe, jnp.zeros_like(qpe))], axis=1)

    def phase(score, value):
        q = q_ext(*score) if score is not None else None
        mrun = acc = None
        for kt in range(nkt):
            cols = slice(kt * kt_w, (kt + 1) * kt_w)
            if score is not None:
                s = _dot(q, kcat_scr[:, cols])
                s_scr[score[1], :, cols] = s
                for j in range(kt_w // LANES):
                    t = s[:, j * LANES:(j + 1) * LANES]
                    mrun = t if mrun is None else jnp.maximum(mrun, t)
            if value is not None:
                mb = m_scr[value]
                p = jnp.concatenate(
                    [jnp.exp2((s_scr[value, :, kt * kt_w + j * LANES:kt * kt_w + (j + 1) * LANES] - mb) * c_exp)
                     for j in range(kt_w // LANES)], axis=1)
                pv = _dot(p.astype(BF16), v_scr[cols, :])
                acc = pv if acc is None else acc + pv
        if score is not None:
            m_scr[score[1]] = jnp.broadcast_to(jnp.max(mrun, axis=1, keepdims=True), (ru, LANES))
        return None if acc is None else acc[:, 0:LANES] / acc[:, LANES:2 * LANES]

    phase((0, 0), None)

    def row_block(rb, carry):
        half_scr[...] = phase((rb, 1), 0)
        out1 = phase((jnp.minimum(rb + 1, nrb - 1), 0), 1)
        rows = rows_of(rb)
        attn = jnp.where(lane < MLA_V, half_scr[...], out1)
        o_ref[rows, :] = (attn * _silu(g_ref[rows, :].astype(F32))).astype(o_ref.dtype)
        return carry

    lax.fori_loop(0, nrb, row_block, 0)


def _mla_attn_seq(qn, qpe, ckv_keys, kpe_keys, wukt, wuv, g, nb, seq_len):
    t = qn.shape[0]
    nkeys = ckv_keys.shape[1]
    assert seq_len % MLA_UNIT_ROWS == 0 and nkeys % MLA_KEY_TILE == 0
    qmap = lambda b, p: (b, p)
    return pl.pallas_call(
        _mla_attn_seq_kernel,
        grid=(nb, MLA_HEADS // 2),
        in_specs=[pl.BlockSpec((seq_len, LANES), qmap),
                  pl.BlockSpec((seq_len, LANES), lambda b, p: (b, p // 2)),
                  pl.BlockSpec((None, nkeys, KV_LORA), lambda b, p: (b, 0, 0)),
                  pl.BlockSpec((None, nkeys, LANES), lambda b, p: (b, 0, 0)),
                  pl.BlockSpec((LANES, KV_LORA), lambda b, p: (p, 0)),
                  pl.BlockSpec((KV_LORA, LANES), lambda b, p: (0, p)),
                  pl.BlockSpec((seq_len, LANES), qmap)],
        out_specs=pl.BlockSpec((seq_len, LANES), qmap),
        out_shape=jax.ShapeDtypeStruct((t, MLA_WIDTH), BF16),
        scratch_shapes=[pltpu.VMEM((2 * LANES, nkeys), BF16),
                        pltpu.VMEM((nkeys, 2 * LANES), BF16),
                        pltpu.VMEM((2, MLA_UNIT_ROWS, nkeys), F32),
                        pltpu.VMEM((2, MLA_UNIT_ROWS, LANES), F32),
                        pltpu.VMEM((MLA_UNIT_ROWS, LANES), F32)],
        compiler_params=_cparams("arbitrary", "arbitrary"),
        name="mla_attention_seq",
    )(qn, qpe, ckv_keys, kpe_keys, wukt, wuv, g)


def _mla_out_kernel(a_ref, w_ref, x_ref, gate_ref, fw_ref, o_ref):
    for rows in _row_splits(x_ref.shape[0]):
        xn = x_ref[rows, :] + gate_ref[...] * _dot(a_ref[rows, :], w_ref[...])
        ms = jnp.mean(xn * xn, axis=-1, keepdims=True)
        o_ref[rows, :] = (xn * lax.rsqrt(ms + EPS)) * fw_ref[...]


def _mla_out(a, w, x, gate, fw, seq_len, tm):
    t, d = x.shape
    per_seq = seq_len // tm if gate.shape[0] > 1 else None
    row = lambda i: (i, 0)
    const = lambda i: (0, 0)
    mod = (lambda i: (i // per_seq, 0, 0)) if per_seq else (lambda i: (0, 0, 0))
    return pl.pallas_call(
        _mla_out_kernel,
        grid=(t // tm,),
        in_specs=[pl.BlockSpec((tm, MLA_WIDTH), row), pl.BlockSpec(w.shape, const),
                  pl.BlockSpec((tm, d), row), pl.BlockSpec((None, 1, d), mod), pl.BlockSpec((1, d), const)],
        out_specs=pl.BlockSpec((tm, d), row),
        out_shape=jax.ShapeDtypeStruct((t, d), F32),
        compiler_params=_cparams("arbitrary"),
        name="mla_out_proj_final_norm",
    )(a, w, x, gate, fw)


def _rope_tables(length, dim):
    rows = length // GRID_W
    f32 = np.float32
    row = np.repeat(np.arange(rows), GRID_W).astype(f32)
    col = np.tile(np.arange(GRID_W), rows).astype(f32)
    nf = dim // 4
    inv = (f32(1.0) / np.power(f32(ROPE_BASE), np.arange(nf, dtype=f32) / f32(nf))).astype(f32)
    ar = row[:, None] * inv[None, :]
    ac = col[:, None] * inv[None, :]
    ang = np.concatenate([ar, ar, ac, ac], axis=-1).astype(f32)
    sign = np.tile(np.concatenate([-np.ones((nf,), f32), np.ones((nf,), f32)]), 2)
    reps = LANES // dim
    return (jnp.asarray(np.tile(np.cos(ang).astype(f32), (1, reps))),
            jnp.asarray(np.tile((np.sin(ang) * sign).astype(f32), (1, reps))))


PROJ_TM = 1024
CONV_TM = 512
FUSED_TM = 512


def _group(x, seq_len, conds_rows, mods, wts, tabs_a, tabs_c, ctx_cache):
    nb, _, d = x.shape
    t = nb * seq_len
    latent = ctx_cache is not None
    xf = x.reshape(t, d)
    tm = PROJ_TM

    def mod_rows(layer):
        m = mods[layer][conds_rows]
        sh, sc, gt = jnp.split(m[:, None, :], 3, axis=-1)
        return sh, sc, gt

    sh, sc, gt = mod_rows(0)
    q, k, v, g, z, xbc, dt, *kv_t = _ab_in(xf, wts["norm_w"][0:1], sc, sh, wts["ab_w_in"], wts["ab_w_dt"],
                                           tabs_a if latent else None, seq_len, tm)
    xbc_c = _conv(xbc, wts["ab_conv_w"], wts["ab_conv_b"], seq_len, min(CONV_TM, seq_len))
    if latent:
        k_ctx, v_ctx, s_f0, s_b0, _, _ = ctx_cache
        attn = _attn_band(wts["ab_sink"], q, k, v, k_ctx, v_ctx, g, nb, seq_len)
    else:
        s_f0 = s_b0 = None
        attn = _attn_ctx(wts["ab_sink"], q, k, v, g, seq_len)
    yf, yb, s_f, s_b = _ssd(xbc_c, dt, wts["ssd_par"], wts["ssd_dskip"], s_f0, s_b0, nb, seq_len)

    gt0 = gt
    sh, sc, gt = mod_rows(1)
    x1, qn, qpe, ckv, kpe, g1, ckv32, kpe32 = _ab_out_mla_in(
        attn, yf, yb, z, wts["ab_gnorm_w"], wts["ab_w_out"], xf, gt0,
        wts["norm_w"][1:2], sc, sh, wts["mla_w_in"], wts["mla_q_norm_w"], wts["mla_kv_norm_w"],
        wts["mla_w_uq"], tabs_c if latent else None, seq_len, FUSED_TM)
    ckv_keys = ckv.reshape(nb, seq_len, KV_LORA)
    kpe_keys = kpe.reshape(nb, seq_len, LANES)
    if latent:
        ckv_x, kpe_x = ctx_cache[4], ctx_cache[5]
        ckv_keys = jnp.concatenate([ckv_keys, ckv_x], axis=1)
        kpe_keys = jnp.concatenate([kpe_keys, kpe_x], axis=1)
        attn1 = _mla_attn_seq(qn, qpe, ckv_keys, kpe_keys, wts["mla_w_ukt"], wts["mla_w_uv"], g1, nb, seq_len)
    else:
        attn1 = _mla_attn(qn, qpe, ckv_keys, kpe_keys, wts["mla_w_ukt"], wts["mla_w_uv"], g1, nb, seq_len,
                          seq_len, MLA_HEADS // 2)
    y = _mla_out(attn1, wts["mla_w_out"], x1, gt, wts["final_norm_w"], seq_len, tm)
    return y.reshape(nb, seq_len, d), (kv_t, s_f, s_b, ckv32, kpe32)


def kernel(x_prompt, x_sample, cache_a_k, cache_a_v, state_ssd_fwd, state_ssd_bwd, cache_mla_ckv, cache_mla_kpe,
           c, c_ctx, ada_w, ada_b, norm_w, ab_w_in, ab_sink, ab_conv_w, ab_conv_b, ab_dt_bias, ab_a_log,
           ab_d_skip, ab_gnorm_w, ab_w_out, mla_w_in, mla_q_norm_w, mla_kv_norm_w, mla_w_uq, mla_w_ukv,
           mla_w_out, final_norm_w):
    batch, seq, d = x_prompt.shape
    dec_batch, dec_seq, _ = x_sample.shape
    assert ada_w.shape[0] == 2 and ab_w_in.shape[0] == 1 and mla_w_in.shape[0] == 1
    assert dec_batch + 1 <= SUBLANES

    conds = jnp.concatenate([c_ctx[None, :], c, jnp.zeros((SUBLANES - 1 - dec_batch, d), F32)], axis=0)
    mods = _modulation(conds, ada_w, ada_b)

    n_main = sum(AB_COLS[:-1])
    w_ab = ab_w_in[0][:, :n_main].astype(BF16)
    w_dt = jnp.concatenate([ab_w_in[0][:, n_main:], jnp.zeros((d, LANES - 2 * SSD_HEADS), F32)], axis=1).astype(BF16)
    mw = mla_w_in[0]
    o_kpe = Q_LORA + KV_LORA
    w_mla = jnp.concatenate([mw[:, :o_kpe], jnp.tile(mw[:, o_kpe:o_kpe + MLA_ROPE], (1, LANES // MLA_ROPE)),
                             mw[:, o_kpe + MLA_ROPE:]], axis=1).astype(BF16)
    wuq = mla_w_uq[0].reshape(Q_LORA, MLA_HEADS, MLA_NOPE + MLA_ROPE)
    wuq = jnp.concatenate([wuq[:, :, :MLA_NOPE].reshape(Q_LORA, -1), wuq[:, :, MLA_NOPE:].reshape(Q_LORA, -1)],
                          axis=1).astype(BF16)
    wukv = mla_w_ukv[0].reshape(KV_LORA, MLA_HEADS, MLA_NOPE + MLA_V)
    wukt = wukv[:, :, :MLA_NOPE].reshape(KV_LORA, -1).T.astype(BF16)
    wuv = wukv[:, :, MLA_NOPE:].reshape(KV_LORA, -1).astype(BF16)
    pad_lanes = lambda r: jnp.concatenate([r.reshape(1, -1), jnp.zeros((1, LANES - r.size), F32)], axis=1)
    ssd_par = jnp.concatenate([pad_lanes(ab_dt_bias[0]), pad_lanes(ab_a_log[0]),
                               jnp.zeros((SUBLANES - 2, LANES), F32)], axis=0)
    wts = dict(
        norm_w=norm_w, ab_w_in=w_ab, ab_w_dt=w_dt, ab_sink=ab_sink[0],
        ab_conv_w=jnp.concatenate([ab_conv_w[0], jnp.zeros((SUBLANES - CONV_K, CONV_CH), F32)], axis=0),
        ab_conv_b=ab_conv_b[0][None, :], ssd_par=ssd_par,
        ssd_dskip=jnp.repeat(ab_d_skip[0], SSD_HEAD_DIM)[None, :],
        ab_gnorm_w=ab_gnorm_w[0][None, :], ab_w_out=ab_w_out[0].astype(BF16),
        mla_w_in=w_mla, mla_q_norm_w=mla_q_norm_w[0][None, :], mla_kv_norm_w=mla_kv_norm_w[0][None, :],
        mla_w_uq=wuq, mla_w_ukt=wukt, mla_w_uv=wuv, mla_w_out=mla_w_out[0].astype(BF16),
        final_norm_w=final_norm_w[None, :],
    )
    tabs_a = _rope_tables(dec_seq, A_HEAD_DIM)
    tabs_c = _rope_tables(dec_seq, MLA_ROPE)

    y_prompt, ((k_t, v_t), s_f, s_b, ckv32, kpe32) = _group(
        x_prompt, seq, jnp.zeros((1,), jnp.int32), mods, wts, tabs_a, tabs_c, None)

    past = cache_a_k.shape[2]
    ctx_cache = (cache_a_k[:, 0].reshape(dec_batch, past, A_KV_WIDTH),
                 cache_a_v[:, 0].reshape(dec_batch, past, A_KV_WIDTH),
                 state_ssd_fwd[:, 0], state_ssd_bwd[:, 0],
                 cache_mla_ckv[:, 0].astype(BF16),
                 jnp.tile(cache_mla_kpe[:, 0], (1, 1, LANES // MLA_ROPE)).astype(BF16))
    y_sample, _ = _group(x_sample, dec_seq, 1 + jnp.arange(dec_batch), mods, wts, tabs_a, tabs_c, ctx_cache)

    cache_layout = lambda u: u.reshape(batch, 1, A_KV_HEADS, A_HEAD_DIM, seq).transpose(0, 1, 4, 2, 3)
    return (y_prompt, y_sample,
            cache_layout(k_t), cache_layout(v_t),
            s_f[:, None], s_b[:, None],
            ckv32.reshape(batch, 1, seq, KV_LORA), kpe32[:, :MLA_ROPE].reshape(batch, 1, seq, MLA_ROPE))
```

```python
import functools
import math

import jax
import jax.numpy as jnp
import numpy as np
from jax import lax
from jax.experimental import pallas as pl
from jax.experimental.pallas import tpu as pltpu

F32 = jnp.float32
BF16 = jnp.bfloat16

LANES = 128
SUBLANES = 8
VMEM_LIMIT_BYTES = 56 * 1024 * 1024

GRID_W = 64
ROPE_BASE = 10000.0
EPS = 1e-6
NEG_INF = -1e30
WINDOW_BLK = 128
CHUNK = 128
A_HEADS, A_KV_HEADS, A_HEAD_DIM = 8, 2, 64
A_WIDTH = A_HEADS * A_HEAD_DIM
A_KV_WIDTH = A_KV_HEADS * A_HEAD_DIM
A_SCALE = A_HEAD_DIM ** -0.5
SSD_HEADS, SSD_HEAD_DIM, SSD_GROUPS, SSD_STATE = 16, 64, 2, 64
SSD_INNER = SSD_HEADS * SSD_HEAD_DIM
CONV_K = 5
BC_WIDTH = SSD_GROUPS * SSD_STATE
CONV_CH = SSD_INNER + 2 * BC_WIDTH
MLA_HEADS, MLA_NOPE, MLA_ROPE, MLA_V = 16, 64, 32, 64
Q_LORA, KV_LORA = 256, 128
MLA_WIDTH = MLA_HEADS * MLA_V
MLA_SCALE = (MLA_NOPE + MLA_ROPE) ** -0.5
LOG2E = 1.4426950408889634


def _cparams(*sem):
    return pltpu.CompilerParams(dimension_semantics=sem, vmem_limit_bytes=VMEM_LIMIT_BYTES)


def _silu(x):
    return x * (1.0 / (1.0 + jnp.exp(-x)))


def _dot(a, b):
    return jnp.dot(a, b, preferred_element_type=F32, precision=lax.Precision.DEFAULT)


def _dot_nt(a, b):
    return lax.dot_general(a, b, (((1,), (1,)), ((), ())), preferred_element_type=F32,
                           precision=lax.Precision.DEFAULT)


def _row_splits(m, parts=2):
    step = m // parts
    return [slice(r * step, (r + 1) * step) for r in range(parts)]


def _lane_iota(shape):
    return lax.broadcasted_iota(jnp.int32, shape, len(shape) - 1)


def _row_iota(shape):
    return lax.broadcasted_iota(jnp.int32, shape, len(shape) - 2)


def _mod_kernel(cond_ref, w_ref, b_ref, o_ref):
    s = _silu(cond_ref[...])
    o_ref[...] = _dot(s.astype(BF16), w_ref[...].astype(BF16)) + b_ref[...]


def _modulation(conds, ada_w, ada_b):
    depth, d, d3 = ada_w.shape
    tn = 768
    return pl.pallas_call(
        _mod_kernel,
        grid=(depth, d3 // tn),
        in_specs=[pl.BlockSpec((SUBLANES, d), lambda l, j: (0, 0)),
                  pl.BlockSpec((None, d, tn), lambda l, j: (l, 0, j)),
                  pl.BlockSpec((None, 1, tn), lambda l, j: (l, 0, j))],
        out_specs=pl.BlockSpec((None, SUBLANES, tn), lambda l, j: (l, 0, j)),
        out_shape=jax.ShapeDtypeStruct((depth, SUBLANES, d3), F32),
        compiler_params=_cparams("arbitrary", "arbitrary"),
        name="modulation",
    )(conds, ada_w, ada_b.reshape(depth, 1, d3))


def _norm_mod(x, nw, scale, shift):
    ms = jnp.mean(x * x, axis=-1, keepdims=True)
    y = (x * lax.rsqrt(ms + EPS)) * nw
    return y * (1.0 + scale) + shift


def _rope(x, cos, sin_signed, half_period):
    outs = []
    first = (_lane_iota((x.shape[0], LANES)) & (2 * half_period - 1)) < half_period
    for j in range(x.shape[1] // LANES):
        xj = x[:, j * LANES:(j + 1) * LANES]
        up = pltpu.roll(xj, LANES - half_period, 1)
        dn = pltpu.roll(xj, half_period, 1)
        outs.append(xj * cos + jnp.where(first, up, dn) * sin_signed)
    return outs[0] if len(outs) == 1 else jnp.concatenate(outs, axis=1)


AB_COLS = (A_WIDTH, A_KV_WIDTH, A_KV_WIDTH, A_WIDTH, SSD_INNER, CONV_CH, LANES)


def _ab_in_kernel(*refs, rope, seq_len):
    if rope:
        (x_ref, nw_ref, sc_ref, sh_ref, w_ref, wdt_ref, cos_ref, sin_ref,
         q_ref, k_ref, v_ref, g_ref, z_ref, xbc_ref, dt_ref) = refs
        kvt_refs = {}
    else:
        (x_ref, nw_ref, sc_ref, sh_ref, w32_ref, wdt_ref,
         q_ref, k_ref, v_ref, g_ref, z_ref, xbc_ref, dt_ref, kt_ref, vt_ref, w_ref) = refs
        kvt_refs = {1: kt_ref, 2: vt_ref}

        @pl.when(pl.program_id(0) == 0)
        def _round_weight():
            w_ref[...] = w32_ref[...].astype(BF16)
    offs = np.concatenate([[0], np.cumsum(AB_COLS)])
    outs = (q_ref, k_ref, v_ref, g_ref, z_ref, xbc_ref, dt_ref)
    for rows in _row_splits(x_ref.shape[0]):
        h = _norm_mod(x_ref[rows, :], nw_ref[...], sc_ref[...], sh_ref[...]).astype(BF16)
        for i, o_ref in enumerate(outs):
            last = i == len(outs) - 1
            y = _dot(h, wdt_ref[...] if last else w_ref[:, int(offs[i]):int(offs[i + 1])])
            if rope and i in (0, 1):
                y = _rope(y, cos_ref[rows, :], sin_ref[rows, :], A_HEAD_DIM // 4)
            o_ref[rows, :] = y.astype(o_ref.dtype)
            if i in kvt_refs:
                for s in range(rows.start // seq_len, rows.stop // seq_len):
                    kvt_refs[i][s] = y[s * seq_len - rows.start:(s + 1) * seq_len - rows.start, :].T


def _ab_in(x, nw, scale, shift, w, wdt, rope_tabs, seq_len, tm):
    t, d = x.shape
    n_main = sum(AB_COLS[:-1])
    per_seq = seq_len // tm if scale.shape[0] > 1 else None
    row = lambda i: (i, 0)
    mod = (lambda i: (i // per_seq, 0, 0)) if per_seq else (lambda i: (0, 0, 0))
    const = lambda i: (0, 0)
    in_specs = [pl.BlockSpec((tm, d), row),
                pl.BlockSpec((1, d), const),
                pl.BlockSpec((None, 1, d), mod),
                pl.BlockSpec((None, 1, d), mod),
                pl.BlockSpec((d, n_main), const) if rope_tabs is not None else
                pl.BlockSpec((None, d, n_main), lambda i: (0, 0, 0), pipeline_mode=pl.Buffered(1)),
                pl.BlockSpec(wdt.shape, const)]
    args = [x, nw, scale, shift, w, wdt]
    if rope_tabs is not None:
        nt = seq_len // tm
        pos = lambda i: (i % nt, 0)
        in_specs += [pl.BlockSpec((tm, LANES), pos), pl.BlockSpec((tm, LANES), pos)]
        args += list(rope_tabs)
    dts = (BF16, F32, F32, BF16, BF16, BF16, F32)
    out_specs = [pl.BlockSpec((tm, c), row) for c in AB_COLS]
    out_shape = [jax.ShapeDtypeStruct((t, c), dt) for c, dt in zip(AB_COLS, dts)]
    if rope_tabs is None:
        assert (tm // 2) % seq_len == 0
        spt = tm // seq_len
        out_specs += [pl.BlockSpec((spt, A_KV_WIDTH, seq_len), lambda i: (i, 0, 0))] * 2
        out_shape += [jax.ShapeDtypeStruct((t // seq_len, A_KV_WIDTH, seq_len), F32)] * 2
        out_specs += [pl.BlockSpec((d, n_main), const)]
        out_shape += [jax.ShapeDtypeStruct((d, n_main), BF16)]
    return pl.pallas_call(
        functools.partial(_ab_in_kernel, rope=rope_tabs is not None, seq_len=seq_len),
        grid=(t // tm,),
        in_specs=in_specs,
        out_specs=out_specs,
        out_shape=out_shape,
        compiler_params=_cparams("arbitrary"),
        name="ab_in_proj",
    )(*args)


def _conv_kernel(prev_ref, main_ref, next_ref, w_ref, b_ref, o_ref, *, tiles_per_seq):
    i = pl.program_id(0) % tiles_per_seq
    tm = main_ref.shape[0]
    prev = jnp.where(i > 0, prev_ref[...].astype(F32), 0.0)
    nxt = jnp.where(i < tiles_per_seq - 1, next_ref[...].astype(F32), 0.0)
    ext = jnp.concatenate([prev, main_ref[...].astype(F32), nxt], axis=0)
    first = SUBLANES - CONV_K // 2
    acc = b_ref[...] + ext[first:first + tm] * w_ref[0:1, :]
    for j in range(1, CONV_K):
        acc = acc + ext[first + j:first + j + tm] * w_ref[j:j + 1, :]
    o_ref[...] = _silu(acc).astype(o_ref.dtype)


def _conv(xbc, w, b, seq_len, tm):
    t, c = xbc.shape
    tps = seq_len // tm
    hb = tm // SUBLANES
    nblk8 = t // SUBLANES
    return pl.pallas_call(
        functools.partial(_conv_kernel, tiles_per_seq=tps),
        grid=(t // tm,),
        in_specs=[pl.BlockSpec((SUBLANES, c), lambda i: (jnp.maximum(i * hb - 1, 0), 0)),
                  pl.BlockSpec((tm, c), lambda i: (i, 0)),
                  pl.BlockSpec((SUBLANES, c), lambda i: (jnp.minimum((i + 1) * hb, nblk8 - 1), 0)),
                  pl.BlockSpec((SUBLANES, c), lambda i: (0, 0)),
                  pl.BlockSpec((1, c), lambda i: (0, 0))],
        out_specs=pl.BlockSpec((tm, c), lambda i: (i, 0)),
        out_shape=jax.ShapeDtypeStruct((t, c), BF16),
        compiler_params=_cparams("arbitrary"),
        name="ssd_conv",
    )(xbc, xbc, xbc, w, b)


def _ssd_init(h0_ref, h_scr, has_h0):
    if not has_h0:
        h_scr[...] = jnp.zeros_like(h_scr)
        return
    for pr in range(SSD_HEADS // 2):
        both = jnp.concatenate([h0_ref[2 * pr], h0_ref[2 * pr + 1]], axis=0)
        both = jnp.concatenate([both, jnp.zeros_like(both)], axis=1)
        st = both.T[0:SSD_STATE]
        lo_st = _lane_iota(st.shape) < SSD_HEAD_DIM
        h_scr[2 * pr] = jnp.where(lo_st, st, 0.0)
        h_scr[2 * pr + 1] = jnp.where(lo_st, 0.0, st)


def _ssd_final(h_scr, hout_ref):
    for pr in range(SSD_HEADS // 2):
        st = h_scr[2 * pr] + h_scr[2 * pr + 1]
        st = jnp.concatenate([st, jnp.zeros_like(st)], axis=0).T
        hout_ref[2 * pr] = st[0:SSD_HEAD_DIM, 0:SSD_STATE]
        hout_ref[2 * pr + 1] = st[SSD_HEAD_DIM:2 * SSD_HEAD_DIM, 0:SSD_STATE]


def _ssd_prelude(b_ref, c_ref, dt_ref, par_ref, direction):
    lane = _lane_iota((CHUNK, LANES))
    rowi = _row_iota((CHUNK, LANES))
    dtp_in = dt_ref[...] + par_ref[0:1, :]
    dtp = jnp.maximum(dtp_in, 0.0) + jnp.log(1.0 + jnp.exp(-jnp.abs(dtp_in)))
    la = dtp * (-jnp.exp(par_ref[1:2, :]))
    causal = (rowi >= lane) if direction == 0 else (rowi <= lane)
    tri = jnp.where(causal, 1.0, 0.0).astype(BF16)
    p1 = la.astype(BF16)
    r1 = la - p1.astype(F32)
    p2 = r1.astype(BF16)
    p3 = (r1 - p2.astype(F32)).astype(BF16)
    cum = _dot(tri, p1) + _dot(tri, p2) + _dot(tri, p3)
    cum_t = cum.T
    dtp_t = dtp.T
    end = CHUNK - 1 if direction == 0 else 0
    w_t = jnp.exp(cum_t[:, end:end + 1] - cum_t) * dtp_t
    lrow_t = cum_t - jnp.log(dtp_t)
    bm = b_ref[...]
    cm = c_ref[...]
    cb, cmask = [], []
    for g in range(SSD_GROUPS):
        in_g = (lane >= g * SSD_STATE) & (lane < (g + 1) * SSD_STATE)
        cg = jnp.where(in_g, cm, jnp.zeros_like(cm))
        cmask.append(cg.astype(F32))
        cb.append(_dot_nt(cg, bm))
    return dict(cum=cum, lrow_t=lrow_t, w_t=w_t, causal=causal, end=end,
                bm_t=bm.astype(F32).T, cb=cb, cmask=cmask, lo_half=lane < SSD_HEAD_DIM)


def _ssd_head_pair(pre, pr, x_ref, dskip_ref, y_ref, h_scr, direction):
    hp = SSD_HEADS // SSD_GROUPS
    cum, lrow_t, w_t, end = pre["cum"], pre["lrow_t"], pre["w_t"], pre["end"]
    lo_half = pre["lo_half"]
    xp = x_ref[:, pr * LANES:(pr + 1) * LANES]
    y_pair = None
    for e in range(2):
        h = 2 * pr + e
        g = h // hp
        col = direction * SSD_HEADS + h
        xe = jnp.where(lo_half if e == 0 else ~lo_half, xp, jnp.zeros_like(xp))
        cumcol = jnp.broadcast_to(cum[:, col:col + 1], (CHUNK, LANES))
        diff = cumcol - lrow_t[col:col + 1, :]
        sc = pre["cb"][g] * jnp.exp(jnp.where(pre["causal"], diff, NEG_INF))
        ce = pre["cmask"][g] * jnp.exp(cumcol)
        hs = h_scr[h]
        hsb = hs.astype(BF16)
        zpad = jnp.zeros_like(hsb)
        bw = (pre["bm_t"][g * SSD_STATE:(g + 1) * SSD_STATE, :] * w_t[col:col + 1, :]).astype(BF16)
        a = jnp.concatenate([jnp.concatenate([sc.astype(BF16), ce.astype(BF16)], axis=1),
                             jnp.concatenate([bw, jnp.zeros_like(bw)], axis=1)], axis=0)
        wmat = jnp.concatenate([xe, hsb, zpad] if g == 0 else [xe, zpad, hsb], axis=0)
        res = _dot(a, wmat)
        ye = res[0:CHUNK]
        y_pair = ye if y_pair is None else y_pair + ye
        cdec = jnp.exp(cumcol[end:end + 1, :])
        h_scr[h] = hs * cdec + res[CHUNK:CHUNK + SSD_STATE]
    if direction == 0:
        y_pair = y_pair + dskip_ref[:, pr * LANES:(pr + 1) * LANES] * xp.astype(F32)
    y_ref[:, pr * LANES:(pr + 1) * LANES] = y_pair.astype(y_ref.dtype)


def _ssd_kernel(xf_ref, bf_ref, cf_ref, dtf_ref, xb_ref, bb_ref, cb_ref, dtb_ref, par_ref, dskip_ref,
                h0f_ref, h0b_ref, yf_ref, yb_ref, houtf_ref, houtb_ref, hf_scr, hb_scr, *, nc, has_h0):
    ci = pl.program_id(1)

    @pl.when(ci == 0)
    def _init():
        _ssd_init(h0f_ref, hf_scr, has_h0)
        _ssd_init(h0b_ref, hb_scr, has_h0)

    pre_f = _ssd_prelude(bf_ref, cf_ref, dtf_ref, par_ref, 0)
    pre_b = _ssd_prelude(bb_ref, cb_ref, dtb_ref, par_ref, 1)
    for pr in range(SSD_HEADS // 2):
        _ssd_head_pair(pre_f, pr, xf_ref, dskip_ref, yf_ref, hf_scr, 0)
        _ssd_head_pair(pre_b, pr, xb_ref, dskip_ref, yb_ref, hb_scr, 1)

    @pl.when(ci == nc - 1)
    def _final():
        _ssd_final(hf_scr, houtf_ref)
        _ssd_final(hb_scr, houtb_ref)


def _ssd(xbc, dt, par, dskip, h0f, h0b, nb, seq_len):
    t = xbc.shape[0]
    nc = seq_len // CHUNK
    fwd = lambda b, c: b * nc + c
    bwd = lambda b, c: b * nc + nc - 1 - c
    xcols = SSD_INNER // LANES
    has_h0 = h0f is not None
    if not has_h0:
        h0f = h0b = jnp.zeros((1, SSD_HEADS, SSD_HEAD_DIM, SSD_STATE), F32)
    h0_map = (lambda b, c: (b, 0, 0, 0)) if has_h0 else (lambda b, c: (0, 0, 0, 0))
    st_shape = (None, SSD_HEADS, SSD_HEAD_DIM, SSD_STATE)
    carry_shape = (None, SSD_HEADS, SSD_STATE, LANES)
    per_b = lambda b, c: (b, 0, 0, 0)
    one_block = lambda b, c: (0, 0, 0, 0)

    def chunk_specs(cidx):
        return [pl.BlockSpec((CHUNK, SSD_INNER), lambda b, c: (cidx(b, c), 0)),
                pl.BlockSpec((CHUNK, LANES), lambda b, c: (cidx(b, c), xcols)),
                pl.BlockSpec((CHUNK, LANES), lambda b, c: (cidx(b, c), xcols + 1)),
                pl.BlockSpec((CHUNK, LANES), lambda b, c: (cidx(b, c), 0))]

    outs = pl.pallas_call(
        functools.partial(_ssd_kernel, nc=nc, has_h0=has_h0),
        grid=(nb, nc),
        in_specs=chunk_specs(fwd) + chunk_specs(bwd) + [
            pl.BlockSpec((SUBLANES, LANES), lambda b, c: (0, 0)),
            pl.BlockSpec((1, SSD_INNER), lambda b, c: (0, 0)),
            pl.BlockSpec(st_shape, h0_map), pl.BlockSpec(st_shape, h0_map)],
        out_specs=[pl.BlockSpec((CHUNK, SSD_INNER), lambda b, c: (fwd(b, c), 0)),
                   pl.BlockSpec((CHUNK, SSD_INNER), lambda b, c: (bwd(b, c), 0)),
                   pl.BlockSpec(st_shape, per_b), pl.BlockSpec(st_shape, per_b),
                   pl.BlockSpec(carry_shape, one_block), pl.BlockSpec(carry_shape, one_block)],
        out_shape=[jax.ShapeDtypeStruct((t, SSD_INNER), BF16), jax.ShapeDtypeStruct((t, SSD_INNER), BF16),
                   jax.ShapeDtypeStruct((nb, SSD_HEADS, SSD_HEAD_DIM, SSD_STATE), F32),
                   jax.ShapeDtypeStruct((nb, SSD_HEADS, SSD_HEAD_DIM, SSD_STATE), F32),
                   jax.ShapeDtypeStruct((1, SSD_HEADS, SSD_STATE, LANES), F32),
                   jax.ShapeDtypeStruct((1, SSD_HEADS, SSD_STATE, LANES), F32)],
        compiler_params=_cparams("arbitrary", "arbitrary"),
        name="ssd_scan_bidir",
    )(xbc, xbc, xbc, dt, xbc, xbc, xbc, dt, par, dskip, h0f, h0b)
    return outs[:4]


def _kv_variants(k, v):
    lane = _lane_iota(k.shape)
    lo = lane < A_HEAD_DIM
    k_sw = pltpu.roll(k, A_HEAD_DIM, 1)
    v_sw = pltpu.roll(v, A_HEAD_DIM, 1)
    ks, vs = {}, {}
    for j in range(A_KV_HEADS):
        for e in range(2):
            src_k, src_v = (k, v) if j == e else (k_sw, v_sw)
            half = lo if e == 0 else ~lo
            ks[j, e] = jnp.where(half, src_k, 0.0).astype(BF16)
            vs[j, e] = src_v.astype(BF16)
    return ks, vs


GQA_KEY_TILE = 256


def _gqa_heads(q_ref, g_ref, o_ref, sink_ref, ks, vs, valid_tile, s_scr):
    m = q_ref.shape[0]
    assert A_HEADS // A_KV_HEADS == 4
    lo = _lane_iota((m, LANES)) < A_HEAD_DIM
    upper_rows = _row_iota((2 * m, 1)) >= m
    scale = jnp.asarray(A_SCALE, BF16)
    units = [(j, e) for j in range(A_KV_HEADS) for e in range(2)]
    nkeys = ks[0, 0].shape[0]
    kt_w = GQA_KEY_TILE
    nkt = nkeys // kt_w
    rows_of = {}

    def q_rows(j):
        if j not in rows_of:
            rows_of[j] = jnp.concatenate([q_ref[:, 2 * j * LANES:(2 * j + 1) * LANES],
                                          q_ref[:, (2 * j + 1) * LANES:(2 * j + 2) * LANES]], axis=0) * scale
        return rows_of[j]

    def score_tile(u, st, kt):
        j, e = units[u]
        cols = slice(kt * kt_w, (kt + 1) * kt_w)
        s = _dot_nt(q_rows(j), ks[j, e][cols, :])
        ok = valid_tile(kt)
        if ok is not None:
            s = jnp.where(ok, s, NEG_INF)
        s_scr[u % 2, :, cols] = s
        for c in range(kt_w // LANES):
            t = s[:, c * LANES:(c + 1) * LANES]
            st["mrun"] = t if st["mrun"] is None else jnp.maximum(st["mrun"], t)

    def value_tile(u, st, kt):
        j, e = units[u]
        cols = slice(kt * kt_w, (kt + 1) * kt_w)
        p = jnp.exp(s_scr[u % 2, :, cols] - st["m"])
        for c in range(kt_w // LANES):
            t = p[:, c * LANES:(c + 1) * LANES]
            st["lrun"] = t if st["lrun"] is None else st["lrun"] + t
        pv = _dot(p.astype(BF16), vs[j, e][cols, :])
        st["acc"] = pv if st["acc"] is None else st["acc"] + pv

    outs = {}
    cur = None
    for u in range(len(units) + 1):
        nxt = dict(mrun=None) if u < len(units) else None
        for kt in range(nkt):
            if nxt is not None:
                score_tile(u, nxt, kt)
            if cur is not None:
                value_tile(u - 1, cur, kt)
        if cur is not None:
            den = jnp.sum(cur["lrun"], axis=1, keepdims=True) + jnp.exp(cur["sink"] - cur["m"])
            outs[units[u - 1]] = cur["acc"] / den
        if nxt is not None:
            j, e = units[u]
            sink = jnp.where(upper_rows, sink_ref[4 * j + 2 + e], sink_ref[4 * j + e])
            nxt.update(sink=sink, m=jnp.maximum(jnp.max(nxt["mrun"], axis=1, keepdims=True), sink),
                       lrun=None, acc=None)
        cur = nxt
    for pr in range(A_HEADS // 2):
        j, half = pr // 2, pr % 2
        r = slice(half * m, (half + 1) * m)
        attn = jnp.where(lo, outs[j, 0][r], outs[j, 1][r])
        gate = g_ref[:, pr * LANES:(pr + 1) * LANES].astype(F32)
        o_ref[:, pr * LANES:(pr + 1) * LANES] = (attn * _silu(gate)).astype(o_ref.dtype)


def _attn_ctx_kernel(sink_ref, q_ref, k_ref, v_ref, g_ref, o_ref, s_scr):
    ks, vs = _kv_variants(k_ref[...], v_ref[...])
    _gqa_heads(q_ref, g_ref, o_ref, sink_ref, ks, vs, lambda kt: None, s_scr)


def _attn_ctx(sink, q, k, v, g, seq_len):
    t = q.shape[0]
    row = lambda b: (b, 0)
    return pl.pallas_call(
        _attn_ctx_kernel,
        grid=(t // seq_len,),
        in_specs=[pl.BlockSpec(memory_space=pltpu.SMEM),
                  pl.BlockSpec((seq_len, A_WIDTH), row),
                  pl.BlockSpec((seq_len, A_KV_WIDTH), row),
                  pl.BlockSpec((seq_len, A_KV_WIDTH), row),
                  pl.BlockSpec((seq_len, A_WIDTH), row)],
        out_specs=pl.BlockSpec((seq_len, A_WIDTH), row),
        out_shape=jax.ShapeDtypeStruct((t, A_WIDTH), BF16),
        scratch_shapes=[pltpu.VMEM((2, 2 * seq_len, seq_len), F32)],
        compiler_params=_cparams("arbitrary"),
        name="swa_context_attention",
    )(sink, q, k, v, g)


BAND_Q = 256


def _attn_band_kernel(sink_ref, q_ref, kp_ref, kc_ref, kn_ref, vp_ref, vc_ref, vn_ref, kx_ref, vx_ref, g_ref,
                      o_ref, s_scr, *, nsteps):
    n = pl.program_id(1)
    tq = q_ref.shape[0]
    blk = WINDOW_BLK
    k = jnp.concatenate([kp_ref[...], kc_ref[...], kn_ref[...], kx_ref[...]], axis=0)
    v = jnp.concatenate([vp_ref[...], vc_ref[...], vn_ref[...], vx_ref[...]], axis=0)
    nloc = tq + 2 * blk
    qi = _row_iota((2 * tq, GQA_KEY_TILE)) & (tq - 1)

    def valid_tile(kt):
        if kt * GQA_KEY_TILE >= nloc:
            return None
        col = _lane_iota((2 * tq, GQA_KEY_TILE)) + kt * GQA_KEY_TILE
        rel = col - blk - qi
        return ((rel >= -blk) & (rel <= blk) & ((col >= blk) | (n > 0)) & ((col < nloc - blk) | (n < nsteps - 1)))

    assert nloc % GQA_KEY_TILE == 0
    ks, vs = _kv_variants(k, v)
    _gqa_heads(q_ref, g_ref, o_ref, sink_ref, ks, vs, valid_tile, s_scr)


def _attn_band(sink, q, k, v, k_ctx, v_ctx, g, nb, seq_len):
    t = q.shape[0]
    blk = WINDOW_BLK
    tq = BAND_Q
    assert tq & (tq - 1) == 0 and tq % blk == 0 and seq_len % tq == 0
    per = tq // blk
    nblk = seq_len // blk
    nsteps = seq_len // tq
    nctx = k_ctx.shape[1]
    cur = lambda b, n: (b * nsteps + n, 0)
    prv = lambda b, n: (b * nblk + jnp.maximum(n * per - 1, 0), 0)
    nxt = lambda b, n: (b * nblk + jnp.minimum((n + 1) * per, nblk - 1), 0)
    ctx = lambda b, n: (b, 0, 0)
    edge = lambda f: pl.BlockSpec((blk, A_KV_WIDTH), f)
    mid = pl.BlockSpec((tq, A_KV_WIDTH), cur)
    return pl.pallas_call(
        functools.partial(_attn_band_kernel, nsteps=nsteps),
        grid=(nb, nsteps),
        in_specs=[pl.BlockSpec(memory_space=pltpu.SMEM),
                  pl.BlockSpec((tq, A_WIDTH), cur),
                  edge(prv), mid, edge(nxt),
                  edge(prv), mid, edge(nxt),
                  pl.BlockSpec((None, nctx, A_KV_WIDTH), ctx),
                  pl.BlockSpec((None, nctx, A_KV_WIDTH), ctx),
                  pl.BlockSpec((tq, A_WIDTH), cur)],
        out_specs=pl.BlockSpec((tq, A_WIDTH), cur),
        out_shape=jax.ShapeDtypeStruct((t, A_WIDTH), BF16),
        scratch_shapes=[pltpu.VMEM((2, 2 * tq, tq + 2 * blk + nctx), F32)],
        compiler_params=_cparams("arbitrary", "arbitrary"),
        name="swa_banded_attention",
    )(sink, q, k, k, k, v, v, v, k_ctx, v_ctx, g)


MLA_COLS = (Q_LORA, KV_LORA, LANES, MLA_WIDTH)


def _ab_out_mla_in_kernel(*refs, rope):
    (a_ref, yf_ref, yb_ref, z_ref, gnw_ref, wout_ref, x_ref, gate_ref), refs = refs[:8], refs[8:]
    if rope:
        (nw_ref, sc_ref, sh_ref, w_ref, qnw_ref, kvnw_ref, wuq_ref, cos_ref, sin_ref,
         x1_ref, qn_ref, qpe_ref, ckv_ref, kpe_ref, g_ref, ckv32_ref, kpe32_ref) = refs
    else:
        (nw_ref, sc_ref, sh_ref, w_ref, qnw_ref, kvnw_ref, wuq_ref,
         x1_ref, qn_ref, qpe_ref, ckv_ref, kpe_ref, g_ref, ckv32_ref, kpe32_ref) = refs
    offs = [int(o) for o in np.concatenate([[0], np.cumsum(MLA_COLS)])]
    nope_w = MLA_HEADS * MLA_NOPE

    def rms(u, w):
        return (u * lax.rsqrt(jnp.mean(u * u, axis=-1, keepdims=True) + EPS)) * w

    for rows in _row_splits(x_ref.shape[0]):
        y = (yf_ref[rows, :].astype(F32) + yb_ref[rows, :].astype(F32)) * _silu(z_ref[rows, :].astype(F32))
        ms = jnp.mean(y * y, axis=-1, keepdims=True)
        s = ((y * lax.rsqrt(ms + EPS)) * gnw_ref[...]).astype(BF16)
        out = _dot(a_ref[rows, :], wout_ref[0:A_WIDTH, :]) + _dot(s, wout_ref[A_WIDTH:, :])
        x1 = x_ref[rows, :] + gate_ref[...] * out
        x1_ref[rows, :] = x1
        h = _norm_mod(x1, nw_ref[...], sc_ref[...], sh_ref[...]).astype(BF16)
        cq = _dot(h, w_ref[:, offs[0]:offs[1]])
        ckv = _dot(h, w_ref[:, offs[1]:offs[2]])
        kpe = _dot(h, w_ref[:, offs[2]:offs[3]])
        g_ref[rows, :] = _dot(h, w_ref[:, offs[3]:offs[4]]).astype(g_ref.dtype)
        cqn = rms(cq, qnw_ref[...]).astype(BF16)
        qn_ref[rows, :] = _dot(cqn, wuq_ref[:, 0:nope_w]).astype(qn_ref.dtype)
        qpe = _dot(cqn, wuq_ref[:, nope_w:])
        if rope:
            qpe = _rope(qpe, cos_ref[rows, :], sin_ref[rows, :], MLA_ROPE // 4)
            kpe = _rope(kpe, cos_ref[rows, :], sin_ref[rows, :], MLA_ROPE // 4)
        qpe_ref[rows, :] = qpe.astype(qpe_ref.dtype)
        ckvn = rms(ckv, kvnw_ref[...])
        ckv_ref[rows, :] = ckvn.astype(ckv_ref.dtype)
        kpe_ref[rows, :] = kpe.astype(kpe_ref.dtype)
        ckv32_ref[rows, :] = ckvn
        kpe32_ref[rows, :] = kpe


def _ab_out_mla_in(a, yf, yb, z, gnw, wout, x, gate, nw, scale, shift, w, qnw, kvnw, wuq, rope_tabs, seq_len, tm):
    t, d = x.shape
    per_seq = seq_len // tm if scale.shape[0] > 1 else None
    row = lambda i: (i, 0)
    const = lambda i: (0, 0)
    mod = (lambda i: (i // per_seq, 0, 0)) if per_seq else (lambda i: (0, 0, 0))
    in_specs = [pl.BlockSpec((tm, A_WIDTH), row), pl.BlockSpec((tm, SSD_INNER), row),
                pl.BlockSpec((tm, SSD_INNER), row), pl.BlockSpec((tm, SSD_INNER), row),
                pl.BlockSpec((1, SSD_INNER), const), pl.BlockSpec(wout.shape, const),
                pl.BlockSpec((tm, d), row), pl.BlockSpec((None, 1, d), mod),
                pl.BlockSpec((1, d), const),
                pl.BlockSpec((None, 1, d), mod), pl.BlockSpec((None, 1, d), mod),
                pl.BlockSpec(w.shape, const), pl.BlockSpec((1, Q_LORA), const),
                pl.BlockSpec((1, KV_LORA), const), pl.BlockSpec(wuq.shape, const)]
    args = [a, yf, yb, z, gnw, wout, x, gate, nw, scale, shift, w, qnw, kvnw, wuq]
    if rope_tabs is not None:
        nt = seq_len // tm
        pos = lambda i: (i % nt, 0)
        in_specs += [pl.BlockSpec((tm, LANES), pos), pl.BlockSpec((tm, LANES), pos)]
        args += list(rope_tabs)
    widths = (d, MLA_HEADS * MLA_NOPE, MLA_HEADS * MLA_ROPE, KV_LORA, LANES, MLA_WIDTH, KV_LORA, LANES)
    dts = (F32, BF16, BF16, BF16, BF16, BF16, F32, F32)
    return pl.pallas_call(
        functools.partial(_ab_out_mla_in_kernel, rope=rope_tabs is not None),
        grid=(t // tm,),
        in_specs=in_specs,
        out_specs=[pl.BlockSpec((tm, c), row) for c in widths],
        out_shape=[jax.ShapeDtypeStruct((t, c), dt) for c, dt in zip(widths, dts)],
        compiler_params=_cparams("arbitrary"),
        name="ab_out_mla_in_proj",
    )(*args)


MLA_UNIT_ROWS = 512
MLA_KEY_TILE = 256


def _mla_attn_kernel(qn_ref, qpe_ref, ckv_ref, kpe_ref, wukt_ref, wuv_ref, g_ref, o_ref, kcat_scr, v_scr, s_scr, *,
                     pairs_per_step):
    pg = pl.program_id(1)
    qb = pl.program_id(2)
    tq = qn_ref.shape[0]
    c_exp = MLA_SCALE * LOG2E

    @pl.when(qb == 0)
    def _expand():
        ckv = ckv_ref[...]
        eye = jnp.where(_row_iota((LANES, LANES)) == _lane_iota((LANES, LANES)), 1.0, 0.0).astype(BF16)
        kpe_t = _dot_nt(eye, kpe_ref[...]).astype(BF16)
        for i in range(pairs_per_step):
            kcat_scr[i, 0:LANES, :] = _dot_nt(wukt_ref[i * LANES:(i + 1) * LANES, :], ckv).astype(BF16)
            kcat_scr[i, LANES:2 * LANES, :] = kpe_t
            v_scr[i] = _dot(ckv, wuv_ref[:, i * LANES:(i + 1) * LANES]).astype(BF16)

    nkeys = kcat_scr.shape[-1]
    kt_w = min(MLA_KEY_TILE, nkeys)
    nkt = nkeys // kt_w
    ru = min(MLA_UNIT_ROWS, tq)
    lane = _lane_iota((ru, LANES))
    units = [(i, r, e) for i in range(pairs_per_step) for r in range(tq // ru) for e in range(2)]

    def lane_tiles(x):
        return [x[:, j * LANES:(j + 1) * LANES] for j in range(x.shape[1] // LANES)]

    def q_ext(u):
        i, r, e = units[u]
        rows = slice(r * ru, (r + 1) * ru)
        qn = qn_ref[rows, i * LANES:(i + 1) * LANES]
        qpe = qpe_ref[rows, (i // 2) * LANES:(i // 2 + 1) * LANES]
        slot = 2 * (i % 2) + e if pairs_per_step % 2 == 0 else 2 * ((pg * pairs_per_step + i) % 2) + e
        nope_half = (lane < MLA_NOPE) if e == 0 else (lane >= MLA_NOPE)
        return jnp.concatenate([jnp.where(nope_half, qn, jnp.zeros_like(qn)),
                                jnp.where((lane >> 5) == slot, qpe, jnp.zeros_like(qpe))], axis=1)

    def score_tile(u, st, kt):
        cols = slice(kt * kt_w, (kt + 1) * kt_w)
        s = _dot(st["q"], kcat_scr[units[u][0], :, cols])
        s_scr[u % 2, :, cols] = s
        for t in lane_tiles(s):
            st["mrun"] = t if st["mrun"] is None else jnp.maximum(st["mrun"], t)

    def value_tile(u, st, kt):
        cols = slice(kt * kt_w, (kt + 1) * kt_w)
        p = jnp.exp2((s_scr[u % 2, :, cols] - st["m"]) * c_exp)
        for t in lane_tiles(p):
            st["lrun"] = t if st["lrun"] is None else st["lrun"] + t
        pv = _dot(p.astype(BF16), v_scr[units[u][0], cols, :])
        st["acc"] = pv if st["acc"] is None else st["acc"] + pv

    def finish(u, st, done):
        i, r, e = units[u]
        den = jnp.sum(st["lrun"], axis=1, keepdims=True)
        done[e] = st["acc"] / den
        if e == 1:
            rows = slice(r * ru, (r + 1) * ru)
            attn = jnp.where(lane < MLA_V, done[0], done[1])
            gate = g_ref[rows, i * LANES:(i + 1) * LANES].astype(F32)
            o_ref[rows, i * LANES:(i + 1) * LANES] = (attn * _silu(gate)).astype(o_ref.dtype)

    done = {}
    cur = None
    for u in range(len(units) + 1):
        nxt = dict(q=q_ext(u), mrun=None) if u < len(units) else None
        for kt in range(nkt):
            if nxt is not None:
                score_tile(u, nxt, kt)
            if cur is not None:
                value_tile(u - 1, cur, kt)
        if cur is not None:
            finish(u - 1, cur, done)
        if nxt is not None:
            nxt.update(m=jnp.max(nxt["mrun"], axis=1, keepdims=True), lrun=None, acc=None)
        cur = nxt


def _mla_attn(qn, qpe, ckv_keys, kpe_keys, wukt, wuv, g, nb, seq_len, tq, pairs_per_step):
    t = qn.shape[0]
    nkeys = ckv_keys.shape[1]
    npairs = MLA_HEADS // 2
    ngrp = npairs // pairs_per_step
    nq = seq_len // tq
    wp = pairs_per_step * LANES
    if pairs_per_step % 2 == 0:
        wpe = wp // 2
        pe_map = lambda b, p, i: (b * nq + i, p)
    else:
        wpe = LANES
        pe_map = lambda b, p, i: (b * nq + i, p // 2)
    qmap = lambda b, p, i: (b * nq + i, p)
    return pl.pallas_call(
        functools.partial(_mla_attn_kernel, pairs_per_step=pairs_per_step),
        grid=(nb, ngrp, nq),
        in_specs=[pl.BlockSpec((tq, wp), qmap),
                  pl.BlockSpec((tq, wpe), pe_map),
                  pl.BlockSpec((None, nkeys, KV_LORA), lambda b, p, i: (b, 0, 0)),
                  pl.BlockSpec((None, nkeys, LANES), lambda b, p, i: (b, 0, 0)),
                  pl.BlockSpec((wp, KV_LORA), lambda b, p, i: (p, 0)),
                  pl.BlockSpec((KV_LORA, wp), lambda b, p, i: (0, p)),
                  pl.BlockSpec((tq, wp), qmap)],
        out_specs=pl.BlockSpec((tq, wp), qmap),
        out_shape=jax.ShapeDtypeStruct((t, MLA_WIDTH), BF16),
        scratch_shapes=[pltpu.VMEM((pairs_per_step, 2 * LANES, nkeys), BF16),
                        pltpu.VMEM((pairs_per_step, nkeys, LANES), BF16),
                        pltpu.VMEM((2, min(MLA_UNIT_ROWS, tq), nkeys), F32)],
        compiler_params=_cparams("arbitrary", "arbitrary", "arbitrary"),
        name="mla_attention",
    )(qn, qpe, ckv_keys, kpe_keys, wukt, wuv, g)


def _mla_attn_seq_kernel(qn_ref, qpe_ref, ckv_ref, kpe_ref, wukt_ref, wuv_ref, g_ref, o_ref,
                         kcat_scr, v_scr, s_scr, m_scr, half_scr):
    pair = pl.program_id(1)
    seq = qn_ref.shape[0]
    nkeys = kcat_scr.shape[-1]
    ru, kt_w = MLA_UNIT_ROWS, MLA_KEY_TILE
    nkt = nkeys // kt_w
    nrb = seq // ru
    c_exp = MLA_SCALE * LOG2E
    lane = _lane_iota((ru, LANES))

    ckv = ckv_ref[...]
    eye = jnp.where(_row_iota((LANES, LANES)) == _lane_iota((LANES, LANES)), 1.0, 0.0).astype(BF16)
    kcat_scr[0:LANES, :] = _dot_nt(wukt_ref[...], ckv).astype(BF16)
    kcat_scr[LANES:2 * LANES, :] = _dot_nt(eye, kpe_ref[...]).astype(BF16)
    v_scr[:, 0:LANES] = _dot(ckv, wuv_ref[...]).astype(BF16)
    v_scr[:, LANES:2 * LANES] = jnp.ones((nkeys, LANES), BF16)

    def rows_of(rb):
        return slice(rb * ru, (rb + 1) * ru) if isinstance(rb, int) else pl.ds(pl.multiple_of(rb * ru, ru), ru)

    def q_ext(rb, e):
        rows = rows_of(rb)
        qn = qn_ref[rows, :]
        qpe = qpe_ref[rows, :]
        nope_half = (lane < MLA_NOPE) if e == 0 else (lane >= MLA_NOPE)
        slot = 2 * (pair % 2) + e
        return jnp.concatenate([jnp.where(nope_half, qn, jnp.zeros_like(qn)),
                                jnp.where((lane >> 5) == slot, qpe, jnp.zeros_like(qpe))], axis=1)

    def phase(score, value):
        q = q_ext(*score) if score is not None else None
        mrun = acc = None
        for kt in range(nkt):
            cols = slice(kt * kt_w, (kt + 1) * kt_w)
            if score is not None:
                s = _dot(q, kcat_scr[:, cols])
                s_scr[score[1], :, cols] = s
                for j in range(kt_w // LANES):
                    t = s[:, j * LANES:(j + 1) * LANES]
                    mrun = t if mrun is None else jnp.maximum(mrun, t)
            if value is not None:
                mb = m_scr[value]
                p = jnp.concatenate(
                    [jnp.exp2((s_scr[value, :, kt * kt_w + j * LANES:kt * kt_w + (j + 1) * LANES] - mb) * c_exp)
                     for j in range(kt_w // LANES)], axis=1)
                pv = _dot(p.astype(BF16), v_scr[cols, :])
                acc = pv if acc is None else acc + pv
        if score is not None:
            m_scr[score[1]] = jnp.broadcast_to(jnp.max(mrun, axis=1, keepdims=True), (ru, LANES))
        return None if acc is None else acc[:, 0:LANES] / acc[:, LANES:2 * LANES]

    phase((0, 0), None)

    def row_block(rb, carry):
        half_scr[...] = phase((rb, 1), 0)
        out1 = phase((jnp.minimum(rb + 1, nrb - 1), 0), 1)
        rows = rows_of(rb)
        attn = jnp.where(lane < MLA_V, half_scr[...], out1)
        o_ref[rows, :] = (attn * _silu(g_ref[rows, :].astype(F32))).astype(o_ref.dtype)
        return carry

    lax.fori_loop(0, nrb, row_block, 0)


def _mla_attn_seq(qn, qpe, ckv_keys, kpe_keys, wukt, wuv, g, nb, seq_len):
    t = qn.shape[0]
    nkeys = ckv_keys.shape[1]
    assert seq_len % MLA_UNIT_ROWS == 0 and nkeys % MLA_KEY_TILE == 0
    qmap = lambda b, p: (b, p)
    return pl.pallas_call(
        _mla_attn_seq_kernel,
        grid=(nb, MLA_HEADS // 2),
        in_specs=[pl.BlockSpec((seq_len, LANES), qmap),
                  pl.BlockSpec((seq_len, LANES), lambda b, p: (b, p // 2)),
                  pl.BlockSpec((None, nkeys, KV_LORA), lambda b, p: (b, 0, 0)),
                  pl.BlockSpec((None, nkeys, LANES), lambda b, p: (b, 0, 0)),
                  pl.BlockSpec((LANES, KV_LORA), lambda b, p: (p, 0)),
                  pl.BlockSpec((KV_LORA, LANES), lambda b, p: (0, p)),
                  pl.BlockSpec((seq_len, LANES), qmap)],
        out_specs=pl.BlockSpec((seq_len, LANES), qmap),
        out_shape=jax.ShapeDtypeStruct((t, MLA_WIDTH), BF16),
        scratch_shapes=[pltpu.VMEM((2 * LANES, nkeys), BF16),
                        pltpu.VMEM((nkeys, 2 * LANES), BF16),
                        pltpu.VMEM((2, MLA_UNIT_ROWS, nkeys), F32),
                        pltpu.VMEM((2, MLA_UNIT_ROWS, LANES), F32),
                        pltpu.VMEM((MLA_UNIT_ROWS, LANES), F32)],
        compiler_params=_cparams("arbitrary", "arbitrary"),
        name="mla_attention_seq",
    )(qn, qpe, ckv_keys, kpe_keys, wukt, wuv, g)


def _mla_out_kernel(a_ref, w_ref, x_ref, gate_ref, fw_ref, o_ref):
    for rows in _row_splits(x_ref.shape[0]):
        xn = x_ref[rows, :] + gate_ref[...] * _dot(a_ref[rows, :], w_ref[...])
        ms = jnp.mean(xn * xn, axis=-1, keepdims=True)
        o_ref[rows, :] = (xn * lax.rsqrt(ms + EPS)) * fw_ref[...]


def _mla_out(a, w, x, gate, fw, seq_len, tm):
    t, d = x.shape
    per_seq = seq_len // tm if gate.shape[0] > 1 else None
    row = lambda i: (i, 0)
    const = lambda i: (0, 0)
    mod = (lambda i: (i // per_seq, 0, 0)) if per_seq else (lambda i: (0, 0, 0))
    return pl.pallas_call(
        _mla_out_kernel,
        grid=(t // tm,),
        in_specs=[pl.BlockSpec((tm, MLA_WIDTH), row), pl.BlockSpec(w.shape, const),
                  pl.BlockSpec((tm, d), row), pl.BlockSpec((None, 1, d), mod), pl.BlockSpec((1, d), const)],
        out_specs=pl.BlockSpec((tm, d), row),
        out_shape=jax.ShapeDtypeStruct((t, d), F32),
        compiler_params=_cparams("arbitrary"),
        name="mla_out_proj_final_norm",
    )(a, w, x, gate, fw)


def _rope_tables(length, dim):
    rows = length // GRID_W
    f32 = np.float32
    row = np.repeat(np.arange(rows), GRID_W).astype(f32)
    col = np.tile(np.arange(GRID_W), rows).astype(f32)
    nf = dim // 4
    inv = (f32(1.0) / np.power(f32(ROPE_BASE), np.arange(nf, dtype=f32) / f32(nf))).astype(f32)
    ar = row[:, None] * inv[None, :]
    ac = col[:, None] * inv[None, :]
    ang = np.concatenate([ar, ar, ac, ac], axis=-1).astype(f32)
    sign = np.tile(np.concatenate([-np.ones((nf,), f32), np.ones((nf,), f32)]), 2)
    reps = LANES // dim
    return (jnp.asarray(np.tile(np.cos(ang).astype(f32), (1, reps))),
            jnp.asarray(np.tile((np.sin(ang) * sign).astype(f32), (1, reps))))


PROJ_TM = 1024
CONV_TM = 512
FUSED_TM = 512
CTX_IN_TM = 512


def _group(x, seq_len, conds_rows, mods, wts, tabs_a, tabs_c, ctx_cache):
    nb, _, d = x.shape
    t = nb * seq_len
    latent = ctx_cache is not None
    xf = x.reshape(t, d)
    tm = PROJ_TM

    def mod_rows(layer):
        m = mods[layer][conds_rows]
        sh, sc, gt = jnp.split(m[:, None, :], 3, axis=-1)
        return sh, sc, gt

    sh, sc, gt = mod_rows(0)
    q, k, v, g, z, xbc, dt, *ctx_extra = _ab_in(xf, wts["norm_w"][0:1], sc, sh, wts["ab_w_in"], wts["ab_w_dt"],
                                                tabs_a if latent else None, seq_len, tm if latent else CTX_IN_TM)
    xbc_c = _conv(xbc, wts["ab_conv_w"], wts["ab_conv_b"], seq_len, min(CONV_TM, seq_len))
    if latent:
        k_ctx, v_ctx, s_f0, s_b0, _, _ = ctx_cache
        attn = _attn_band(wts["ab_sink"], q, k, v, k_ctx, v_ctx, g, nb, seq_len)
    else:
        s_f0 = s_b0 = None
        attn = _attn_ctx(wts["ab_sink"], q, k, v, g, seq_len)
    yf, yb, s_f, s_b = _ssd(xbc_c, dt, wts["ssd_par"], wts["ssd_dskip"], s_f0, s_b0, nb, seq_len)

    gt0 = gt
    sh, sc, gt = mod_rows(1)
    x1, qn, qpe, ckv, kpe, g1, ckv32, kpe32 = _ab_out_mla_in(
        attn, yf, yb, z, wts["ab_gnorm_w"], wts["ab_w_out"], xf, gt0,
        wts["norm_w"][1:2], sc, sh, wts["mla_w_in"], wts["mla_q_norm_w"], wts["mla_kv_norm_w"],
        wts["mla_w_uq"], tabs_c if latent else None, seq_len, FUSED_TM)
    ckv_keys = ckv.reshape(nb, seq_len, KV_LORA)
    kpe_keys = kpe.reshape(nb, seq_len, LANES)
    if latent:
        ckv_x, kpe_x = ctx_cache[4], ctx_cache[5]
        ckv_keys = jnp.concatenate([ckv_keys, ckv_x], axis=1)
        kpe_keys = jnp.concatenate([kpe_keys, kpe_x], axis=1)
        attn1 = _mla_attn_seq(qn, qpe, ckv_keys, kpe_keys, wts["mla_w_ukt"], wts["mla_w_uv"], g1, nb, seq_len)
    else:
        attn1 = _mla_attn(qn, qpe, ckv_keys, kpe_keys, wts["mla_w_ukt"], wts["mla_w_uv"], g1, nb, seq_len,
                          seq_len, MLA_HEADS // 2)
    y = _mla_out(attn1, wts["mla_w_out"], x1, gt, wts["final_norm_w"], seq_len, tm)
    return y.reshape(nb, seq_len, d), (ctx_extra, s_f, s_b, ckv32, kpe32)


def kernel(x_prompt, x_sample, cache_a_k, cache_a_v, state_ssd_fwd, state_ssd_bwd, cache_mla_ckv, cache_mla_kpe,
           c, c_ctx, ada_w, ada_b, norm_w, ab_w_in, ab_sink, ab_conv_w, ab_conv_b, ab_dt_bias, ab_a_log,
           ab_d_skip, ab_gnorm_w, ab_w_out, mla_w_in, mla_q_norm_w, mla_kv_norm_w, mla_w_uq, mla_w_ukv,
           mla_w_out, final_norm_w):
    batch, seq, d = x_prompt.shape
    dec_batch, dec_seq, _ = x_sample.shape
    assert ada_w.shape[0] == 2 and ab_w_in.shape[0] == 1 and mla_w_in.shape[0] == 1
    assert dec_batch + 1 <= SUBLANES

    conds = jnp.concatenate([c_ctx[None, :], c, jnp.zeros((SUBLANES - 1 - dec_batch, d), F32)], axis=0)
    mods = _modulation(conds, ada_w, ada_b)

    n_main = sum(AB_COLS[:-1])
    w_dt = jnp.concatenate([ab_w_in[0][:, n_main:], jnp.zeros((d, LANES - 2 * SSD_HEADS), F32)], axis=1).astype(BF16)
    mw = mla_w_in[0]
    o_kpe = Q_LORA + KV_LORA
    w_mla = jnp.concatenate([mw[:, :o_kpe], jnp.tile(mw[:, o_kpe:o_kpe + MLA_ROPE], (1, LANES // MLA_ROPE)),
                             mw[:, o_kpe + MLA_ROPE:]], axis=1).astype(BF16)
    wuq = mla_w_uq[0].reshape(Q_LORA, MLA_HEADS, MLA_NOPE + MLA_ROPE)
    wuq = jnp.concatenate([wuq[:, :, :MLA_NOPE].reshape(Q_LORA, -1), wuq[:, :, MLA_NOPE:].reshape(Q_LORA, -1)],
                          axis=1).astype(BF16)
    wukv = mla_w_ukv[0].reshape(KV_LORA, MLA_HEADS, MLA_NOPE + MLA_V)
    wukt = wukv[:, :, :MLA_NOPE].reshape(KV_LORA, -1).T.astype(BF16)
    wuv = wukv[:, :, MLA_NOPE:].reshape(KV_LORA, -1).astype(BF16)
    pad_lanes = lambda r: jnp.concatenate([r.reshape(1, -1), jnp.zeros((1, LANES - r.size), F32)], axis=1)
    ssd_par = jnp.concatenate([pad_lanes(ab_dt_bias[0]), pad_lanes(ab_a_log[0]),
                               jnp.zeros((SUBLANES - 2, LANES), F32)], axis=0)
    wts = dict(
        norm_w=norm_w, ab_w_in=ab_w_in, ab_w_dt=w_dt, ab_sink=ab_sink[0],
        ab_conv_w=jnp.concatenate([ab_conv_w[0], jnp.zeros((SUBLANES - CONV_K, CONV_CH), F32)], axis=0),
        ab_conv_b=ab_conv_b[0][None, :], ssd_par=ssd_par,
        ssd_dskip=jnp.repeat(ab_d_skip[0], SSD_HEAD_DIM)[None, :],
        ab_gnorm_w=ab_gnorm_w[0][None, :], ab_w_out=ab_w_out[0].astype(BF16),
        mla_w_in=w_mla, mla_q_norm_w=mla_q_norm_w[0][None, :], mla_kv_norm_w=mla_kv_norm_w[0][None, :],
        mla_w_uq=wuq, mla_w_ukt=wukt, mla_w_uv=wuv, mla_w_out=mla_w_out[0].astype(BF16),
        final_norm_w=final_norm_w[None, :],
    )
    tabs_a = _rope_tables(dec_seq, A_HEAD_DIM)
    tabs_c = _rope_tables(dec_seq, MLA_ROPE)

    y_prompt, ((k_t, v_t, w_ab), s_f, s_b, ckv32, kpe32) = _group(
        x_prompt, seq, jnp.zeros((1,), jnp.int32), mods, wts, tabs_a, tabs_c, None)

    past = cache_a_k.shape[2]
    ctx_cache = (cache_a_k[:, 0].reshape(dec_batch, past, A_KV_WIDTH),
                 cache_a_v[:, 0].reshape(dec_batch, past, A_KV_WIDTH),
                 state_ssd_fwd[:, 0], state_ssd_bwd[:, 0],
                 cache_mla_ckv[:, 0].astype(BF16),
                 jnp.tile(cache_mla_kpe[:, 0], (1, 1, LANES // MLA_ROPE)).astype(BF16))
    y_sample, _ = _group(x_sample, dec_seq, 1 + jnp.arange(dec_batch), mods, dict(wts, ab_w_in=w_ab), tabs_a, tabs_c,
                         ctx_cache)

    cache_layout = lambda u: u.reshape(batch, 1, A_KV_HEADS, A_HEAD_DIM, seq).transpose(0, 1, 4, 2, 3)
    return (y_prompt, y_sample,
            cache_layout(k_t), cache_layout(v_t),
            s_f[:, None], s_b[:, None],
            ckv32.reshape(batch, 1, seq, KV_LORA), kpe32[:, :MLA_ROPE].reshape(batch, 1, seq, MLA_ROPE))
```

```python
import functools
import math

import jax
import jax.numpy as jnp
import numpy as np
from jax import lax
from jax.experimental import pallas as pl
from jax.experimental.pallas import tpu as pltpu

F32 = jnp.float32
BF16 = jnp.bfloat16

LANES = 128
SUBLANES = 8
VMEM_LIMIT_BYTES = 56 * 1024 * 1024

GRID_W = 64
ROPE_BASE = 10000.0
EPS = 1e-6
NEG_INF = -1e30
WINDOW_BLK = 128
CHUNK = 128
A_HEADS, A_KV_HEADS, A_HEAD_DIM = 8, 2, 64
A_WIDTH = A_HEADS * A_HEAD_DIM
A_KV_WIDTH = A_KV_HEADS * A_HEAD_DIM
A_SCALE = A_HEAD_DIM ** -0.5
SSD_HEADS, SSD_HEAD_DIM, SSD_GROUPS, SSD_STATE = 16, 64, 2, 64
SSD_INNER = SSD_HEADS * SSD_HEAD_DIM
CONV_K = 5
BC_WIDTH = SSD_GROUPS * SSD_STATE
CONV_CH = SSD_INNER + 2 * BC_WIDTH
MLA_HEADS, MLA_NOPE, MLA_ROPE, MLA_V = 16, 64, 32, 64
Q_LORA, KV_LORA = 256, 128
MLA_WIDTH = MLA_HEADS * MLA_V
MLA_SCALE = (MLA_NOPE + MLA_ROPE) ** -0.5
LOG2E = 1.4426950408889634


def _cparams(*sem):
    return pltpu.CompilerParams(dimension_semantics=sem, vmem_limit_bytes=VMEM_LIMIT_BYTES)


def _silu(x):
    return x * (1.0 / (1.0 + jnp.exp(-x)))


def _dot(a, b):
    return jnp.dot(a, b, preferred_element_type=F32, precision=lax.Precision.DEFAULT)


def _dot_nt(a, b):
    return lax.dot_general(a, b, (((1,), (1,)), ((), ())), preferred_element_type=F32,
                           precision=lax.Precision.DEFAULT)


def _row_splits(m, parts=2):
    step = m // parts
    return [slice(r * step, (r + 1) * step) for r in range(parts)]


def _lane_iota(shape):
    return lax.broadcasted_iota(jnp.int32, shape, len(shape) - 1)


def _row_iota(shape):
    return lax.broadcasted_iota(jnp.int32, shape, len(shape) - 2)


def _mod_kernel(cond_ref, w_ref, b_ref, o_ref):
    s = _silu(cond_ref[...])
    o_ref[...] = _dot(s.astype(BF16), w_ref[...].astype(BF16)) + b_ref[...]


def _modulation(conds, ada_w, ada_b):
    depth, d, d3 = ada_w.shape
    tn = 768
    return pl.pallas_call(
        _mod_kernel,
        grid=(depth, d3 // tn),
        in_specs=[pl.BlockSpec((SUBLANES, d), lambda l, j: (0, 0)),
                  pl.BlockSpec((None, d, tn), lambda l, j: (l, 0, j)),
                  pl.BlockSpec((None, 1, tn), lambda l, j: (l, 0, j))],
        out_specs=pl.BlockSpec((None, SUBLANES, tn), lambda l, j: (l, 0, j)),
        out_shape=jax.ShapeDtypeStruct((depth, SUBLANES, d3), F32),
        compiler_params=_cparams("arbitrary", "arbitrary"),
        name="modulation",
    )(conds, ada_w, ada_b.reshape(depth, 1, d3))


def _norm_mod(x, nw, scale, shift):
    ms = jnp.mean(x * x, axis=-1, keepdims=True)
    y = (x * lax.rsqrt(ms + EPS)) * nw
    return y * (1.0 + scale) + shift


def _rope(x, cos, sin_signed, half_period):
    outs = []
    first = (_lane_iota((x.shape[0], LANES)) & (2 * half_period - 1)) < half_period
    for j in range(x.shape[1] // LANES):
        xj = x[:, j * LANES:(j + 1) * LANES]
        up = pltpu.roll(xj, LANES - half_period, 1)
        dn = pltpu.roll(xj, half_period, 1)
        outs.append(xj * cos + jnp.where(first, up, dn) * sin_signed)
    return outs[0] if len(outs) == 1 else jnp.concatenate(outs, axis=1)


AB_COLS = (A_WIDTH, A_KV_WIDTH, A_KV_WIDTH, A_WIDTH, SSD_INNER, CONV_CH, LANES)


def _ab_in_kernel(*refs, rope, seq_len):
    if rope:
        (x_ref, nw_ref, sc_ref, sh_ref, w_ref, wdt_ref, cos_ref, sin_ref,
         q_ref, k_ref, v_ref, g_ref, z_ref, xbc_ref, dt_ref) = refs
        kvt_refs = {}
    else:
        (x_ref, nw_ref, sc_ref, sh_ref, w_ref, wdt_ref,
         q_ref, k_ref, v_ref, g_ref, z_ref, xbc_ref, dt_ref, kt_ref, vt_ref) = refs
        kvt_refs = {1: kt_ref, 2: vt_ref}
    offs = np.concatenate([[0], np.cumsum(AB_COLS)])
    outs = (q_ref, k_ref, v_ref, g_ref, z_ref, xbc_ref, dt_ref)
    for rows in _row_splits(x_ref.shape[0]):
        h = _norm_mod(x_ref[rows, :], nw_ref[...], sc_ref[...], sh_ref[...]).astype(BF16)
        for i, o_ref in enumerate(outs):
            last = i == len(outs) - 1
            y = _dot(h, wdt_ref[...] if last else w_ref[:, int(offs[i]):int(offs[i + 1])])
            if rope and i in (0, 1):
                y = _rope(y, cos_ref[rows, :], sin_ref[rows, :], A_HEAD_DIM // 4)
            if i in (3, 4):
                y = _silu(y)
            o_ref[rows, :] = y.astype(o_ref.dtype)
            if i in kvt_refs:
                for s in range(rows.start // seq_len, rows.stop // seq_len):
                    kvt_refs[i][s] = y[s * seq_len - rows.start:(s + 1) * seq_len - rows.start, :].T


def _ab_in(x, nw, scale, shift, w, wdt, rope_tabs, seq_len, tm):
    t, d = x.shape
    per_seq = seq_len // tm if scale.shape[0] > 1 else None
    row = lambda i: (i, 0)
    mod = (lambda i: (i // per_seq, 0, 0)) if per_seq else (lambda i: (0, 0, 0))
    const = lambda i: (0, 0)
    in_specs = [pl.BlockSpec((tm, d), row),
                pl.BlockSpec((1, d), const),
                pl.BlockSpec((None, 1, d), mod),
                pl.BlockSpec((None, 1, d), mod),
                pl.BlockSpec(w.shape, const),
                pl.BlockSpec(wdt.shape, const)]
    args = [x, nw, scale, shift, w, wdt]
    if rope_tabs is not None:
        nt = seq_len // tm
        pos = lambda i: (i % nt, 0)
        in_specs += [pl.BlockSpec((tm, LANES), pos), pl.BlockSpec((tm, LANES), pos)]
        args += list(rope_tabs)
    dts = (BF16, F32, F32, BF16, BF16, BF16, F32)
    out_specs = [pl.BlockSpec((tm, c), row) for c in AB_COLS]
    out_shape = [jax.ShapeDtypeStruct((t, c), dt) for c, dt in zip(AB_COLS, dts)]
    if rope_tabs is None:
        assert (tm // 2) % seq_len == 0
        spt = tm // seq_len
        out_specs += [pl.BlockSpec((spt, A_KV_WIDTH, seq_len), lambda i: (i, 0, 0))] * 2
        out_shape += [jax.ShapeDtypeStruct((t // seq_len, A_KV_WIDTH, seq_len), F32)] * 2
    return pl.pallas_call(
        functools.partial(_ab_in_kernel, rope=rope_tabs is not None, seq_len=seq_len),
        grid=(t // tm,),
        in_specs=in_specs,
        out_specs=out_specs,
        out_shape=out_shape,
        compiler_params=_cparams("arbitrary"),
        name="ab_in_proj",
    )(*args)


def _conv_kernel(prev_ref, main_ref, next_ref, w_ref, b_ref, o_ref, *, tiles_per_seq):
    i = pl.program_id(0) % tiles_per_seq
    tm = main_ref.shape[0]
    prev = jnp.where(i > 0, prev_ref[...].astype(F32), 0.0)
    nxt = jnp.where(i < tiles_per_seq - 1, next_ref[...].astype(F32), 0.0)
    ext = jnp.concatenate([prev, main_ref[...].astype(F32), nxt], axis=0)
    first = SUBLANES - CONV_K // 2
    acc = b_ref[...] + ext[first:first + tm] * w_ref[0:1, :]
    for j in range(1, CONV_K):
        acc = acc + ext[first + j:first + j + tm] * w_ref[j:j + 1, :]
    o_ref[...] = _silu(acc).astype(o_ref.dtype)


def _conv(xbc, w, b, seq_len, tm):
    t, c = xbc.shape
    tps = seq_len // tm
    hb = tm // SUBLANES
    nblk8 = t // SUBLANES
    return pl.pallas_call(
        functools.partial(_conv_kernel, tiles_per_seq=tps),
        grid=(t // tm,),
        in_specs=[pl.BlockSpec((SUBLANES, c), lambda i: (jnp.maximum(i * hb - 1, 0), 0)),
                  pl.BlockSpec((tm, c), lambda i: (i, 0)),
                  pl.BlockSpec((SUBLANES, c), lambda i: (jnp.minimum((i + 1) * hb, nblk8 - 1), 0)),
                  pl.BlockSpec((SUBLANES, c), lambda i: (0, 0)),
                  pl.BlockSpec((1, c), lambda i: (0, 0))],
        out_specs=pl.BlockSpec((tm, c), lambda i: (i, 0)),
        out_shape=jax.ShapeDtypeStruct((t, c), BF16),
        compiler_params=_cparams("arbitrary"),
        name="ssd_conv",
    )(xbc, xbc, xbc, w, b)


def _ssd_init(h0_ref, h_scr, has_h0):
    if not has_h0:
        h_scr[...] = jnp.zeros_like(h_scr)
        return
    for pr in range(SSD_HEADS // 2):
        both = jnp.concatenate([h0_ref[2 * pr], h0_ref[2 * pr + 1]], axis=0)
        both = jnp.concatenate([both, jnp.zeros_like(both)], axis=1)
        st = both.T[0:SSD_STATE]
        lo_st = _lane_iota(st.shape) < SSD_HEAD_DIM
        h_scr[2 * pr] = jnp.where(lo_st, st, 0.0)
        h_scr[2 * pr + 1] = jnp.where(lo_st, 0.0, st)


def _ssd_final(h_scr, hout_ref):
    for pr in range(SSD_HEADS // 2):
        st = h_scr[2 * pr] + h_scr[2 * pr + 1]
        st = jnp.concatenate([st, jnp.zeros_like(st)], axis=0).T
        hout_ref[2 * pr] = st[0:SSD_HEAD_DIM, 0:SSD_STATE]
        hout_ref[2 * pr + 1] = st[SSD_HEAD_DIM:2 * SSD_HEAD_DIM, 0:SSD_STATE]


def _ssd_prelude(b_ref, c_ref, dt_ref, par_ref, direction):
    lane = _lane_iota((CHUNK, LANES))
    rowi = _row_iota((CHUNK, LANES))
    dtp_in = dt_ref[...] + par_ref[0:1, :]
    dtp = jnp.maximum(dtp_in, 0.0) + jnp.log(1.0 + jnp.exp(-jnp.abs(dtp_in)))
    la = dtp * (-jnp.exp(par_ref[1:2, :]))
    causal = (rowi >= lane) if direction == 0 else (rowi <= lane)
    tri = jnp.where(causal, 1.0, 0.0).astype(BF16)
    p1 = la.astype(BF16)
    r1 = la - p1.astype(F32)
    p2 = r1.astype(BF16)
    p3 = (r1 - p2.astype(F32)).astype(BF16)
    cum = _dot(tri, p1) + _dot(tri, p2) + _dot(tri, p3)
    cum_t = cum.T
    dtp_t = dtp.T
    end = CHUNK - 1 if direction == 0 else 0
    w_t = jnp.exp(cum_t[:, end:end + 1] - cum_t) * dtp_t
    lrow_t = cum_t - jnp.log(dtp_t)
    bm = b_ref[...]
    cm = c_ref[...]
    cb, cmask = [], []
    for g in range(SSD_GROUPS):
        in_g = (lane >= g * SSD_STATE) & (lane < (g + 1) * SSD_STATE)
        cg = jnp.where(in_g, cm, jnp.zeros_like(cm))
        cmask.append(cg.astype(F32))
        cb.append(_dot_nt(cg, bm))
    return dict(cum=cum, lrow_t=lrow_t, w_t=w_t, causal=causal, end=end,
                bm_t=bm.astype(F32).T, cb=cb, cmask=cmask, lo_half=lane < SSD_HEAD_DIM)


def _ssd_head_pair(pre, pr, x_ref, dskip_ref, y_ref, h_scr, direction):
    hp = SSD_HEADS // SSD_GROUPS
    cum, lrow_t, w_t, end = pre["cum"], pre["lrow_t"], pre["w_t"], pre["end"]
    lo_half = pre["lo_half"]
    xp = x_ref[:, pr * LANES:(pr + 1) * LANES]
    y_pair = None
    for e in range(2):
        h = 2 * pr + e
        g = h // hp
        col = direction * SSD_HEADS + h
        xe = jnp.where(lo_half if e == 0 else ~lo_half, xp, jnp.zeros_like(xp))
        cumcol = jnp.broadcast_to(cum[:, col:col + 1], (CHUNK, LANES))
        diff = cumcol - lrow_t[col:col + 1, :]
        sc = pre["cb"][g] * jnp.exp(jnp.where(pre["causal"], diff, NEG_INF))
        ce = pre["cmask"][g] * jnp.exp(cumcol)
        hs = h_scr[h]
        hsb = hs.astype(BF16)
        zpad = jnp.zeros_like(hsb)
        bw = (pre["bm_t"][g * SSD_STATE:(g + 1) * SSD_STATE, :] * w_t[col:col + 1, :]).astype(BF16)
        a = jnp.concatenate([jnp.concatenate([sc.astype(BF16), ce.astype(BF16)], axis=1),
                             jnp.concatenate([bw, jnp.zeros_like(bw)], axis=1)], axis=0)
        wmat = jnp.concatenate([xe, hsb, zpad] if g == 0 else [xe, zpad, hsb], axis=0)
        res = _dot(a, wmat)
        ye = res[0:CHUNK]
        y_pair = ye if y_pair is None else y_pair + ye
        cdec = jnp.exp(cumcol[end:end + 1, :])
        h_scr[h] = hs * cdec + res[CHUNK:CHUNK + SSD_STATE]
    if direction == 0:
        y_pair = y_pair + dskip_ref[:, pr * LANES:(pr + 1) * LANES] * xp.astype(F32)
    y_ref[:, pr * LANES:(pr + 1) * LANES] = y_pair.astype(y_ref.dtype)


def _ssd_kernel(xf_ref, bf_ref, cf_ref, dtf_ref, xb_ref, bb_ref, cb_ref, dtb_ref, par_ref, dskip_ref,
                h0f_ref, h0b_ref, yf_ref, yb_ref, houtf_ref, houtb_ref, hf_scr, hb_scr, *, nc, has_h0):
    ci = pl.program_id(1)

    @pl.when(ci == 0)
    def _init():
        _ssd_init(h0f_ref, hf_scr, has_h0)
        _ssd_init(h0b_ref, hb_scr, has_h0)

    pre_f = _ssd_prelude(bf_ref, cf_ref, dtf_ref, par_ref, 0)
    pre_b = _ssd_prelude(bb_ref, cb_ref, dtb_ref, par_ref, 1)
    for pr in range(SSD_HEADS // 2):
        _ssd_head_pair(pre_f, pr, xf_ref, dskip_ref, yf_ref, hf_scr, 0)
        _ssd_head_pair(pre_b, pr, xb_ref, dskip_ref, yb_ref, hb_scr, 1)

    @pl.when(ci == nc - 1)
    def _final():
        _ssd_final(hf_scr, houtf_ref)
        _ssd_final(hb_scr, houtb_ref)


def _ssd(xbc, dt, par, dskip, h0f, h0b, nb, seq_len):
    t = xbc.shape[0]
    nc = seq_len // CHUNK
    fwd = lambda b, c: b * nc + c
    bwd = lambda b, c: b * nc + nc - 1 - c
    xcols = SSD_INNER // LANES
    has_h0 = h0f is not None
    if not has_h0:
        h0f = h0b = jnp.zeros((1, SSD_HEADS, SSD_HEAD_DIM, SSD_STATE), F32)
    h0_map = (lambda b, c: (b, 0, 0, 0)) if has_h0 else (lambda b, c: (0, 0, 0, 0))
    st_shape = (None, SSD_HEADS, SSD_HEAD_DIM, SSD_STATE)
    carry_shape = (None, SSD_HEADS, SSD_STATE, LANES)
    per_b = lambda b, c: (b, 0, 0, 0)
    one_block = lambda b, c: (0, 0, 0, 0)

    def chunk_specs(cidx):
        return [pl.BlockSpec((CHUNK, SSD_INNER), lambda b, c: (cidx(b, c), 0)),
                pl.BlockSpec((CHUNK, LANES), lambda b, c: (cidx(b, c), xcols)),
                pl.BlockSpec((CHUNK, LANES), lambda b, c: (cidx(b, c), xcols + 1)),
                pl.BlockSpec((CHUNK, LANES), lambda b, c: (cidx(b, c), 0))]

    outs = pl.pallas_call(
        functools.partial(_ssd_kernel, nc=nc, has_h0=has_h0),
        grid=(nb, nc),
        in_specs=chunk_specs(fwd) + chunk_specs(bwd) + [
            pl.BlockSpec((SUBLANES, LANES), lambda b, c: (0, 0)),
            pl.BlockSpec((1, SSD_INNER), lambda b, c: (0, 0)),
            pl.BlockSpec(st_shape, h0_map), pl.BlockSpec(st_shape, h0_map)],
        out_specs=[pl.BlockSpec((CHUNK, SSD_INNER), lambda b, c: (fwd(b, c), 0)),
                   pl.BlockSpec((CHUNK, SSD_INNER), lambda b, c: (bwd(b, c), 0)),
                   pl.BlockSpec(st_shape, per_b), pl.BlockSpec(st_shape, per_b),
                   pl.BlockSpec(carry_shape, one_block), pl.BlockSpec(carry_shape, one_block)],
        out_shape=[jax.ShapeDtypeStruct((t, SSD_INNER), BF16), jax.ShapeDtypeStruct((t, SSD_INNER), BF16),
                   jax.ShapeDtypeStruct((nb, SSD_HEADS, SSD_HEAD_DIM, SSD_STATE), F32),
                   jax.ShapeDtypeStruct((nb, SSD_HEADS, SSD_HEAD_DIM, SSD_STATE), F32),
                   jax.ShapeDtypeStruct((1, SSD_HEADS, SSD_STATE, LANES), F32),
                   jax.ShapeDtypeStruct((1, SSD_HEADS, SSD_STATE, LANES), F32)],
        compiler_params=_cparams("arbitrary", "arbitrary"),
        name="ssd_scan_bidir",
    )(xbc, xbc, xbc, dt, xbc, xbc, xbc, dt, par, dskip, h0f, h0b)
    return outs[:4]


def _kv_variants(k, v):
    lane = _lane_iota(k.shape)
    lo = lane < A_HEAD_DIM
    k_sw = pltpu.roll(k, A_HEAD_DIM, 1)
    v_sw = pltpu.roll(v, A_HEAD_DIM, 1)
    ks, vs = {}, {}
    for j in range(A_KV_HEADS):
        for e in range(2):
            src_k, src_v = (k, v) if j == e else (k_sw, v_sw)
            half = lo if e == 0 else ~lo
            ks[j, e] = jnp.where(half, src_k, 0.0).astype(BF16)
            vs[j, e] = src_v.astype(BF16)
    return ks, vs


GQA_KEY_TILE = 256


def _gqa_heads(q_ref, g_ref, o_ref, sink_ref, ks, vs, valid_tile, s_scr):
    m = q_ref.shape[0]
    assert A_HEADS // A_KV_HEADS == 4
    lo = _lane_iota((m, LANES)) < A_HEAD_DIM
    upper_rows = _row_iota((2 * m, 1)) >= m
    scale = jnp.asarray(A_SCALE, BF16)
    units = [(j, e) for j in range(A_KV_HEADS) for e in range(2)]
    nkeys = ks[0, 0].shape[0]
    kt_w = GQA_KEY_TILE
    nkt = nkeys // kt_w
    rows_of = {}

    def q_rows(j):
        if j not in rows_of:
            rows_of[j] = jnp.concatenate([q_ref[:, 2 * j * LANES:(2 * j + 1) * LANES],
                                          q_ref[:, (2 * j + 1) * LANES:(2 * j + 2) * LANES]], axis=0) * scale
        return rows_of[j]

    def score_tile(u, st, kt):
        j, e = units[u]
        cols = slice(kt * kt_w, (kt + 1) * kt_w)
        s = _dot_nt(q_rows(j), ks[j, e][cols, :])
        ok = valid_tile(kt)
        if ok is not None:
            s = jnp.where(ok, s, NEG_INF)
        s_scr[u % 2, :, cols] = s
        for c in range(kt_w // LANES):
            t = s[:, c * LANES:(c + 1) * LANES]
            st["mrun"] = t if st["mrun"] is None else jnp.maximum(st["mrun"], t)

    def value_tile(u, st, kt):
        j, e = units[u]
        cols = slice(kt * kt_w, (kt + 1) * kt_w)
        p = jnp.exp(s_scr[u % 2, :, cols] - st["m"])
        for c in range(kt_w // LANES):
            t = p[:, c * LANES:(c + 1) * LANES]
            st["lrun"] = t if st["lrun"] is None else st["lrun"] + t
        pv = _dot(p.astype(BF16), vs[j, e][cols, :])
        st["acc"] = pv if st["acc"] is None else st["acc"] + pv

    outs = {}
    cur = None
    for u in range(len(units) + 1):
        nxt = dict(mrun=None) if u < len(units) else None
        for kt in range(nkt):
            if nxt is not None:
                score_tile(u, nxt, kt)
            if cur is not None:
                value_tile(u - 1, cur, kt)
        if cur is not None:
            den = jnp.sum(cur["lrun"], axis=1, keepdims=True) + jnp.exp(cur["sink"] - cur["m"])
            outs[units[u - 1]] = cur["acc"] / den
        if nxt is not None:
            j, e = units[u]
            sink = jnp.where(upper_rows, sink_ref[4 * j + 2 + e], sink_ref[4 * j + e])
            nxt.update(sink=sink, m=jnp.maximum(jnp.max(nxt["mrun"], axis=1, keepdims=True), sink),
                       lrun=None, acc=None)
        cur = nxt
    for pr in range(A_HEADS // 2):
        j, half = pr // 2, pr % 2
        r = slice(half * m, (half + 1) * m)
        attn = jnp.where(lo, outs[j, 0][r], outs[j, 1][r])
        gate = g_ref[:, pr * LANES:(pr + 1) * LANES].astype(F32)
        o_ref[:, pr * LANES:(pr + 1) * LANES] = (attn * gate).astype(o_ref.dtype)


def _attn_ctx_kernel(sink_ref, q_ref, k_ref, v_ref, g_ref, o_ref, s_scr):
    ks, vs = _kv_variants(k_ref[...], v_ref[...])
    _gqa_heads(q_ref, g_ref, o_ref, sink_ref, ks, vs, lambda kt: None, s_scr)


def _attn_ctx(sink, q, k, v, g, seq_len):
    t = q.shape[0]
    row = lambda b: (b, 0)
    return pl.pallas_call(
        _attn_ctx_kernel,
        grid=(t // seq_len,),
        in_specs=[pl.BlockSpec(memory_space=pltpu.SMEM),
                  pl.BlockSpec((seq_len, A_WIDTH), row),
                  pl.BlockSpec((seq_len, A_KV_WIDTH), row),
                  pl.BlockSpec((seq_len, A_KV_WIDTH), row),
                  pl.BlockSpec((seq_len, A_WIDTH), row)],
        out_specs=pl.BlockSpec((seq_len, A_WIDTH), row),
        out_shape=jax.ShapeDtypeStruct((t, A_WIDTH), BF16),
        scratch_shapes=[pltpu.VMEM((2, 2 * seq_len, seq_len), F32)],
        compiler_params=_cparams("arbitrary"),
        name="swa_context_attention",
    )(sink, q, k, v, g)


BAND_Q = 256


def _attn_band_kernel(sink_ref, q_ref, kp_ref, kc_ref, kn_ref, vp_ref, vc_ref, vn_ref, kx_ref, vx_ref, g_ref,
                      o_ref, s_scr, *, nsteps):
    n = pl.program_id(1)
    tq = q_ref.shape[0]
    blk = WINDOW_BLK
    k = jnp.concatenate([kp_ref[...], kc_ref[...], kn_ref[...], kx_ref[...]], axis=0)
    v = jnp.concatenate([vp_ref[...], vc_ref[...], vn_ref[...], vx_ref[...]], axis=0)
    nloc = tq + 2 * blk
    qi = _row_iota((2 * tq, GQA_KEY_TILE)) & (tq - 1)

    def valid_tile(kt):
        if kt * GQA_KEY_TILE >= nloc:
            return None
        col = _lane_iota((2 * tq, GQA_KEY_TILE)) + kt * GQA_KEY_TILE
        rel = col - blk - qi
        return ((rel >= -blk) & (rel <= blk) & ((col >= blk) | (n > 0)) & ((col < nloc - blk) | (n < nsteps - 1)))

    assert nloc % GQA_KEY_TILE == 0
    ks, vs = _kv_variants(k, v)
    _gqa_heads(q_ref, g_ref, o_ref, sink_ref, ks, vs, valid_tile, s_scr)


def _attn_band(sink, q, k, v, k_ctx, v_ctx, g, nb, seq_len):
    t = q.shape[0]
    blk = WINDOW_BLK
    tq = BAND_Q
    assert tq & (tq - 1) == 0 and tq % blk == 0 and seq_len % tq == 0
    per = tq // blk
    nblk = seq_len // blk
    nsteps = seq_len // tq
    nctx = k_ctx.shape[1]
    cur = lambda b, n: (b * nsteps + n, 0)
    prv = lambda b, n: (b * nblk + jnp.maximum(n * per - 1, 0), 0)
    nxt = lambda b, n: (b * nblk + jnp.minimum((n + 1) * per, nblk - 1), 0)
    ctx = lambda b, n: (b, 0, 0)
    edge = lambda f: pl.BlockSpec((blk, A_KV_WIDTH), f)
    mid = pl.BlockSpec((tq, A_KV_WIDTH), cur)
    return pl.pallas_call(
        functools.partial(_attn_band_kernel, nsteps=nsteps),
        grid=(nb, nsteps),
        in_specs=[pl.BlockSpec(memory_space=pltpu.SMEM),
                  pl.BlockSpec((tq, A_WIDTH), cur),
                  edge(prv), mid, edge(nxt),
                  edge(prv), mid, edge(nxt),
                  pl.BlockSpec((None, nctx, A_KV_WIDTH), ctx),
                  pl.BlockSpec((None, nctx, A_KV_WIDTH), ctx),
                  pl.BlockSpec((tq, A_WIDTH), cur)],
        out_specs=pl.BlockSpec((tq, A_WIDTH), cur),
        out_shape=jax.ShapeDtypeStruct((t, A_WIDTH), BF16),
        scratch_shapes=[pltpu.VMEM((2, 2 * tq, tq + 2 * blk + nctx), F32)],
        compiler_params=_cparams("arbitrary", "arbitrary"),
        name="swa_banded_attention",
    )(sink, q, k, k, k, v, v, v, k_ctx, v_ctx, g)


def _ab_out_kernel(a_ref, yf_ref, yb_ref, z_ref, gnw_ref, w_ref, x_ref, gate_ref, o_ref):
    for rows in _row_splits(x_ref.shape[0]):
        y = (yf_ref[rows, :].astype(F32) + yb_ref[rows, :].astype(F32)) * z_ref[rows, :].astype(F32)
        ms = jnp.mean(y * y, axis=-1, keepdims=True)
        s = ((y * lax.rsqrt(ms + EPS)) * gnw_ref[...]).astype(BF16)
        out = _dot(a_ref[rows, :], w_ref[0:A_WIDTH, :]) + _dot(s, w_ref[A_WIDTH:, :])
        o_ref[rows, :] = x_ref[rows, :] + gate_ref[...] * out


def _ab_out(a, yf, yb, z, gnw, w, x, gate, seq_len, tm):
    t, d = x.shape
    per_seq = seq_len // tm if gate.shape[0] > 1 else None
    row = lambda i: (i, 0)
    const = lambda i: (0, 0)
    mod = (lambda i: (i // per_seq, 0, 0)) if per_seq else (lambda i: (0, 0, 0))
    return pl.pallas_call(
        _ab_out_kernel,
        grid=(t // tm,),
        in_specs=[pl.BlockSpec((tm, A_WIDTH), row),
                  pl.BlockSpec((tm, SSD_INNER), row),
                  pl.BlockSpec((tm, SSD_INNER), row),
                  pl.BlockSpec((tm, SSD_INNER), row),
                  pl.BlockSpec((1, SSD_INNER), const),
                  pl.BlockSpec(w.shape, const),
                  pl.BlockSpec((tm, d), row),
                  pl.BlockSpec((None, 1, d), mod)],
        out_specs=pl.BlockSpec((tm, d), row),
        out_shape=jax.ShapeDtypeStruct((t, d), F32),
        compiler_params=_cparams("arbitrary"),
        name="ab_out_proj",
    )(a, yf, yb, z, gnw, w, x, gate)


MLA_COLS = (Q_LORA, KV_LORA, LANES, MLA_WIDTH)


def _mla_in_kernel(*refs, rope):
    if rope:
        (x_ref, nw_ref, sc_ref, sh_ref, w_ref, qnw_ref, kvnw_ref, wuq_ref, cos_ref, sin_ref,
         qn_ref, qpe_ref, ckv_ref, kpe_ref, g_ref, ckv32_ref, kpe32_ref) = refs
    else:
        (x_ref, nw_ref, sc_ref, sh_ref, w_ref, qnw_ref, kvnw_ref, wuq_ref,
         qn_ref, qpe_ref, ckv_ref, kpe_ref, g_ref, ckv32_ref, kpe32_ref) = refs
    offs = [int(o) for o in np.concatenate([[0], np.cumsum(MLA_COLS)])]
    nope_w = MLA_HEADS * MLA_NOPE

    def rms(u, w):
        return (u * lax.rsqrt(jnp.mean(u * u, axis=-1, keepdims=True) + EPS)) * w

    for rows in _row_splits(x_ref.shape[0]):
        h = _norm_mod(x_ref[rows, :], nw_ref[...], sc_ref[...], sh_ref[...]).astype(BF16)
        cq = _dot(h, w_ref[:, offs[0]:offs[1]])
        ckv = _dot(h, w_ref[:, offs[1]:offs[2]])
        kpe = _dot(h, w_ref[:, offs[2]:offs[3]])
        g_ref[rows, :] = _silu(_dot(h, w_ref[:, offs[3]:offs[4]])).astype(g_ref.dtype)
        cqn = rms(cq, qnw_ref[...]).astype(BF16)
        qn_ref[rows, :] = _dot(cqn, wuq_ref[:, 0:nope_w]).astype(qn_ref.dtype)
        qpe = _dot(cqn, wuq_ref[:, nope_w:])
        if rope:
            qpe = _rope(qpe, cos_ref[rows, :], sin_ref[rows, :], MLA_ROPE // 4)
            kpe = _rope(kpe, cos_ref[rows, :], sin_ref[rows, :], MLA_ROPE // 4)
        qpe_ref[rows, :] = qpe.astype(qpe_ref.dtype)
        ckvn = rms(ckv, kvnw_ref[...])
        ckv_ref[rows, :] = ckvn.astype(ckv_ref.dtype)
        kpe_ref[rows, :] = kpe.astype(kpe_ref.dtype)
        ckv32_ref[rows, :] = ckvn
        kpe32_ref[rows, :] = kpe


def _mla_in(x, nw, scale, shift, w, qnw, kvnw, wuq, rope_tabs, seq_len, tm):
    t, d = x.shape
    per_seq = seq_len // tm if scale.shape[0] > 1 else None
    row = lambda i: (i, 0)
    const = lambda i: (0, 0)
    mod = (lambda i: (i // per_seq, 0, 0)) if per_seq else (lambda i: (0, 0, 0))
    in_specs = [pl.BlockSpec((tm, d), row), pl.BlockSpec((1, d), const),
                pl.BlockSpec((None, 1, d), mod), pl.BlockSpec((None, 1, d), mod),
                pl.BlockSpec(w.shape, const), pl.BlockSpec((1, Q_LORA), const),
                pl.BlockSpec((1, KV_LORA), const), pl.BlockSpec(wuq.shape, const)]
    args = [x, nw, scale, shift, w, qnw, kvnw, wuq]
    if rope_tabs is not None:
        nt = seq_len // tm
        pos = lambda i: (i % nt, 0)
        in_specs += [pl.BlockSpec((tm, LANES), pos), pl.BlockSpec((tm, LANES), pos)]
        args += list(rope_tabs)
    widths = (MLA_HEADS * MLA_NOPE, MLA_HEADS * MLA_ROPE, KV_LORA, LANES, MLA_WIDTH, KV_LORA, LANES)
    dts = (BF16, BF16, BF16, BF16, BF16, F32, F32)
    return pl.pallas_call(
        functools.partial(_mla_in_kernel, rope=rope_tabs is not None),
        grid=(t // tm,),
        in_specs=in_specs,
        out_specs=[pl.BlockSpec((tm, c), row) for c in widths],
        out_shape=[jax.ShapeDtypeStruct((t, c), dt) for c, dt in zip(widths, dts)],
        compiler_params=_cparams("arbitrary"),
        name="mla_in_proj",
    )(*args)


MLA_UNIT_ROWS = 512
MLA_KEY_TILE = 256


def _mla_attn_kernel(qn_ref, qpe_ref, ckv_ref, kpe_ref, wukt_ref, wuv_ref, g_ref, o_ref, kcat_scr, v_scr, s_scr, *,
                     pairs_per_step):
    pg = pl.program_id(1)
    qb = pl.program_id(2)
    tq = qn_ref.shape[0]
    c_exp = MLA_SCALE * LOG2E

    @pl.when(qb == 0)
    def _expand():
        ckv = ckv_ref[...]
        eye = jnp.where(_row_iota((LANES, LANES)) == _lane_iota((LANES, LANES)), 1.0, 0.0).astype(BF16)
        kpe_t = _dot_nt(eye, kpe_ref[...]).astype(BF16)
        for i in range(pairs_per_step):
            kcat_scr[i, 0:LANES, :] = _dot_nt(wukt_ref[i * LANES:(i + 1) * LANES, :], ckv).astype(BF16)
            kcat_scr[i, LANES:2 * LANES, :] = kpe_t
            v_scr[i] = _dot(ckv, wuv_ref[:, i * LANES:(i + 1) * LANES]).astype(BF16)

    nkeys = kcat_scr.shape[-1]
    kt_w = min(MLA_KEY_TILE, nkeys)
    nkt = nkeys // kt_w
    ru = min(MLA_UNIT_ROWS, tq)
    lane = _lane_iota((ru, LANES))
    units = [(i, r, e) for i in range(pairs_per_step) for r in range(tq // ru) for e in range(2)]

    def lane_tiles(x):
        return [x[:, j * LANES:(j + 1) * LANES] for j in range(x.shape[1] // LANES)]

    def q_ext(u):
        i, r, e = units[u]
        rows = slice(r * ru, (r + 1) * ru)
        qn = qn_ref[rows, i * LANES:(i + 1) * LANES]
        qpe = qpe_ref[rows, (i // 2) * LANES:(i // 2 + 1) * LANES]
        slot = 2 * (i % 2) + e if pairs_per_step % 2 == 0 else 2 * ((pg * pairs_per_step + i) % 2) + e
        nope_half = (lane < MLA_NOPE) if e == 0 else (lane >= MLA_NOPE)
        return jnp.concatenate([jnp.where(nope_half, qn, jnp.zeros_like(qn)),
                                jnp.where((lane >> 5) == slot, qpe, jnp.zeros_like(qpe))], axis=1)

    def score_tile(u, st, kt):
        cols = slice(kt * kt_w, (kt + 1) * kt_w)
        s = _dot(st["q"], kcat_scr[units[u][0], :, cols])
        s_scr[u % 2, :, cols] = s
        for t in lane_tiles(s):
            st["mrun"] = t if st["mrun"] is None else jnp.maximum(st["mrun"], t)

    def value_tile(u, st, kt):
        cols = slice(kt * kt_w, (kt + 1) * kt_w)
        p = jnp.exp2((s_scr[u % 2, :, cols] - st["m"]) * c_exp)
        for t in lane_tiles(p):
            st["lrun"] = t if st["lrun"] is None else st["lrun"] + t
        pv = _dot(p.astype(BF16), v_scr[units[u][0], cols, :])
        st["acc"] = pv if st["acc"] is None else st["acc"] + pv

    def finish(u, st, done):
        i, r, e = units[u]
        den = jnp.sum(st["lrun"], axis=1, keepdims=True)
        done[e] = st["acc"] / den
        if e == 1:
            rows = slice(r * ru, (r + 1) * ru)
            attn = jnp.where(lane < MLA_V, done[0], done[1])
            gate = g_ref[rows, i * LANES:(i + 1) * LANES].astype(F32)
            o_ref[rows, i * LANES:(i + 1) * LANES] = (attn * gate).astype(o_ref.dtype)

    done = {}
    cur = None
    for u in range(len(units) + 1):
        nxt = dict(q=q_ext(u), mrun=None) if u < len(units) else None
        for kt in range(nkt):
            if nxt is not None:
                score_tile(u, nxt, kt)
            if cur is not None:
                value_tile(u - 1, cur, kt)
        if cur is not None:
            finish(u - 1, cur, done)
        if nxt is not None:
            nxt.update(m=jnp.max(nxt["mrun"], axis=1, keepdims=True), lrun=None, acc=None)
        cur = nxt


def _mla_attn(qn, qpe, ckv_keys, kpe_keys, wukt, wuv, g, nb, seq_len, tq, pairs_per_step):
    t = qn.shape[0]
    nkeys = ckv_keys.shape[1]
    npairs = MLA_HEADS // 2
    ngrp = npairs // pairs_per_step
    nq = seq_len // tq
    wp = pairs_per_step * LANES
    if pairs_per_step % 2 == 0:
        wpe = wp // 2
        pe_map = lambda b, p, i: (b * nq + i, p)
    else:
        wpe = LANES
        pe_map = lambda b, p, i: (b * nq + i, p // 2)
    qmap = lambda b, p, i: (b * nq + i, p)
    return pl.pallas_call(
        functools.partial(_mla_attn_kernel, pairs_per_step=pairs_per_step),
        grid=(nb, ngrp, nq),
        in_specs=[pl.BlockSpec((tq, wp), qmap),
                  pl.BlockSpec((tq, wpe), pe_map),
                  pl.BlockSpec((None, nkeys, KV_LORA), lambda b, p, i: (b, 0, 0)),
                  pl.BlockSpec((None, nkeys, LANES), lambda b, p, i: (b, 0, 0)),
                  pl.BlockSpec((wp, KV_LORA), lambda b, p, i: (p, 0)),
                  pl.BlockSpec((KV_LORA, wp), lambda b, p, i: (0, p)),
                  pl.BlockSpec((tq, wp), qmap)],
        out_specs=pl.BlockSpec((tq, wp), qmap),
        out_shape=jax.ShapeDtypeStruct((t, MLA_WIDTH), BF16),
        scratch_shapes=[pltpu.VMEM((pairs_per_step, 2 * LANES, nkeys), BF16),
                        pltpu.VMEM((pairs_per_step, nkeys, LANES), BF16),
                        pltpu.VMEM((2, min(MLA_UNIT_ROWS, tq), nkeys), F32)],
        compiler_params=_cparams("arbitrary", "arbitrary", "arbitrary"),
        name="mla_attention",
    )(qn, qpe, ckv_keys, kpe_keys, wukt, wuv, g)


def _mla_attn_seq_kernel(qn_ref, qpe_ref, ckv_ref, kpe_ref, wukt_ref, wuv_ref, g_ref, o_ref,
                         kcat_scr, v_scr, s_scr, m_scr, half_scr):
    pair = pl.program_id(1)
    seq = qn_ref.shape[0]
    nkeys = kcat_scr.shape[-1]
    ru, kt_w = MLA_UNIT_ROWS, MLA_KEY_TILE
    nkt = nkeys // kt_w
    nrb = seq // ru
    c_exp = MLA_SCALE * LOG2E
    lane = _lane_iota((ru, LANES))

    ckv = ckv_ref[...]
    eye = jnp.where(_row_iota((LANES, LANES)) == _lane_iota((LANES, LANES)), 1.0, 0.0).astype(BF16)
    kcat_scr[0:LANES, :] = _dot_nt(wukt_ref[...], ckv).astype(BF16)
    kcat_scr[LANES:2 * LANES, :] = _dot_nt(eye, kpe_ref[...]).astype(BF16)
    v_scr[:, 0:LANES] = _dot(ckv, wuv_ref[...]).astype(BF16)
    v_scr[:, LANES:2 * LANES] = jnp.ones((nkeys, LANES), BF16)

    def rows_of(rb):
        return slice(rb * ru, (rb + 1) * ru) if isinstance(rb, int) else pl.ds(pl.multiple_of(rb * ru, ru), ru)

    def q_ext(rb, e):
        rows = rows_of(rb)
        qn = qn_ref[rows, :]
        qpe = qpe_ref[rows, :]
        nope_half = (lane < MLA_NOPE) if e == 0 else (lane >= MLA_NOPE)
        slot = 2 * (pair % 2) + e
        return jnp.concatenate([jnp.where(nope_half, qn, jnp.zeros_like(qn)),
                                jnp.where((lane >> 5) == slot, qpe, jnp.zeros_like(qpe))], axis=1)

    def phase(score, value):
        q = q_ext(*score) if score is not None else None
        mrun = acc = None
        for kt in range(nkt):
            cols = slice(kt * kt_w, (kt + 1) * kt_w)
            if score is not None:
                s = _dot(q, kcat_scr[:, cols])
                s_scr[score[1], :, cols] = s
                for j in range(kt_w // LANES):
                    t = s[:, j * LANES:(j + 1) * LANES]
                    mrun = t if mrun is None else jnp.maximum(mrun, t)
            if value is not None:
                mb = m_scr[value]
                p = jnp.concatenate(
                    [jnp.exp2((s_scr[value, :, kt * kt_w + j * LANES:kt * kt_w + (j + 1) * LANES] - mb) * c_exp)
                     for j in range(kt_w // LANES)], axis=1)
                pv = _dot(p.astype(BF16), v_scr[cols, :])
                acc = pv if acc is None else acc + pv
        if score is not None:
            m_scr[score[1]] = jnp.broadcast_to(jnp.max(mrun, axis=1, keepdims=True), (ru, LANES))
        return None if acc is None else acc[:, 0:LANES] / acc[:, LANES:2 * LANES]

    phase((0, 0), None)

    def row_block(rb, carry):
        half_scr[...] = phase((rb, 1), 0)
        out1 = phase((jnp.minimum(rb + 1, nrb - 1), 0), 1)
        rows = rows_of(rb)
        attn = jnp.where(lane < MLA_V, half_scr[...], out1)
        o_ref[rows, :] = (attn * g_ref[rows, :].astype(F32)).astype(o_ref.dtype)
        return carry

    lax.fori_loop(0, nrb, row_block, 0)


def _mla_attn_seq(qn, qpe, ckv_keys, kpe_keys, wukt, wuv, g, nb, seq_len):
    t = qn.shape[0]
    nkeys = ckv_keys.shape[1]
    assert seq_len % MLA_UNIT_ROWS == 0 and nkeys % MLA_KEY_TILE == 0
    qmap = lambda b, p: (b, p)
    return pl.pallas_call(
        _mla_attn_seq_kernel,
        grid=(nb, MLA_HEADS // 2),
        in_specs=[pl.BlockSpec((seq_len, LANES), qmap),
                  pl.BlockSpec((seq_len, LANES), lambda b, p: (b, p // 2)),
                  pl.BlockSpec((None, nkeys, KV_LORA), lambda b, p: (b, 0, 0)),
                  pl.BlockSpec((None, nkeys, LANES), lambda b, p: (b, 0, 0)),
                  pl.BlockSpec((LANES, KV_LORA), lambda b, p: (p, 0)),
                  pl.BlockSpec((KV_LORA, LANES), lambda b, p: (0, p)),
                  pl.BlockSpec((seq_len, LANES), qmap)],
        out_specs=pl.BlockSpec((seq_len, LANES), qmap),
        out_shape=jax.ShapeDtypeStruct((t, MLA_WIDTH), BF16),
        scratch_shapes=[pltpu.VMEM((2 * LANES, nkeys), BF16),
                        pltpu.VMEM((nkeys, 2 * LANES), BF16),
                        pltpu.VMEM((2, MLA_UNIT_ROWS, nkeys), F32),
                        pltpu.VMEM((2, MLA_UNIT_ROWS, LANES), F32),
                        pltpu.VMEM((MLA_UNIT_ROWS, LANES), F32)],
        compiler_params=_cparams("arbitrary", "arbitrary"),
        name="mla_attention_seq",
    )(qn, qpe, ckv_keys, kpe_keys, wukt, wuv, g)


def _mla_out_kernel(a_ref, w_ref, x_ref, gate_ref, fw_ref, o_ref):
    for rows in _row_splits(x_ref.shape[0]):
        xn = x_ref[rows, :] + gate_ref[...] * _dot(a_ref[rows, :], w_ref[...])
        ms = jnp.mean(xn * xn, axis=-1, keepdims=True)
        o_ref[rows, :] = (xn * lax.rsqrt(ms + EPS)) * fw_ref[...]


def _mla_out(a, w, x, gate, fw, seq_len, tm):
    t, d = x.shape
    per_seq = seq_len // tm if gate.shape[0] > 1 else None
    row = lambda i: (i, 0)
    const = lambda i: (0, 0)
    mod = (lambda i: (i // per_seq, 0, 0)) if per_seq else (lambda i: (0, 0, 0))
    return pl.pallas_call(
        _mla_out_kernel,
        grid=(t // tm,),
        in_specs=[pl.BlockSpec((tm, MLA_WIDTH), row), pl.BlockSpec(w.shape, const),
                  pl.BlockSpec((tm, d), row), pl.BlockSpec((None, 1, d), mod), pl.BlockSpec((1, d), const)],
        out_specs=pl.BlockSpec((tm, d), row),
        out_shape=jax.ShapeDtypeStruct((t, d), F32),
        compiler_params=_cparams("arbitrary"),
        name="mla_out_proj_final_norm",
    )(a, w, x, gate, fw)


def _rope_tables(length, dim):
    rows = length // GRID_W
    f32 = np.float32
    row = np.repeat(np.arange(rows), GRID_W).astype(f32)
    col = np.tile(np.arange(GRID_W), rows).astype(f32)
    nf = dim // 4
    inv = (f32(1.0) / np.power(f32(ROPE_BASE), np.arange(nf, dtype=f32) / f32(nf))).astype(f32)
    ar = row[:, None] * inv[None, :]
    ac = col[:, None] * inv[None, :]
    ang = np.concatenate([ar, ar, ac, ac], axis=-1).astype(f32)
    sign = np.tile(np.concatenate([-np.ones((nf,), f32), np.ones((nf,), f32)]), 2)
    reps = LANES // dim
    return (jnp.asarray(np.tile(np.cos(ang).astype(f32), (1, reps))),
            jnp.asarray(np.tile((np.sin(ang) * sign).astype(f32), (1, reps))))


PROJ_TM = 1024
CONV_TM = 512


def _group(x, seq_len, conds_rows, mods, wts, tabs_a, tabs_c, ctx_cache):
    nb, _, d = x.shape
    t = nb * seq_len
    latent = ctx_cache is not None
    xf = x.reshape(t, d)
    tm = PROJ_TM

    def mod_rows(layer):
        m = mods[layer][conds_rows]
        sh, sc, gt = jnp.split(m[:, None, :], 3, axis=-1)
        return sh, sc, gt

    sh, sc, gt = mod_rows(0)
    q, k, v, g, z, xbc, dt, *kv_t = _ab_in(xf, wts["norm_w"][0:1], sc, sh, wts["ab_w_in"], wts["ab_w_dt"],
                                           tabs_a if latent else None, seq_len, tm)
    xbc_c = _conv(xbc, wts["ab_conv_w"], wts["ab_conv_b"], seq_len, min(CONV_TM, seq_len))
    if latent:
        k_ctx, v_ctx, s_f0, s_b0, _, _ = ctx_cache
        attn = _attn_band(wts["ab_sink"], q, k, v, k_ctx, v_ctx, g, nb, seq_len)
    else:
        s_f0 = s_b0 = None
        attn = _attn_ctx(wts["ab_sink"], q, k, v, g, seq_len)
    yf, yb, s_f, s_b = _ssd(xbc_c, dt, wts["ssd_par"], wts["ssd_dskip"], s_f0, s_b0, nb, seq_len)
    x1 = _ab_out(attn, yf, yb, z, wts["ab_gnorm_w"], wts["ab_w_out"], xf, gt, seq_len, tm)

    sh, sc, gt = mod_rows(1)
    qn, qpe, ckv, kpe, g1, ckv32, kpe32 = _mla_in(
        x1, wts["norm_w"][1:2], sc, sh, wts["mla_w_in"], wts["mla_q_norm_w"], wts["mla_kv_norm_w"],
        wts["mla_w_uq"], tabs_c if latent else None, seq_len, tm)
    ckv_keys = ckv.reshape(nb, seq_len, KV_LORA)
    kpe_keys = kpe.reshape(nb, seq_len, LANES)
    if latent:
        ckv_x, kpe_x = ctx_cache[4], ctx_cache[5]
        ckv_keys = jnp.concatenate([ckv_keys, ckv_x], axis=1)
        kpe_keys = jnp.concatenate([kpe_keys, kpe_x], axis=1)
        attn1 = _mla_attn_seq(qn, qpe, ckv_keys, kpe_keys, wts["mla_w_ukt"], wts["mla_w_uv"], g1, nb, seq_len)
    else:
        attn1 = _mla_attn(qn, qpe, ckv_keys, kpe_keys, wts["mla_w_ukt"], wts["mla_w_uv"], g1, nb, seq_len,
                          seq_len, MLA_HEADS // 2)
    y = _mla_out(attn1, wts["mla_w_out"], x1, gt, wts["final_norm_w"], seq_len, tm)
    return y.reshape(nb, seq_len, d), (kv_t, s_f, s_b, ckv32, kpe32)


def kernel(x_prompt, x_sample, cache_a_k, cache_a_v, state_ssd_fwd, state_ssd_bwd, cache_mla_ckv, cache_mla_kpe,
           c, c_ctx, ada_w, ada_b, norm_w, ab_w_in, ab_sink, ab_conv_w, ab_conv_b, ab_dt_bias, ab_a_log,
           ab_d_skip, ab_gnorm_w, ab_w_out, mla_w_in, mla_q_norm_w, mla_kv_norm_w, mla_w_uq, mla_w_ukv,
           mla_w_out, final_norm_w):
    batch, seq, d = x_prompt.shape
    dec_batch, dec_seq, _ = x_sample.shape
    assert ada_w.shape[0] == 2 and ab_w_in.shape[0] == 1 and mla_w_in.shape[0] == 1
    assert dec_batch + 1 <= SUBLANES

    conds = jnp.concatenate([c_ctx[None, :], c, jnp.zeros((SUBLANES - 1 - dec_batch, d), F32)], axis=0)
    mods = _modulation(conds, ada_w, ada_b)

    n_main = sum(AB_COLS[:-1])
    w_ab = ab_w_in[0][:, :n_main].astype(BF16)
    w_dt = jnp.concatenate([ab_w_in[0][:, n_main:], jnp.zeros((d, LANES - 2 * SSD_HEADS), F32)], axis=1).astype(BF16)
    mw = mla_w_in[0]
    o_kpe = Q_LORA + KV_LORA
    w_mla = jnp.concatenate([mw[:, :o_kpe], jnp.tile(mw[:, o_kpe:o_kpe + MLA_ROPE], (1, LANES // MLA_ROPE)),
                             mw[:, o_kpe + MLA_ROPE:]], axis=1).astype(BF16)
    wuq = mla_w_uq[0].reshape(Q_LORA, MLA_HEADS, MLA_NOPE + MLA_ROPE)
    wuq = jnp.concatenate([wuq[:, :, :MLA_NOPE].reshape(Q_LORA, -1), wuq[:, :, MLA_NOPE:].reshape(Q_LORA, -1)],
                          axis=1).astype(BF16)
    wukv = mla_w_ukv[0].reshape(KV_LORA, MLA_HEADS, MLA_NOPE + MLA_V)
    wukt = wukv[:, :, :MLA_NOPE].reshape(KV_LORA, -1).T.astype(BF16)
    wuv = wukv[:, :, MLA_NOPE:].reshape(KV_LORA, -1).astype(BF16)
    pad_lanes = lambda r: jnp.concatenate([r.reshape(1, -1), jnp.zeros((1, LANES - r.size), F32)], axis=1)
    ssd_par = jnp.concatenate([pad_lanes(ab_dt_bias[0]), pad_lanes(ab_a_log[0]),
                               jnp.zeros((SUBLANES - 2, LANES), F32)], axis=0)
    wts = dict(
        norm_w=norm_w, ab_w_in=w_ab, ab_w_dt=w_dt, ab_sink=ab_sink[0],
        ab_conv_w=jnp.concatenate([ab_conv_w[0], jnp.zeros((SUBLANES - CONV_K, CONV_CH), F32)], axis=0),
        ab_conv_b=ab_conv_b[0][None, :], ssd_par=ssd_par,
        ssd_dskip=jnp.repeat(ab_d_skip[0], SSD_HEAD_DIM)[None, :],
        ab_gnorm_w=ab_gnorm_w[0][None, :], ab_w_out=ab_w_out[0].astype(BF16),
        mla_w_in=w_mla, mla_q_norm_w=mla_q_norm_w[0][None, :], mla_kv_norm_w=mla_kv_norm_w[0][None, :],
        mla_w_uq=wuq, mla_w_ukt=wukt, mla_w_uv=wuv, mla_w_out=mla_w_out[0].astype(BF16),
        final_norm_w=final_norm_w[None, :],
    )
    tabs_a = _rope_tables(dec_seq, A_HEAD_DIM)
    tabs_c = _rope_tables(dec_seq, MLA_ROPE)

    y_prompt, ((k_t, v_t), s_f, s_b, ckv32, kpe32) = _group(
        x_prompt, seq, jnp.zeros((1,), jnp.int32), mods, wts, tabs_a, tabs_c, None)

    past = cache_a_k.shape[2]
    ctx_cache = (cache_a_k[:, 0].reshape(dec_batch, past, A_KV_WIDTH),
                 cache_a_v[:, 0].reshape(dec_batch, past, A_KV_WIDTH),
                 state_ssd_fwd[:, 0], state_ssd_bwd[:, 0],
                 cache_mla_ckv[:, 0].astype(BF16),
                 jnp.tile(cache_mla_kpe[:, 0], (1, 1, LANES // MLA_ROPE)).astype(BF16))
    y_sample, _ = _group(x_sample, dec_seq, 1 + jnp.arange(dec_batch), mods, wts, tabs_a, tabs_c, ctx_cache)

    cache_layout = lambda u: u.reshape(batch, 1, A_KV_HEADS, A_HEAD_DIM, seq).transpose(0, 1, 4, 2, 3)
    return (y_prompt, y_sample,
            cache_layout(k_t), cache_layout(v_t),
            s_f[:, None], s_b[:, None],
            ckv32.reshape(batch, 1, seq, KV_LORA), kpe32[:, :MLA_ROPE].reshape(batch, 1, seq, MLA_ROPE))
```

```python
import functools
import math

import jax
import jax.numpy as jnp
import numpy as np
from jax import lax
from jax.experimental import pallas as pl
from jax.experimental.pallas import tpu as pltpu

F32 = jnp.float32
BF16 = jnp.bfloat16

LANES = 128
SUBLANES = 8
VMEM_LIMIT_BYTES = 56 * 1024 * 1024

GRID_W = 64
ROPE_BASE = 10000.0
EPS = 1e-6
NEG_INF = -1e30
WINDOW_BLK = 128
CHUNK = 128
A_HEADS, A_KV_HEADS, A_HEAD_DIM = 8, 2, 64
A_WIDTH = A_HEADS * A_HEAD_DIM
A_KV_WIDTH = A_KV_HEADS * A_HEAD_DIM
A_SCALE = A_HEAD_DIM ** -0.5
SSD_HEADS, SSD_HEAD_DIM, SSD_GROUPS, SSD_STATE = 16, 64, 2, 64
SSD_INNER = SSD_HEADS * SSD_HEAD_DIM
CONV_K = 5
BC_WIDTH = SSD_GROUPS * SSD_STATE
CONV_CH = SSD_INNER + 2 * BC_WIDTH
MLA_HEADS, MLA_NOPE, MLA_ROPE, MLA_V = 16, 64, 32, 64
Q_LORA, KV_LORA = 256, 128
MLA_WIDTH = MLA_HEADS * MLA_V
MLA_SCALE = (MLA_NOPE + MLA_ROPE) ** -0.5
LOG2E = 1.4426950408889634


def _cparams(*sem):
    return pltpu.CompilerParams(dimension_semantics=sem, vmem_limit_bytes=VMEM_LIMIT_BYTES)


def _silu(x):
    return x * (1.0 / (1.0 + jnp.exp(-x)))


def _dot(a, b):
    return jnp.dot(a, b, preferred_element_type=F32, precision=lax.Precision.DEFAULT)


def _dot_nt(a, b):
    return lax.dot_general(a, b, (((1,), (1,)), ((), ())), preferred_element_type=F32,
                           precision=lax.Precision.DEFAULT)


def _row_splits(m, parts=2):
    step = m // parts
    return [slice(r * step, (r + 1) * step) for r in range(parts)]


def _lane_iota(shape):
    return lax.broadcasted_iota(jnp.int32, shape, len(shape) - 1)


def _row_iota(shape):
    return lax.broadcasted_iota(jnp.int32, shape, len(shape) - 2)


def _mod_kernel(cond_ref, w_ref, b_ref, o_ref):
    s = _silu(cond_ref[...])
    o_ref[...] = _dot(s.astype(BF16), w_ref[...].astype(BF16)) + b_ref[...]


def _modulation(conds, ada_w, ada_b):
    depth, d, d3 = ada_w.shape
    tn = 768
    return pl.pallas_call(
        _mod_kernel,
        grid=(depth, d3 // tn),
        in_specs=[pl.BlockSpec((SUBLANES, d), lambda l, j: (0, 0)),
                  pl.BlockSpec((None, d, tn), lambda l, j: (l, 0, j)),
                  pl.BlockSpec((None, 1, tn), lambda l, j: (l, 0, j))],
        out_specs=pl.BlockSpec((None, SUBLANES, tn), lambda l, j: (l, 0, j)),
        out_shape=jax.ShapeDtypeStruct((depth, SUBLANES, d3), F32),
        compiler_params=_cparams("arbitrary", "arbitrary"),
        name="modulation",
    )(conds, ada_w, ada_b.reshape(depth, 1, d3))


def _norm_mod(x, nw, scale, shift):
    ms = jnp.mean(x * x, axis=-1, keepdims=True)
    y = (x * lax.rsqrt(ms + EPS)) * nw
    return y * (1.0 + scale) + shift


def _rope(x, cos, sin_signed, half_period):
    outs = []
    first = (_lane_iota((x.shape[0], LANES)) & (2 * half_period - 1)) < half_period
    for j in range(x.shape[1] // LANES):
        xj = x[:, j * LANES:(j + 1) * LANES]
        up = pltpu.roll(xj, LANES - half_period, 1)
        dn = pltpu.roll(xj, half_period, 1)
        outs.append(xj * cos + jnp.where(first, up, dn) * sin_signed)
    return outs[0] if len(outs) == 1 else jnp.concatenate(outs, axis=1)


AB_COLS = (A_WIDTH, A_KV_WIDTH, A_KV_WIDTH, A_WIDTH, SSD_INNER, CONV_CH, LANES)


def _ab_in_kernel(*refs, rope, seq_len):
    if rope:
        (x_ref, nw_ref, sc_ref, sh_ref, w_ref, wdt_ref, cos_ref, sin_ref,
         q_ref, k_ref, v_ref, g_ref, z_ref, xbc_ref, dt_ref) = refs
        kvt_refs = {}
    else:
        (x_ref, nw_ref, sc_ref, sh_ref, w_ref, wdt_ref,
         q_ref, k_ref, v_ref, g_ref, z_ref, xbc_ref, dt_ref, kt_ref, vt_ref) = refs
        kvt_refs = {1: kt_ref, 2: vt_ref}
    offs = np.concatenate([[0], np.cumsum(AB_COLS)])
    outs = (q_ref, k_ref, v_ref, g_ref, z_ref, xbc_ref, dt_ref)
    for rows in _row_splits(x_ref.shape[0]):
        h = _norm_mod(x_ref[rows, :], nw_ref[...], sc_ref[...], sh_ref[...]).astype(BF16)
        for i, o_ref in enumerate(outs):
            last = i == len(outs) - 1
            y = _dot(h, wdt_ref[...] if last else w_ref[:, int(offs[i]):int(offs[i + 1])])
            if rope and i in (0, 1):
                y = _rope(y, cos_ref[rows, :], sin_ref[rows, :], A_HEAD_DIM // 4)
            o_ref[rows, :] = y.astype(o_ref.dtype)
            if i in kvt_refs:
                for s in range(rows.start // seq_len, rows.stop // seq_len):
                    kvt_refs[i][s] = y[s * seq_len - rows.start:(s + 1) * seq_len - rows.start, :].T


def _ab_in(x, nw, scale, shift, w, wdt, rope_tabs, seq_len, tm):
    t, d = x.shape
    per_seq = seq_len // tm if scale.shape[0] > 1 else None
    row = lambda i: (i, 0)
    mod = (lambda i: (i // per_seq, 0, 0)) if per_seq else (lambda i: (0, 0, 0))
    const = lambda i: (0, 0)
    in_specs = [pl.BlockSpec((tm, d), row),
                pl.BlockSpec((1, d), const),
                pl.BlockSpec((None, 1, d), mod),
                pl.BlockSpec((None, 1, d), mod),
                pl.BlockSpec(w.shape, const),
                pl.BlockSpec(wdt.shape, const)]
    args = [x, nw, scale, shift, w, wdt]
    if rope_tabs is not None:
        nt = seq_len // tm
        pos = lambda i: (i % nt, 0)
        in_specs += [pl.BlockSpec((tm, LANES), pos), pl.BlockSpec((tm, LANES), pos)]
        args += list(rope_tabs)
    dts = (BF16, F32, F32, BF16, BF16, BF16, F32)
    out_specs = [pl.BlockSpec((tm, c), row) for c in AB_COLS]
    out_shape = [jax.ShapeDtypeStruct((t, c), dt) for c, dt in zip(AB_COLS, dts)]
    if rope_tabs is None:
        assert (tm // 2) % seq_len == 0
        spt = tm // seq_len
        out_specs += [pl.BlockSpec((spt, A_KV_WIDTH, seq_len), lambda i: (i, 0, 0))] * 2
        out_shape += [jax.ShapeDtypeStruct((t // seq_len, A_KV_WIDTH, seq_len), F32)] * 2
    return pl.pallas_call(
        functools.partial(_ab_in_kernel, rope=rope_tabs is not None, seq_len=seq_len),
        grid=(t // tm,),
        in_specs=in_specs,
        out_specs=out_specs,
        out_shape=out_shape,
        compiler_params=_cparams("arbitrary"),
        name="ab_in_proj",
    )(*args)


def _conv_kernel(prev_ref, main_ref, next_ref, w_ref, b_ref, o_ref, *, tiles_per_seq):
    i = pl.program_id(0) % tiles_per_seq
    tm = main_ref.shape[0]
    prev = jnp.where(i > 0, prev_ref[...].astype(F32), 0.0)
    nxt = jnp.where(i < tiles_per_seq - 1, next_ref[...].astype(F32), 0.0)
    ext = jnp.concatenate([prev, main_ref[...].astype(F32), nxt], axis=0)
    first = SUBLANES - CONV_K // 2
    acc = b_ref[...] + ext[first:first + tm] * w_ref[0:1, :]
    for j in range(1, CONV_K):
        acc = acc + ext[first + j:first + j + tm] * w_ref[j:j + 1, :]
    o_ref[...] = _silu(acc).astype(o_ref.dtype)


def _conv(xbc, w, b, seq_len, tm):
    t, c = xbc.shape
    tps = seq_len // tm
    hb = tm // SUBLANES
    nblk8 = t // SUBLANES
    return pl.pallas_call(
        functools.partial(_conv_kernel, tiles_per_seq=tps),
        grid=(t // tm,),
        in_specs=[pl.BlockSpec((SUBLANES, c), lambda i: (jnp.maximum(i * hb - 1, 0), 0)),
                  pl.BlockSpec((tm, c), lambda i: (i, 0)),
                  pl.BlockSpec((SUBLANES, c), lambda i: (jnp.minimum((i + 1) * hb, nblk8 - 1), 0)),
                  pl.BlockSpec((SUBLANES, c), lambda i: (0, 0)),
                  pl.BlockSpec((1, c), lambda i: (0, 0))],
        out_specs=pl.BlockSpec((tm, c), lambda i: (i, 0)),
        out_shape=jax.ShapeDtypeStruct((t, c), BF16),
        compiler_params=_cparams("arbitrary"),
        name="ssd_conv",
    )(xbc, xbc, xbc, w, b)


def _ssd_init(h0_ref, h_scr, has_h0):
    if not has_h0:
        h_scr[...] = jnp.zeros_like(h_scr)
        return
    for pr in range(SSD_HEADS // 2):
        both = jnp.concatenate([h0_ref[2 * pr], h0_ref[2 * pr + 1]], axis=0)
        both = jnp.concatenate([both, jnp.zeros_like(both)], axis=1)
        st = both.T[0:SSD_STATE]
        lo_st = _lane_iota(st.shape) < SSD_HEAD_DIM
        h_scr[2 * pr] = jnp.where(lo_st, st, 0.0)
        h_scr[2 * pr + 1] = jnp.where(lo_st, 0.0, st)


def _ssd_final(h_scr, hout_ref):
    for pr in range(SSD_HEADS // 2):
        st = h_scr[2 * pr] + h_scr[2 * pr + 1]
        st = jnp.concatenate([st, jnp.zeros_like(st)], axis=0).T
        hout_ref[2 * pr] = st[0:SSD_HEAD_DIM, 0:SSD_STATE]
        hout_ref[2 * pr + 1] = st[SSD_HEAD_DIM:2 * SSD_HEAD_DIM, 0:SSD_STATE]


def _ssd_prelude(b_ref, c_ref, dt_ref, par_ref, direction):
    lane = _lane_iota((CHUNK, LANES))
    rowi = _row_iota((CHUNK, LANES))
    dtp_in = dt_ref[...] + par_ref[0:1, :]
    dtp = jnp.maximum(dtp_in, 0.0) + jnp.log(1.0 + jnp.exp(-jnp.abs(dtp_in)))
    la = dtp * (-jnp.exp(par_ref[1:2, :]))
    causal = (rowi >= lane) if direction == 0 else (rowi <= lane)
    tri = jnp.where(causal, 1.0, 0.0).astype(BF16)
    p1 = la.astype(BF16)
    r1 = la - p1.astype(F32)
    p2 = r1.astype(BF16)
    p3 = (r1 - p2.astype(F32)).astype(BF16)
    cum = _dot(tri, p1) + _dot(tri, p2) + _dot(tri, p3)
    cum_t = cum.T
    dtp_t = dtp.T
    end = CHUNK - 1 if direction == 0 else 0
    w_t = jnp.exp(cum_t[:, end:end + 1] - cum_t) * dtp_t
    lrow_t = cum_t - jnp.log(dtp_t)
    bm = b_ref[...]
    cm = c_ref[...]
    cb, cmask = [], []
    for g in range(SSD_GROUPS):
        in_g = (lane >= g * SSD_STATE) & (lane < (g + 1) * SSD_STATE)
        cg = jnp.where(in_g, cm, jnp.zeros_like(cm))
        cmask.append(cg.astype(F32))
        cb.append(_dot_nt(cg, bm))
    return dict(cum=cum, lrow_t=lrow_t, w_t=w_t, causal=causal, end=end,
                bm_t=bm.astype(F32).T, cb=cb, cmask=cmask, lo_half=lane < SSD_HEAD_DIM)


def _ssd_head_pair(pre, pr, x_ref, dskip_ref, y_ref, h_scr, direction):
    hp = SSD_HEADS // SSD_GROUPS
    cum, lrow_t, w_t, end = pre["cum"], pre["lrow_t"], pre["w_t"], pre["end"]
    lo_half = pre["lo_half"]
    xp = x_ref[:, pr * LANES:(pr + 1) * LANES]
    y_pair = None
    for e in range(2):
        h = 2 * pr + e
        g = h // hp
        col = direction * SSD_HEADS + h
        xe = jnp.where(lo_half if e == 0 else ~lo_half, xp, jnp.zeros_like(xp))
        cumcol = jnp.broadcast_to(cum[:, col:col + 1], (CHUNK, LANES))
        diff = cumcol - lrow_t[col:col + 1, :]
        sc = pre["cb"][g] * jnp.exp(jnp.where(pre["causal"], diff, NEG_INF))
        ce = pre["cmask"][g] * jnp.exp(cumcol)
        hs = h_scr[h]
        hsb = hs.astype(BF16)
        zpad = jnp.zeros_like(hsb)
        bw = (pre["bm_t"][g * SSD_STATE:(g + 1) * SSD_STATE, :] * w_t[col:col + 1, :]).astype(BF16)
        a = jnp.concatenate([jnp.concatenate([sc.astype(BF16), ce.astype(BF16)], axis=1),
                             jnp.concatenate([bw, jnp.zeros_like(bw)], axis=1)], axis=0)
        wmat = jnp.concatenate([xe, hsb, zpad] if g == 0 else [xe, zpad, hsb], axis=0)
        res = _dot(a, wmat)
        ye = res[0:CHUNK]
        y_pair = ye if y_pair is None else y_pair + ye
        cdec = jnp.exp(cumcol[end:end + 1, :])
        h_scr[h] = hs * cdec + res[CHUNK:CHUNK + SSD_STATE]
    if direction == 0:
        y_pair = y_pair + dskip_ref[:, pr * LANES:(pr + 1) * LANES] * xp.astype(F32)
    y_ref[:, pr * LANES:(pr + 1) * LANES] = y_pair.astype(y_ref.dtype)


def _ssd_kernel(xf_ref, bf_ref, cf_ref, dtf_ref, xb_ref, bb_ref, cb_ref, dtb_ref, par_ref, dskip_ref,
                h0f_ref, h0b_ref, yf_ref, yb_ref, houtf_ref, houtb_ref, hf_scr, hb_scr, *, nc, has_h0):
    ci = pl.program_id(1)

    @pl.when(ci == 0)
    def _init():
        _ssd_init(h0f_ref, hf_scr, has_h0)
        _ssd_init(h0b_ref, hb_scr, has_h0)

    pre_f = _ssd_prelude(bf_ref, cf_ref, dtf_ref, par_ref, 0)
    pre_b = _ssd_prelude(bb_ref, cb_ref, dtb_ref, par_ref, 1)
    for pr in range(SSD_HEADS // 2):
        _ssd_head_pair(pre_f, pr, xf_ref, dskip_ref, yf_ref, hf_scr, 0)
        _ssd_head_pair(pre_b, pr, xb_ref, dskip_ref, yb_ref, hb_scr, 1)

    @pl.when(ci == nc - 1)
    def _final():
        _ssd_final(hf_scr, houtf_ref)
        _ssd_final(hb_scr, houtb_ref)


def _ssd(xbc, dt, par, dskip, h0f, h0b, nb, seq_len):
    t = xbc.shape[0]
    nc = seq_len // CHUNK
    fwd = lambda b, c: b * nc + c
    bwd = lambda b, c: b * nc + nc - 1 - c
    xcols = SSD_INNER // LANES
    has_h0 = h0f is not None
    if not has_h0:
        h0f = h0b = jnp.zeros((1, SSD_HEADS, SSD_HEAD_DIM, SSD_STATE), F32)
    h0_map = (lambda b, c: (b, 0, 0, 0)) if has_h0 else (lambda b, c: (0, 0, 0, 0))
    st_shape = (None, SSD_HEADS, SSD_HEAD_DIM, SSD_STATE)
    carry_shape = (None, SSD_HEADS, SSD_STATE, LANES)
    per_b = lambda b, c: (b, 0, 0, 0)
    one_block = lambda b, c: (0, 0, 0, 0)

    def chunk_specs(cidx):
        return [pl.BlockSpec((CHUNK, SSD_INNER), lambda b, c: (cidx(b, c), 0)),
                pl.BlockSpec((CHUNK, LANES), lambda b, c: (cidx(b, c), xcols)),
                pl.BlockSpec((CHUNK, LANES), lambda b, c: (cidx(b, c), xcols + 1)),
                pl.BlockSpec((CHUNK, LANES), lambda b, c: (cidx(b, c), 0))]

    outs = pl.pallas_call(
        functools.partial(_ssd_kernel, nc=nc, has_h0=has_h0),
        grid=(nb, nc),
        in_specs=chunk_specs(fwd) + chunk_specs(bwd) + [
            pl.BlockSpec((SUBLANES, LANES), lambda b, c: (0, 0)),
            pl.BlockSpec((1, SSD_INNER), lambda b, c: (0, 0)),
            pl.BlockSpec(st_shape, h0_map), pl.BlockSpec(st_shape, h0_map)],
        out_specs=[pl.BlockSpec((CHUNK, SSD_INNER), lambda b, c: (fwd(b, c), 0)),
                   pl.BlockSpec((CHUNK, SSD_INNER), lambda b, c: (bwd(b, c), 0)),
                   pl.BlockSpec(st_shape, per_b), pl.BlockSpec(st_shape, per_b),
                   pl.BlockSpec(carry_shape, one_block), pl.BlockSpec(carry_shape, one_block)],
        out_shape=[jax.ShapeDtypeStruct((t, SSD_INNER), BF16), jax.ShapeDtypeStruct((t, SSD_INNER), BF16),
                   jax.ShapeDtypeStruct((nb, SSD_HEADS, SSD_HEAD_DIM, SSD_STATE), F32),
                   jax.ShapeDtypeStruct((nb, SSD_HEADS, SSD_HEAD_DIM, SSD_STATE), F32),
                   jax.ShapeDtypeStruct((1, SSD_HEADS, SSD_STATE, LANES), F32),
                   jax.ShapeDtypeStruct((1, SSD_HEADS, SSD_STATE, LANES), F32)],
        compiler_params=_cparams("arbitrary", "arbitrary"),
        name="ssd_scan_bidir",
    )(xbc, xbc, xbc, dt, xbc, xbc, xbc, dt, par, dskip, h0f, h0b)
    return outs[:4]


def _kv_variants(k, v):
    lane = _lane_iota(k.shape)
    lo = lane < A_HEAD_DIM
    k_sw = pltpu.roll(k, A_HEAD_DIM, 1)
    v_sw = pltpu.roll(v, A_HEAD_DIM, 1)
    ks, vs = {}, {}
    for j in range(A_KV_HEADS):
        for e in range(2):
            src_k, src_v = (k, v) if j == e else (k_sw, v_sw)
            half = lo if e == 0 else ~lo
            ks[j, e] = jnp.where(half, src_k, 0.0).astype(BF16)
            vs[j, e] = src_v.astype(BF16)
    return ks, vs


GQA_KEY_TILE = 256


def _gqa_heads(q_ref, g_ref, o_ref, sink_ref, ks, vs, valid_tile, s_scr):
    m = q_ref.shape[0]
    assert A_HEADS // A_KV_HEADS == 4
    lo = _lane_iota((m, LANES)) < A_HEAD_DIM
    upper_rows = _row_iota((2 * m, 1)) >= m
    scale = jnp.asarray(A_SCALE, BF16)
    units = [(j, e) for j in range(A_KV_HEADS) for e in range(2)]
    nkeys = ks[0, 0].shape[0]
    kt_w = GQA_KEY_TILE
    nkt = nkeys // kt_w
    rows_of = {}

    def q_rows(j):
        if j not in rows_of:
            rows_of[j] = jnp.concatenate([q_ref[:, 2 * j * LANES:(2 * j + 1) * LANES],
                                          q_ref[:, (2 * j + 1) * LANES:(2 * j + 2) * LANES]], axis=0) * scale
        return rows_of[j]

    def score_tile(u, st, kt):
        j, e = units[u]
        cols = slice(kt * kt_w, (kt + 1) * kt_w)
        s = _dot_nt(q_rows(j), ks[j, e][cols, :])
        ok = valid_tile(kt)
        if ok is not None:
            s = jnp.where(ok, s, NEG_INF)
        s_scr[u % 2, :, cols] = s
        for c in range(kt_w // LANES):
            t = s[:, c * LANES:(c + 1) * LANES]
            st["mrun"] = t if st["mrun"] is None else jnp.maximum(st["mrun"], t)

    def value_tile(u, st, kt):
        j, e = units[u]
        cols = slice(kt * kt_w, (kt + 1) * kt_w)
        p = jnp.exp(s_scr[u % 2, :, cols] - st["m"])
        for c in range(kt_w // LANES):
            t = p[:, c * LANES:(c + 1) * LANES]
            st["lrun"] = t if st["lrun"] is None else st["lrun"] + t
        pv = _dot(p.astype(BF16), vs[j, e][cols, :])
        st["acc"] = pv if st["acc"] is None else st["acc"] + pv

    outs = {}
    cur = None
    for u in range(len(units) + 1):
        nxt = dict(mrun=None) if u < len(units) else None
        for kt in range(nkt):
            if nxt is not None:
                score_tile(u, nxt, kt)
            if cur is not None:
                value_tile(u - 1, cur, kt)
        if cur is not None:
            den = jnp.sum(cur["lrun"], axis=1, keepdims=True) + jnp.exp(cur["sink"] - cur["m"])
            outs[units[u - 1]] = cur["acc"] / den
        if nxt is not None:
            j, e = units[u]
            sink = jnp.where(upper_rows, sink_ref[4 * j + 2 + e], sink_ref[4 * j + e])
            nxt.update(sink=sink, m=jnp.maximum(jnp.max(nxt["mrun"], axis=1, keepdims=True), sink),
                       lrun=None, acc=None)
        cur = nxt
    for pr in range(A_HEADS // 2):
        j, half = pr // 2, pr % 2
        r = slice(half * m, (half + 1) * m)
        attn = jnp.where(lo, outs[j, 0][r], outs[j, 1][r])
        gate = g_ref[:, pr * LANES:(pr + 1) * LANES].astype(F32)
        o_ref[:, pr * LANES:(pr + 1) * LANES] = (attn * _silu(gate)).astype(o_ref.dtype)


def _attn_ctx_kernel(sink_ref, q_ref, k_ref, v_ref, g_ref, o_ref, s_scr):
    ks, vs = _kv_variants(k_ref[...], v_ref[...])
    _gqa_heads(q_ref, g_ref, o_ref, sink_ref, ks, vs, lambda kt: None, s_scr)


def _attn_ctx(sink, q, k, v, g, seq_len):
    t = q.shape[0]
    row = lambda b: (b, 0)
    return pl.pallas_call(
        _attn_ctx_kernel,
        grid=(t // seq_len,),
        in_specs=[pl.BlockSpec(memory_space=pltpu.SMEM),
                  pl.BlockSpec((seq_len, A_WIDTH), row),
                  pl.BlockSpec((seq_len, A_KV_WIDTH), row),
                  pl.BlockSpec((seq_len, A_KV_WIDTH), row),
                  pl.BlockSpec((seq_len, A_WIDTH), row)],
        out_specs=pl.BlockSpec((seq_len, A_WIDTH), row),
        out_shape=jax.ShapeDtypeStruct((t, A_WIDTH), BF16),
        scratch_shapes=[pltpu.VMEM((2, 2 * seq_len, seq_len), F32)],
        compiler_params=_cparams("arbitrary"),
        name="swa_context_attention",
    )(sink, q, k, v, g)


BAND_Q = 256


def _attn_band_kernel(sink_ref, q_ref, kp_ref, kc_ref, kn_ref, vp_ref, vc_ref, vn_ref, kx_ref, vx_ref, g_ref,
                      o_ref, s_scr, *, nsteps):
    n = pl.program_id(1)
    tq = q_ref.shape[0]
    blk = WINDOW_BLK
    k = jnp.concatenate([kp_ref[...], kc_ref[...], kn_ref[...], kx_ref[...]], axis=0)
    v = jnp.concatenate([vp_ref[...], vc_ref[...], vn_ref[...], vx_ref[...]], axis=0)
    nloc = tq + 2 * blk
    qi = _row_iota((2 * tq, GQA_KEY_TILE)) & (tq - 1)

    def valid_tile(kt):
        if kt * GQA_KEY_TILE >= nloc:
            return None
        col = _lane_iota((2 * tq, GQA_KEY_TILE)) + kt * GQA_KEY_TILE
        rel = col - blk - qi
        return ((rel >= -blk) & (rel <= blk) & ((col >= blk) | (n > 0)) & ((col < nloc - blk) | (n < nsteps - 1)))

    assert nloc % GQA_KEY_TILE == 0
    ks, vs = _kv_variants(k, v)
    _gqa_heads(q_ref, g_ref, o_ref, sink_ref, ks, vs, valid_tile, s_scr)


def _attn_band(sink, q, k, v, k_ctx, v_ctx, g, nb, seq_len):
    t = q.shape[0]
    blk = WINDOW_BLK
    tq = BAND_Q
    assert tq & (tq - 1) == 0 and tq % blk == 0 and seq_len % tq == 0
    per = tq // blk
    nblk = seq_len // blk
    nsteps = seq_len // tq
    nctx = k_ctx.shape[1]
    cur = lambda b, n: (b * nsteps + n, 0)
    prv = lambda b, n: (b * nblk + jnp.maximum(n * per - 1, 0), 0)
    nxt = lambda b, n: (b * nblk + jnp.minimum((n + 1) * per, nblk - 1), 0)
    ctx = lambda b, n: (b, 0, 0)
    edge = lambda f: pl.BlockSpec((blk, A_KV_WIDTH), f)
    mid = pl.BlockSpec((tq, A_KV_WIDTH), cur)
    return pl.pallas_call(
        functools.partial(_attn_band_kernel, nsteps=nsteps),
        grid=(nb, nsteps),
        in_specs=[pl.BlockSpec(memory_space=pltpu.SMEM),
                  pl.BlockSpec((tq, A_WIDTH), cur),
                  edge(prv), mid, edge(nxt),
                  edge(prv), mid, edge(nxt),
                  pl.BlockSpec((None, nctx, A_KV_WIDTH), ctx),
                  pl.BlockSpec((None, nctx, A_KV_WIDTH), ctx),
                  pl.BlockSpec((tq, A_WIDTH), cur)],
        out_specs=pl.BlockSpec((tq, A_WIDTH), cur),
        out_shape=jax.ShapeDtypeStruct((t, A_WIDTH), BF16),
        scratch_shapes=[pltpu.VMEM((2, 2 * tq, tq + 2 * blk + nctx), F32)],
        compiler_params=_cparams("arbitrary", "arbitrary"),
        name="swa_banded_attention",
    )(sink, q, k, k, k, v, v, v, k_ctx, v_ctx, g)


def _ab_out_kernel(a_ref, yf_ref, yb_ref, z_ref, gnw_ref, w_ref, x_ref, gate_ref, o_ref):
    for rows in _row_splits(x_ref.shape[0]):
        y = (yf_ref[rows, :].astype(F32) + yb_ref[rows, :].astype(F32)) * _silu(z_ref[rows, :].astype(F32))
        ms = jnp.mean(y * y, axis=-1, keepdims=True)
        s = ((y * lax.rsqrt(ms + EPS)) * gnw_ref[...]).astype(BF16)
        out = _dot(a_ref[rows, :], w_ref[0:A_WIDTH, :]) + _dot(s, w_ref[A_WIDTH:, :])
        o_ref[rows, :] = x_ref[rows, :] + gate_ref[...] * out


def _ab_out(a, yf, yb, z, gnw, w, x, gate, seq_len, tm):
    t, d = x.shape
    per_seq = seq_len // tm if gate.shape[0] > 1 else None
    row = lambda i: (i, 0)
    const = lambda i: (0, 0)
    mod = (lambda i: (i // per_seq, 0, 0)) if per_seq else (lambda i: (0, 0, 0))
    return pl.pallas_call(
        _ab_out_kernel,
        grid=(t // tm,),
        in_specs=[pl.BlockSpec((tm, A_WIDTH), row),
                  pl.BlockSpec((tm, SSD_INNER), row),
                  pl.BlockSpec((tm, SSD_INNER), row),
                  pl.BlockSpec((tm, SSD_INNER), row),
                  pl.BlockSpec((1, SSD_INNER), const),
                  pl.BlockSpec(w.shape, const),
                  pl.BlockSpec((tm, d), row),
                  pl.BlockSpec((None, 1, d), mod)],
        out_specs=pl.BlockSpec((tm, d), row),
        out_shape=jax.ShapeDtypeStruct((t, d), F32),
        compiler_params=_cparams("arbitrary"),
        name="ab_out_proj",
    )(a, yf, yb, z, gnw, w, x, gate)


MLA_COLS = (Q_LORA, KV_LORA, LANES, MLA_WIDTH)


def _mla_in_kernel(*refs, rope):
    if rope:
        (x_ref, nw_ref, sc_ref, sh_ref, w_ref, qnw_ref, kvnw_ref, wuq_ref, cos_ref, sin_ref,
         qn_ref, qpe_ref, ckv_ref, kpe_ref, g_ref, ckv32_ref, kpe32_ref) = refs
    else:
        (x_ref, nw_ref, sc_ref, sh_ref, w_ref, qnw_ref, kvnw_ref, wuq_ref,
         qn_ref, qpe_ref, ckv_ref, kpe_ref, g_ref, ckv32_ref, kpe32_ref) = refs
    offs = [int(o) for o in np.concatenate([[0], np.cumsum(MLA_COLS)])]
    nope_w = MLA_HEADS * MLA_NOPE

    def rms(u, w):
        return (u * lax.rsqrt(jnp.mean(u * u, axis=-1, keepdims=True) + EPS)) * w

    for rows in _row_splits(x_ref.shape[0]):
        h = _norm_mod(x_ref[rows, :], nw_ref[...], sc_ref[...], sh_ref[...]).astype(BF16)
        cq = _dot(h, w_ref[:, offs[0]:offs[1]])
        ckv = _dot(h, w_ref[:, offs[1]:offs[2]])
        kpe = _dot(h, w_ref[:, offs[2]:offs[3]])
        g_ref[rows, :] = _silu(_dot(h, w_ref[:, offs[3]:offs[4]])).astype(g_ref.dtype)
        cqn = rms(cq, qnw_ref[...]).astype(BF16)
        qn_ref[rows, :] = _dot(cqn, wuq_ref[:, 0:nope_w]).astype(qn_ref.dtype)
        qpe = _dot(cqn, wuq_ref[:, nope_w:])
        if rope:
            qpe = _rope(qpe, cos_ref[rows, :], sin_ref[rows, :], MLA_ROPE // 4)
            kpe = _rope(kpe, cos_ref[rows, :], sin_ref[rows, :], MLA_ROPE // 4)
        qpe_ref[rows, :] = qpe.astype(qpe_ref.dtype)
        ckvn = rms(ckv, kvnw_ref[...])
        ckv_ref[rows, :] = ckvn.astype(ckv_ref.dtype)
        kpe_ref[rows, :] = kpe.astype(kpe_ref.dtype)
        ckv32_ref[rows, :] = ckvn
        kpe32_ref[rows, :] = kpe


def _mla_in(x, nw, scale, shift, w, qnw, kvnw, wuq, rope_tabs, seq_len, tm):
    t, d = x.shape
    per_seq = seq_len // tm if scale.shape[0] > 1 else None
    row = lambda i: (i, 0)
    const = lambda i: (0, 0)
    mod = (lambda i: (i // per_seq, 0, 0)) if per_seq else (lambda i: (0, 0, 0))
    in_specs = [pl.BlockSpec((tm, d), row), pl.BlockSpec((1, d), const),
                pl.BlockSpec((None, 1, d), mod), pl.BlockSpec((None, 1, d), mod),
                pl.BlockSpec(w.shape, const), pl.BlockSpec((1, Q_LORA), const),
                pl.BlockSpec((1, KV_LORA), const), pl.BlockSpec(wuq.shape, const)]
    args = [x, nw, scale, shift, w, qnw, kvnw, wuq]
    if rope_tabs is not None:
        nt = seq_len // tm
        pos = lambda i: (i % nt, 0)
        in_specs += [pl.BlockSpec((tm, LANES), pos), pl.BlockSpec((tm, LANES), pos)]
        args += list(rope_tabs)
    widths = (MLA_HEADS * MLA_NOPE, MLA_HEADS * MLA_ROPE, KV_LORA, LANES, MLA_WIDTH, KV_LORA, LANES)
    dts = (BF16, BF16, BF16, BF16, BF16, F32, F32)
    return pl.pallas_call(
        functools.partial(_mla_in_kernel, rope=rope_tabs is not None),
        grid=(t // tm,),
        in_specs=in_specs,
        out_specs=[pl.BlockSpec((tm, c), row) for c in widths],
        out_shape=[jax.ShapeDtypeStruct((t, c), dt) for c, dt in zip(widths, dts)],
        compiler_params=_cparams("arbitrary"),
        name="mla_in_proj",
    )(*args)


MLA_UNIT_ROWS = 512
MLA_KEY_TILE = 256


def _mla_attn_kernel(qn_ref, qpe_ref, ckv_ref, kpe_ref, wukt_ref, wuv_ref, g_ref, o_ref, kcat_scr, v_scr, s_scr, *,
                     pairs_per_step):
    pg = pl.program_id(1)
    qb = pl.program_id(2)
    tq = qn_ref.shape[0]
    c_exp = MLA_SCALE * LOG2E

    @pl.when(qb == 0)
    def _expand():
        ckv = ckv_ref[...]
        eye = jnp.where(_row_iota((LANES, LANES)) == _lane_iota((LANES, LANES)), 1.0, 0.0).astype(BF16)
        kpe_t = _dot_nt(eye, kpe_ref[...]).astype(BF16)
        for i in range(pairs_per_step):
            kcat_scr[i, 0:LANES, :] = _dot_nt(wukt_ref[i * LANES:(i + 1) * LANES, :], ckv).astype(BF16)
            kcat_scr[i, LANES:2 * LANES, :] = kpe_t
            v_scr[i] = _dot(ckv, wuv_ref[:, i * LANES:(i + 1) * LANES]).astype(BF16)

    nkeys = kcat_scr.shape[-1]
    kt_w = min(MLA_KEY_TILE, nkeys)
    nkt = nkeys // kt_w
    ru = min(MLA_UNIT_ROWS, tq)
    lane = _lane_iota((ru, LANES))
    units = [(i, r, e) for i in range(pairs_per_step) for r in range(tq // ru) for e in range(2)]

    def lane_tiles(x):
        return [x[:, j * LANES:(j + 1) * LANES] for j in range(x.shape[1] // LANES)]

    def q_ext(u):
        i, r, e = units[u]
        rows = slice(r * ru, (r + 1) * ru)
        qn = qn_ref[rows, i * LANES:(i + 1) * LANES]
        qpe = qpe_ref[rows, (i // 2) * LANES:(i // 2 + 1) * LANES]
        slot = 2 * (i % 2) + e if pairs_per_step % 2 == 0 else 2 * ((pg * pairs_per_step + i) % 2) + e
        nope_half = (lane < MLA_NOPE) if e == 0 else (lane >= MLA_NOPE)
        return jnp.concatenate([jnp.where(nope_half, qn, jnp.zeros_like(qn)),
                                jnp.where((lane >> 5) == slot, qpe, jnp.zeros_like(qpe))], axis=1)

    def score_tile(u, st, kt):
        cols = slice(kt * kt_w, (kt + 1) * kt_w)
        s = _dot(st["q"], kcat_scr[units[u][0], :, cols])
        s_scr[u % 2, :, cols] = s
        for t in lane_tiles(s):
            st["mrun"] = t if st["mrun"] is None else jnp.maximum(st["mrun"], t)

    def value_tile(u, st, kt):
        cols = slice(kt * kt_w, (kt + 1) * kt_w)
        p = jnp.exp2((s_scr[u % 2, :, cols] - st["m"]) * c_exp)
        for t in lane_tiles(p):
            st["lrun"] = t if st["lrun"] is None else st["lrun"] + t
        pv = _dot(p.astype(BF16), v_scr[units[u][0], cols, :])
        st["acc"] = pv if st["acc"] is None else st["acc"] + pv

    def finish(u, st, done):
        i, r, e = units[u]
        den = jnp.sum(st["lrun"], axis=1, keepdims=True)
        done[e] = st["acc"] / den
        if e == 1:
            rows = slice(r * ru, (r + 1) * ru)
            attn = jnp.where(lane < MLA_V, done[0], done[1])
            gate = g_ref[rows, i * LANES:(i + 1) * LANES].astype(F32)
            o_ref[rows, i * LANES:(i + 1) * LANES] = (attn * gate).astype(o_ref.dtype)

    done = {}
    cur = None
    for u in range(len(units) + 1):
        nxt = dict(q=q_ext(u), mrun=None) if u < len(units) else None
        for kt in range(nkt):
            if nxt is not None:
                score_tile(u, nxt, kt)
            if cur is not None:
                value_tile(u - 1, cur, kt)
        if cur is not None:
            finish(u - 1, cur, done)
        if nxt is not None:
            nxt.update(m=jnp.max(nxt["mrun"], axis=1, keepdims=True), lrun=None, acc=None)
        cur = nxt


def _mla_attn(qn, qpe, ckv_keys, kpe_keys, wukt, wuv, g, nb, seq_len, tq, pairs_per_step):
    t = qn.shape[0]
    nkeys = ckv_keys.shape[1]
    npairs = MLA_HEADS // 2
    ngrp = npairs // pairs_per_step
    nq = seq_len // tq
    wp = pairs_per_step * LANES
    if pairs_per_step % 2 == 0:
        wpe = wp // 2
        pe_map = lambda b, p, i: (b * nq + i, p)
    else:
        wpe = LANES
        pe_map = lambda b, p, i: (b * nq + i, p // 2)
    qmap = lambda b, p, i: (b * nq + i, p)
    return pl.pallas_call(
        functools.partial(_mla_attn_kernel, pairs_per_step=pairs_per_step),
        grid=(nb, ngrp, nq),
        in_specs=[pl.BlockSpec((tq, wp), qmap),
                  pl.BlockSpec((tq, wpe), pe_map),
                  pl.BlockSpec((None, nkeys, KV_LORA), lambda b, p, i: (b, 0, 0)),
                  pl.BlockSpec((None, nkeys, LANES), lambda b, p, i: (b, 0, 0)),
                  pl.BlockSpec((wp, KV_LORA), lambda b, p, i: (p, 0)),
                  pl.BlockSpec((KV_LORA, wp), lambda b, p, i: (0, p)),
                  pl.BlockSpec((tq, wp), qmap)],
        out_specs=pl.BlockSpec((tq, wp), qmap),
        out_shape=jax.ShapeDtypeStruct((t, MLA_WIDTH), BF16),
        scratch_shapes=[pltpu.VMEM((pairs_per_step, 2 * LANES, nkeys), BF16),
                        pltpu.VMEM((pairs_per_step, nkeys, LANES), BF16),
                        pltpu.VMEM((2, min(MLA_UNIT_ROWS, tq), nkeys), F32)],
        compiler_params=_cparams("arbitrary", "arbitrary", "arbitrary"),
        name="mla_attention",
    )(qn, qpe, ckv_keys, kpe_keys, wukt, wuv, g)


def _mla_attn_seq_kernel(qn_ref, qpe_ref, ckv_ref, kpe_ref, wukt_ref, wuv_ref, g_ref, o_ref,
                         kcat_scr, v_scr, s_scr, m_scr, half_scr):
    pair = pl.program_id(1)
    seq = qn_ref.shape[0]
    nkeys = kcat_scr.shape[-1]
    ru, kt_w = MLA_UNIT_ROWS, MLA_KEY_TILE
    nkt = nkeys // kt_w
    nrb = seq // ru
    c_exp = MLA_SCALE * LOG2E
    lane = _lane_iota((ru, LANES))

    ckv = ckv_ref[...]
    eye = jnp.where(_row_iota((LANES, LANES)) == _lane_iota((LANES, LANES)), 1.0, 0.0).astype(BF16)
    kcat_scr[0:LANES, :] = _dot_nt(wukt_ref[...], ckv).astype(BF16)
    kcat_scr[LANES:2 * LANES, :] = _dot_nt(eye, kpe_ref[...]).astype(BF16)
    v_scr[:, 0:LANES] = _dot(ckv, wuv_ref[...]).astype(BF16)
    v_scr[:, LANES:2 * LANES] = jnp.ones((nkeys, LANES), BF16)

    def rows_of(rb):
        return slice(rb * ru, (rb + 1) * ru) if isinstance(rb, int) else pl.ds(pl.multiple_of(rb * ru, ru), ru)

    def q_ext(rb, e):
        rows = rows_of(rb)
        qn = qn_ref[rows, :]
        qpe = qpe_ref[rows, :]
        nope_half = (lane < MLA_NOPE) if e == 0 else (lane >= MLA_NOPE)
        slot = 2 * (pair % 2) + e
        return jnp.concatenate([jnp.where(nope_half, qn, jnp.zeros_like(qn)),
                                jnp.where((lane >> 5) == slot, qpe, jnp.zeros_like(qpe))], axis=1)

    def phase(score, value):
        q = q_ext(*score) if score is not None else None
        mrun = acc = None
        for kt in range(nkt):
            cols = slice(kt * kt_w, (kt + 1) * kt_w)
            if score is not None:
                s = _dot(q, kcat_scr[:, cols])
                s_scr[score[1], :, cols] = s
                for j in range(kt_w // LANES):
                    t = s[:, j * LANES:(j + 1) * LANES]
                    mrun = t if mrun is None else jnp.maximum(mrun, t)
            if value is not None:
                mb = m_scr[value]
                p = jnp.concatenate(
                    [jnp.exp2((s_scr[value, :, kt * kt_w + j * LANES:kt * kt_w + (j + 1) * LANES] - mb) * c_exp)
                     for j in range(kt_w // LANES)], axis=1)
                pv = _dot(p.astype(BF16), v_scr[cols, :])
                acc = pv if acc is None else acc + pv
        if score is not None:
            m_scr[score[1]] = jnp.broadcast_to(jnp.max(mrun, axis=1, keepdims=True), (ru, LANES))
        return None if acc is None else acc[:, 0:LANES] / acc[:, LANES:2 * LANES]

    phase((0, 0), None)

    def row_block(rb, carry):
        half_scr[...] = phase((rb, 1), 0)
        out1 = phase((jnp.minimum(rb + 1, nrb - 1), 0), 1)
        rows = rows_of(rb)
        attn = jnp.where(lane < MLA_V, half_scr[...], out1)
        o_ref[rows, :] = (attn * g_ref[rows, :].astype(F32)).astype(o_ref.dtype)
        return carry

    lax.fori_loop(0, nrb, row_block, 0)


def _mla_attn_seq(qn, qpe, ckv_keys, kpe_keys, wukt, wuv, g, nb, seq_len):
    t = qn.shape[0]
    nkeys = ckv_keys.shape[1]
    assert seq_len % MLA_UNIT_ROWS == 0 and nkeys % MLA_KEY_TILE == 0
    qmap = lambda b, p: (b, p)
    return pl.pallas_call(
        _mla_attn_seq_kernel,
        grid=(nb, MLA_HEADS // 2),
        in_specs=[pl.BlockSpec((seq_len, LANES), qmap),
                  pl.BlockSpec((seq_len, LANES), lambda b, p: (b, p // 2)),
                  pl.BlockSpec((None, nkeys, KV_LORA), lambda b, p: (b, 0, 0)),
                  pl.BlockSpec((None, nkeys, LANES), lambda b, p: (b, 0, 0)),
                  pl.BlockSpec((LANES, KV_LORA), lambda b, p: (p, 0)),
                  pl.BlockSpec((KV_LORA, LANES), lambda b, p: (0, p)),
                  pl.BlockSpec((seq_len, LANES), qmap)],
        out_specs=pl.BlockSpec((seq_len, LANES), qmap),
        out_shape=jax.ShapeDtypeStruct((t, MLA_WIDTH), BF16),
        scratch_shapes=[pltpu.VMEM((2 * LANES, nkeys), BF16),
                        pltpu.VMEM((nkeys, 2 * LANES), BF16),
                        pltpu.VMEM((2, MLA_UNIT_ROWS, nkeys), F32),
                        pltpu.VMEM((2, MLA_UNIT_ROWS, LANES), F32),
                        pltpu.VMEM((MLA_UNIT_ROWS, LANES), F32)],
        compiler_params=_cparams("arbitrary", "arbitrary"),
        name="mla_attention_seq",
    )(qn, qpe, ckv_keys, kpe_keys, wukt, wuv, g)


def _mla_out_kernel(a_ref, w_ref, x_ref, gate_ref, fw_ref, o_ref):
    for rows in _row_splits(x_ref.shape[0]):
        xn = x_ref[rows, :] + gate_ref[...] * _dot(a_ref[rows, :], w_ref[...])
        ms = jnp.mean(xn * xn, axis=-1, keepdims=True)
        o_ref[rows, :] = (xn * lax.rsqrt(ms + EPS)) * fw_ref[...]


def _mla_out(a, w, x, gate, fw, seq_len, tm):
    t, d = x.shape
    per_seq = seq_len // tm if gate.shape[0] > 1 else None
    row = lambda i: (i, 0)
    const = lambda i: (0, 0)
    mod = (lambda i: (i // per_seq, 0, 0)) if per_seq else (lambda i: (0, 0, 0))
    return pl.pallas_call(
        _mla_out_kernel,
        grid=(t // tm,),
        in_specs=[pl.BlockSpec((tm, MLA_WIDTH), row), pl.BlockSpec(w.shape, const),
                  pl.BlockSpec((tm, d), row), pl.BlockSpec((None, 1, d), mod), pl.BlockSpec((1, d), const)],
        out_specs=pl.BlockSpec((tm, d), row),
        out_shape=jax.ShapeDtypeStruct((t, d), F32),
        compiler_params=_cparams("arbitrary"),
        name="mla_out_proj_final_norm",
    )(a, w, x, gate, fw)


def _rope_tables(length, dim):
    rows = length // GRID_W
    f32 = np.float32
    row = np.repeat(np.arange(rows), GRID_W).astype(f32)
    col = np.tile(np.arange(GRID_W), rows).astype(f32)
    nf = dim // 4
    inv = (f32(1.0) / np.power(f32(ROPE_BASE), np.arange(nf, dtype=f32) / f32(nf))).astype(f32)
    ar = row[:, None] * inv[None, :]
    ac = col[:, None] * inv[None, :]
    ang = np.concatenate([ar, ar, ac, ac], axis=-1).astype(f32)
    sign = np.tile(np.concatenate([-np.ones((nf,), f32), np.ones((nf,), f32)]), 2)
    reps = LANES // dim
    return (jnp.asarray(np.tile(np.cos(ang).astype(f32), (1, reps))),
            jnp.asarray(np.tile((np.sin(ang) * sign).astype(f32), (1, reps))))


PROJ_TM = 1024
CONV_TM = 512


def _group(x, seq_len, conds_rows, mods, wts, tabs_a, tabs_c, ctx_cache):
    nb, _, d = x.shape
    t = nb * seq_len
    latent = ctx_cache is not None
    xf = x.reshape(t, d)
    tm = PROJ_TM

    def mod_rows(layer):
        m = mods[layer][conds_rows]
        sh, sc, gt = jnp.split(m[:, None, :], 3, axis=-1)
        return sh, sc, gt

    sh, sc, gt = mod_rows(0)
    q, k, v, g, z, xbc, dt, *kv_t = _ab_in(xf, wts["norm_w"][0:1], sc, sh, wts["ab_w_in"], wts["ab_w_dt"],
                                           tabs_a if latent else None, seq_len, tm)
    xbc_c = _conv(xbc, wts["ab_conv_w"], wts["ab_conv_b"], seq_len, min(CONV_TM, seq_len))
    if latent:
        k_ctx, v_ctx, s_f0, s_b0, _, _ = ctx_cache
        attn = _attn_band(wts["ab_sink"], q, k, v, k_ctx, v_ctx, g, nb, seq_len)
    else:
        s_f0 = s_b0 = None
        attn = _attn_ctx(wts["ab_sink"], q, k, v, g, seq_len)
    yf, yb, s_f, s_b = _ssd(xbc_c, dt, wts["ssd_par"], wts["ssd_dskip"], s_f0, s_b0, nb, seq_len)
    x1 = _ab_out(attn, yf, yb, z, wts["ab_gnorm_w"], wts["ab_w_out"], xf, gt, seq_len, tm)

    sh, sc, gt = mod_rows(1)
    qn, qpe, ckv, kpe, g1, ckv32, kpe32 = _mla_in(
        x1, wts["norm_w"][1:2], sc, sh, wts["mla_w_in"], wts["mla_q_norm_w"], wts["mla_kv_norm_w"],
        wts["mla_w_uq"], tabs_c if latent else None, seq_len, tm)
    ckv_keys = ckv.reshape(nb, seq_len, KV_LORA)
    kpe_keys = kpe.reshape(nb, seq_len, LANES)
    if latent:
        ckv_x, kpe_x = ctx_cache[4], ctx_cache[5]
        ckv_keys = jnp.concatenate([ckv_keys, ckv_x], axis=1)
        kpe_keys = jnp.concatenate([kpe_keys, kpe_x], axis=1)
        attn1 = _mla_attn_seq(qn, qpe, ckv_keys, kpe_keys, wts["mla_w_ukt"], wts["mla_w_uv"], g1, nb, seq_len)
    else:
        attn1 = _mla_attn(qn, qpe, ckv_keys, kpe_keys, wts["mla_w_ukt"], wts["mla_w_uv"], g1, nb, seq_len,
                          seq_len, MLA_HEADS // 2)
    y = _mla_out(attn1, wts["mla_w_out"], x1, gt, wts["final_norm_w"], seq_len, tm)
    return y.reshape(nb, seq_len, d), (kv_t, s_f, s_b, ckv32, kpe32)


def kernel(x_prompt, x_sample, cache_a_k, cache_a_v, state_ssd_fwd, state_ssd_bwd, cache_mla_ckv, cache_mla_kpe,
           c, c_ctx, ada_w, ada_b, norm_w, ab_w_in, ab_sink, ab_conv_w, ab_conv_b, ab_dt_bias, ab_a_log,
           ab_d_skip, ab_gnorm_w, ab_w_out, mla_w_in, mla_q_norm_w, mla_kv_norm_w, mla_w_uq, mla_w_ukv,
           mla_w_out, final_norm_w):
    batch, seq, d = x_prompt.shape
    dec_batch, dec_seq, _ = x_sample.shape
    assert ada_w.shape[0] == 2 and ab_w_in.shape[0] == 1 and mla_w_in.shape[0] == 1
    assert dec_batch + 1 <= SUBLANES

    conds = jnp.concatenate([c_ctx[None, :], c, jnp.zeros((SUBLANES - 1 - dec_batch, d), F32)], axis=0)
    mods = _modulation(conds, ada_w, ada_b)

    n_main = sum(AB_COLS[:-1])
    w_ab = ab_w_in[0].astype(BF16)
    w_dt = jnp.concatenate([ab_w_in[0][:, n_main:], jnp.zeros((d, LANES - 2 * SSD_HEADS), F32)], axis=1).astype(BF16)
    mw = mla_w_in[0]
    o_kpe = Q_LORA + KV_LORA
    w_mla = jnp.concatenate([mw[:, :o_kpe], jnp.tile(mw[:, o_kpe:o_kpe + MLA_ROPE], (1, LANES // MLA_ROPE)),
                             mw[:, o_kpe + MLA_ROPE:]], axis=1).astype(BF16)
    wuq = mla_w_uq[0].reshape(Q_LORA, MLA_HEADS, MLA_NOPE + MLA_ROPE)
    wuq = jnp.concatenate([wuq[:, :, :MLA_NOPE].reshape(Q_LORA, -1), wuq[:, :, MLA_NOPE:].reshape(Q_LORA, -1)],
                          axis=1).astype(BF16)
    wukv = mla_w_ukv[0].reshape(KV_LORA, MLA_HEADS, MLA_NOPE + MLA_V)
    wukt = wukv[:, :, :MLA_NOPE].reshape(KV_LORA, -1).T.astype(BF16)
    wuv = wukv[:, :, MLA_NOPE:].reshape(KV_LORA, -1).astype(BF16)
    pad_lanes = lambda r: jnp.concatenate([r.reshape(1, -1), jnp.zeros((1, LANES - r.size), F32)], axis=1)
    ssd_par = jnp.concatenate([pad_lanes(ab_dt_bias[0]), pad_lanes(ab_a_log[0]),
                               jnp.zeros((SUBLANES - 2, LANES), F32)], axis=0)
    wts = dict(
        norm_w=norm_w, ab_w_in=w_ab, ab_w_dt=w_dt, ab_sink=ab_sink[0],
        ab_conv_w=jnp.concatenate([ab_conv_w[0], jnp.zeros((SUBLANES - CONV_K, CONV_CH), F32)], axis=0),
        ab_conv_b=ab_conv_b[0][None, :], ssd_par=ssd_par,
        ssd_dskip=jnp.repeat(ab_d_skip[0], SSD_HEAD_DIM)[None, :],
        ab_gnorm_w=ab_gnorm_w[0][None, :], ab_w_out=ab_w_out[0].astype(BF16),
        mla_w_in=w_mla, mla_q_norm_w=mla_q_norm_w[0][None, :], mla_kv_norm_w=mla_kv_norm_w[0][None, :],
        mla_w_uq=wuq, mla_w_ukt=wukt, mla_w_uv=wuv, mla_w_out=mla_w_out[0].astype(BF16),
        final_norm_w=final_norm_w[None, :],
    )
    tabs_a = _rope_tables(dec_seq, A_HEAD_DIM)
    tabs_c = _rope_tables(dec_seq, MLA_ROPE)

    y_prompt, ((k_t, v_t), s_f, s_b, ckv32, kpe32) = _group(
        x_prompt, seq, jnp.zeros((1,), jnp.int32), mods, wts, tabs_a, tabs_c, None)

    past = cache_a_k.shape[2]
    ctx_cache = (cache_a_k[:, 0].reshape(dec_batch, past, A_KV_WIDTH),
                 cache_a_v[:, 0].reshape(dec_batch, past, A_KV_WIDTH),
                 state_ssd_fwd[:, 0], state_ssd_bwd[:, 0],
                 cache_mla_ckv[:, 0].astype(BF16),
                 jnp.tile(cache_mla_kpe[:, 0], (1, 1, LANES // MLA_ROPE)).astype(BF16))
    y_sample, _ = _group(x_sample, dec_seq, 1 + jnp.arange(dec_batch), mods, wts, tabs_a, tabs_c, ctx_cache)

    cache_layout = lambda u: u.reshape(batch, 1, A_KV_HEADS, A_HEAD_DIM, seq).transpose(0, 1, 4, 2, 3)
    return (y_prompt, y_sample,
            cache_layout(k_t), cache_layout(v_t),
            s_f[:, None], s_b[:, None],
            ckv32.reshape(batch, 1, seq, KV_LORA), kpe32[:, :MLA_ROPE].reshape(batch, 1, seq, MLA_ROPE))
```

```python
import functools
import math

import jax
import jax.numpy as jnp
import numpy as np
from jax import lax
from jax.experimental import pallas as pl
from jax.experimental.pallas import tpu as pltpu

F32 = jnp.float32
BF16 = jnp.bfloat16

LANES = 128
SUBLANES = 8
VMEM_LIMIT_BYTES = 56 * 1024 * 1024

GRID_W = 64
ROPE_BASE = 10000.0
EPS = 1e-6
NEG_INF = -1e30
WINDOW_BLK = 128
CHUNK = 128
A_HEADS, A_KV_HEADS, A_HEAD_DIM = 8, 2, 64
A_WIDTH = A_HEADS * A_HEAD_DIM
A_KV_WIDTH = A_KV_HEADS * A_HEAD_DIM
A_SCALE = A_HEAD_DIM ** -0.5
SSD_HEADS, SSD_HEAD_DIM, SSD_GROUPS, SSD_STATE = 16, 64, 2, 64
SSD_INNER = SSD_HEADS * SSD_HEAD_DIM
CONV_K = 5
BC_WIDTH = SSD_GROUPS * SSD_STATE
CONV_CH = SSD_INNER + 2 * BC_WIDTH
MLA_HEADS, MLA_NOPE, MLA_ROPE, MLA_V = 16, 64, 32, 64
Q_LORA, KV_LORA = 256, 128
MLA_WIDTH = MLA_HEADS * MLA_V
MLA_SCALE = (MLA_NOPE + MLA_ROPE) ** -0.5
LOG2E = 1.4426950408889634


def _cparams(*sem):
    return pltpu.CompilerParams(dimension_semantics=sem, vmem_limit_bytes=VMEM_LIMIT_BYTES)


def _silu(x):
    return x * (1.0 / (1.0 + jnp.exp(-x)))


def _dot(a, b):
    return jnp.dot(a, b, preferred_element_type=F32, precision=lax.Precision.DEFAULT)


def _dot_nt(a, b):
    return lax.dot_general(a, b, (((1,), (1,)), ((), ())), preferred_element_type=F32,
                           precision=lax.Precision.DEFAULT)


def _row_splits(m, parts=2):
    step = m // parts
    return [slice(r * step, (r + 1) * step) for r in range(parts)]


def _lane_iota(shape):
    return lax.broadcasted_iota(jnp.int32, shape, len(shape) - 1)


def _row_iota(shape):
    return lax.broadcasted_iota(jnp.int32, shape, len(shape) - 2)


def _mod_kernel(cond_ref, w_ref, b_ref, o_ref):
    s = _silu(cond_ref[...])
    o_ref[...] = _dot(s.astype(BF16), w_ref[...].astype(BF16)) + b_ref[...]


def _modulation(conds, ada_w, ada_b):
    depth, d, d3 = ada_w.shape
    tn = 768
    return pl.pallas_call(
        _mod_kernel,
        grid=(depth, d3 // tn),
        in_specs=[pl.BlockSpec((SUBLANES, d), lambda l, j: (0, 0)),
                  pl.BlockSpec((None, d, tn), lambda l, j: (l, 0, j)),
                  pl.BlockSpec((None, 1, tn), lambda l, j: (l, 0, j))],
        out_specs=pl.BlockSpec((None, SUBLANES, tn), lambda l, j: (l, 0, j)),
        out_shape=jax.ShapeDtypeStruct((depth, SUBLANES, d3), F32),
        compiler_params=_cparams("arbitrary", "arbitrary"),
        name="modulation",
    )(conds, ada_w, ada_b.reshape(depth, 1, d3))


def _norm_mod(x, nw, scale, shift):
    ms = jnp.mean(x * x, axis=-1, keepdims=True)
    y = (x * lax.rsqrt(ms + EPS)) * nw
    return y * (1.0 + scale) + shift


def _rope(x, cos, sin_signed, half_period):
    outs = []
    first = (_lane_iota((x.shape[0], LANES)) & (2 * half_period - 1)) < half_period
    for j in range(x.shape[1] // LANES):
        xj = x[:, j * LANES:(j + 1) * LANES]
        up = pltpu.roll(xj, LANES - half_period, 1)
        dn = pltpu.roll(xj, half_period, 1)
        outs.append(xj * cos + jnp.where(first, up, dn) * sin_signed)
    return outs[0] if len(outs) == 1 else jnp.concatenate(outs, axis=1)


AB_COLS = (A_WIDTH, A_KV_WIDTH, A_KV_WIDTH, A_WIDTH, SSD_INNER, CONV_CH, LANES)


def _ab_in_kernel(*refs, rope, seq_len):
    if rope:
        (x_ref, nw_ref, sc_ref, sh_ref, w_ref, wdt_ref, cos_ref, sin_ref,
         q_ref, k_ref, v_ref, g_ref, z_ref, xbc_ref, dt_ref) = refs
        kvt_refs = {}
    else:
        (x_ref, nw_ref, sc_ref, sh_ref, w_ref, wdt_ref,
         q_ref, k_ref, v_ref, g_ref, z_ref, xbc_ref, dt_ref, kt_ref, vt_ref) = refs
        kvt_refs = {1: kt_ref, 2: vt_ref}
    offs = np.concatenate([[0], np.cumsum(AB_COLS)])
    outs = (q_ref, k_ref, v_ref, g_ref, z_ref, xbc_ref, dt_ref)
    for rows in _row_splits(x_ref.shape[0]):
        h = _norm_mod(x_ref[rows, :], nw_ref[...], sc_ref[...], sh_ref[...]).astype(BF16)
        for i, o_ref in enumerate(outs):
            last = i == len(outs) - 1
            y = _dot(h, wdt_ref[...] if last else w_ref[:, int(offs[i]):int(offs[i + 1])])
            if rope and i in (0, 1):
                y = _rope(y, cos_ref[rows, :], sin_ref[rows, :], A_HEAD_DIM // 4)
            o_ref[rows, :] = y.astype(o_ref.dtype)
            if i in kvt_refs:
                for s in range(rows.start // seq_len, rows.stop // seq_len):
                    kvt_refs[i][s] = y[s * seq_len - rows.start:(s + 1) * seq_len - rows.start, :].T


def _ab_in(x, nw, scale, shift, w, wdt, rope_tabs, seq_len, tm):
    t, d = x.shape
    per_seq = seq_len // tm if scale.shape[0] > 1 else None
    row = lambda i: (i, 0)
    mod = (lambda i: (i // per_seq, 0, 0)) if per_seq else (lambda i: (0, 0, 0))
    const = lambda i: (0, 0)
    in_specs = [pl.BlockSpec((tm, d), row),
                pl.BlockSpec((1, d), const),
                pl.BlockSpec((None, 1, d), mod),
                pl.BlockSpec((None, 1, d), mod),
                pl.BlockSpec(w.shape, const),
                pl.BlockSpec(wdt.shape, const)]
    args = [x, nw, scale, shift, w, wdt]
    if rope_tabs is not None:
        nt = seq_len // tm
        pos = lambda i: (i % nt, 0)
        in_specs += [pl.BlockSpec((tm, LANES), pos), pl.BlockSpec((tm, LANES), pos)]
        args += list(rope_tabs)
    dts = (BF16, F32, F32, BF16, BF16, BF16, F32)
    out_specs = [pl.BlockSpec((tm, c), row) for c in AB_COLS]
    out_shape = [jax.ShapeDtypeStruct((t, c), dt) for c, dt in zip(AB_COLS, dts)]
    if rope_tabs is None:
        assert (tm // 2) % seq_len == 0
        spt = tm // seq_len
        out_specs += [pl.BlockSpec((spt, A_KV_WIDTH, seq_len), lambda i: (i, 0, 0))] * 2
        out_shape += [jax.ShapeDtypeStruct((t // seq_len, A_KV_WIDTH, seq_len), F32)] * 2
    return pl.pallas_call(
        functools.partial(_ab_in_kernel, rope=rope_tabs is not None, seq_len=seq_len),
        grid=(t // tm,),
        in_specs=in_specs,
        out_specs=out_specs,
        out_shape=out_shape,
        compiler_params=_cparams("arbitrary"),
        name="ab_in_proj",
    )(*args)


def _conv_kernel(prev_ref, main_ref, next_ref, w_ref, b_ref, o_ref, *, tiles_per_seq):
    i = pl.program_id(0) % tiles_per_seq
    tm = main_ref.shape[0]
    prev = jnp.where(i > 0, prev_ref[...].astype(F32), 0.0)
    nxt = jnp.where(i < tiles_per_seq - 1, next_ref[...].astype(F32), 0.0)
    ext = jnp.concatenate([prev, main_ref[...].astype(F32), nxt], axis=0)
    first = SUBLANES - CONV_K // 2
    acc = b_ref[...] + ext[first:first + tm] * w_ref[0:1, :]
    for j in range(1, CONV_K):
        acc = acc + ext[first + j:first + j + tm] * w_ref[j:j + 1, :]
    o_ref[...] = _silu(acc).astype(o_ref.dtype)


def _conv(xbc, w, b, seq_len, tm):
    t, c = xbc.shape
    tps = seq_len // tm
    hb = tm // SUBLANES
    nblk8 = t // SUBLANES
    return pl.pallas_call(
        functools.partial(_conv_kernel, tiles_per_seq=tps),
        grid=(t // tm,),
        in_specs=[pl.BlockSpec((SUBLANES, c), lambda i: (jnp.maximum(i * hb - 1, 0), 0)),
                  pl.BlockSpec((tm, c), lambda i: (i, 0)),
                  pl.BlockSpec((SUBLANES, c), lambda i: (jnp.minimum((i + 1) * hb, nblk8 - 1), 0)),
                  pl.BlockSpec((SUBLANES, c), lambda i: (0, 0)),
                  pl.BlockSpec((1, c), lambda i: (0, 0))],
        out_specs=pl.BlockSpec((tm, c), lambda i: (i, 0)),
        out_shape=jax.ShapeDtypeStruct((t, c), BF16),
        compiler_params=_cparams("arbitrary"),
        name="ssd_conv",
    )(xbc, xbc, xbc, w, b)


def _ssd_init(h0_ref, h_scr, has_h0):
    if not has_h0:
        h_scr[...] = jnp.zeros_like(h_scr)
        return
    for pr in range(SSD_HEADS // 2):
        both = jnp.concatenate([h0_ref[2 * pr], h0_ref[2 * pr + 1]], axis=0)
        both = jnp.concatenate([both, jnp.zeros_like(both)], axis=1)
        st = both.T[0:SSD_STATE]
        lo_st = _lane_iota(st.shape) < SSD_HEAD_DIM
        h_scr[2 * pr] = jnp.where(lo_st, st, 0.0)
        h_scr[2 * pr + 1] = jnp.where(lo_st, 0.0, st)


def _ssd_final(h_scr, hout_ref):
    for pr in range(SSD_HEADS // 2):
        st = h_scr[2 * pr] + h_scr[2 * pr + 1]
        st = jnp.concatenate([st, jnp.zeros_like(st)], axis=0).T
        hout_ref[2 * pr] = st[0:SSD_HEAD_DIM, 0:SSD_STATE]
        hout_ref[2 * pr + 1] = st[SSD_HEAD_DIM:2 * SSD_HEAD_DIM, 0:SSD_STATE]


def _ssd_prelude(b_ref, c_ref, dt_ref, par_ref, direction):
    lane = _lane_iota((CHUNK, LANES))
    rowi = _row_iota((CHUNK, LANES))
    dtp_in = dt_ref[...] + par_ref[0:1, :]
    dtp = jnp.maximum(dtp_in, 0.0) + jnp.log(1.0 + jnp.exp(-jnp.abs(dtp_in)))
    la = dtp * (-jnp.exp(par_ref[1:2, :]))
    causal = (rowi >= lane) if direction == 0 else (rowi <= lane)
    tri = jnp.where(causal, 1.0, 0.0).astype(BF16)
    p1 = la.astype(BF16)
    r1 = la - p1.astype(F32)
    p2 = r1.astype(BF16)
    p3 = (r1 - p2.astype(F32)).astype(BF16)
    cum = _dot(tri, p1) + _dot(tri, p2) + _dot(tri, p3)
    cum_t = cum.T
    dtp_t = dtp.T
    end = CHUNK - 1 if direction == 0 else 0
    w_t = jnp.exp(cum_t[:, end:end + 1] - cum_t) * dtp_t
    lrow_t = cum_t - jnp.log(dtp_t)
    bm = b_ref[...]
    cm = c_ref[...]
    cb, cmask = [], []
    for g in range(SSD_GROUPS):
        in_g = (lane >= g * SSD_STATE) & (lane < (g + 1) * SSD_STATE)
        cg = jnp.where(in_g, cm, jnp.zeros_like(cm))
        cmask.append(cg.astype(F32))
        cb.append(_dot_nt(cg, bm))
    return dict(cum=cum, lrow_t=lrow_t, w_t=w_t, causal=causal, end=end,
                bm_t=bm.astype(F32).T, cb=cb, cmask=cmask, lo_half=lane < SSD_HEAD_DIM)


def _ssd_head_pair(pre, pr, x_ref, dskip_ref, y_ref, h_scr, direction, accumulate=False):
    hp = SSD_HEADS // SSD_GROUPS
    cum, lrow_t, w_t, end = pre["cum"], pre["lrow_t"], pre["w_t"], pre["end"]
    lo_half = pre["lo_half"]
    xp = x_ref[:, pr * LANES:(pr + 1) * LANES]
    y_pair = None
    for e in range(2):
        h = 2 * pr + e
        g = h // hp
        col = direction * SSD_HEADS + h
        xe = jnp.where(lo_half if e == 0 else ~lo_half, xp, jnp.zeros_like(xp))
        cumcol = jnp.broadcast_to(cum[:, col:col + 1], (CHUNK, LANES))
        diff = cumcol - lrow_t[col:col + 1, :]
        sc = pre["cb"][g] * jnp.exp(jnp.where(pre["causal"], diff, NEG_INF))
        ce = pre["cmask"][g] * jnp.exp(cumcol)
        hs = h_scr[h]
        hsb = hs.astype(BF16)
        zpad = jnp.zeros_like(hsb)
        bw = (pre["bm_t"][g * SSD_STATE:(g + 1) * SSD_STATE, :] * w_t[col:col + 1, :]).astype(BF16)
        a = jnp.concatenate([jnp.concatenate([sc.astype(BF16), ce.astype(BF16)], axis=1),
                             jnp.concatenate([bw, jnp.zeros_like(bw)], axis=1)], axis=0)
        wmat = jnp.concatenate([xe, hsb, zpad] if g == 0 else [xe, zpad, hsb], axis=0)
        res = _dot(a, wmat)
        ye = res[0:CHUNK]
        y_pair = ye if y_pair is None else y_pair + ye
        cdec = jnp.exp(cumcol[end:end + 1, :])
        h_scr[h] = hs * cdec + res[CHUNK:CHUNK + SSD_STATE]
    if direction == 0:
        y_pair = y_pair + dskip_ref[:, pr * LANES:(pr + 1) * LANES] * xp.astype(F32)
    if accumulate:
        y_pair = y_pair + y_ref[:, pr * LANES:(pr + 1) * LANES]
    y_ref[:, pr * LANES:(pr + 1) * LANES] = y_pair.astype(y_ref.dtype)


def _ssd_kernel(xf_ref, bf_ref, cf_ref, dtf_ref, xb_ref, bb_ref, cb_ref, dtb_ref, par_ref, dskip_ref,
                h0f_ref, h0b_ref, yf_ref, yb_ref, houtf_ref, houtb_ref, hf_scr, hb_scr, *, nc, has_h0):
    ci = pl.program_id(1)

    @pl.when(ci == 0)
    def _init():
        _ssd_init(h0f_ref, hf_scr, has_h0)
        _ssd_init(h0b_ref, hb_scr, has_h0)

    pre_f = _ssd_prelude(bf_ref, cf_ref, dtf_ref, par_ref, 0)
    pre_b = _ssd_prelude(bb_ref, cb_ref, dtb_ref, par_ref, 1)
    for pr in range(SSD_HEADS // 2):
        _ssd_head_pair(pre_f, pr, xf_ref, dskip_ref, yf_ref, hf_scr, 0)
        _ssd_head_pair(pre_b, pr, xb_ref, dskip_ref, yb_ref, hb_scr, 1)

    @pl.when(ci == nc - 1)
    def _final():
        _ssd_final(hf_scr, houtf_ref)
        _ssd_final(hb_scr, houtb_ref)


def _ssd(xbc, dt, par, dskip, h0f, h0b, nb, seq_len):
    t = xbc.shape[0]
    nc = seq_len // CHUNK
    fwd = lambda b, c: b * nc + c
    bwd = lambda b, c: b * nc + nc - 1 - c
    xcols = SSD_INNER // LANES
    has_h0 = h0f is not None
    if not has_h0:
        h0f = h0b = jnp.zeros((1, SSD_HEADS, SSD_HEAD_DIM, SSD_STATE), F32)
    h0_map = (lambda b, c: (b, 0, 0, 0)) if has_h0 else (lambda b, c: (0, 0, 0, 0))
    st_shape = (None, SSD_HEADS, SSD_HEAD_DIM, SSD_STATE)
    carry_shape = (None, SSD_HEADS, SSD_STATE, LANES)
    per_b = lambda b, c: (b, 0, 0, 0)

    def chunk_specs(cidx):
        return [pl.BlockSpec((CHUNK, SSD_INNER), lambda b, c: (cidx(b, c), 0)),
                pl.BlockSpec((CHUNK, LANES), lambda b, c: (cidx(b, c), xcols)),
                pl.BlockSpec((CHUNK, LANES), lambda b, c: (cidx(b, c), xcols + 1)),
                pl.BlockSpec((CHUNK, LANES), lambda b, c: (cidx(b, c), 0))]

    outs = pl.pallas_call(
        functools.partial(_ssd_kernel, nc=nc, has_h0=has_h0),
        grid=(nb, nc),
        in_specs=chunk_specs(fwd) + chunk_specs(bwd) + [
            pl.BlockSpec((SUBLANES, LANES), lambda b, c: (0, 0)),
            pl.BlockSpec((1, SSD_INNER), lambda b, c: (0, 0)),
            pl.BlockSpec(st_shape, h0_map), pl.BlockSpec(st_shape, h0_map)],
        out_specs=[pl.BlockSpec((CHUNK, SSD_INNER), lambda b, c: (fwd(b, c), 0)),
                   pl.BlockSpec((CHUNK, SSD_INNER), lambda b, c: (bwd(b, c), 0)),
                   pl.BlockSpec(st_shape, per_b), pl.BlockSpec(st_shape, per_b),
                   pl.BlockSpec(carry_shape, per_b), pl.BlockSpec(carry_shape, per_b)],
        out_shape=[jax.ShapeDtypeStruct((t, SSD_INNER), BF16), jax.ShapeDtypeStruct((t, SSD_INNER), BF16),
                   jax.ShapeDtypeStruct((nb, SSD_HEADS, SSD_HEAD_DIM, SSD_STATE), F32),
                   jax.ShapeDtypeStruct((nb, SSD_HEADS, SSD_HEAD_DIM, SSD_STATE), F32),
                   jax.ShapeDtypeStruct((nb, SSD_HEADS, SSD_STATE, LANES), F32),
                   jax.ShapeDtypeStruct((nb, SSD_HEADS, SSD_STATE, LANES), F32)],
        compiler_params=_cparams("arbitrary", "arbitrary"),
        name="ssd_scan_bidir",
    )(xbc, xbc, xbc, dt, xbc, xbc, xbc, dt, par, dskip, h0f, h0b)
    return outs[:4]


def _ssd_seq_kernel(x_ref, b_ref, c_ref, dt_ref, par_ref, dskip_ref, y_ref, houtf_ref, houtb_ref,
                    hf_scr, hb_scr, yacc_scr, *, nc):
    _ssd_init(None, hf_scr, False)
    _ssd_init(None, hb_scr, False)
    for c in range(nc):
        rf = pl.ds(c * CHUNK, CHUNK)
        rb = pl.ds((nc - 1 - c) * CHUNK, CHUNK)
        pre_f = _ssd_prelude(b_ref.at[rf], c_ref.at[rf], dt_ref.at[rf], par_ref, 0)
        pre_b = _ssd_prelude(b_ref.at[rb], c_ref.at[rb], dt_ref.at[rb], par_ref, 1)
        for pr in range(SSD_HEADS // 2):
            _ssd_head_pair(pre_f, pr, x_ref.at[rf], dskip_ref, yacc_scr.at[rf], hf_scr, 0, accumulate=c > nc - 1 - c)
            _ssd_head_pair(pre_b, pr, x_ref.at[rb], dskip_ref, yacc_scr.at[rb], hb_scr, 1, accumulate=c >= nc - 1 - c)
    y_ref[...] = yacc_scr[...].astype(y_ref.dtype)
    _ssd_final(hf_scr, houtf_ref)
    _ssd_final(hb_scr, houtb_ref)


def _ssd_seq(xbc, dt, par, dskip, nb, seq_len):
    t = xbc.shape[0]
    nc = seq_len // CHUNK
    xcols = SSD_INNER // LANES
    st_shape = (None, SSD_HEADS, SSD_HEAD_DIM, SSD_STATE)
    carry_shape = (None, SSD_HEADS, SSD_STATE, LANES)
    per_b = lambda b: (b, 0, 0, 0)
    one_block = lambda b: (0, 0, 0, 0)
    return pl.pallas_call(
        functools.partial(_ssd_seq_kernel, nc=nc),
        grid=(nb,),
        in_specs=[pl.BlockSpec((seq_len, SSD_INNER), lambda b: (b, 0)),
                  pl.BlockSpec((seq_len, LANES), lambda b: (b, xcols)),
                  pl.BlockSpec((seq_len, LANES), lambda b: (b, xcols + 1)),
                  pl.BlockSpec((seq_len, LANES), lambda b: (b, 0)),
                  pl.BlockSpec((SUBLANES, LANES), lambda b: (0, 0)),
                  pl.BlockSpec((1, SSD_INNER), lambda b: (0, 0))],
        out_specs=[pl.BlockSpec((seq_len, SSD_INNER), lambda b: (b, 0)),
                   pl.BlockSpec(st_shape, per_b), pl.BlockSpec(st_shape, per_b),
                   pl.BlockSpec(carry_shape, one_block), pl.BlockSpec(carry_shape, one_block)],
        out_shape=[jax.ShapeDtypeStruct((t, SSD_INNER), BF16),
                   jax.ShapeDtypeStruct((nb, SSD_HEADS, SSD_HEAD_DIM, SSD_STATE), F32),
                   jax.ShapeDtypeStruct((nb, SSD_HEADS, SSD_HEAD_DIM, SSD_STATE), F32),
                   jax.ShapeDtypeStruct((1, SSD_HEADS, SSD_STATE, LANES), F32),
                   jax.ShapeDtypeStruct((1, SSD_HEADS, SSD_STATE, LANES), F32)],
        scratch_shapes=[pltpu.VMEM((seq_len, SSD_INNER), F32)],
        compiler_params=_cparams("arbitrary"),
        name="ssd_scan_seq",
    )(xbc, xbc, xbc, dt, par, dskip)[:3]


def _kv_variants(k, v):
    lane = _lane_iota(k.shape)
    lo = lane < A_HEAD_DIM
    k_sw = pltpu.roll(k, A_HEAD_DIM, 1)
    v_sw = pltpu.roll(v, A_HEAD_DIM, 1)
    ks, vs = {}, {}
    for j in range(A_KV_HEADS):
        for e in range(2):
            src_k, src_v = (k, v) if j == e else (k_sw, v_sw)
            half = lo if e == 0 else ~lo
            ks[j, e] = jnp.where(half, src_k, 0.0).astype(BF16)
            vs[j, e] = src_v.astype(BF16)
    return ks, vs


GQA_KEY_TILE = 256


def _gqa_heads(q_ref, g_ref, o_ref, sink_ref, ks, vs, valid_tile, s_scr):
    m = q_ref.shape[0]
    assert A_HEADS // A_KV_HEADS == 4
    lo = _lane_iota((m, LANES)) < A_HEAD_DIM
    upper_rows = _row_iota((2 * m, 1)) >= m
    scale = jnp.asarray(A_SCALE, BF16)
    units = [(j, e) for j in range(A_KV_HEADS) for e in range(2)]
    nkeys = ks[0, 0].shape[0]
    kt_w = GQA_KEY_TILE
    nkt = nkeys // kt_w
    rows_of = {}

    def q_rows(j):
        if j not in rows_of:
            rows_of[j] = jnp.concatenate([q_ref[:, 2 * j * LANES:(2 * j + 1) * LANES],
                                          q_ref[:, (2 * j + 1) * LANES:(2 * j + 2) * LANES]], axis=0) * scale
        return rows_of[j]

    def score_tile(u, st, kt):
        j, e = units[u]
        cols = slice(kt * kt_w, (kt + 1) * kt_w)
        s = _dot_nt(q_rows(j), ks[j, e][cols, :])
        ok = valid_tile(kt)
        if ok is not None:
            s = jnp.where(ok, s, NEG_INF)
        s_scr[u % 2, :, cols] = s
        for c in range(kt_w // LANES):
            t = s[:, c * LANES:(c + 1) * LANES]
            st["mrun"] = t if st["mrun"] is None else jnp.maximum(st["mrun"], t)

    def value_tile(u, st, kt):
        j, e = units[u]
        cols = slice(kt * kt_w, (kt + 1) * kt_w)
        p = jnp.exp(s_scr[u % 2, :, cols] - st["m"])
        for c in range(kt_w // LANES):
            t = p[:, c * LANES:(c + 1) * LANES]
            st["lrun"] = t if st["lrun"] is None else st["lrun"] + t
        pv = _dot(p.astype(BF16), vs[j, e][cols, :])
        st["acc"] = pv if st["acc"] is None else st["acc"] + pv

    outs = {}
    cur = None
    for u in range(len(units) + 1):
        nxt = dict(mrun=None) if u < len(units) else None
        for kt in range(nkt):
            if nxt is not None:
                score_tile(u, nxt, kt)
            if cur is not None:
                value_tile(u - 1, cur, kt)
        if cur is not None:
            den = jnp.sum(cur["lrun"], axis=1, keepdims=True) + jnp.exp(cur["sink"] - cur["m"])
            outs[units[u - 1]] = cur["acc"] / den
        if nxt is not None:
            j, e = units[u]
            sink = jnp.where(upper_rows, sink_ref[4 * j + 2 + e], sink_ref[4 * j + e])
            nxt.update(sink=sink, m=jnp.maximum(jnp.max(nxt["mrun"], axis=1, keepdims=True), sink),
                       lrun=None, acc=None)
        cur = nxt
    for pr in range(A_HEADS // 2):
        j, half = pr // 2, pr % 2
        r = slice(half * m, (half + 1) * m)
        attn = jnp.where(lo, outs[j, 0][r], outs[j, 1][r])
        gate = g_ref[:, pr * LANES:(pr + 1) * LANES].astype(F32)
        o_ref[:, pr * LANES:(pr + 1) * LANES] = (attn * _silu(gate)).astype(o_ref.dtype)


def _attn_ctx_kernel(sink_ref, q_ref, k_ref, v_ref, g_ref, o_ref, s_scr):
    ks, vs = _kv_variants(k_ref[...], v_ref[...])
    _gqa_heads(q_ref, g_ref, o_ref, sink_ref, ks, vs, lambda kt: None, s_scr)


def _attn_ctx(sink, q, k, v, g, seq_len):
    t = q.shape[0]
    row = lambda b: (b, 0)
    return pl.pallas_call(
        _attn_ctx_kernel,
        grid=(t // seq_len,),
        in_specs=[pl.BlockSpec(memory_space=pltpu.SMEM),
                  pl.BlockSpec((seq_len, A_WIDTH), row),
                  pl.BlockSpec((seq_len, A_KV_WIDTH), row),
                  pl.BlockSpec((seq_len, A_KV_WIDTH), row),
                  pl.BlockSpec((seq_len, A_WIDTH), row)],
        out_specs=pl.BlockSpec((seq_len, A_WIDTH), row),
        out_shape=jax.ShapeDtypeStruct((t, A_WIDTH), BF16),
        scratch_shapes=[pltpu.VMEM((2, 2 * seq_len, seq_len), F32)],
        compiler_params=_cparams("arbitrary"),
        name="swa_context_attention",
    )(sink, q, k, v, g)


BAND_Q = 256


def _attn_band_kernel(sink_ref, q_ref, kp_ref, kc_ref, kn_ref, vp_ref, vc_ref, vn_ref, kx_ref, vx_ref, g_ref,
                      o_ref, s_scr, *, nsteps):
    n = pl.program_id(1)
    tq = q_ref.shape[0]
    blk = WINDOW_BLK
    k = jnp.concatenate([kp_ref[...], kc_ref[...], kn_ref[...], kx_ref[...]], axis=0)
    v = jnp.concatenate([vp_ref[...], vc_ref[...], vn_ref[...], vx_ref[...]], axis=0)
    nloc = tq + 2 * blk
    qi = _row_iota((2 * tq, GQA_KEY_TILE)) & (tq - 1)

    def valid_tile(kt):
        if kt * GQA_KEY_TILE >= nloc:
            return None
        col = _lane_iota((2 * tq, GQA_KEY_TILE)) + kt * GQA_KEY_TILE
        rel = col - blk - qi
        return ((rel >= -blk) & (rel <= blk) & ((col >= blk) | (n > 0)) & ((col < nloc - blk) | (n < nsteps - 1)))

    assert nloc % GQA_KEY_TILE == 0
    ks, vs = _kv_variants(k, v)
    _gqa_heads(q_ref, g_ref, o_ref, sink_ref, ks, vs, valid_tile, s_scr)


def _attn_band(sink, q, k, v, k_ctx, v_ctx, g, nb, seq_len):
    t = q.shape[0]
    blk = WINDOW_BLK
    tq = BAND_Q
    assert tq & (tq - 1) == 0 and tq % blk == 0 and seq_len % tq == 0
    per = tq // blk
    nblk = seq_len // blk
    nsteps = seq_len // tq
    nctx = k_ctx.shape[1]
    cur = lambda b, n: (b * nsteps + n, 0)
    prv = lambda b, n: (b * nblk + jnp.maximum(n * per - 1, 0), 0)
    nxt = lambda b, n: (b * nblk + jnp.minimum((n + 1) * per, nblk - 1), 0)
    ctx = lambda b, n: (b, 0, 0)
    edge = lambda f: pl.BlockSpec((blk, A_KV_WIDTH), f)
    mid = pl.BlockSpec((tq, A_KV_WIDTH), cur)
    return pl.pallas_call(
        functools.partial(_attn_band_kernel, nsteps=nsteps),
        grid=(nb, nsteps),
        in_specs=[pl.BlockSpec(memory_space=pltpu.SMEM),
                  pl.BlockSpec((tq, A_WIDTH), cur),
                  edge(prv), mid, edge(nxt),
                  edge(prv), mid, edge(nxt),
                  pl.BlockSpec((None, nctx, A_KV_WIDTH), ctx),
                  pl.BlockSpec((None, nctx, A_KV_WIDTH), ctx),
                  pl.BlockSpec((tq, A_WIDTH), cur)],
        out_specs=pl.BlockSpec((tq, A_WIDTH), cur),
        out_shape=jax.ShapeDtypeStruct((t, A_WIDTH), BF16),
        scratch_shapes=[pltpu.VMEM((2, 2 * tq, tq + 2 * blk + nctx), F32)],
        compiler_params=_cparams("arbitrary", "arbitrary"),
        name="swa_banded_attention",
    )(sink, q, k, k, k, v, v, v, k_ctx, v_ctx, g)


def _ab_out_kernel(*refs, two_y):
    if two_y:
        a_ref, yf_ref, yb_ref, z_ref, gnw_ref, w_ref, x_ref, gate_ref, o_ref = refs
    else:
        a_ref, yf_ref, z_ref, gnw_ref, w_ref, x_ref, gate_ref, o_ref = refs
    for rows in _row_splits(x_ref.shape[0]):
        y = yf_ref[rows, :].astype(F32)
        if two_y:
            y = y + yb_ref[rows, :].astype(F32)
        y = y * _silu(z_ref[rows, :].astype(F32))
        ms = jnp.mean(y * y, axis=-1, keepdims=True)
        s = ((y * lax.rsqrt(ms + EPS)) * gnw_ref[...]).astype(BF16)
        out = _dot(a_ref[rows, :], w_ref[0:A_WIDTH, :]) + _dot(s, w_ref[A_WIDTH:, :])
        o_ref[rows, :] = x_ref[rows, :] + gate_ref[...] * out


def _ab_out(a, yf, yb, z, gnw, w, x, gate, seq_len, tm):
    t, d = x.shape
    per_seq = seq_len // tm if gate.shape[0] > 1 else None
    row = lambda i: (i, 0)
    const = lambda i: (0, 0)
    mod = (lambda i: (i // per_seq, 0, 0)) if per_seq else (lambda i: (0, 0, 0))
    ys = [yf] if yb is None else [yf, yb]
    return pl.pallas_call(
        functools.partial(_ab_out_kernel, two_y=yb is not None),
        grid=(t // tm,),
        in_specs=[pl.BlockSpec((tm, A_WIDTH), row)] + [pl.BlockSpec((tm, SSD_INNER), row)] * (len(ys) + 1) + [
                  pl.BlockSpec((1, SSD_INNER), const),
                  pl.BlockSpec(w.shape, const),
                  pl.BlockSpec((tm, d), row),
                  pl.BlockSpec((None, 1, d), mod)],
        out_specs=pl.BlockSpec((tm, d), row),
        out_shape=jax.ShapeDtypeStruct((t, d), F32),
        compiler_params=_cparams("arbitrary"),
        name="ab_out_proj",
    )(a, *ys, z, gnw, w, x, gate)


MLA_COLS = (Q_LORA, KV_LORA, LANES, MLA_WIDTH)


def _mla_in_kernel(*refs, rope):
    if rope:
        (x_ref, nw_ref, sc_ref, sh_ref, w_ref, qnw_ref, kvnw_ref, wuq_ref, cos_ref, sin_ref,
         qn_ref, qpe_ref, ckv_ref, kpe_ref, g_ref, ckv32_ref, kpe32_ref) = refs
    else:
        (x_ref, nw_ref, sc_ref, sh_ref, w_ref, qnw_ref, kvnw_ref, wuq_ref,
         qn_ref, qpe_ref, ckv_ref, kpe_ref, g_ref, ckv32_ref, kpe32_ref) = refs
    offs = [int(o) for o in np.concatenate([[0], np.cumsum(MLA_COLS)])]
    nope_w = MLA_HEADS * MLA_NOPE

    def rms(u, w):
        return (u * lax.rsqrt(jnp.mean(u * u, axis=-1, keepdims=True) + EPS)) * w

    for rows in _row_splits(x_ref.shape[0]):
        h = _norm_mod(x_ref[rows, :], nw_ref[...], sc_ref[...], sh_ref[...]).astype(BF16)
        cq = _dot(h, w_ref[:, offs[0]:offs[1]])
        ckv = _dot(h, w_ref[:, offs[1]:offs[2]])
        kpe = _dot(h, w_ref[:, offs[2]:offs[3]])
        g_ref[rows, :] = _dot(h, w_ref[:, offs[3]:offs[4]]).astype(g_ref.dtype)
        cqn = rms(cq, qnw_ref[...]).astype(BF16)
        qn_ref[rows, :] = _dot(cqn, wuq_ref[:, 0:nope_w]).astype(qn_ref.dtype)
        qpe = _dot(cqn, wuq_ref[:, nope_w:])
        if rope:
            qpe = _rope(qpe, cos_ref[rows, :], sin_ref[rows, :], MLA_ROPE // 4)
            kpe = _rope(kpe, cos_ref[rows, :], sin_ref[rows, :], MLA_ROPE // 4)
        qpe_ref[rows, :] = qpe.astype(qpe_ref.dtype)
        ckvn = rms(ckv, kvnw_ref[...])
        ckv_ref[rows, :] = ckvn.astype(ckv_ref.dtype)
        kpe_ref[rows, :] = kpe.astype(kpe_ref.dtype)
        ckv32_ref[rows, :] = ckvn
        kpe32_ref[rows, :] = kpe


def _mla_in(x, nw, scale, shift, w, qnw, kvnw, wuq, rope_tabs, seq_len, tm):
    t, d = x.shape
    per_seq = seq_len // tm if scale.shape[0] > 1 else None
    row = lambda i: (i, 0)
    const = lambda i: (0, 0)
    mod = (lambda i: (i // per_seq, 0, 0)) if per_seq else (lambda i: (0, 0, 0))
    in_specs = [pl.BlockSpec((tm, d), row), pl.BlockSpec((1, d), const),
                pl.BlockSpec((None, 1, d), mod), pl.BlockSpec((None, 1, d), mod),
                pl.BlockSpec(w.shape, const), pl.BlockSpec((1, Q_LORA), const),
                pl.BlockSpec((1, KV_LORA), const), pl.BlockSpec(wuq.shape, const)]
    args = [x, nw, scale, shift, w, qnw, kvnw, wuq]
    if rope_tabs is not None:
        nt = seq_len // tm
        pos = lambda i: (i % nt, 0)
        in_specs += [pl.BlockSpec((tm, LANES), pos), pl.BlockSpec((tm, LANES), pos)]
        args += list(rope_tabs)
    widths = (MLA_HEADS * MLA_NOPE, MLA_HEADS * MLA_ROPE, KV_LORA, LANES, MLA_WIDTH, KV_LORA, LANES)
    dts = (BF16, BF16, BF16, BF16, BF16, F32, F32)
    return pl.pallas_call(
        functools.partial(_mla_in_kernel, rope=rope_tabs is not None),
        grid=(t // tm,),
        in_specs=in_specs,
        out_specs=[pl.BlockSpec((tm, c), row) for c in widths],
        out_shape=[jax.ShapeDtypeStruct((t, c), dt) for c, dt in zip(widths, dts)],
        compiler_params=_cparams("arbitrary"),
        name="mla_in_proj",
    )(*args)


MLA_UNIT_ROWS = 512
MLA_KEY_TILE = 256


def _mla_attn_kernel(qn_ref, qpe_ref, ckv_ref, kpe_ref, wukt_ref, wuv_ref, g_ref, o_ref, kcat_scr, v_scr, s_scr, *,
                     pairs_per_step):
    pg = pl.program_id(1)
    qb = pl.program_id(2)
    tq = qn_ref.shape[0]
    c_exp = MLA_SCALE * LOG2E

    @pl.when(qb == 0)
    def _expand():
        ckv = ckv_ref[...]
        eye = jnp.where(_row_iota((LANES, LANES)) == _lane_iota((LANES, LANES)), 1.0, 0.0).astype(BF16)
        kpe_t = _dot_nt(eye, kpe_ref[...]).astype(BF16)
        for i in range(pairs_per_step):
            kcat_scr[i, 0:LANES, :] = _dot_nt(wukt_ref[i * LANES:(i + 1) * LANES, :], ckv).astype(BF16)
            kcat_scr[i, LANES:2 * LANES, :] = kpe_t
            v_scr[i] = _dot(ckv, wuv_ref[:, i * LANES:(i + 1) * LANES]).astype(BF16)

    nkeys = kcat_scr.shape[-1]
    kt_w = min(MLA_KEY_TILE, nkeys)
    nkt = nkeys // kt_w
    ru = min(MLA_UNIT_ROWS, tq)
    lane = _lane_iota((ru, LANES))
    units = [(i, r, e) for i in range(pairs_per_step) for r in range(tq // ru) for e in range(2)]

    def lane_tiles(x):
        return [x[:, j * LANES:(j + 1) * LANES] for j in range(x.shape[1] // LANES)]

    def q_ext(u):
        i, r, e = units[u]
        rows = slice(r * ru, (r + 1) * ru)
        qn = qn_ref[rows, i * LANES:(i + 1) * LANES]
        qpe = qpe_ref[rows, (i // 2) * LANES:(i // 2 + 1) * LANES]
        slot = 2 * (i % 2) + e if pairs_per_step % 2 == 0 else 2 * ((pg * pairs_per_step + i) % 2) + e
        nope_half = (lane < MLA_NOPE) if e == 0 else (lane >= MLA_NOPE)
        return jnp.concatenate([jnp.where(nope_half, qn, jnp.zeros_like(qn)),
                                jnp.where((lane >> 5) == slot, qpe, jnp.zeros_like(qpe))], axis=1)

    def score_tile(u, st, kt):
        cols = slice(kt * kt_w, (kt + 1) * kt_w)
        s = _dot(st["q"], kcat_scr[units[u][0], :, cols])
        s_scr[u % 2, :, cols] = s
        for t in lane_tiles(s):
            st["mrun"] = t if st["mrun"] is None else jnp.maximum(st["mrun"], t)

    def value_tile(u, st, kt):
        cols = slice(kt * kt_w, (kt + 1) * kt_w)
        p = jnp.exp2((s_scr[u % 2, :, cols] - st["m"]) * c_exp)
        for t in lane_tiles(p):
            st["lrun"] = t if st["lrun"] is None else st["lrun"] + t
        pv = _dot(p.astype(BF16), v_scr[units[u][0], cols, :])
        st["acc"] = pv if st["acc"] is None else st["acc"] + pv

    def finish(u, st, done):
        i, r, e = units[u]
        den = jnp.sum(st["lrun"], axis=1, keepdims=True)
        done[e] = st["acc"] / den
        if e == 1:
            rows = slice(r * ru, (r + 1) * ru)
            attn = jnp.where(lane < MLA_V, done[0], done[1])
            gate = g_ref[rows, i * LANES:(i + 1) * LANES].astype(F32)
            o_ref[rows, i * LANES:(i + 1) * LANES] = (attn * _silu(gate)).astype(o_ref.dtype)

    done = {}
    cur = None
    for u in range(len(units) + 1):
        nxt = dict(q=q_ext(u), mrun=None) if u < len(units) else None
        for kt in range(nkt):
            if nxt is not None:
                score_tile(u, nxt, kt)
            if cur is not None:
                value_tile(u - 1, cur, kt)
        if cur is not None:
            finish(u - 1, cur, done)
        if nxt is not None:
            nxt.update(m=jnp.max(nxt["mrun"], axis=1, keepdims=True), lrun=None, acc=None)
        cur = nxt


def _mla_attn(qn, qpe, ckv_keys, kpe_keys, wukt, wuv, g, nb, seq_len, tq, pairs_per_step):
    t = qn.shape[0]
    nkeys = ckv_keys.shape[1]
    npairs = MLA_HEADS // 2
    ngrp = npairs // pairs_per_step
    nq = seq_len // tq
    wp = pairs_per_step * LANES
    if pairs_per_step % 2 == 0:
        wpe = wp // 2
        pe_map = lambda b, p, i: (b * nq + i, p)
    else:
        wpe = LANES
        pe_map = lambda b, p, i: (b * nq + i, p // 2)
    qmap = lambda b, p, i: (b * nq + i, p)
    return pl.pallas_call(
        functools.partial(_mla_attn_kernel, pairs_per_step=pairs_per_step),
        grid=(nb, ngrp, nq),
        in_specs=[pl.BlockSpec((tq, wp), qmap),
                  pl.BlockSpec((tq, wpe), pe_map),
                  pl.BlockSpec((None, nkeys, KV_LORA), lambda b, p, i: (b, 0, 0)),
                  pl.BlockSpec((None, nkeys, LANES), lambda b, p, i: (b, 0, 0)),
                  pl.BlockSpec((wp, KV_LORA), lambda b, p, i: (p, 0)),
                  pl.BlockSpec((KV_LORA, wp), lambda b, p, i: (0, p)),
                  pl.BlockSpec((tq, wp), qmap)],
        out_specs=pl.BlockSpec((tq, wp), qmap),
        out_shape=jax.ShapeDtypeStruct((t, MLA_WIDTH), BF16),
        scratch_shapes=[pltpu.VMEM((pairs_per_step, 2 * LANES, nkeys), BF16),
                        pltpu.VMEM((pairs_per_step, nkeys, LANES), BF16),
                        pltpu.VMEM((2, min(MLA_UNIT_ROWS, tq), nkeys), F32)],
        compiler_params=_cparams("arbitrary", "arbitrary", "arbitrary"),
        name="mla_attention",
    )(qn, qpe, ckv_keys, kpe_keys, wukt, wuv, g)


def _mla_attn_seq_kernel(qn_ref, qpe_ref, ckv_ref, kpe_ref, wukt_ref, wuv_ref, g_ref, o_ref,
                         kcat_scr, v_scr, s_scr, m_scr, half_scr):
    pair = pl.program_id(1)
    seq = qn_ref.shape[0]
    nkeys = kcat_scr.shape[-1]
    ru, kt_w = MLA_UNIT_ROWS, MLA_KEY_TILE
    nkt = nkeys // kt_w
    nrb = seq // ru
    c_exp = MLA_SCALE * LOG2E
    lane = _lane_iota((ru, LANES))

    ckv = ckv_ref[...]
    eye = jnp.where(_row_iota((LANES, LANES)) == _lane_iota((LANES, LANES)), 1.0, 0.0).astype(BF16)
    kcat_scr[0:LANES, :] = _dot_nt(wukt_ref[...], ckv).astype(BF16)
    kcat_scr[LANES:2 * LANES, :] = _dot_nt(eye, kpe_ref[...]).astype(BF16)
    v_scr[:, 0:LANES] = _dot(ckv, wuv_ref[...]).astype(BF16)
    v_scr[:, LANES:2 * LANES] = jnp.ones((nkeys, LANES), BF16)

    def rows_of(rb):
        return slice(rb * ru, (rb + 1) * ru) if isinstance(rb, int) else pl.ds(pl.multiple_of(rb * ru, ru), ru)

    def q_ext(rb, e):
        rows = rows_of(rb)
        qn = qn_ref[rows, :]
        qpe = qpe_ref[rows, :]
        nope_half = (lane < MLA_NOPE) if e == 0 else (lane >= MLA_NOPE)
        slot = 2 * (pair % 2) + e
        return jnp.concatenate([jnp.where(nope_half, qn, jnp.zeros_like(qn)),
                                jnp.where((lane >> 5) == slot, qpe, jnp.zeros_like(qpe))], axis=1)

    def phase(score, value):
        q = q_ext(*score) if score is not None else None
        mrun = acc = None
        for kt in range(nkt):
            cols = slice(kt * kt_w, (kt + 1) * kt_w)
            if score is not None:
                s = _dot(q, kcat_scr[:, cols])
                s_scr[score[1], :, cols] = s
                for j in range(kt_w // LANES):
                    t = s[:, j * LANES:(j + 1) * LANES]
                    mrun = t if mrun is None else jnp.maximum(mrun, t)
            if value is not None:
                mb = m_scr[value]
                p = jnp.concatenate(
                    [jnp.exp2((s_scr[value, :, kt * kt_w + j * LANES:kt * kt_w + (j + 1) * LANES] - mb) * c_exp)
                     for j in range(kt_w // LANES)], axis=1)
                pv = _dot(p.astype(BF16), v_scr[cols, :])
                acc = pv if acc is None else acc + pv
        if score is not None:
            m_scr[score[1]] = jnp.broadcast_to(jnp.max(mrun, axis=1, keepdims=True), (ru, LANES))
        return None if acc is None else acc[:, 0:LANES] / acc[:, LANES:2 * LANES]

    phase((0, 0), None)

    def row_block(rb, carry):
        half_scr[...] = phase((rb, 1), 0)
        out1 = phase((jnp.minimum(rb + 1, nrb - 1), 0), 1)
        rows = rows_of(rb)
        attn = jnp.where(lane < MLA_V, half_scr[...], out1)
        o_ref[rows, :] = (attn * _silu(g_ref[rows, :].astype(F32))).astype(o_ref.dtype)
        return carry

    lax.fori_loop(0, nrb, row_block, 0)


def _mla_attn_seq(qn, qpe, ckv_keys, kpe_keys, wukt, wuv, g, nb, seq_len):
    t = qn.shape[0]
    nkeys = ckv_keys.shape[1]
    assert seq_len % MLA_UNIT_ROWS == 0 and nkeys % MLA_KEY_TILE == 0
    qmap = lambda b, p: (b, p)
    return pl.pallas_call(
        _mla_attn_seq_kernel,
        grid=(nb, MLA_HEADS // 2),
        in_specs=[pl.BlockSpec((seq_len, LANES), qmap),
                  pl.BlockSpec((seq_len, LANES), lambda b, p: (b, p // 2)),
                  pl.BlockSpec((None, nkeys, KV_LORA), lambda b, p: (b, 0, 0)),
                  pl.BlockSpec((None, nkeys, LANES), lambda b, p: (b, 0, 0)),
                  pl.BlockSpec((LANES, KV_LORA), lambda b, p: (p, 0)),
                  pl.BlockSpec((KV_LORA, LANES), lambda b, p: (0, p)),
                  pl.BlockSpec((seq_len, LANES), qmap)],
        out_specs=pl.BlockSpec((seq_len, LANES), qmap),
        out_shape=jax.ShapeDtypeStruct((t, MLA_WIDTH), BF16),
        scratch_shapes=[pltpu.VMEM((2 * LANES, nkeys), BF16),
                        pltpu.VMEM((nkeys, 2 * LANES), BF16),
                        pltpu.VMEM((2, MLA_UNIT_ROWS, nkeys), F32),
                        pltpu.VMEM((2, MLA_UNIT_ROWS, LANES), F32),
                        pltpu.VMEM((MLA_UNIT_ROWS, LANES), F32)],
        compiler_params=_cparams("arbitrary", "arbitrary"),
        name="mla_attention_seq",
    )(qn, qpe, ckv_keys, kpe_keys, wukt, wuv, g)


def _mla_out_kernel(a_ref, w_ref, x_ref, gate_ref, fw_ref, o_ref):
    for rows in _row_splits(x_ref.shape[0]):
        xn = x_ref[rows, :] + gate_ref[...] * _dot(a_ref[rows, :], w_ref[...])
        ms = jnp.mean(xn * xn, axis=-1, keepdims=True)
        o_ref[rows, :] = (xn * lax.rsqrt(ms + EPS)) * fw_ref[...]


def _mla_out(a, w, x, gate, fw, seq_len, tm):
    t, d = x.shape
    per_seq = seq_len // tm if gate.shape[0] > 1 else None
    row = lambda i: (i, 0)
    const = lambda i: (0, 0)
    mod = (lambda i: (i // per_seq, 0, 0)) if per_seq else (lambda i: (0, 0, 0))
    return pl.pallas_call(
        _mla_out_kernel,
        grid=(t // tm,),
        in_specs=[pl.BlockSpec((tm, MLA_WIDTH), row), pl.BlockSpec(w.shape, const),
                  pl.BlockSpec((tm, d), row), pl.BlockSpec((None, 1, d), mod), pl.BlockSpec((1, d), const)],
        out_specs=pl.BlockSpec((tm, d), row),
        out_shape=jax.ShapeDtypeStruct((t, d), F32),
        compiler_params=_cparams("arbitrary"),
        name="mla_out_proj_final_norm",
    )(a, w, x, gate, fw)


def _rope_tables(length, dim):
    rows = length // GRID_W
    f32 = np.float32
    row = np.repeat(np.arange(rows), GRID_W).astype(f32)
    col = np.tile(np.arange(GRID_W), rows).astype(f32)
    nf = dim // 4
    inv = (f32(1.0) / np.power(f32(ROPE_BASE), np.arange(nf, dtype=f32) / f32(nf))).astype(f32)
    ar = row[:, None] * inv[None, :]
    ac = col[:, None] * inv[None, :]
    ang = np.concatenate([ar, ar, ac, ac], axis=-1).astype(f32)
    sign = np.tile(np.concatenate([-np.ones((nf,), f32), np.ones((nf,), f32)]), 2)
    reps = LANES // dim
    return (jnp.asarray(np.tile(np.cos(ang).astype(f32), (1, reps))),
            jnp.asarray(np.tile((np.sin(ang) * sign).astype(f32), (1, reps))))


PROJ_TM = 1024
CONV_TM = 512


def _group(x, seq_len, conds_rows, mods, wts, tabs_a, tabs_c, ctx_cache):
    nb, _, d = x.shape
    t = nb * seq_len
    latent = ctx_cache is not None
    xf = x.reshape(t, d)
    tm = PROJ_TM

    def mod_rows(layer):
        m = mods[layer][conds_rows]
        sh, sc, gt = jnp.split(m[:, None, :], 3, axis=-1)
        return sh, sc, gt

    sh, sc, gt = mod_rows(0)
    q, k, v, g, z, xbc, dt, *kv_t = _ab_in(xf, wts["norm_w"][0:1], sc, sh, wts["ab_w_in"], wts["ab_w_dt"],
                                           tabs_a if latent else None, seq_len, tm)
    xbc_c = _conv(xbc, wts["ab_conv_w"], wts["ab_conv_b"], seq_len, min(CONV_TM, seq_len))
    if latent:
        k_ctx, v_ctx, s_f0, s_b0, _, _ = ctx_cache
        attn = _attn_band(wts["ab_sink"], q, k, v, k_ctx, v_ctx, g, nb, seq_len)
    else:
        attn = _attn_ctx(wts["ab_sink"], q, k, v, g, seq_len)
    if latent:
        yf, yb, s_f, s_b = _ssd(xbc_c, dt, wts["ssd_par"], wts["ssd_dskip"], s_f0, s_b0, nb, seq_len)
    else:
        yf, s_f, s_b = _ssd_seq(xbc_c, dt, wts["ssd_par"], wts["ssd_dskip"], nb, seq_len)
        yb = None
    x1 = _ab_out(attn, yf, yb, z, wts["ab_gnorm_w"], wts["ab_w_out"], xf, gt, seq_len, tm)

    sh, sc, gt = mod_rows(1)
    qn, qpe, ckv, kpe, g1, ckv32, kpe32 = _mla_in(
        x1, wts["norm_w"][1:2], sc, sh, wts["mla_w_in"], wts["mla_q_norm_w"], wts["mla_kv_norm_w"],
        wts["mla_w_uq"], tabs_c if latent else None, seq_len, tm)
    ckv_keys = ckv.reshape(nb, seq_len, KV_LORA)
    kpe_keys = kpe.reshape(nb, seq_len, LANES)
    if latent:
        ckv_x, kpe_x = ctx_cache[4], ctx_cache[5]
        ckv_keys = jnp.concatenate([ckv_keys, ckv_x], axis=1)
        kpe_keys = jnp.concatenate([kpe_keys, kpe_x], axis=1)
        attn1 = _mla_attn_seq(qn, qpe, ckv_keys, kpe_keys, wts["mla_w_ukt"], wts["mla_w_uv"], g1, nb, seq_len)
    else:
        attn1 = _mla_attn(qn, qpe, ckv_keys, kpe_keys, wts["mla_w_ukt"], wts["mla_w_uv"], g1, nb, seq_len,
                          seq_len, MLA_HEADS // 2)
    y = _mla_out(attn1, wts["mla_w_out"], x1, gt, wts["final_norm_w"], seq_len, tm)
    return y.reshape(nb, seq_len, d), (kv_t, s_f, s_b, ckv32, kpe32)


def kernel(x_prompt, x_sample, cache_a_k, cache_a_v, state_ssd_fwd, state_ssd_bwd, cache_mla_ckv, cache_mla_kpe,
           c, c_ctx, ada_w, ada_b, norm_w, ab_w_in, ab_sink, ab_conv_w, ab_conv_b, ab_dt_bias, ab_a_log,
           ab_d_skip, ab_gnorm_w, ab_w_out, mla_w_in, mla_q_norm_w, mla_kv_norm_w, mla_w_uq, mla_w_ukv,
           mla_w_out, final_norm_w):
    batch, seq, d = x_prompt.shape
    dec_batch, dec_seq, _ = x_sample.shape
    assert ada_w.shape[0] == 2 and ab_w_in.shape[0] == 1 and mla_w_in.shape[0] == 1
    assert dec_batch + 1 <= SUBLANES

    conds = jnp.concatenate([c_ctx[None, :], c, jnp.zeros((SUBLANES - 1 - dec_batch, d), F32)], axis=0)
    mods = _modulation(conds, ada_w, ada_b)

    n_main = sum(AB_COLS[:-1])
    w_ab = ab_w_in[0][:, :n_main].astype(BF16)
    w_dt = jnp.concatenate([ab_w_in[0][:, n_main:], jnp.zeros((d, LANES - 2 * SSD_HEADS), F32)], axis=1).astype(BF16)
    mw = mla_w_in[0]
    o_kpe = Q_LORA + KV_LORA
    w_mla = jnp.concatenate([mw[:, :o_kpe], jnp.tile(mw[:, o_kpe:o_kpe + MLA_ROPE], (1, LANES // MLA_ROPE)),
                             mw[:, o_kpe + MLA_ROPE:]], axis=1).astype(BF16)
    wuq = mla_w_uq[0].reshape(Q_LORA, MLA_HEADS, MLA_NOPE + MLA_ROPE)
    wuq = jnp.concatenate([wuq[:, :, :MLA_NOPE].reshape(Q_LORA, -1), wuq[:, :, MLA_NOPE:].reshape(Q_LORA, -1)],
                          axis=1).astype(BF16)
    wukv = mla_w_ukv[0].reshape(KV_LORA, MLA_HEADS, MLA_NOPE + MLA_V)
    wukt = wukv[:, :, :MLA_NOPE].reshape(KV_LORA, -1).T.astype(BF16)
    wuv = wukv[:, :, MLA_NOPE:].reshape(KV_LORA, -1).astype(BF16)
    pad_lanes = lambda r: jnp.concatenate([r.reshape(1, -1), jnp.zeros((1, LANES - r.size), F32)], axis=1)
    ssd_par = jnp.concatenate([pad_lanes(ab_dt_bias[0]), pad_lanes(ab_a_log[0]),
                               jnp.zeros((SUBLANES - 2, LANES), F32)], axis=0)
    wts = dict(
        norm_w=norm_w, ab_w_in=w_ab, ab_w_dt=w_dt, ab_sink=ab_sink[0],
        ab_conv_w=jnp.concatenate([ab_conv_w[0], jnp.zeros((SUBLANES - CONV_K, CONV_CH), F32)], axis=0),
        ab_conv_b=ab_conv_b[0][None, :], ssd_par=ssd_par,
        ssd_dskip=jnp.repeat(ab_d_skip[0], SSD_HEAD_DIM)[None, :],
        ab_gnorm_w=ab_gnorm_w[0][None, :], ab_w_out=ab_w_out[0].astype(BF16),
        mla_w_in=w_mla, mla_q_norm_w=mla_q_norm_w[0][None, :], mla_kv_norm_w=mla_kv_norm_w[0][None, :],
        mla_w_uq=wuq, mla_w_ukt=wukt, mla_w_uv=wuv, mla_w_out=mla_w_out[0].astype(BF16),
        final_norm_w=final_norm_w[None, :],
    )
    tabs_a = _rope_tables(dec_seq, A_HEAD_DIM)
    tabs_c = _rope_tables(dec_seq, MLA_ROPE)

    y_prompt, ((k_t, v_t), s_f, s_b, ckv32, kpe32) = _group(
        x_prompt, seq, jnp.zeros((1,), jnp.int32), mods, wts, tabs_a, tabs_c, None)

    past = cache_a_k.shape[2]
    ctx_cache = (cache_a_k[:, 0].reshape(dec_batch, past, A_KV_WIDTH),
                 cache_a_v[:, 0].reshape(dec_batch, past, A_KV_WIDTH),
                 state_ssd_fwd[:, 0], state_ssd_bwd[:, 0],
                 cache_mla_ckv[:, 0].astype(BF16),
                 jnp.tile(cache_mla_kpe[:, 0], (1, 1, LANES // MLA_ROPE)).astype(BF16))
    y_sample, _ = _group(x_sample, dec_seq, 1 + jnp.arange(dec_batch), mods, wts, tabs_a, tabs_c, ctx_cache)

    cache_layout = lambda u: u.reshape(batch, 1, A_KV_HEADS, A_HEAD_DIM, seq).transpose(0, 1, 4, 2, 3)
    return (y_prompt, y_sample,
            cache_layout(k_t), cache_layout(v_t),
            s_f[:, None], s_b[:, None],
            ckv32.reshape(batch, 1, seq, KV_LORA), kpe32[:, :MLA_ROPE].reshape(batch, 1, seq, MLA_ROPE))
```

```python
import functools
import math

import jax
import jax.numpy as jnp
import numpy as np
from jax import lax
from jax.experimental import pallas as pl
from jax.experimental.pallas import tpu as pltpu

F32 = jnp.float32
BF16 = jnp.bfloat16

LANES = 128
SUBLANES = 8
VMEM_LIMIT_BYTES = 56 * 1024 * 1024

GRID_W = 64
ROPE_BASE = 10000.0
EPS = 1e-6
NEG_INF = -1e30
WINDOW_BLK = 128
CHUNK = 128
A_HEADS, A_KV_HEADS, A_HEAD_DIM = 8, 2, 64
A_WIDTH = A_HEADS * A_HEAD_DIM
A_KV_WIDTH = A_KV_HEADS * A_HEAD_DIM
A_SCALE = A_HEAD_DIM ** -0.5
SSD_HEADS, SSD_HEAD_DIM, SSD_GROUPS, SSD_STATE = 16, 64, 2, 64
SSD_INNER = SSD_HEADS * SSD_HEAD_DIM
CONV_K = 5
BC_WIDTH = SSD_GROUPS * SSD_STATE
CONV_CH = SSD_INNER + 2 * BC_WIDTH
MLA_HEADS, MLA_NOPE, MLA_ROPE, MLA_V = 16, 64, 32, 64
Q_LORA, KV_LORA = 256, 128
MLA_WIDTH = MLA_HEADS * MLA_V
MLA_SCALE = (MLA_NOPE + MLA_ROPE) ** -0.5
LOG2E = 1.4426950408889634


def _cparams(*sem):
    return pltpu.CompilerParams(dimension_semantics=sem, vmem_limit_bytes=VMEM_LIMIT_BYTES)


def _silu(x):
    return x * (1.0 / (1.0 + jnp.exp(-x)))


def _dot(a, b):
    return jnp.dot(a, b, preferred_element_type=F32, precision=lax.Precision.DEFAULT)


def _dot_nt(a, b):
    return lax.dot_general(a, b, (((1,), (1,)), ((), ())), preferred_element_type=F32,
                           precision=lax.Precision.DEFAULT)


def _row_splits(m, parts=2):
    step = m // parts
    return [slice(r * step, (r + 1) * step) for r in range(parts)]


def _lane_iota(shape):
    return lax.broadcasted_iota(jnp.int32, shape, len(shape) - 1)


def _row_iota(shape):
    return lax.broadcasted_iota(jnp.int32, shape, len(shape) - 2)


def _mod_kernel(cond_ref, w_ref, b_ref, o_ref):
    s = _silu(cond_ref[...])
    o_ref[...] = _dot(s.astype(BF16), w_ref[...].astype(BF16)) + b_ref[...]


def _modulation(conds, ada_w, ada_b):
    depth, d, d3 = ada_w.shape
    tn = 768
    return pl.pallas_call(
        _mod_kernel,
        grid=(depth, d3 // tn),
        in_specs=[pl.BlockSpec((SUBLANES, d), lambda l, j: (0, 0)),
                  pl.BlockSpec((None, d, tn), lambda l, j: (l, 0, j)),
                  pl.BlockSpec((None, 1, tn), lambda l, j: (l, 0, j))],
        out_specs=pl.BlockSpec((None, SUBLANES, tn), lambda l, j: (l, 0, j)),
        out_shape=jax.ShapeDtypeStruct((depth, SUBLANES, d3), F32),
        compiler_params=_cparams("arbitrary", "arbitrary"),
        name="modulation",
    )(conds, ada_w, ada_b.reshape(depth, 1, d3))


def _norm_mod(x, nw, scale, shift):
    ms = jnp.mean(x * x, axis=-1, keepdims=True)
    y = (x * lax.rsqrt(ms + EPS)) * nw
    return y * (1.0 + scale) + shift


def _rope(x, cos, sin_signed, half_period):
    outs = []
    first = (_lane_iota((x.shape[0], LANES)) & (2 * half_period - 1)) < half_period
    for j in range(x.shape[1] // LANES):
        xj = x[:, j * LANES:(j + 1) * LANES]
        up = pltpu.roll(xj, LANES - half_period, 1)
        dn = pltpu.roll(xj, half_period, 1)
        outs.append(xj * cos + jnp.where(first, up, dn) * sin_signed)
    return outs[0] if len(outs) == 1 else jnp.concatenate(outs, axis=1)


AB_COLS = (A_WIDTH, A_KV_WIDTH, A_KV_WIDTH, A_WIDTH, SSD_INNER, CONV_CH, LANES)


def _ab_in_kernel(*refs, rope, seq_len):
    if rope:
        (x_ref, nw_ref, sc_ref, sh_ref, w_ref, wdt_ref, cos_ref, sin_ref,
         q_ref, k_ref, v_ref, g_ref, z_ref, xbc_ref, dt_ref) = refs
        kvt_refs = {}
    else:
        (x_ref, nw_ref, sc_ref, sh_ref, w_ref, wdt_ref,
         q_ref, k_ref, v_ref, g_ref, z_ref, xbc_ref, dt_ref, kt_ref, vt_ref) = refs
        kvt_refs = {1: kt_ref, 2: vt_ref}
    offs = np.concatenate([[0], np.cumsum(AB_COLS)])
    outs = (q_ref, k_ref, v_ref, g_ref, z_ref, xbc_ref, dt_ref)
    for rows in _row_splits(x_ref.shape[0]):
        h = _norm_mod(x_ref[rows, :], nw_ref[...], sc_ref[...], sh_ref[...]).astype(BF16)
        for i, o_ref in enumerate(outs):
            last = i == len(outs) - 1
            y = _dot(h, wdt_ref[...] if last else w_ref[:, int(offs[i]):int(offs[i + 1])])
            if rope and i in (0, 1):
                y = _rope(y, cos_ref[rows, :], sin_ref[rows, :], A_HEAD_DIM // 4)
            o_ref[rows, :] = y.astype(o_ref.dtype)
            if i in kvt_refs:
                for s in range(rows.start // seq_len, rows.stop // seq_len):
                    kvt_refs[i][s] = y[s * seq_len - rows.start:(s + 1) * seq_len - rows.start, :].T


def _ab_in(x, nw, scale, shift, w, wdt, rope_tabs, seq_len, tm):
    t, d = x.shape
    per_seq = seq_len // tm if scale.shape[0] > 1 else None
    row = lambda i: (i, 0)
    mod = (lambda i: (i // per_seq, 0, 0)) if per_seq else (lambda i: (0, 0, 0))
    const = lambda i: (0, 0)
    in_specs = [pl.BlockSpec((tm, d), row),
                pl.BlockSpec((1, d), const),
                pl.BlockSpec((None, 1, d), mod),
                pl.BlockSpec((None, 1, d), mod),
                pl.BlockSpec(w.shape, const),
                pl.BlockSpec(wdt.shape, const)]
    args = [x, nw, scale, shift, w, wdt]
    if rope_tabs is not None:
        nt = seq_len // tm
        pos = lambda i: (i % nt, 0)
        in_specs += [pl.BlockSpec((tm, LANES), pos), pl.BlockSpec((tm, LANES), pos)]
        args += list(rope_tabs)
    dts = (BF16, F32, F32, BF16, BF16, BF16, F32)
    out_specs = [pl.BlockSpec((tm, c), row) for c in AB_COLS]
    out_shape = [jax.ShapeDtypeStruct((t, c), dt) for c, dt in zip(AB_COLS, dts)]
    if rope_tabs is None:
        assert (tm // 2) % seq_len == 0
        spt = tm // seq_len
        out_specs += [pl.BlockSpec((spt, A_KV_WIDTH, seq_len), lambda i: (i, 0, 0))] * 2
        out_shape += [jax.ShapeDtypeStruct((t // seq_len, A_KV_WIDTH, seq_len), F32)] * 2
    return pl.pallas_call(
        functools.partial(_ab_in_kernel, rope=rope_tabs is not None, seq_len=seq_len),
        grid=(t // tm,),
        in_specs=in_specs,
        out_specs=out_specs,
        out_shape=out_shape,
        compiler_params=_cparams("arbitrary"),
        name="ab_in_proj",
    )(*args)


def _conv_kernel(prev_ref, main_ref, next_ref, w_ref, b_ref, o_ref, *, tiles_per_seq):
    i = pl.program_id(0) % tiles_per_seq
    tm = main_ref.shape[0]
    prev = jnp.where(i > 0, prev_ref[...].astype(F32), 0.0)
    nxt = jnp.where(i < tiles_per_seq - 1, next_ref[...].astype(F32), 0.0)
    ext = jnp.concatenate([prev, main_ref[...].astype(F32), nxt], axis=0)
    first = SUBLANES - CONV_K // 2
    acc = b_ref[...] + ext[first:first + tm] * w_ref[0:1, :]
    for j in range(1, CONV_K):
        acc = acc + ext[first + j:first + j + tm] * w_ref[j:j + 1, :]
    o_ref[...] = _silu(acc).astype(o_ref.dtype)


def _conv(xbc, w, b, seq_len, tm):
    t, c = xbc.shape
    tps = seq_len // tm
    hb = tm // SUBLANES
    nblk8 = t // SUBLANES
    return pl.pallas_call(
        functools.partial(_conv_kernel, tiles_per_seq=tps),
        grid=(t // tm,),
        in_specs=[pl.BlockSpec((SUBLANES, c), lambda i: (jnp.maximum(i * hb - 1, 0), 0)),
                  pl.BlockSpec((tm, c), lambda i: (i, 0)),
                  pl.BlockSpec((SUBLANES, c), lambda i: (jnp.minimum((i + 1) * hb, nblk8 - 1), 0)),
                  pl.BlockSpec((SUBLANES, c), lambda i: (0, 0)),
                  pl.BlockSpec((1, c), lambda i: (0, 0))],
        out_specs=pl.BlockSpec((tm, c), lambda i: (i, 0)),
        out_shape=jax.ShapeDtypeStruct((t, c), BF16),
        compiler_params=_cparams("arbitrary"),
        name="ssd_conv",
    )(xbc, xbc, xbc, w, b)


def _ssd_init(h0_ref, h_scr, has_h0):
    if not has_h0:
        h_scr[...] = jnp.zeros_like(h_scr)
        return
    for pr in range(SSD_HEADS // 2):
        both = jnp.concatenate([h0_ref[2 * pr], h0_ref[2 * pr + 1]], axis=0)
        both = jnp.concatenate([both, jnp.zeros_like(both)], axis=1)
        st = both.T[0:SSD_STATE]
        lo_st = _lane_iota(st.shape) < SSD_HEAD_DIM
        h_scr[2 * pr] = jnp.where(lo_st, st, 0.0)
        h_scr[2 * pr + 1] = jnp.where(lo_st, 0.0, st)


def _ssd_final(h_scr, hout_ref):
    for pr in range(SSD_HEADS // 2):
        st = h_scr[2 * pr] + h_scr[2 * pr + 1]
        st = jnp.concatenate([st, jnp.zeros_like(st)], axis=0).T
        hout_ref[2 * pr] = st[0:SSD_HEAD_DIM, 0:SSD_STATE]
        hout_ref[2 * pr + 1] = st[SSD_HEAD_DIM:2 * SSD_HEAD_DIM, 0:SSD_STATE]


def _ssd_prelude(b_ref, c_ref, dt_ref, par_ref, direction):
    lane = _lane_iota((CHUNK, LANES))
    rowi = _row_iota((CHUNK, LANES))
    dtp_in = dt_ref[...] + par_ref[0:1, :]
    dtp = jnp.maximum(dtp_in, 0.0) + jnp.log(1.0 + jnp.exp(-jnp.abs(dtp_in)))
    la = dtp * (-jnp.exp(par_ref[1:2, :]))
    causal = (rowi >= lane) if direction == 0 else (rowi <= lane)
    tri = jnp.where(causal, 1.0, 0.0).astype(BF16)
    p1 = la.astype(BF16)
    r1 = la - p1.astype(F32)
    p2 = r1.astype(BF16)
    p3 = (r1 - p2.astype(F32)).astype(BF16)
    cum = _dot(tri, p1) + _dot(tri, p2) + _dot(tri, p3)
    cum_t = cum.T
    dtp_t = dtp.T
    end = CHUNK - 1 if direction == 0 else 0
    w_t = jnp.exp(cum_t[:, end:end + 1] - cum_t) * dtp_t
    lrow_t = cum_t - jnp.log(dtp_t)
    bm = b_ref[...]
    cm = c_ref[...]
    cb, cmask = [], []
    for g in range(SSD_GROUPS):
        in_g = (lane >= g * SSD_STATE) & (lane < (g + 1) * SSD_STATE)
        cg = jnp.where(in_g, cm, jnp.zeros_like(cm))
        cmask.append(cg.astype(F32))
        cb.append(_dot_nt(cg, bm))
    return dict(cum=cum, lrow_t=lrow_t, w_t=w_t, causal=causal, end=end,
                bm_t=bm.astype(F32).T, cb=cb, cmask=cmask, lo_half=lane < SSD_HEAD_DIM)


def _ssd_head_pair(pre, pr, x_ref, dskip_ref, y_ref, h_scr, direction, accumulate=False):
    hp = SSD_HEADS // SSD_GROUPS
    cum, lrow_t, w_t, end = pre["cum"], pre["lrow_t"], pre["w_t"], pre["end"]
    lo_half = pre["lo_half"]
    xp = x_ref[:, pr * LANES:(pr + 1) * LANES]
    y_pair = None
    for e in range(2):
        h = 2 * pr + e
        g = h // hp
        col = direction * SSD_HEADS + h
        xe = jnp.where(lo_half if e == 0 else ~lo_half, xp, jnp.zeros_like(xp))
        cumcol = jnp.broadcast_to(cum[:, col:col + 1], (CHUNK, LANES))
        diff = cumcol - lrow_t[col:col + 1, :]
        sc = pre["cb"][g] * jnp.exp(jnp.where(pre["causal"], diff, NEG_INF))
        ce = pre["cmask"][g] * jnp.exp(cumcol)
        hs = h_scr[h]
        hsb = hs.astype(BF16)
        zpad = jnp.zeros_like(hsb)
        bw = (pre["bm_t"][g * SSD_STATE:(g + 1) * SSD_STATE, :] * w_t[col:col + 1, :]).astype(BF16)
        a = jnp.concatenate([jnp.concatenate([sc.astype(BF16), ce.astype(BF16)], axis=1),
                             jnp.concatenate([bw, jnp.zeros_like(bw)], axis=1)], axis=0)
        wmat = jnp.concatenate([xe, hsb, zpad] if g == 0 else [xe, zpad, hsb], axis=0)
        res = _dot(a, wmat)
        ye = res[0:CHUNK]
        y_pair = ye if y_pair is None else y_pair + ye
        cdec = jnp.exp(cumcol[end:end + 1, :])
        h_scr[h] = hs * cdec + res[CHUNK:CHUNK + SSD_STATE]
    if direction == 0:
        y_pair = y_pair + dskip_ref[:, pr * LANES:(pr + 1) * LANES] * xp.astype(F32)
    if accumulate:
        y_pair = y_pair + y_ref[:, pr * LANES:(pr + 1) * LANES]
    y_ref[:, pr * LANES:(pr + 1) * LANES] = y_pair.astype(y_ref.dtype)


def _ssd_kernel(xf_ref, bf_ref, cf_ref, dtf_ref, xb_ref, bb_ref, cb_ref, dtb_ref, par_ref, dskip_ref,
                h0f_ref, h0b_ref, yf_ref, yb_ref, houtf_ref, houtb_ref, hf_scr, hb_scr, *, nc, has_h0):
    ci = pl.program_id(1)

    @pl.when(ci == 0)
    def _init():
        _ssd_init(h0f_ref, hf_scr, has_h0)
        _ssd_init(h0b_ref, hb_scr, has_h0)

    pre_f = _ssd_prelude(bf_ref, cf_ref, dtf_ref, par_ref, 0)
    pre_b = _ssd_prelude(bb_ref, cb_ref, dtb_ref, par_ref, 1)
    for pr in range(SSD_HEADS // 2):
        _ssd_head_pair(pre_f, pr, xf_ref, dskip_ref, yf_ref, hf_scr, 0)
        _ssd_head_pair(pre_b, pr, xb_ref, dskip_ref, yb_ref, hb_scr, 1)

    @pl.when(ci == nc - 1)
    def _final():
        _ssd_final(hf_scr, houtf_ref)
        _ssd_final(hb_scr, houtb_ref)


def _ssd(xbc, dt, par, dskip, h0f, h0b, nb, seq_len):
    t = xbc.shape[0]
    nc = seq_len // CHUNK
    fwd = lambda b, c: b * nc + c
    bwd = lambda b, c: b * nc + nc - 1 - c
    xcols = SSD_INNER // LANES
    has_h0 = h0f is not None
    if not has_h0:
        h0f = h0b = jnp.zeros((1, SSD_HEADS, SSD_HEAD_DIM, SSD_STATE), F32)
    h0_map = (lambda b, c: (b, 0, 0, 0)) if has_h0 else (lambda b, c: (0, 0, 0, 0))
    st_shape = (None, SSD_HEADS, SSD_HEAD_DIM, SSD_STATE)
    carry_shape = (None, SSD_HEADS, SSD_STATE, LANES)
    per_b = lambda b, c: (b, 0, 0, 0)

    def chunk_specs(cidx):
        return [pl.BlockSpec((CHUNK, SSD_INNER), lambda b, c: (cidx(b, c), 0)),
                pl.BlockSpec((CHUNK, LANES), lambda b, c: (cidx(b, c), xcols)),
                pl.BlockSpec((CHUNK, LANES), lambda b, c: (cidx(b, c), xcols + 1)),
                pl.BlockSpec((CHUNK, LANES), lambda b, c: (cidx(b, c), 0))]

    outs = pl.pallas_call(
        functools.partial(_ssd_kernel, nc=nc, has_h0=has_h0),
        grid=(nb, nc),
        in_specs=chunk_specs(fwd) + chunk_specs(bwd) + [
            pl.BlockSpec((SUBLANES, LANES), lambda b, c: (0, 0)),
            pl.BlockSpec((1, SSD_INNER), lambda b, c: (0, 0)),
            pl.BlockSpec(st_shape, h0_map), pl.BlockSpec(st_shape, h0_map)],
        out_specs=[pl.BlockSpec((CHUNK, SSD_INNER), lambda b, c: (fwd(b, c), 0)),
                   pl.BlockSpec((CHUNK, SSD_INNER), lambda b, c: (bwd(b, c), 0)),
                   pl.BlockSpec(st_shape, per_b), pl.BlockSpec(st_shape, per_b),
                   pl.BlockSpec(carry_shape, lambda b, c: (0, 0, 0, 0)),
                   pl.BlockSpec(carry_shape, lambda b, c: (0, 0, 0, 0))],
        out_shape=[jax.ShapeDtypeStruct((t, SSD_INNER), BF16), jax.ShapeDtypeStruct((t, SSD_INNER), BF16),
                   jax.ShapeDtypeStruct((nb, SSD_HEADS, SSD_HEAD_DIM, SSD_STATE), F32),
                   jax.ShapeDtypeStruct((nb, SSD_HEADS, SSD_HEAD_DIM, SSD_STATE), F32),
                   jax.ShapeDtypeStruct((1, SSD_HEADS, SSD_STATE, LANES), F32),
                   jax.ShapeDtypeStruct((1, SSD_HEADS, SSD_STATE, LANES), F32)],
        compiler_params=_cparams("arbitrary", "arbitrary"),
        name="ssd_scan_bidir",
    )(xbc, xbc, xbc, dt, xbc, xbc, xbc, dt, par, dskip, h0f, h0b)
    return outs[:4]


def _ssd_seq_kernel(x_ref, b_ref, c_ref, dt_ref, par_ref, dskip_ref, y_ref, houtf_ref, houtb_ref,
                    hf_scr, hb_scr, yacc_scr, *, nc):
    _ssd_init(None, hf_scr, False)
    _ssd_init(None, hb_scr, False)
    for c in range(nc):
        rf = pl.ds(c * CHUNK, CHUNK)
        rb = pl.ds((nc - 1 - c) * CHUNK, CHUNK)
        pre_f = _ssd_prelude(b_ref.at[rf], c_ref.at[rf], dt_ref.at[rf], par_ref, 0)
        pre_b = _ssd_prelude(b_ref.at[rb], c_ref.at[rb], dt_ref.at[rb], par_ref, 1)
        for pr in range(SSD_HEADS // 2):
            _ssd_head_pair(pre_f, pr, x_ref.at[rf], dskip_ref, yacc_scr.at[rf], hf_scr, 0, accumulate=c > nc - 1 - c)
            _ssd_head_pair(pre_b, pr, x_ref.at[rb], dskip_ref, yacc_scr.at[rb], hb_scr, 1, accumulate=c >= nc - 1 - c)
    y_ref[...] = yacc_scr[...].astype(y_ref.dtype)
    _ssd_final(hf_scr, houtf_ref)
    _ssd_final(hb_scr, houtb_ref)


def _ssd_seq(xbc, dt, par, dskip, nb, seq_len):
    t = xbc.shape[0]
    nc = seq_len // CHUNK
    xcols = SSD_INNER // LANES
    st_shape = (None, SSD_HEADS, SSD_HEAD_DIM, SSD_STATE)
    carry_shape = (None, SSD_HEADS, SSD_STATE, LANES)
    per_b = lambda b: (b, 0, 0, 0)
    one_block = lambda b: (0, 0, 0, 0)
    return pl.pallas_call(
        functools.partial(_ssd_seq_kernel, nc=nc),
        grid=(nb,),
        in_specs=[pl.BlockSpec((seq_len, SSD_INNER), lambda b: (b, 0)),
                  pl.BlockSpec((seq_len, LANES), lambda b: (b, xcols)),
                  pl.BlockSpec((seq_len, LANES), lambda b: (b, xcols + 1)),
                  pl.BlockSpec((seq_len, LANES), lambda b: (b, 0)),
                  pl.BlockSpec((SUBLANES, LANES), lambda b: (0, 0)),
                  pl.BlockSpec((1, SSD_INNER), lambda b: (0, 0))],
        out_specs=[pl.BlockSpec((seq_len, SSD_INNER), lambda b: (b, 0)),
                   pl.BlockSpec(st_shape, per_b), pl.BlockSpec(st_shape, per_b),
                   pl.BlockSpec(carry_shape, one_block), pl.BlockSpec(carry_shape, one_block)],
        out_shape=[jax.ShapeDtypeStruct((t, SSD_INNER), BF16),
                   jax.ShapeDtypeStruct((nb, SSD_HEADS, SSD_HEAD_DIM, SSD_STATE), F32),
                   jax.ShapeDtypeStruct((nb, SSD_HEADS, SSD_HEAD_DIM, SSD_STATE), F32),
                   jax.ShapeDtypeStruct((1, SSD_HEADS, SSD_STATE, LANES), F32),
                   jax.ShapeDtypeStruct((1, SSD_HEADS, SSD_STATE, LANES), F32)],
        scratch_shapes=[pltpu.VMEM((seq_len, SSD_INNER), F32)],
        compiler_params=_cparams("arbitrary"),
        name="ssd_scan_seq",
    )(xbc, xbc, xbc, dt, par, dskip)[:3]


def _kv_variants(k, v):
    lane = _lane_iota(k.shape)
    lo = lane < A_HEAD_DIM
    k_sw = pltpu.roll(k, A_HEAD_DIM, 1)
    v_sw = pltpu.roll(v, A_HEAD_DIM, 1)
    ks, vs = {}, {}
    for j in range(A_KV_HEADS):
        for e in range(2):
            src_k, src_v = (k, v) if j == e else (k_sw, v_sw)
            half = lo if e == 0 else ~lo
            ks[j, e] = jnp.where(half, src_k, 0.0).astype(BF16)
            vs[j, e] = src_v.astype(BF16)
    return ks, vs


GQA_KEY_TILE = 256


def _gqa_heads(q_ref, g_ref, o_ref, sink_ref, ks, vs, valid_tile, s_scr):
    m = q_ref.shape[0]
    assert A_HEADS // A_KV_HEADS == 4
    lo = _lane_iota((m, LANES)) < A_HEAD_DIM
    upper_rows = _row_iota((2 * m, 1)) >= m
    scale = jnp.asarray(A_SCALE, BF16)
    units = [(j, e) for j in range(A_KV_HEADS) for e in range(2)]
    nkeys = ks[0, 0].shape[0]
    kt_w = GQA_KEY_TILE
    nkt = nkeys // kt_w
    rows_of = {}

    def q_rows(j):
        if j not in rows_of:
            rows_of[j] = jnp.concatenate([q_ref[:, 2 * j * LANES:(2 * j + 1) * LANES],
                                          q_ref[:, (2 * j + 1) * LANES:(2 * j + 2) * LANES]], axis=0) * scale
        return rows_of[j]

    def score_tile(u, st, kt):
        j, e = units[u]
        cols = slice(kt * kt_w, (kt + 1) * kt_w)
        s = _dot_nt(q_rows(j), ks[j, e][cols, :])
        ok = valid_tile(kt)
        if ok is not None:
            s = jnp.where(ok, s, NEG_INF)
        s_scr[u % 2, :, cols] = s
        for c in range(kt_w // LANES):
            t = s[:, c * LANES:(c + 1) * LANES]
            st["mrun"] = t if st["mrun"] is None else jnp.maximum(st["mrun"], t)

    def value_tile(u, st, kt):
        j, e = units[u]
        cols = slice(kt * kt_w, (kt + 1) * kt_w)
        p = jnp.exp(s_scr[u % 2, :, cols] - st["m"])
        for c in range(kt_w // LANES):
            t = p[:, c * LANES:(c + 1) * LANES]
            st["lrun"] = t if st["lrun"] is None else st["lrun"] + t
        pv = _dot(p.astype(BF16), vs[j, e][cols, :])
        st["acc"] = pv if st["acc"] is None else st["acc"] + pv

    outs = {}
    cur = None
    for u in range(len(units) + 1):
        nxt = dict(mrun=None) if u < len(units) else None
        for kt in range(nkt):
            if nxt is not None:
                score_tile(u, nxt, kt)
            if cur is not None:
                value_tile(u - 1, cur, kt)
        if cur is not None:
            den = jnp.sum(cur["lrun"], axis=1, keepdims=True) + jnp.exp(cur["sink"] - cur["m"])
            outs[units[u - 1]] = cur["acc"] / den
        if nxt is not None:
            j, e = units[u]
            sink = jnp.where(upper_rows, sink_ref[4 * j + 2 + e], sink_ref[4 * j + e])
            nxt.update(sink=sink, m=jnp.maximum(jnp.max(nxt["mrun"], axis=1, keepdims=True), sink),
                       lrun=None, acc=None)
        cur = nxt
    for pr in range(A_HEADS // 2):
        j, half = pr // 2, pr % 2
        r = slice(half * m, (half + 1) * m)
        attn = jnp.where(lo, outs[j, 0][r], outs[j, 1][r])
        gate = g_ref[:, pr * LANES:(pr + 1) * LANES].astype(F32)
        o_ref[:, pr * LANES:(pr + 1) * LANES] = (attn * _silu(gate)).astype(o_ref.dtype)


def _attn_ctx_kernel(sink_ref, q_ref, k_ref, v_ref, g_ref, o_ref, s_scr):
    ks, vs = _kv_variants(k_ref[...], v_ref[...])
    _gqa_heads(q_ref, g_ref, o_ref, sink_ref, ks, vs, lambda kt: None, s_scr)


def _attn_ctx(sink, q, k, v, g, seq_len):
    t = q.shape[0]
    row = lambda b: (b, 0)
    return pl.pallas_call(
        _attn_ctx_kernel,
        grid=(t // seq_len,),
        in_specs=[pl.BlockSpec(memory_space=pltpu.SMEM),
                  pl.BlockSpec((seq_len, A_WIDTH), row),
                  pl.BlockSpec((seq_len, A_KV_WIDTH), row),
                  pl.BlockSpec((seq_len, A_KV_WIDTH), row),
                  pl.BlockSpec((seq_len, A_WIDTH), row)],
        out_specs=pl.BlockSpec((seq_len, A_WIDTH), row),
        out_shape=jax.ShapeDtypeStruct((t, A_WIDTH), BF16),
        scratch_shapes=[pltpu.VMEM((2, 2 * seq_len, seq_len), F32)],
        compiler_params=_cparams("arbitrary"),
        name="swa_context_attention",
    )(sink, q, k, v, g)


BAND_Q = 256


def _attn_band_kernel(sink_ref, q_ref, kp_ref, kc_ref, kn_ref, vp_ref, vc_ref, vn_ref, kx_ref, vx_ref, g_ref,
                      o_ref, s_scr, *, nsteps):
    n = pl.program_id(1)
    tq = q_ref.shape[0]
    blk = WINDOW_BLK
    k = jnp.concatenate([kp_ref[...], kc_ref[...], kn_ref[...], kx_ref[...]], axis=0)
    v = jnp.concatenate([vp_ref[...], vc_ref[...], vn_ref[...], vx_ref[...]], axis=0)
    nloc = tq + 2 * blk
    qi = _row_iota((2 * tq, GQA_KEY_TILE)) & (tq - 1)

    def valid_tile(kt):
        if kt * GQA_KEY_TILE >= nloc:
            return None
        col = _lane_iota((2 * tq, GQA_KEY_TILE)) + kt * GQA_KEY_TILE
        rel = col - blk - qi
        return ((rel >= -blk) & (rel <= blk) & ((col >= blk) | (n > 0)) & ((col < nloc - blk) | (n < nsteps - 1)))

    assert nloc % GQA_KEY_TILE == 0
    ks, vs = _kv_variants(k, v)
    _gqa_heads(q_ref, g_ref, o_ref, sink_ref, ks, vs, valid_tile, s_scr)


def _attn_band(sink, q, k, v, k_ctx, v_ctx, g, nb, seq_len):
    t = q.shape[0]
    blk = WINDOW_BLK
    tq = BAND_Q
    assert tq & (tq - 1) == 0 and tq % blk == 0 and seq_len % tq == 0
    per = tq // blk
    nblk = seq_len // blk
    nsteps = seq_len // tq
    nctx = k_ctx.shape[1]
    cur = lambda b, n: (b * nsteps + n, 0)
    prv = lambda b, n: (b * nblk + jnp.maximum(n * per - 1, 0), 0)
    nxt = lambda b, n: (b * nblk + jnp.minimum((n + 1) * per, nblk - 1), 0)
    ctx = lambda b, n: (b, 0, 0)
    edge = lambda f: pl.BlockSpec((blk, A_KV_WIDTH), f)
    mid = pl.BlockSpec((tq, A_KV_WIDTH), cur)
    return pl.pallas_call(
        functools.partial(_attn_band_kernel, nsteps=nsteps),
        grid=(nb, nsteps),
        in_specs=[pl.BlockSpec(memory_space=pltpu.SMEM),
                  pl.BlockSpec((tq, A_WIDTH), cur),
                  edge(prv), mid, edge(nxt),
                  edge(prv), mid, edge(nxt),
                  pl.BlockSpec((None, nctx, A_KV_WIDTH), ctx),
                  pl.BlockSpec((None, nctx, A_KV_WIDTH), ctx),
                  pl.BlockSpec((tq, A_WIDTH), cur)],
        out_specs=pl.BlockSpec((tq, A_WIDTH), cur),
        out_shape=jax.ShapeDtypeStruct((t, A_WIDTH), BF16),
        scratch_shapes=[pltpu.VMEM((2, 2 * tq, tq + 2 * blk + nctx), F32)],
        compiler_params=_cparams("arbitrary", "arbitrary"),
        name="swa_banded_attention",
    )(sink, q, k, k, k, v, v, v, k_ctx, v_ctx, g)


def _ab_out_kernel(*refs, two_y):
    if two_y:
        a_ref, yf_ref, yb_ref, z_ref, gnw_ref, w_ref, x_ref, gate_ref, o_ref = refs
    else:
        a_ref, yf_ref, z_ref, gnw_ref, w_ref, x_ref, gate_ref, o_ref = refs
    for rows in _row_splits(x_ref.shape[0]):
        y = yf_ref[rows, :].astype(F32)
        if two_y:
            y = y + yb_ref[rows, :].astype(F32)
        y = y * _silu(z_ref[rows, :].astype(F32))
        ms = jnp.mean(y * y, axis=-1, keepdims=True)
        s = ((y * lax.rsqrt(ms + EPS)) * gnw_ref[...]).astype(BF16)
        out = _dot(a_ref[rows, :], w_ref[0:A_WIDTH, :]) + _dot(s, w_ref[A_WIDTH:, :])
        o_ref[rows, :] = x_ref[rows, :] + gate_ref[...] * out


def _ab_out(a, yf, yb, z, gnw, w, x, gate, seq_len, tm):
    t, d = x.shape
    per_seq = seq_len // tm if gate.shape[0] > 1 else None
    row = lambda i: (i, 0)
    const = lambda i: (0, 0)
    mod = (lambda i: (i // per_seq, 0, 0)) if per_seq else (lambda i: (0, 0, 0))
    ys = [yf] if yb is None else [yf, yb]
    return pl.pallas_call(
        functools.partial(_ab_out_kernel, two_y=yb is not None),
        grid=(t // tm,),
        in_specs=[pl.BlockSpec((tm, A_WIDTH), row)] + [pl.BlockSpec((tm, SSD_INNER), row)] * (len(ys) + 1) + [
                  pl.BlockSpec((1, SSD_INNER), const),
                  pl.BlockSpec(w.shape, const),
                  pl.BlockSpec((tm, d), row),
                  pl.BlockSpec((None, 1, d), mod)],
        out_specs=pl.BlockSpec((tm, d), row),
        out_shape=jax.ShapeDtypeStruct((t, d), F32),
        compiler_params=_cparams("arbitrary"),
        name="ab_out_proj",
    )(a, *ys, z, gnw, w, x, gate)


MLA_COLS = (Q_LORA, KV_LORA, LANES, MLA_WIDTH)


def _mla_in_kernel(*refs, rope):
    if rope:
        (x_ref, nw_ref, sc_ref, sh_ref, w_ref, qnw_ref, kvnw_ref, wuq_ref, cos_ref, sin_ref,
         qn_ref, qpe_ref, ckv_ref, kpe_ref, g_ref, ckv32_ref, kpe32_ref) = refs
    else:
        (x_ref, nw_ref, sc_ref, sh_ref, w_ref, qnw_ref, kvnw_ref, wuq_ref,
         qn_ref, qpe_ref, ckv_ref, kpe_ref, g_ref, ckv32_ref, kpe32_ref) = refs
    offs = [int(o) for o in np.concatenate([[0], np.cumsum(MLA_COLS)])]
    nope_w = MLA_HEADS * MLA_NOPE

    def rms(u, w):
        return (u * lax.rsqrt(jnp.mean(u * u, axis=-1, keepdims=True) + EPS)) * w

    for rows in _row_splits(x_ref.shape[0]):
        h = _norm_mod(x_ref[rows, :], nw_ref[...], sc_ref[...], sh_ref[...]).astype(BF16)
        cq = _dot(h, w_ref[:, offs[0]:offs[1]])
        ckv = _dot(h, w_ref[:, offs[1]:offs[2]])
        kpe = _dot(h, w_ref[:, offs[2]:offs[3]])
        g_ref[rows, :] = _silu(_dot(h, w_ref[:, offs[3]:offs[4]])).astype(g_ref.dtype)
        cqn = rms(cq, qnw_ref[...]).astype(BF16)
        qn_ref[rows, :] = _dot(cqn, wuq_ref[:, 0:nope_w]).astype(qn_ref.dtype)
        qpe = _dot(cqn, wuq_ref[:, nope_w:])
        if rope:
            qpe = _rope(qpe, cos_ref[rows, :], sin_ref[rows, :], MLA_ROPE // 4)
            kpe = _rope(kpe, cos_ref[rows, :], sin_ref[rows, :], MLA_ROPE // 4)
        qpe_ref[rows, :] = qpe.astype(qpe_ref.dtype)
        ckvn = rms(ckv, kvnw_ref[...])
        ckv_ref[rows, :] = ckvn.astype(ckv_ref.dtype)
        kpe_ref[rows, :] = kpe.astype(kpe_ref.dtype)
        ckv32_ref[rows, :] = ckvn
        kpe32_ref[rows, :] = kpe


def _mla_in(x, nw, scale, shift, w, qnw, kvnw, wuq, rope_tabs, seq_len, tm):
    t, d = x.shape
    per_seq = seq_len // tm if scale.shape[0] > 1 else None
    row = lambda i: (i, 0)
    const = lambda i: (0, 0)
    mod = (lambda i: (i // per_seq, 0, 0)) if per_seq else (lambda i: (0, 0, 0))
    in_specs = [pl.BlockSpec((tm, d), row), pl.BlockSpec((1, d), const),
                pl.BlockSpec((None, 1, d), mod), pl.BlockSpec((None, 1, d), mod),
                pl.BlockSpec(w.shape, const), pl.BlockSpec((1, Q_LORA), const),
                pl.BlockSpec((1, KV_LORA), const), pl.BlockSpec(wuq.shape, const)]
    args = [x, nw, scale, shift, w, qnw, kvnw, wuq]
    if rope_tabs is not None:
        nt = seq_len // tm
        pos = lambda i: (i % nt, 0)
        in_specs += [pl.BlockSpec((tm, LANES), pos), pl.BlockSpec((tm, LANES), pos)]
        args += list(rope_tabs)
    widths = (MLA_HEADS * MLA_NOPE, MLA_HEADS * MLA_ROPE, KV_LORA, LANES, MLA_WIDTH, KV_LORA, LANES)
    dts = (BF16, BF16, BF16, BF16, BF16, F32, F32)
    return pl.pallas_call(
        functools.partial(_mla_in_kernel, rope=rope_tabs is not None),
        grid=(t // tm,),
        in_specs=in_specs,
        out_specs=[pl.BlockSpec((tm, c), row) for c in widths],
        out_shape=[jax.ShapeDtypeStruct((t, c), dt) for c, dt in zip(widths, dts)],
        compiler_params=_cparams("arbitrary"),
        name="mla_in_proj",
    )(*args)


MLA_UNIT_ROWS = 512
MLA_KEY_TILE = 256


def _mla_attn_kernel(qn_ref, qpe_ref, ckv_ref, kpe_ref, wukt_ref, wuv_ref, g_ref, o_ref, kcat_scr, v_scr, s_scr, *,
                     pairs_per_step):
    pg = pl.program_id(1)
    qb = pl.program_id(2)
    tq = qn_ref.shape[0]
    c_exp = MLA_SCALE * LOG2E

    @pl.when(qb == 0)
    def _expand():
        ckv = ckv_ref[...]
        eye = jnp.where(_row_iota((LANES, LANES)) == _lane_iota((LANES, LANES)), 1.0, 0.0).astype(BF16)
        kpe_t = _dot_nt(eye, kpe_ref[...]).astype(BF16)
        for i in range(pairs_per_step):
            kcat_scr[i, 0:LANES, :] = _dot_nt(wukt_ref[i * LANES:(i + 1) * LANES, :], ckv).astype(BF16)
            kcat_scr[i, LANES:2 * LANES, :] = kpe_t
            v_scr[i] = _dot(ckv, wuv_ref[:, i * LANES:(i + 1) * LANES]).astype(BF16)

    nkeys = kcat_scr.shape[-1]
    kt_w = min(MLA_KEY_TILE, nkeys)
    nkt = nkeys // kt_w
    ru = min(MLA_UNIT_ROWS, tq)
    lane = _lane_iota((ru, LANES))
    units = [(i, r, e) for i in range(pairs_per_step) for r in range(tq // ru) for e in range(2)]

    def lane_tiles(x):
        return [x[:, j * LANES:(j + 1) * LANES] for j in range(x.shape[1] // LANES)]

    def q_ext(u):
        i, r, e = units[u]
        rows = slice(r * ru, (r + 1) * ru)
        qn = qn_ref[rows, i * LANES:(i + 1) * LANES]
        qpe = qpe_ref[rows, (i // 2) * LANES:(i // 2 + 1) * LANES]
        slot = 2 * (i % 2) + e if pairs_per_step % 2 == 0 else 2 * ((pg * pairs_per_step + i) % 2) + e
        nope_half = (lane < MLA_NOPE) if e == 0 else (lane >= MLA_NOPE)
        return jnp.concatenate([jnp.where(nope_half, qn, jnp.zeros_like(qn)),
                                jnp.where((lane >> 5) == slot, qpe, jnp.zeros_like(qpe))], axis=1)

    def score_tile(u, st, kt):
        cols = slice(kt * kt_w, (kt + 1) * kt_w)
        s = _dot(st["q"], kcat_scr[units[u][0], :, cols])
        s_scr[u % 2, :, cols] = s
        for t in lane_tiles(s):
            st["mrun"] = t if st["mrun"] is None else jnp.maximum(st["mrun"], t)

    def value_tile(u, st, kt):
        cols = slice(kt * kt_w, (kt + 1) * kt_w)
        p = jnp.exp2((s_scr[u % 2, :, cols] - st["m"]) * c_exp)
        for t in lane_tiles(p):
            st["lrun"] = t if st["lrun"] is None else st["lrun"] + t
        pv = _dot(p.astype(BF16), v_scr[units[u][0], cols, :])
        st["acc"] = pv if st["acc"] is None else st["acc"] + pv

    def finish(u, st, done):
        i, r, e = units[u]
        den = jnp.sum(st["lrun"], axis=1, keepdims=True)
        done[e] = st["acc"] / den
        if e == 1:
            rows = slice(r * ru, (r + 1) * ru)
            attn = jnp.where(lane < MLA_V, done[0], done[1])
            gate = g_ref[rows, i * LANES:(i + 1) * LANES].astype(F32)
            o_ref[rows, i * LANES:(i + 1) * LANES] = (attn * gate).astype(o_ref.dtype)

    done = {}
    cur = None
    for u in range(len(units) + 1):
        nxt = dict(q=q_ext(u), mrun=None) if u < len(units) else None
        for kt in range(nkt):
            if nxt is not None:
                score_tile(u, nxt, kt)
            if cur is not None:
                value_tile(u - 1, cur, kt)
        if cur is not None:
            finish(u - 1, cur, done)
        if nxt is not None:
            nxt.update(m=jnp.max(nxt["mrun"], axis=1, keepdims=True), lrun=None, acc=None)
        cur = nxt


def _mla_attn(qn, qpe, ckv_keys, kpe_keys, wukt, wuv, g, nb, seq_len, tq, pairs_per_step):
    t = qn.shape[0]
    nkeys = ckv_keys.shape[1]
    npairs = MLA_HEADS // 2
    ngrp = npairs // pairs_per_step
    nq = seq_len // tq
    wp = pairs_per_step * LANES
    if pairs_per_step % 2 == 0:
        wpe = wp // 2
        pe_map = lambda b, p, i: (b * nq + i, p)
    else:
        wpe = LANES
        pe_map = lambda b, p, i: (b * nq + i, p // 2)
    qmap = lambda b, p, i: (b * nq + i, p)
    return pl.pallas_call(
        functools.partial(_mla_attn_kernel, pairs_per_step=pairs_per_step),
        grid=(nb, ngrp, nq),
        in_specs=[pl.BlockSpec((tq, wp), qmap),
                  pl.BlockSpec((tq, wpe), pe_map),
                  pl.BlockSpec((None, nkeys, KV_LORA), lambda b, p, i: (b, 0, 0)),
                  pl.BlockSpec((None, nkeys, LANES), lambda b, p, i: (b, 0, 0)),
                  pl.BlockSpec((wp, KV_LORA), lambda b, p, i: (p, 0)),
                  pl.BlockSpec((KV_LORA, wp), lambda b, p, i: (0, p)),
                  pl.BlockSpec((tq, wp), qmap)],
        out_specs=pl.BlockSpec((tq, wp), qmap),
        out_shape=jax.ShapeDtypeStruct((t, MLA_WIDTH), BF16),
        scratch_shapes=[pltpu.VMEM((pairs_per_step, 2 * LANES, nkeys), BF16),
                        pltpu.VMEM((pairs_per_step, nkeys, LANES), BF16),
                        pltpu.VMEM((2, min(MLA_UNIT_ROWS, tq), nkeys), F32)],
        compiler_params=_cparams("arbitrary", "arbitrary", "arbitrary"),
        name="mla_attention",
    )(qn, qpe, ckv_keys, kpe_keys, wukt, wuv, g)


def _mla_attn_seq_kernel(qn_ref, qpe_ref, ckv_ref, kpe_ref, wukt_ref, wuv_ref, g_ref, o_ref,
                         kcat_scr, v_scr, s_scr, m_scr, half_scr):
    pair = pl.program_id(1)
    seq = qn_ref.shape[0]
    nkeys = kcat_scr.shape[-1]
    ru, kt_w = MLA_UNIT_ROWS, MLA_KEY_TILE
    nkt = nkeys // kt_w
    nrb = seq // ru
    c_exp = MLA_SCALE * LOG2E
    lane = _lane_iota((ru, LANES))

    ckv = ckv_ref[...]
    eye = jnp.where(_row_iota((LANES, LANES)) == _lane_iota((LANES, LANES)), 1.0, 0.0).astype(BF16)
    kcat_scr[0:LANES, :] = _dot_nt(wukt_ref[...], ckv).astype(BF16)
    kcat_scr[LANES:2 * LANES, :] = _dot_nt(eye, kpe_ref[...]).astype(BF16)
    v_scr[:, 0:LANES] = _dot(ckv, wuv_ref[...]).astype(BF16)
    v_scr[:, LANES:2 * LANES] = jnp.ones((nkeys, LANES), BF16)

    def rows_of(rb):
        return slice(rb * ru, (rb + 1) * ru) if isinstance(rb, int) else pl.ds(pl.multiple_of(rb * ru, ru), ru)

    def q_ext(rb, e):
        rows = rows_of(rb)
        qn = qn_ref[rows, :]
        qpe = qpe_ref[rows, :]
        nope_half = (lane < MLA_NOPE) if e == 0 else (lane >= MLA_NOPE)
        slot = 2 * (pair % 2) + e
        return jnp.concatenate([jnp.where(nope_half, qn, jnp.zeros_like(qn)),
                                jnp.where((lane >> 5) == slot, qpe, jnp.zeros_like(qpe))], axis=1)

    def phase(score, value):
        q = q_ext(*score) if score is not None else None
        mrun = acc = None
        for kt in range(nkt):
            cols = slice(kt * kt_w, (kt + 1) * kt_w)
            if score is not None:
                s = _dot(q, kcat_scr[:, cols])
                s_scr[score[1], :, cols] = s
                for j in range(kt_w // LANES):
                    t = s[:, j * LANES:(j + 1) * LANES]
                    mrun = t if mrun is None else jnp.maximum(mrun, t)
            if value is not None:
                mb = m_scr[value]
                p = jnp.concatenate(
                    [jnp.exp2((s_scr[value, :, kt * kt_w + j * LANES:kt * kt_w + (j + 1) * LANES] - mb) * c_exp)
                     for j in range(kt_w // LANES)], axis=1)
                pv = _dot(p.astype(BF16), v_scr[cols, :])
                acc = pv if acc is None else acc + pv
        if score is not None:
            m_scr[score[1]] = jnp.broadcast_to(jnp.max(mrun, axis=1, keepdims=True), (ru, LANES))
        return None if acc is None else acc[:, 0:LANES] / acc[:, LANES:2 * LANES]

    phase((0, 0), None)

    def row_block(rb, carry):
        half_scr[...] = phase((rb, 1), 0)
        out1 = phase((jnp.minimum(rb + 1, nrb - 1), 0), 1)
        rows = rows_of(rb)
        attn = jnp.where(lane < MLA_V, half_scr[...], out1)
        o_ref[rows, :] = (attn * g_ref[rows, :].astype(F32)).astype(o_ref.dtype)
        return carry

    lax.fori_loop(0, nrb, row_block, 0)


def _mla_attn_seq(qn, qpe, ckv_keys, kpe_keys, wukt, wuv, g, nb, seq_len):
    t = qn.shape[0]
    nkeys = ckv_keys.shape[1]
    assert seq_len % MLA_UNIT_ROWS == 0 and nkeys % MLA_KEY_TILE == 0
    qmap = lambda b, p: (b, p)
    return pl.pallas_call(
        _mla_attn_seq_kernel,
        grid=(nb, MLA_HEADS // 2),
        in_specs=[pl.BlockSpec((seq_len, LANES), qmap),
                  pl.BlockSpec((seq_len, LANES), lambda b, p: (b, p // 2)),
                  pl.BlockSpec((None, nkeys, KV_LORA), lambda b, p: (b, 0, 0)),
                  pl.BlockSpec((None, nkeys, LANES), lambda b, p: (b, 0, 0)),
                  pl.BlockSpec((LANES, KV_LORA), lambda b, p: (p, 0)),
                  pl.BlockSpec((KV_LORA, LANES), lambda b, p: (0, p)),
                  pl.BlockSpec((seq_len, LANES), qmap)],
        out_specs=pl.BlockSpec((seq_len, LANES), qmap),
        out_shape=jax.ShapeDtypeStruct((t, MLA_WIDTH), BF16),
        scratch_shapes=[pltpu.VMEM((2 * LANES, nkeys), BF16),
                        pltpu.VMEM((nkeys, 2 * LANES), BF16),
                        pltpu.VMEM((2, MLA_UNIT_ROWS, nkeys), F32),
                        pltpu.VMEM((2, MLA_UNIT_ROWS, LANES), F32),
                        pltpu.VMEM((MLA_UNIT_ROWS, LANES), F32)],
        compiler_params=_cparams("arbitrary", "arbitrary"),
        name="mla_attention_seq",
    )(qn, qpe, ckv_keys, kpe_keys, wukt, wuv, g)


def _mla_out_kernel(a_ref, w_ref, x_ref, gate_ref, fw_ref, o_ref):
    for rows in _row_splits(x_ref.shape[0]):
        xn = x_ref[rows, :] + gate_ref[...] * _dot(a_ref[rows, :], w_ref[...])
        ms = jnp.mean(xn * xn, axis=-1, keepdims=True)
        o_ref[rows, :] = (xn * lax.rsqrt(ms + EPS)) * fw_ref[...]


def _mla_out(a, w, x, gate, fw, seq_len, tm):
    t, d = x.shape
    per_seq = seq_len // tm if gate.shape[0] > 1 else None
    row = lambda i: (i, 0)
    const = lambda i: (0, 0)
    mod = (lambda i: (i // per_seq, 0, 0)) if per_seq else (lambda i: (0, 0, 0))
    return pl.pallas_call(
        _mla_out_kernel,
        grid=(t // tm,),
        in_specs=[pl.BlockSpec((tm, MLA_WIDTH), row), pl.BlockSpec(w.shape, const),
                  pl.BlockSpec((tm, d), row), pl.BlockSpec((None, 1, d), mod), pl.BlockSpec((1, d), const)],
        out_specs=pl.BlockSpec((tm, d), row),
        out_shape=jax.ShapeDtypeStruct((t, d), F32),
        compiler_params=_cparams("arbitrary"),
        name="mla_out_proj_final_norm",
    )(a, w, x, gate, fw)


def _rope_tables(length, dim):
    rows = length // GRID_W
    f32 = np.float32
    row = np.repeat(np.arange(rows), GRID_W).astype(f32)
    col = np.tile(np.arange(GRID_W), rows).astype(f32)
    nf = dim // 4
    inv = (f32(1.0) / np.power(f32(ROPE_BASE), np.arange(nf, dtype=f32) / f32(nf))).astype(f32)
    ar = row[:, None] * inv[None, :]
    ac = col[:, None] * inv[None, :]
    ang = np.concatenate([ar, ar, ac, ac], axis=-1).astype(f32)
    sign = np.tile(np.concatenate([-np.ones((nf,), f32), np.ones((nf,), f32)]), 2)
    reps = LANES // dim
    return (jnp.asarray(np.tile(np.cos(ang).astype(f32), (1, reps))),
            jnp.asarray(np.tile((np.sin(ang) * sign).astype(f32), (1, reps))))


PROJ_TM = 1024
CONV_TM = 512


def _group(x, seq_len, conds_rows, mods, wts, tabs_a, tabs_c, ctx_cache):
    nb, _, d = x.shape
    t = nb * seq_len
    latent = ctx_cache is not None
    xf = x.reshape(t, d)
    tm = PROJ_TM

    def mod_rows(layer):
        m = mods[layer][conds_rows]
        sh, sc, gt = jnp.split(m[:, None, :], 3, axis=-1)
        return sh, sc, gt

    sh, sc, gt = mod_rows(0)
    q, k, v, g, z, xbc, dt, *kv_t = _ab_in(xf, wts["norm_w"][0:1], sc, sh, wts["ab_w_in"], wts["ab_w_dt"],
                                           tabs_a if latent else None, seq_len, tm)
    xbc_c = _conv(xbc, wts["ab_conv_w"], wts["ab_conv_b"], seq_len, min(CONV_TM, seq_len))
    if latent:
        k_ctx, v_ctx, s_f0, s_b0, _, _ = ctx_cache
        attn = _attn_band(wts["ab_sink"], q, k, v, k_ctx, v_ctx, g, nb, seq_len)
    else:
        attn = _attn_ctx(wts["ab_sink"], q, k, v, g, seq_len)
    if latent:
        yf, yb, s_f, s_b = _ssd(xbc_c, dt, wts["ssd_par"], wts["ssd_dskip"], s_f0, s_b0, nb, seq_len)
    else:
        yf, s_f, s_b = _ssd_seq(xbc_c, dt, wts["ssd_par"], wts["ssd_dskip"], nb, seq_len)
        yb = None
    x1 = _ab_out(attn, yf, yb, z, wts["ab_gnorm_w"], wts["ab_w_out"], xf, gt, seq_len, tm)

    sh, sc, gt = mod_rows(1)
    qn, qpe, ckv, kpe, g1, ckv32, kpe32 = _mla_in(
        x1, wts["norm_w"][1:2], sc, sh, wts["mla_w_in"], wts["mla_q_norm_w"], wts["mla_kv_norm_w"],
        wts["mla_w_uq"], tabs_c if latent else None, seq_len, tm)
    ckv_keys = ckv.reshape(nb, seq_len, KV_LORA)
    kpe_keys = kpe.reshape(nb, seq_len, LANES)
    if latent:
        ckv_x, kpe_x = ctx_cache[4], ctx_cache[5]
        ckv_keys = jnp.concatenate([ckv_keys, ckv_x], axis=1)
        kpe_keys = jnp.concatenate([kpe_keys, kpe_x], axis=1)
        attn1 = _mla_attn_seq(qn, qpe, ckv_keys, kpe_keys, wts["mla_w_ukt"], wts["mla_w_uv"], g1, nb, seq_len)
    else:
        attn1 = _mla_attn(qn, qpe, ckv_keys, kpe_keys, wts["mla_w_ukt"], wts["mla_w_uv"], g1, nb, seq_len,
                          seq_len, MLA_HEADS // 2)
    y = _mla_out(attn1, wts["mla_w_out"], x1, gt, wts["final_norm_w"], seq_len, tm)
    return y.reshape(nb, seq_len, d), (kv_t, s_f, s_b, ckv32, kpe32)


def kernel(x_prompt, x_sample, cache_a_k, cache_a_v, state_ssd_fwd, state_ssd_bwd, cache_mla_ckv, cache_mla_kpe,
           c, c_ctx, ada_w, ada_b, norm_w, ab_w_in, ab_sink, ab_conv_w, ab_conv_b, ab_dt_bias, ab_a_log,
           ab_d_skip, ab_gnorm_w, ab_w_out, mla_w_in, mla_q_norm_w, mla_kv_norm_w, mla_w_uq, mla_w_ukv,
           mla_w_out, final_norm_w):
    batch, seq, d = x_prompt.shape
    dec_batch, dec_seq, _ = x_sample.shape
    assert ada_w.shape[0] == 2 and ab_w_in.shape[0] == 1 and mla_w_in.shape[0] == 1
    assert dec_batch + 1 <= SUBLANES

    conds = jnp.concatenate([c_ctx[None, :], c, jnp.zeros((SUBLANES - 1 - dec_batch, d), F32)], axis=0)
    mods = _modulation(conds, ada_w, ada_b)

    n_main = sum(AB_COLS[:-1])
    w_ab = ab_w_in[0].astype(BF16)
    w_dt = jnp.concatenate([ab_w_in[0][:, n_main:], jnp.zeros((d, LANES - 2 * SSD_HEADS), F32)], axis=1).astype(BF16)
    mw = mla_w_in[0]
    o_kpe = Q_LORA + KV_LORA
    w_mla = jnp.concatenate([mw[:, :o_kpe], jnp.tile(mw[:, o_kpe:o_kpe + MLA_ROPE], (1, LANES // MLA_ROPE)),
                             mw[:, o_kpe + MLA_ROPE:]], axis=1).astype(BF16)
    wuq = mla_w_uq[0].reshape(Q_LORA, MLA_HEADS, MLA_NOPE + MLA_ROPE)
    wuq = jnp.concatenate([wuq[:, :, :MLA_NOPE].reshape(Q_LORA, -1), wuq[:, :, MLA_NOPE:].reshape(Q_LORA, -1)],
                          axis=1).astype(BF16)
    wukv = mla_w_ukv[0].reshape(KV_LORA, MLA_HEADS, MLA_NOPE + MLA_V)
    wukt = wukv[:, :, :MLA_NOPE].reshape(KV_LORA, -1).T.astype(BF16)
    wuv = wukv[:, :, MLA_NOPE:].reshape(KV_LORA, -1).astype(BF16)
    pad_lanes = lambda r: jnp.concatenate([r.reshape(1, -1), jnp.zeros((1, LANES - r.size), F32)], axis=1)
    ssd_par = jnp.concatenate([pad_lanes(ab_dt_bias[0]), pad_lanes(ab_a_log[0]),
                               jnp.zeros((SUBLANES - 2, LANES), F32)], axis=0)
    wts = dict(
        norm_w=norm_w, ab_w_in=w_ab, ab_w_dt=w_dt, ab_sink=ab_sink[0],
        ab_conv_w=jnp.concatenate([ab_conv_w[0], jnp.zeros((SUBLANES - CONV_K, CONV_CH), F32)], axis=0),
        ab_conv_b=ab_conv_b[0][None, :], ssd_par=ssd_par,
        ssd_dskip=jnp.repeat(ab_d_skip[0], SSD_HEAD_DIM)[None, :],
        ab_gnorm_w=ab_gnorm_w[0][None, :], ab_w_out=ab_w_out[0].astype(BF16),
        mla_w_in=w_mla, mla_q_norm_w=mla_q_norm_w[0][None, :], mla_kv_norm_w=mla_kv_norm_w[0][None, :],
        mla_w_uq=wuq, mla_w_ukt=wukt, mla_w_uv=wuv, mla_w_out=mla_w_out[0].astype(BF16),
        final_norm_w=final_norm_w[None, :],
    )
    tabs_a = _rope_tables(dec_seq, A_HEAD_DIM)
    tabs_c = _rope_tables(dec_seq, MLA_ROPE)

    y_prompt, ((k_t, v_t), s_f, s_b, ckv32, kpe32) = _group(
        x_prompt, seq, jnp.zeros((1,), jnp.int32), mods, wts, tabs_a, tabs_c, None)

    past = cache_a_k.shape[2]
    ctx_cache = (cache_a_k[:, 0].reshape(dec_batch, past, A_KV_WIDTH),
                 cache_a_v[:, 0].reshape(dec_batch, past, A_KV_WIDTH),
                 state_ssd_fwd[:, 0], state_ssd_bwd[:, 0],
                 cache_mla_ckv[:, 0].astype(BF16),
                 jnp.tile(cache_mla_kpe[:, 0], (1, 1, LANES // MLA_ROPE)).astype(BF16))
    y_sample, _ = _group(x_sample, dec_seq, 1 + jnp.arange(dec_batch), mods, wts, tabs_a, tabs_c, ctx_cache)

    cache_layout = lambda u: u.reshape(batch, 1, A_KV_HEADS, A_HEAD_DIM, seq).transpose(0, 1, 4, 2, 3)
    return (y_prompt, y_sample,
            cache_layout(k_t), cache_layout(v_t),
            s_f[:, None], s_b[:, None],
            ckv32.reshape(batch, 1, seq, KV_LORA), kpe32[:, :MLA_ROPE].reshape(batch, 1, seq, MLA_ROPE))
```

```python
import functools
import math

import jax
import jax.numpy as jnp
import numpy as np
from jax import lax
from jax.experimental import pallas as pl
from jax.experimental.pallas import tpu as pltpu

F32 = jnp.float32
BF16 = jnp.bfloat16

LANES = 128
SUBLANES = 8
VMEM_LIMIT_BYTES = 56 * 1024 * 1024

GRID_W = 64
ROPE_BASE = 10000.0
EPS = 1e-6
NEG_INF = -1e30
WINDOW_BLK = 128
CHUNK = 128
A_HEADS, A_KV_HEADS, A_HEAD_DIM = 8, 2, 64
A_WIDTH = A_HEADS * A_HEAD_DIM
A_KV_WIDTH = A_KV_HEADS * A_HEAD_DIM
A_SCALE = A_HEAD_DIM ** -0.5
SSD_HEADS, SSD_HEAD_DIM, SSD_GROUPS, SSD_STATE = 16, 64, 2, 64
SSD_INNER = SSD_HEADS * SSD_HEAD_DIM
CONV_K = 5
BC_WIDTH = SSD_GROUPS * SSD_STATE
CONV_CH = SSD_INNER + 2 * BC_WIDTH
MLA_HEADS, MLA_NOPE, MLA_ROPE, MLA_V = 16, 64, 32, 64
Q_LORA, KV_LORA = 256, 128
MLA_WIDTH = MLA_HEADS * MLA_V
MLA_SCALE = (MLA_NOPE + MLA_ROPE) ** -0.5
LOG2E = 1.4426950408889634


def _cparams(*sem):
    return pltpu.CompilerParams(dimension_semantics=sem, vmem_limit_bytes=VMEM_LIMIT_BYTES)


def _silu(x):
    return x * (1.0 / (1.0 + jnp.exp(-x)))


def _dot(a, b):
    return jnp.dot(a, b, preferred_element_type=F32, precision=lax.Precision.DEFAULT)


def _dot_nt(a, b):
    return lax.dot_general(a, b, (((1,), (1,)), ((), ())), preferred_element_type=F32,
                           precision=lax.Precision.DEFAULT)


def _row_splits(m, parts=2):
    step = m // parts
    return [slice(r * step, (r + 1) * step) for r in range(parts)]


def _lane_iota(shape):
    return lax.broadcasted_iota(jnp.int32, shape, len(shape) - 1)


def _row_iota(shape):
    return lax.broadcasted_iota(jnp.int32, shape, len(shape) - 2)


def _mod_kernel(cond_ref, w_ref, b_ref, o_ref):
    s = _silu(cond_ref[...])
    o_ref[...] = _dot(s.astype(BF16), w_ref[...].astype(BF16)) + b_ref[...]


def _modulation(conds, ada_w, ada_b):
    depth, d, d3 = ada_w.shape
    tn = 768
    return pl.pallas_call(
        _mod_kernel,
        grid=(depth, d3 // tn),
        in_specs=[pl.BlockSpec((SUBLANES, d), lambda l, j: (0, 0)),
                  pl.BlockSpec((None, d, tn), lambda l, j: (l, 0, j)),
                  pl.BlockSpec((None, 1, tn), lambda l, j: (l, 0, j))],
        out_specs=pl.BlockSpec((None, SUBLANES, tn), lambda l, j: (l, 0, j)),
        out_shape=jax.ShapeDtypeStruct((depth, SUBLANES, d3), F32),
        compiler_params=_cparams("arbitrary", "arbitrary"),
        name="modulation",
    )(conds, ada_w, ada_b.reshape(depth, 1, d3))


def _norm_mod(x, nw, scale, shift):
    ms = jnp.mean(x * x, axis=-1, keepdims=True)
    y = (x * lax.rsqrt(ms + EPS)) * nw
    return y * (1.0 + scale) + shift


def _rope(x, cos, sin_signed, half_period):
    outs = []
    first = (_lane_iota((x.shape[0], LANES)) & (2 * half_period - 1)) < half_period
    for j in range(x.shape[1] // LANES):
        xj = x[:, j * LANES:(j + 1) * LANES]
        up = pltpu.roll(xj, LANES - half_period, 1)
        dn = pltpu.roll(xj, half_period, 1)
        outs.append(xj * cos + jnp.where(first, up, dn) * sin_signed)
    return outs[0] if len(outs) == 1 else jnp.concatenate(outs, axis=1)


AB_COLS = (A_WIDTH, A_KV_WIDTH, A_KV_WIDTH, A_WIDTH, SSD_INNER, CONV_CH, LANES)


def _ab_in_kernel(*refs, rope, seq_len):
    if rope:
        (x_ref, nw_ref, sc_ref, sh_ref, w_ref, wdt_ref, cos_ref, sin_ref,
         q_ref, k_ref, v_ref, g_ref, z_ref, xbc_ref, dt_ref) = refs
        kvt_refs = {}
    else:
        (x_ref, nw_ref, sc_ref, sh_ref, w_ref, wdt_ref,
         q_ref, k_ref, v_ref, g_ref, z_ref, xbc_ref, dt_ref, kt_ref, vt_ref) = refs
        kvt_refs = {1: kt_ref, 2: vt_ref}
    offs = np.concatenate([[0], np.cumsum(AB_COLS)])
    outs = (q_ref, k_ref, v_ref, g_ref, z_ref, xbc_ref, dt_ref)
    for rows in _row_splits(x_ref.shape[0]):
        h = _norm_mod(x_ref[rows, :], nw_ref[...], sc_ref[...], sh_ref[...]).astype(BF16)
        for i, o_ref in enumerate(outs):
            last = i == len(outs) - 1
            y = _dot(h, wdt_ref[...] if last else w_ref[:, int(offs[i]):int(offs[i + 1])])
            if rope and i in (0, 1):
                y = _rope(y, cos_ref[rows, :], sin_ref[rows, :], A_HEAD_DIM // 4)
            o_ref[rows, :] = y.astype(o_ref.dtype)
            if i in kvt_refs:
                for s in range(rows.start // seq_len, rows.stop // seq_len):
                    kvt_refs[i][s] = y[s * seq_len - rows.start:(s + 1) * seq_len - rows.start, :].T


def _ab_in(x, nw, scale, shift, w, wdt, rope_tabs, seq_len, tm):
    t, d = x.shape
    per_seq = seq_len // tm if scale.shape[0] > 1 else None
    row = lambda i: (i, 0)
    mod = (lambda i: (i // per_seq, 0, 0)) if per_seq else (lambda i: (0, 0, 0))
    const = lambda i: (0, 0)
    in_specs = [pl.BlockSpec((tm, d), row),
                pl.BlockSpec((1, d), const),
                pl.BlockSpec((None, 1, d), mod),
                pl.BlockSpec((None, 1, d), mod),
                pl.BlockSpec(w.shape, const),
                pl.BlockSpec(wdt.shape, const)]
    args = [x, nw, scale, shift, w, wdt]
    if rope_tabs is not None:
        nt = seq_len // tm
        pos = lambda i: (i % nt, 0)
        in_specs += [pl.BlockSpec((tm, LANES), pos), pl.BlockSpec((tm, LANES), pos)]
        args += list(rope_tabs)
    dts = (BF16, F32, F32, BF16, BF16, BF16, F32)
    out_specs = [pl.BlockSpec((tm, c), row) for c in AB_COLS]
    out_shape = [jax.ShapeDtypeStruct((t, c), dt) for c, dt in zip(AB_COLS, dts)]
    if rope_tabs is None:
        assert (tm // 2) % seq_len == 0
        spt = tm // seq_len
        out_specs += [pl.BlockSpec((spt, A_KV_WIDTH, seq_len), lambda i: (i, 0, 0))] * 2
        out_shape += [jax.ShapeDtypeStruct((t // seq_len, A_KV_WIDTH, seq_len), F32)] * 2
    return pl.pallas_call(
        functools.partial(_ab_in_kernel, rope=rope_tabs is not None, seq_len=seq_len),
        grid=(t // tm,),
        in_specs=in_specs,
        out_specs=out_specs,
        out_shape=out_shape,
        compiler_params=_cparams("arbitrary"),
        name="ab_in_proj",
    )(*args)


def _conv_kernel(prev_ref, main_ref, next_ref, w_ref, b_ref, o_ref, *, tiles_per_seq):
    i = pl.program_id(0) % tiles_per_seq
    tm = main_ref.shape[0]
    prev = jnp.where(i > 0, prev_ref[...].astype(F32), 0.0)
    nxt = jnp.where(i < tiles_per_seq - 1, next_ref[...].astype(F32), 0.0)
    ext = jnp.concatenate([prev, main_ref[...].astype(F32), nxt], axis=0)
    first = SUBLANES - CONV_K // 2
    acc = b_ref[...] + ext[first:first + tm] * w_ref[0:1, :]
    for j in range(1, CONV_K):
        acc = acc + ext[first + j:first + j + tm] * w_ref[j:j + 1, :]
    o_ref[...] = _silu(acc).astype(o_ref.dtype)


def _conv(xbc, w, b, seq_len, tm):
    t, c = xbc.shape
    tps = seq_len // tm
    hb = tm // SUBLANES
    nblk8 = t // SUBLANES
    return pl.pallas_call(
        functools.partial(_conv_kernel, tiles_per_seq=tps),
        grid=(t // tm,),
        in_specs=[pl.BlockSpec((SUBLANES, c), lambda i: (jnp.maximum(i * hb - 1, 0), 0)),
                  pl.BlockSpec((tm, c), lambda i: (i, 0)),
                  pl.BlockSpec((SUBLANES, c), lambda i: (jnp.minimum((i + 1) * hb, nblk8 - 1), 0)),
                  pl.BlockSpec((SUBLANES, c), lambda i: (0, 0)),
                  pl.BlockSpec((1, c), lambda i: (0, 0))],
        out_specs=pl.BlockSpec((tm, c), lambda i: (i, 0)),
        out_shape=jax.ShapeDtypeStruct((t, c), BF16),
        compiler_params=_cparams("arbitrary"),
        name="ssd_conv",
    )(xbc, xbc, xbc, w, b)


def _ssd_init(h0_ref, h_scr, has_h0):
    if not has_h0:
        h_scr[...] = jnp.zeros_like(h_scr)
        return
    for pr in range(SSD_HEADS // 2):
        both = jnp.concatenate([h0_ref[2 * pr], h0_ref[2 * pr + 1]], axis=0)
        both = jnp.concatenate([both, jnp.zeros_like(both)], axis=1)
        st = both.T[0:SSD_STATE]
        lo_st = _lane_iota(st.shape) < SSD_HEAD_DIM
        h_scr[2 * pr] = jnp.where(lo_st, st, 0.0)
        h_scr[2 * pr + 1] = jnp.where(lo_st, 0.0, st)


def _ssd_final(h_scr, hout_ref):
    for pr in range(SSD_HEADS // 2):
        st = h_scr[2 * pr] + h_scr[2 * pr + 1]
        st = jnp.concatenate([st, jnp.zeros_like(st)], axis=0).T
        hout_ref[2 * pr] = st[0:SSD_HEAD_DIM, 0:SSD_STATE]
        hout_ref[2 * pr + 1] = st[SSD_HEAD_DIM:2 * SSD_HEAD_DIM, 0:SSD_STATE]


def _ssd_prelude(b_ref, c_ref, dt_ref, par_ref, direction):
    lane = _lane_iota((CHUNK, LANES))
    rowi = _row_iota((CHUNK, LANES))
    dtp_in = dt_ref[...] + par_ref[0:1, :]
    dtp = jnp.maximum(dtp_in, 0.0) + jnp.log(1.0 + jnp.exp(-jnp.abs(dtp_in)))
    la = dtp * (-jnp.exp(par_ref[1:2, :]))
    causal = (rowi >= lane) if direction == 0 else (rowi <= lane)
    tri = jnp.where(causal, 1.0, 0.0).astype(BF16)
    p1 = la.astype(BF16)
    r1 = la - p1.astype(F32)
    p2 = r1.astype(BF16)
    p3 = (r1 - p2.astype(F32)).astype(BF16)
    cum = _dot(tri, p1) + _dot(tri, p2) + _dot(tri, p3)
    cum_t = cum.T
    dtp_t = dtp.T
    end = CHUNK - 1 if direction == 0 else 0
    w_t = jnp.exp(cum_t[:, end:end + 1] - cum_t) * dtp_t
    lrow_t = cum_t - jnp.log(dtp_t)
    bm = b_ref[...]
    cm = c_ref[...]
    cb, cmask = [], []
    for g in range(SSD_GROUPS):
        in_g = (lane >= g * SSD_STATE) & (lane < (g + 1) * SSD_STATE)
        cg = jnp.where(in_g, cm, jnp.zeros_like(cm))
        cmask.append(cg.astype(F32))
        cb.append(_dot_nt(cg, bm))
    return dict(cum=cum, lrow_t=lrow_t, w_t=w_t, causal=causal, end=end,
                bm_t=bm.astype(F32).T, cb=cb, cmask=cmask, lo_half=lane < SSD_HEAD_DIM)


def _ssd_head_pair(pre, pr, x_ref, dskip_ref, y_ref, h_scr, direction, accumulate=False):
    hp = SSD_HEADS // SSD_GROUPS
    cum, lrow_t, w_t, end = pre["cum"], pre["lrow_t"], pre["w_t"], pre["end"]
    lo_half = pre["lo_half"]
    xp = x_ref[:, pr * LANES:(pr + 1) * LANES]
    y_pair = None
    for e in range(2):
        h = 2 * pr + e
        g = h // hp
        col = direction * SSD_HEADS + h
        xe = jnp.where(lo_half if e == 0 else ~lo_half, xp, jnp.zeros_like(xp))
        cumcol = jnp.broadcast_to(cum[:, col:col + 1], (CHUNK, LANES))
        diff = cumcol - lrow_t[col:col + 1, :]
        sc = pre["cb"][g] * jnp.exp(jnp.where(pre["causal"], diff, NEG_INF))
        ce = pre["cmask"][g] * jnp.exp(cumcol)
        hs = h_scr[h]
        hsb = hs.astype(BF16)
        zpad = jnp.zeros_like(hsb)
        bw = (pre["bm_t"][g * SSD_STATE:(g + 1) * SSD_STATE, :] * w_t[col:col + 1, :]).astype(BF16)
        a = jnp.concatenate([jnp.concatenate([sc.astype(BF16), ce.astype(BF16)], axis=1),
                             jnp.concatenate([bw, jnp.zeros_like(bw)], axis=1)], axis=0)
        wmat = jnp.concatenate([xe, hsb, zpad] if g == 0 else [xe, zpad, hsb], axis=0)
        res = _dot(a, wmat)
        ye = res[0:CHUNK]
        y_pair = ye if y_pair is None else y_pair + ye
        cdec = jnp.exp(cumcol[end:end + 1, :])
        h_scr[h] = hs * cdec + res[CHUNK:CHUNK + SSD_STATE]
    if direction == 0:
        y_pair = y_pair + dskip_ref[:, pr * LANES:(pr + 1) * LANES] * xp.astype(F32)
    if accumulate:
        y_pair = y_pair + y_ref[:, pr * LANES:(pr + 1) * LANES]
    y_ref[:, pr * LANES:(pr + 1) * LANES] = y_pair.astype(y_ref.dtype)


def _ssd_kernel(xf_ref, bf_ref, cf_ref, dtf_ref, xb_ref, bb_ref, cb_ref, dtb_ref, par_ref, dskip_ref,
                h0f_ref, h0b_ref, yf_ref, yb_ref, houtf_ref, houtb_ref, hf_scr, hb_scr, *, nc, cps, has_h0):
    ci = pl.program_id(1)

    @pl.when(ci == 0)
    def _init():
        _ssd_init(h0f_ref, hf_scr, has_h0)
        _ssd_init(h0b_ref, hb_scr, has_h0)

    for j in range(cps):
        rf = pl.ds(j * CHUNK, CHUNK)
        rb = pl.ds((cps - 1 - j) * CHUNK, CHUNK)
        pre_f = _ssd_prelude(bf_ref.at[rf], cf_ref.at[rf], dtf_ref.at[rf], par_ref, 0)
        pre_b = _ssd_prelude(bb_ref.at[rb], cb_ref.at[rb], dtb_ref.at[rb], par_ref, 1)
        for pr in range(SSD_HEADS // 2):
            _ssd_head_pair(pre_f, pr, xf_ref.at[rf], dskip_ref, yf_ref.at[rf], hf_scr, 0)
            _ssd_head_pair(pre_b, pr, xb_ref.at[rb], dskip_ref, yb_ref.at[rb], hb_scr, 1)

    @pl.when(ci == nc - 1)
    def _final():
        _ssd_final(hf_scr, houtf_ref)
        _ssd_final(hb_scr, houtb_ref)


def _ssd(xbc, dt, par, dskip, h0f, h0b, nb, seq_len):
    t = xbc.shape[0]
    cps = 2 if (seq_len // CHUNK) % 2 == 0 else 1
    blk = cps * CHUNK
    nc = seq_len // blk
    fwd = lambda b, c: b * nc + c
    bwd = lambda b, c: b * nc + nc - 1 - c
    xcols = SSD_INNER // LANES
    has_h0 = h0f is not None
    if not has_h0:
        h0f = h0b = jnp.zeros((1, SSD_HEADS, SSD_HEAD_DIM, SSD_STATE), F32)
    h0_map = (lambda b, c: (b, 0, 0, 0)) if has_h0 else (lambda b, c: (0, 0, 0, 0))
    st_shape = (None, SSD_HEADS, SSD_HEAD_DIM, SSD_STATE)
    carry_shape = (None, SSD_HEADS, SSD_STATE, LANES)
    per_b = lambda b, c: (b, 0, 0, 0)

    def chunk_specs(cidx):
        return [pl.BlockSpec((blk, SSD_INNER), lambda b, c: (cidx(b, c), 0)),
                pl.BlockSpec((blk, LANES), lambda b, c: (cidx(b, c), xcols)),
                pl.BlockSpec((blk, LANES), lambda b, c: (cidx(b, c), xcols + 1)),
                pl.BlockSpec((blk, LANES), lambda b, c: (cidx(b, c), 0))]

    outs = pl.pallas_call(
        functools.partial(_ssd_kernel, nc=nc, cps=cps, has_h0=has_h0),
        grid=(nb, nc),
        in_specs=chunk_specs(fwd) + chunk_specs(bwd) + [
            pl.BlockSpec((SUBLANES, LANES), lambda b, c: (0, 0)),
            pl.BlockSpec((1, SSD_INNER), lambda b, c: (0, 0)),
            pl.BlockSpec(st_shape, h0_map), pl.BlockSpec(st_shape, h0_map)],
        out_specs=[pl.BlockSpec((blk, SSD_INNER), lambda b, c: (fwd(b, c), 0)),
                   pl.BlockSpec((blk, SSD_INNER), lambda b, c: (bwd(b, c), 0)),
                   pl.BlockSpec(st_shape, per_b), pl.BlockSpec(st_shape, per_b),
                   pl.BlockSpec(carry_shape, lambda b, c: (0, 0, 0, 0)),
                   pl.BlockSpec(carry_shape, lambda b, c: (0, 0, 0, 0))],
        out_shape=[jax.ShapeDtypeStruct((t, SSD_INNER), BF16), jax.ShapeDtypeStruct((t, SSD_INNER), BF16),
                   jax.ShapeDtypeStruct((nb, SSD_HEADS, SSD_HEAD_DIM, SSD_STATE), F32),
                   jax.ShapeDtypeStruct((nb, SSD_HEADS, SSD_HEAD_DIM, SSD_STATE), F32),
                   jax.ShapeDtypeStruct((1, SSD_HEADS, SSD_STATE, LANES), F32),
                   jax.ShapeDtypeStruct((1, SSD_HEADS, SSD_STATE, LANES), F32)],
        compiler_params=_cparams("arbitrary", "arbitrary"),
        name="ssd_scan_bidir",
    )(xbc, xbc, xbc, dt, xbc, xbc, xbc, dt, par, dskip, h0f, h0b)
    return outs[:4]


def _ssd_seq_kernel(x_ref, b_ref, c_ref, dt_ref, par_ref, dskip_ref, y_ref, houtf_ref, houtb_ref,
                    hf_scr, hb_scr, yacc_scr, *, nc):
    _ssd_init(None, hf_scr, False)
    _ssd_init(None, hb_scr, False)
    for c in range(nc):
        rf = pl.ds(c * CHUNK, CHUNK)
        rb = pl.ds((nc - 1 - c) * CHUNK, CHUNK)
        pre_f = _ssd_prelude(b_ref.at[rf], c_ref.at[rf], dt_ref.at[rf], par_ref, 0)
        pre_b = _ssd_prelude(b_ref.at[rb], c_ref.at[rb], dt_ref.at[rb], par_ref, 1)
        for pr in range(SSD_HEADS // 2):
            _ssd_head_pair(pre_f, pr, x_ref.at[rf], dskip_ref, yacc_scr.at[rf], hf_scr, 0, accumulate=c > nc - 1 - c)
            _ssd_head_pair(pre_b, pr, x_ref.at[rb], dskip_ref, yacc_scr.at[rb], hb_scr, 1, accumulate=c >= nc - 1 - c)
    y_ref[...] = yacc_scr[...].astype(y_ref.dtype)
    _ssd_final(hf_scr, houtf_ref)
    _ssd_final(hb_scr, houtb_ref)


def _ssd_seq(xbc, dt, par, dskip, nb, seq_len):
    t = xbc.shape[0]
    nc = seq_len // CHUNK
    xcols = SSD_INNER // LANES
    st_shape = (None, SSD_HEADS, SSD_HEAD_DIM, SSD_STATE)
    carry_shape = (None, SSD_HEADS, SSD_STATE, LANES)
    per_b = lambda b: (b, 0, 0, 0)
    one_block = lambda b: (0, 0, 0, 0)
    return pl.pallas_call(
        functools.partial(_ssd_seq_kernel, nc=nc),
        grid=(nb,),
        in_specs=[pl.BlockSpec((seq_len, SSD_INNER), lambda b: (b, 0)),
                  pl.BlockSpec((seq_len, LANES), lambda b: (b, xcols)),
                  pl.BlockSpec((seq_len, LANES), lambda b: (b, xcols + 1)),
                  pl.BlockSpec((seq_len, LANES), lambda b: (b, 0)),
                  pl.BlockSpec((SUBLANES, LANES), lambda b: (0, 0)),
                  pl.BlockSpec((1, SSD_INNER), lambda b: (0, 0))],
        out_specs=[pl.BlockSpec((seq_len, SSD_INNER), lambda b: (b, 0)),
                   pl.BlockSpec(st_shape, per_b), pl.BlockSpec(st_shape, per_b),
                   pl.BlockSpec(carry_shape, one_block), pl.BlockSpec(carry_shape, one_block)],
        out_shape=[jax.ShapeDtypeStruct((t, SSD_INNER), BF16),
                   jax.ShapeDtypeStruct((nb, SSD_HEADS, SSD_HEAD_DIM, SSD_STATE), F32),
                   jax.ShapeDtypeStruct((nb, SSD_HEADS, SSD_HEAD_DIM, SSD_STATE), F32),
                   jax.ShapeDtypeStruct((1, SSD_HEADS, SSD_STATE, LANES), F32),
                   jax.ShapeDtypeStruct((1, SSD_HEADS, SSD_STATE, LANES), F32)],
        scratch_shapes=[pltpu.VMEM((seq_len, SSD_INNER), F32)],
        compiler_params=_cparams("arbitrary"),
        name="ssd_scan_seq",
    )(xbc, xbc, xbc, dt, par, dskip)[:3]


def _kv_variants(k, v):
    lane = _lane_iota(k.shape)
    lo = lane < A_HEAD_DIM
    k_sw = pltpu.roll(k, A_HEAD_DIM, 1)
    v_sw = pltpu.roll(v, A_HEAD_DIM, 1)
    ks, vs = {}, {}
    for j in range(A_KV_HEADS):
        for e in range(2):
            src_k, src_v = (k, v) if j == e else (k_sw, v_sw)
            half = lo if e == 0 else ~lo
            ks[j, e] = jnp.where(half, src_k, 0.0).astype(BF16)
            vs[j, e] = src_v.astype(BF16)
    return ks, vs


GQA_KEY_TILE = 256


def _gqa_heads(q_ref, g_ref, o_ref, sink_ref, ks, vs, valid_tile, s_scr):
    m = q_ref.shape[0]
    assert A_HEADS // A_KV_HEADS == 4
    lo = _lane_iota((m, LANES)) < A_HEAD_DIM
    upper_rows = _row_iota((2 * m, 1)) >= m
    scale = jnp.asarray(A_SCALE, BF16)
    units = [(j, e) for j in range(A_KV_HEADS) for e in range(2)]
    nkeys = ks[0, 0].shape[0]
    kt_w = GQA_KEY_TILE
    nkt = nkeys // kt_w
    rows_of = {}

    def q_rows(j):
        if j not in rows_of:
            rows_of[j] = jnp.concatenate([q_ref[:, 2 * j * LANES:(2 * j + 1) * LANES],
                                          q_ref[:, (2 * j + 1) * LANES:(2 * j + 2) * LANES]], axis=0) * scale
        return rows_of[j]

    def score_tile(u, st, kt):
        j, e = units[u]
        cols = slice(kt * kt_w, (kt + 1) * kt_w)
        s = _dot_nt(q_rows(j), ks[j, e][cols, :])
        ok = valid_tile(kt)
        if ok is not None:
            s = jnp.where(ok, s, NEG_INF)
        s_scr[u % 2, :, cols] = s
        for c in range(kt_w // LANES):
            t = s[:, c * LANES:(c + 1) * LANES]
            st["mrun"] = t if st["mrun"] is None else jnp.maximum(st["mrun"], t)

    def value_tile(u, st, kt):
        j, e = units[u]
        cols = slice(kt * kt_w, (kt + 1) * kt_w)
        p = jnp.exp(s_scr[u % 2, :, cols] - st["m"])
        for c in range(kt_w // LANES):
            t = p[:, c * LANES:(c + 1) * LANES]
            st["lrun"] = t if st["lrun"] is None else st["lrun"] + t
        pv = _dot(p.astype(BF16), vs[j, e][cols, :])
        st["acc"] = pv if st["acc"] is None else st["acc"] + pv

    outs = {}
    cur = None
    for u in range(len(units) + 1):
        nxt = dict(mrun=None) if u < len(units) else None
        for kt in range(nkt):
            if nxt is not None:
                score_tile(u, nxt, kt)
            if cur is not None:
                value_tile(u - 1, cur, kt)
        if cur is not None:
            den = jnp.sum(cur["lrun"], axis=1, keepdims=True) + jnp.exp(cur["sink"] - cur["m"])
            outs[units[u - 1]] = cur["acc"] / den
        if nxt is not None:
            j, e = units[u]
            sink = jnp.where(upper_rows, sink_ref[4 * j + 2 + e], sink_ref[4 * j + e])
            nxt.update(sink=sink, m=jnp.maximum(jnp.max(nxt["mrun"], axis=1, keepdims=True), sink),
                       lrun=None, acc=None)
        cur = nxt
    for pr in range(A_HEADS // 2):
        j, half = pr // 2, pr % 2
        r = slice(half * m, (half + 1) * m)
        attn = jnp.where(lo, outs[j, 0][r], outs[j, 1][r])
        gate = g_ref[:, pr * LANES:(pr + 1) * LANES].astype(F32)
        o_ref[:, pr * LANES:(pr + 1) * LANES] = (attn * _silu(gate)).astype(o_ref.dtype)


def _attn_ctx_kernel(sink_ref, q_ref, k_ref, v_ref, g_ref, o_ref, s_scr):
    ks, vs = _kv_variants(k_ref[...], v_ref[...])
    _gqa_heads(q_ref, g_ref, o_ref, sink_ref, ks, vs, lambda kt: None, s_scr)


def _attn_ctx(sink, q, k, v, g, seq_len):
    t = q.shape[0]
    row = lambda b: (b, 0)
    return pl.pallas_call(
        _attn_ctx_kernel,
        grid=(t // seq_len,),
        in_specs=[pl.BlockSpec(memory_space=pltpu.SMEM),
                  pl.BlockSpec((seq_len, A_WIDTH), row),
                  pl.BlockSpec((seq_len, A_KV_WIDTH), row),
                  pl.BlockSpec((seq_len, A_KV_WIDTH), row),
                  pl.BlockSpec((seq_len, A_WIDTH), row)],
        out_specs=pl.BlockSpec((seq_len, A_WIDTH), row),
        out_shape=jax.ShapeDtypeStruct((t, A_WIDTH), BF16),
        scratch_shapes=[pltpu.VMEM((2, 2 * seq_len, seq_len), F32)],
        compiler_params=_cparams("arbitrary"),
        name="swa_context_attention",
    )(sink, q, k, v, g)


BAND_Q = 256


def _attn_band_kernel(sink_ref, q_ref, kp_ref, kc_ref, kn_ref, vp_ref, vc_ref, vn_ref, kx_ref, vx_ref, g_ref,
                      o_ref, s_scr, *, nsteps):
    n = pl.program_id(1)
    tq = q_ref.shape[0]
    blk = WINDOW_BLK
    k = jnp.concatenate([kp_ref[...], kc_ref[...], kn_ref[...], kx_ref[...]], axis=0)
    v = jnp.concatenate([vp_ref[...], vc_ref[...], vn_ref[...], vx_ref[...]], axis=0)
    nloc = tq + 2 * blk
    qi = _row_iota((2 * tq, GQA_KEY_TILE)) & (tq - 1)

    def valid_tile(kt):
        if kt * GQA_KEY_TILE >= nloc:
            return None
        col = _lane_iota((2 * tq, GQA_KEY_TILE)) + kt * GQA_KEY_TILE
        rel = col - blk - qi
        return ((rel >= -blk) & (rel <= blk) & ((col >= blk) | (n > 0)) & ((col < nloc - blk) | (n < nsteps - 1)))

    assert nloc % GQA_KEY_TILE == 0
    ks, vs = _kv_variants(k, v)
    _gqa_heads(q_ref, g_ref, o_ref, sink_ref, ks, vs, valid_tile, s_scr)


def _attn_band(sink, q, k, v, k_ctx, v_ctx, g, nb, seq_len):
    t = q.shape[0]
    blk = WINDOW_BLK
    tq = BAND_Q
    assert tq & (tq - 1) == 0 and tq % blk == 0 and seq_len % tq == 0
    per = tq // blk
    nblk = seq_len // blk
    nsteps = seq_len // tq
    nctx = k_ctx.shape[1]
    cur = lambda b, n: (b * nsteps + n, 0)
    prv = lambda b, n: (b * nblk + jnp.maximum(n * per - 1, 0), 0)
    nxt = lambda b, n: (b * nblk + jnp.minimum((n + 1) * per, nblk - 1), 0)
    ctx = lambda b, n: (b, 0, 0)
    edge = lambda f: pl.BlockSpec((blk, A_KV_WIDTH), f)
    mid = pl.BlockSpec((tq, A_KV_WIDTH), cur)
    return pl.pallas_call(
        functools.partial(_attn_band_kernel, nsteps=nsteps),
        grid=(nb, nsteps),
        in_specs=[pl.BlockSpec(memory_space=pltpu.SMEM),
                  pl.BlockSpec((tq, A_WIDTH), cur),
                  edge(prv), mid, edge(nxt),
                  edge(prv), mid, edge(nxt),
                  pl.BlockSpec((None, nctx, A_KV_WIDTH), ctx),
                  pl.BlockSpec((None, nctx, A_KV_WIDTH), ctx),
                  pl.BlockSpec((tq, A_WIDTH), cur)],
        out_specs=pl.BlockSpec((tq, A_WIDTH), cur),
        out_shape=jax.ShapeDtypeStruct((t, A_WIDTH), BF16),
        scratch_shapes=[pltpu.VMEM((2, 2 * tq, tq + 2 * blk + nctx), F32)],
        compiler_params=_cparams("arbitrary", "arbitrary"),
        name="swa_banded_attention",
    )(sink, q, k, k, k, v, v, v, k_ctx, v_ctx, g)


def _ab_out_kernel(*refs, two_y):
    if two_y:
        a_ref, yf_ref, yb_ref, z_ref, gnw_ref, w_ref, x_ref, gate_ref, o_ref = refs
    else:
        a_ref, yf_ref, z_ref, gnw_ref, w_ref, x_ref, gate_ref, o_ref = refs
    for rows in _row_splits(x_ref.shape[0]):
        y = yf_ref[rows, :].astype(F32)
        if two_y:
            y = y + yb_ref[rows, :].astype(F32)
        y = y * _silu(z_ref[rows, :].astype(F32))
        ms = jnp.mean(y * y, axis=-1, keepdims=True)
        s = ((y * lax.rsqrt(ms + EPS)) * gnw_ref[...]).astype(BF16)
        out = _dot(a_ref[rows, :], w_ref[0:A_WIDTH, :]) + _dot(s, w_ref[A_WIDTH:, :])
        o_ref[rows, :] = x_ref[rows, :] + gate_ref[...] * out


def _ab_out(a, yf, yb, z, gnw, w, x, gate, seq_len, tm):
    t, d = x.shape
    per_seq = seq_len // tm if gate.shape[0] > 1 else None
    row = lambda i: (i, 0)
    const = lambda i: (0, 0)
    mod = (lambda i: (i // per_seq, 0, 0)) if per_seq else (lambda i: (0, 0, 0))
    ys = [yf] if yb is None else [yf, yb]
    return pl.pallas_call(
        functools.partial(_ab_out_kernel, two_y=yb is not None),
        grid=(t // tm,),
        in_specs=[pl.BlockSpec((tm, A_WIDTH), row)] + [pl.BlockSpec((tm, SSD_INNER), row)] * (len(ys) + 1) + [
                  pl.BlockSpec((1, SSD_INNER), const),
                  pl.BlockSpec(w.shape, const),
                  pl.BlockSpec((tm, d), row),
                  pl.BlockSpec((None, 1, d), mod)],
        out_specs=pl.BlockSpec((tm, d), row),
        out_shape=jax.ShapeDtypeStruct((t, d), F32),
        compiler_params=_cparams("arbitrary"),
        name="ab_out_proj",
    )(a, *ys, z, gnw, w, x, gate)


MLA_COLS = (Q_LORA, KV_LORA, LANES, MLA_WIDTH)


def _mla_in_kernel(*refs, rope):
    if rope:
        (x_ref, nw_ref, sc_ref, sh_ref, w_ref, qnw_ref, kvnw_ref, wuq_ref, cos_ref, sin_ref,
         qn_ref, qpe_ref, ckv_ref, kpe_ref, g_ref, ckv32_ref, kpe32_ref) = refs
    else:
        (x_ref, nw_ref, sc_ref, sh_ref, w_ref, qnw_ref, kvnw_ref, wuq_ref,
         qn_ref, qpe_ref, ckv_ref, kpe_ref, g_ref, ckv32_ref, kpe32_ref) = refs
    offs = [int(o) for o in np.concatenate([[0], np.cumsum(MLA_COLS)])]
    nope_w = MLA_HEADS * MLA_NOPE

    def rms(u, w):
        return (u * lax.rsqrt(jnp.mean(u * u, axis=-1, keepdims=True) + EPS)) * w

    for rows in _row_splits(x_ref.shape[0]):
        h = _norm_mod(x_ref[rows, :], nw_ref[...], sc_ref[...], sh_ref[...]).astype(BF16)
        cq = _dot(h, w_ref[:, offs[0]:offs[1]])
        ckv = _dot(h, w_ref[:, offs[1]:offs[2]])
        kpe = _dot(h, w_ref[:, offs[2]:offs[3]])
        g_ref[rows, :] = _silu(_dot(h, w_ref[:, offs[3]:offs[4]])).astype(g_ref.dtype)
        cqn = rms(cq, qnw_ref[...]).astype(BF16)
        qn_ref[rows, :] = _dot(cqn, wuq_ref[:, 0:nope_w]).astype(qn_ref.dtype)
        qpe = _dot(cqn, wuq_ref[:, nope_w:])
        if rope:
            qpe = _rope(qpe, cos_ref[rows, :], sin_ref[rows, :], MLA_ROPE // 4)
            kpe = _rope(kpe, cos_ref[rows, :], sin_ref[rows, :], MLA_ROPE // 4)
        qpe_ref[rows, :] = qpe.astype(qpe_ref.dtype)
        ckvn = rms(ckv, kvnw_ref[...])
        ckv_ref[rows, :] = ckvn.astype(ckv_ref.dtype)
        kpe_ref[rows, :] = kpe.astype(kpe_ref.dtype)
        ckv32_ref[rows, :] = ckvn
        kpe32_ref[rows, :] = kpe


def _mla_in(x, nw, scale, shift, w, qnw, kvnw, wuq, rope_tabs, seq_len, tm):
    t, d = x.shape
    per_seq = seq_len // tm if scale.shape[0] > 1 else None
    row = lambda i: (i, 0)
    const = lambda i: (0, 0)
    mod = (lambda i: (i // per_seq, 0, 0)) if per_seq else (lambda i: (0, 0, 0))
    in_specs = [pl.BlockSpec((tm, d), row), pl.BlockSpec((1, d), const),
                pl.BlockSpec((None, 1, d), mod), pl.BlockSpec((None, 1, d), mod),
                pl.BlockSpec(w.shape, const), pl.BlockSpec((1, Q_LORA), const),
                pl.BlockSpec((1, KV_LORA), const), pl.BlockSpec(wuq.shape, const)]
    args = [x, nw, scale, shift, w, qnw, kvnw, wuq]
    if rope_tabs is not None:
        nt = seq_len // tm
        pos = lambda i: (i % nt, 0)
        in_specs += [pl.BlockSpec((tm, LANES), pos), pl.BlockSpec((tm, LANES), pos)]
        args += list(rope_tabs)
    widths = (MLA_HEADS * MLA_NOPE, MLA_HEADS * MLA_ROPE, KV_LORA, LANES, MLA_WIDTH, KV_LORA, LANES)
    dts = (BF16, BF16, BF16, BF16, BF16, F32, F32)
    return pl.pallas_call(
        functools.partial(_mla_in_kernel, rope=rope_tabs is not None),
        grid=(t // tm,),
        in_specs=in_specs,
        out_specs=[pl.BlockSpec((tm, c), row) for c in widths],
        out_shape=[jax.ShapeDtypeStruct((t, c), dt) for c, dt in zip(widths, dts)],
        compiler_params=_cparams("arbitrary"),
        name="mla_in_proj",
    )(*args)


MLA_UNIT_ROWS = 512
MLA_KEY_TILE = 256


def _mla_attn_kernel(qn_ref, qpe_ref, ckv_ref, kpe_ref, wukt_ref, wuv_ref, g_ref, o_ref, kcat_scr, v_scr, s_scr, *,
                     pairs_per_step):
    pg = pl.program_id(1)
    qb = pl.program_id(2)
    tq = qn_ref.shape[0]
    c_exp = MLA_SCALE * LOG2E

    @pl.when(qb == 0)
    def _expand():
        ckv = ckv_ref[...]
        eye = jnp.where(_row_iota((LANES, LANES)) == _lane_iota((LANES, LANES)), 1.0, 0.0).astype(BF16)
        kpe_t = _dot_nt(eye, kpe_ref[...]).astype(BF16)
        for i in range(pairs_per_step):
            kcat_scr[i, 0:LANES, :] = _dot_nt(wukt_ref[i * LANES:(i + 1) * LANES, :], ckv).astype(BF16)
            kcat_scr[i, LANES:2 * LANES, :] = kpe_t
            v_scr[i] = _dot(ckv, wuv_ref[:, i * LANES:(i + 1) * LANES]).astype(BF16)

    nkeys = kcat_scr.shape[-1]
    kt_w = min(MLA_KEY_TILE, nkeys)
    nkt = nkeys // kt_w
    ru = min(MLA_UNIT_ROWS, tq)
    lane = _lane_iota((ru, LANES))
    units = [(i, r, e) for i in range(pairs_per_step) for r in range(tq // ru) for e in range(2)]

    def lane_tiles(x):
        return [x[:, j * LANES:(j + 1) * LANES] for j in range(x.shape[1] // LANES)]

    def q_ext(u):
        i, r, e = units[u]
        rows = slice(r * ru, (r + 1) * ru)
        qn = qn_ref[rows, i * LANES:(i + 1) * LANES]
        qpe = qpe_ref[rows, (i // 2) * LANES:(i // 2 + 1) * LANES]
        slot = 2 * (i % 2) + e if pairs_per_step % 2 == 0 else 2 * ((pg * pairs_per_step + i) % 2) + e
        nope_half = (lane < MLA_NOPE) if e == 0 else (lane >= MLA_NOPE)
        return jnp.concatenate([jnp.where(nope_half, qn, jnp.zeros_like(qn)),
                                jnp.where((lane >> 5) == slot, qpe, jnp.zeros_like(qpe))], axis=1)

    def score_tile(u, st, kt):
        cols = slice(kt * kt_w, (kt + 1) * kt_w)
        s = _dot(st["q"], kcat_scr[units[u][0], :, cols])
        s_scr[u % 2, :, cols] = s
        for t in lane_tiles(s):
            st["mrun"] = t if st["mrun"] is None else jnp.maximum(st["mrun"], t)

    def value_tile(u, st, kt):
        cols = slice(kt * kt_w, (kt + 1) * kt_w)
        p = jnp.exp2((s_scr[u % 2, :, cols] - st["m"]) * c_exp)
        for t in lane_tiles(p):
            st["lrun"] = t if st["lrun"] is None else st["lrun"] + t
        pv = _dot(p.astype(BF16), v_scr[units[u][0], cols, :])
        st["acc"] = pv if st["acc"] is None else st["acc"] + pv

    def finish(u, st, done):
        i, r, e = units[u]
        den = jnp.sum(st["lrun"], axis=1, keepdims=True)
        done[e] = st["acc"] / den
        if e == 1:
            rows = slice(r * ru, (r + 1) * ru)
            attn = jnp.where(lane < MLA_V, done[0], done[1])
            gate = g_ref[rows, i * LANES:(i + 1) * LANES].astype(F32)
            o_ref[rows, i * LANES:(i + 1) * LANES] = (attn * gate).astype(o_ref.dtype)

    done = {}
    cur = None
    for u in range(len(units) + 1):
        nxt = dict(q=q_ext(u), mrun=None) if u < len(units) else None
        for kt in range(nkt):
            if nxt is not None:
                score_tile(u, nxt, kt)
            if cur is not None:
                value_tile(u - 1, cur, kt)
        if cur is not None:
            finish(u - 1, cur, done)
        if nxt is not None:
            nxt.update(m=jnp.max(nxt["mrun"], axis=1, keepdims=True), lrun=None, acc=None)
        cur = nxt


def _mla_attn(qn, qpe, ckv_keys, kpe_keys, wukt, wuv, g, nb, seq_len, tq, pairs_per_step):
    t = qn.shape[0]
    nkeys = ckv_keys.shape[1]
    npairs = MLA_HEADS // 2
    ngrp = npairs // pairs_per_step
    nq = seq_len // tq
    wp = pairs_per_step * LANES
    if pairs_per_step % 2 == 0:
        wpe = wp // 2
        pe_map = lambda b, p, i: (b * nq + i, p)
    else:
        wpe = LANES
        pe_map = lambda b, p, i: (b * nq + i, p // 2)
    qmap = lambda b, p, i: (b * nq + i, p)
    return pl.pallas_call(
        functools.partial(_mla_attn_kernel, pairs_per_step=pairs_per_step),
        grid=(nb, ngrp, nq),
        in_specs=[pl.BlockSpec((tq, wp), qmap),
                  pl.BlockSpec((tq, wpe), pe_map),
                  pl.BlockSpec((None, nkeys, KV_LORA), lambda b, p, i: (b, 0, 0)),
                  pl.BlockSpec((None, nkeys, LANES), lambda b, p, i: (b, 0, 0)),
                  pl.BlockSpec((wp, KV_LORA), lambda b, p, i: (p, 0)),
                  pl.BlockSpec((KV_LORA, wp), lambda b, p, i: (0, p)),
                  pl.BlockSpec((tq, wp), qmap)],
        out_specs=pl.BlockSpec((tq, wp), qmap),
        out_shape=jax.ShapeDtypeStruct((t, MLA_WIDTH), BF16),
        scratch_shapes=[pltpu.VMEM((pairs_per_step, 2 * LANES, nkeys), BF16),
                        pltpu.VMEM((pairs_per_step, nkeys, LANES), BF16),
                        pltpu.VMEM((2, min(MLA_UNIT_ROWS, tq), nkeys), F32)],
        compiler_params=_cparams("arbitrary", "arbitrary", "arbitrary"),
        name="mla_attention",
    )(qn, qpe, ckv_keys, kpe_keys, wukt, wuv, g)


def _mla_attn_seq_kernel(qn_ref, qpe_ref, ckv_ref, kpe_ref, wukt_ref, wuv_ref, g_ref, o_ref,
                         kcat_scr, v_scr, s_scr, m_scr, half_scr):
    pair = pl.program_id(1)
    seq = qn_ref.shape[0]
    nkeys = kcat_scr.shape[-1]
    ru, kt_w = MLA_UNIT_ROWS, MLA_KEY_TILE
    nkt = nkeys // kt_w
    nrb = seq // ru
    c_exp = MLA_SCALE * LOG2E
    lane = _lane_iota((ru, LANES))

    ckv = ckv_ref[...]
    eye = jnp.where(_row_iota((LANES, LANES)) == _lane_iota((LANES, LANES)), 1.0, 0.0).astype(BF16)
    kcat_scr[0:LANES, :] = _dot_nt(wukt_ref[...], ckv).astype(BF16)
    kcat_scr[LANES:2 * LANES, :] = _dot_nt(eye, kpe_ref[...]).astype(BF16)
    v_scr[:, 0:LANES] = _dot(ckv, wuv_ref[...]).astype(BF16)
    v_scr[:, LANES:2 * LANES] = jnp.ones((nkeys, LANES), BF16)

    def rows_of(rb):
        return slice(rb * ru, (rb + 1) * ru) if isinstance(rb, int) else pl.ds(pl.multiple_of(rb * ru, ru), ru)

    def q_ext(rb, e):
        rows = rows_of(rb)
        qn = qn_ref[rows, :]
        qpe = qpe_ref[rows, :]
        nope_half = (lane < MLA_NOPE) if e == 0 else (lane >= MLA_NOPE)
        slot = 2 * (pair % 2) + e
        return jnp.concatenate([jnp.where(nope_half, qn, jnp.zeros_like(qn)),
                                jnp.where((lane >> 5) == slot, qpe, jnp.zeros_like(qpe))], axis=1)

    def phase(score, value):
        q = q_ext(*score) if score is not None else None
        mrun = acc = None
        for kt in range(nkt):
            cols = slice(kt * kt_w, (kt + 1) * kt_w)
            if score is not None:
                s = _dot(q, kcat_scr[:, cols])
                s_scr[score[1], :, cols] = s
                for j in range(kt_w // LANES):
                    t = s[:, j * LANES:(j + 1) * LANES]
                    mrun = t if mrun is None else jnp.maximum(mrun, t)
            if value is not None:
                mb = m_scr[value]
                p = jnp.concatenate(
                    [jnp.exp2((s_scr[value, :, kt * kt_w + j * LANES:kt * kt_w + (j + 1) * LANES] - mb) * c_exp)
                     for j in range(kt_w // LANES)], axis=1)
                pv = _dot(p.astype(BF16), v_scr[cols, :])
                acc = pv if acc is None else acc + pv
        if score is not None:
            m_scr[score[1]] = jnp.broadcast_to(jnp.max(mrun, axis=1, keepdims=True), (ru, LANES))
        return None if acc is None else acc[:, 0:LANES] / acc[:, LANES:2 * LANES]

    phase((0, 0), None)

    def row_block(rb, carry):
        half_scr[...] = phase((rb, 1), 0)
        out1 = phase((jnp.minimum(rb + 1, nrb - 1), 0), 1)
        rows = rows_of(rb)
        attn = jnp.where(lane < MLA_V, half_scr[...], out1)
        o_ref[rows, :] = (attn * g_ref[rows, :].astype(F32)).astype(o_ref.dtype)
        return carry

    lax.fori_loop(0, nrb, row_block, 0)


def _mla_attn_seq(qn, qpe, ckv_keys, kpe_keys, wukt, wuv, g, nb, seq_len):
    t = qn.shape[0]
    nkeys = ckv_keys.shape[1]
    assert seq_len % MLA_UNIT_ROWS == 0 and nkeys % MLA_KEY_TILE == 0
    qmap = lambda b, p: (b, p)
    return pl.pallas_call(
        _mla_attn_seq_kernel,
        grid=(nb, MLA_HEADS // 2),
        in_specs=[pl.BlockSpec((seq_len, LANES), qmap),
                  pl.BlockSpec((seq_len, LANES), lambda b, p: (b, p // 2)),
                  pl.BlockSpec((None, nkeys, KV_LORA), lambda b, p: (b, 0, 0)),
                  pl.BlockSpec((None, nkeys, LANES), lambda b, p: (b, 0, 0)),
                  pl.BlockSpec((LANES, KV_LORA), lambda b, p: (p, 0)),
                  pl.BlockSpec((KV_LORA, LANES), lambda b, p: (0, p)),
                  pl.BlockSpec((seq_len, LANES), qmap)],
        out_specs=pl.BlockSpec((seq_len, LANES), qmap),
        out_shape=jax.ShapeDtypeStruct((t, MLA_WIDTH), BF16),
        scratch_shapes=[pltpu.VMEM((2 * LANES, nkeys), BF16),
                        pltpu.VMEM((nkeys, 2 * LANES), BF16),
                        pltpu.VMEM((2, MLA_UNIT_ROWS, nkeys), F32),
                        pltpu.VMEM((2, MLA_UNIT_ROWS, LANES), F32),
                        pltpu.VMEM((MLA_UNIT_ROWS, LANES), F32)],
        compiler_params=_cparams("arbitrary", "arbitrary"),
        name="mla_attention_seq",
    )(qn, qpe, ckv_keys, kpe_keys, wukt, wuv, g)


def _mla_out_kernel(a_ref, w_ref, x_ref, gate_ref, fw_ref, o_ref):
    for rows in _row_splits(x_ref.shape[0]):
        xn = x_ref[rows, :] + gate_ref[...] * _dot(a_ref[rows, :], w_ref[...])
        ms = jnp.mean(xn * xn, axis=-1, keepdims=True)
        o_ref[rows, :] = (xn * lax.rsqrt(ms + EPS)) * fw_ref[...]


def _mla_out(a, w, x, gate, fw, seq_len, tm):
    t, d = x.shape
    per_seq = seq_len // tm if gate.shape[0] > 1 else None
    row = lambda i: (i, 0)
    const = lambda i: (0, 0)
    mod = (lambda i: (i // per_seq, 0, 0)) if per_seq else (lambda i: (0, 0, 0))
    return pl.pallas_call(
        _mla_out_kernel,
        grid=(t // tm,),
        in_specs=[pl.BlockSpec((tm, MLA_WIDTH), row), pl.BlockSpec(w.shape, const),
                  pl.BlockSpec((tm, d), row), pl.BlockSpec((None, 1, d), mod), pl.BlockSpec((1, d), const)],
        out_specs=pl.BlockSpec((tm, d), row),
        out_shape=jax.ShapeDtypeStruct((t, d), F32),
        compiler_params=_cparams("arbitrary"),
        name="mla_out_proj_final_norm",
    )(a, w, x, gate, fw)


def _rope_tables(length, dim):
    rows = length // GRID_W
    f32 = np.float32
    row = np.repeat(np.arange(rows), GRID_W).astype(f32)
    col = np.tile(np.arange(GRID_W), rows).astype(f32)
    nf = dim // 4
    inv = (f32(1.0) / np.power(f32(ROPE_BASE), np.arange(nf, dtype=f32) / f32(nf))).astype(f32)
    ar = row[:, None] * inv[None, :]
    ac = col[:, None] * inv[None, :]
    ang = np.concatenate([ar, ar, ac, ac], axis=-1).astype(f32)
    sign = np.tile(np.concatenate([-np.ones((nf,), f32), np.ones((nf,), f32)]), 2)
    reps = LANES // dim
    return (jnp.asarray(np.tile(np.cos(ang).astype(f32), (1, reps))),
            jnp.asarray(np.tile((np.sin(ang) * sign).astype(f32), (1, reps))))


PROJ_TM = 1024
CONV_TM = 512


def _group(x, seq_len, conds_rows, mods, wts, tabs_a, tabs_c, ctx_cache):
    nb, _, d = x.shape
    t = nb * seq_len
    latent = ctx_cache is not None
    xf = x.reshape(t, d)
    tm = PROJ_TM

    def mod_rows(layer):
        m = mods[layer][conds_rows]
        sh, sc, gt = jnp.split(m[:, None, :], 3, axis=-1)
        return sh, sc, gt

    sh, sc, gt = mod_rows(0)
    q, k, v, g, z, xbc, dt, *kv_t = _ab_in(xf, wts["norm_w"][0:1], sc, sh, wts["ab_w_in"], wts["ab_w_dt"],
                                           tabs_a if latent else None, seq_len, tm)
    xbc_c = _conv(xbc, wts["ab_conv_w"], wts["ab_conv_b"], seq_len, min(CONV_TM, seq_len))
    if latent:
        k_ctx, v_ctx, s_f0, s_b0, _, _ = ctx_cache
        attn = _attn_band(wts["ab_sink"], q, k, v, k_ctx, v_ctx, g, nb, seq_len)
    else:
        attn = _attn_ctx(wts["ab_sink"], q, k, v, g, seq_len)
    if latent:
        yf, yb, s_f, s_b = _ssd(xbc_c, dt, wts["ssd_par"], wts["ssd_dskip"], s_f0, s_b0, nb, seq_len)
    else:
        yf, s_f, s_b = _ssd_seq(xbc_c, dt, wts["ssd_par"], wts["ssd_dskip"], nb, seq_len)
        yb = None
    x1 = _ab_out(attn, yf, yb, z, wts["ab_gnorm_w"], wts["ab_w_out"], xf, gt, seq_len, tm)

    sh, sc, gt = mod_rows(1)
    qn, qpe, ckv, kpe, g1, ckv32, kpe32 = _mla_in(
        x1, wts["norm_w"][1:2], sc, sh, wts["mla_w_in"], wts["mla_q_norm_w"], wts["mla_kv_norm_w"],
        wts["mla_w_uq"], tabs_c if latent else None, seq_len, tm)
    ckv_keys = ckv.reshape(nb, seq_len, KV_LORA)
    kpe_keys = kpe.reshape(nb, seq_len, LANES)
    if latent:
        ckv_x, kpe_x = ctx_cache[4], ctx_cache[5]
        ckv_keys = jnp.concatenate([ckv_keys, ckv_x], axis=1)
        kpe_keys = jnp.concatenate([kpe_keys, kpe_x], axis=1)
        attn1 = _mla_attn_seq(qn, qpe, ckv_keys, kpe_keys, wts["mla_w_ukt"], wts["mla_w_uv"], g1, nb, seq_len)
    else:
        attn1 = _mla_attn(qn, qpe, ckv_keys, kpe_keys, wts["mla_w_ukt"], wts["mla_w_uv"], g1, nb, seq_len,
                          seq_len, MLA_HEADS // 2)
    y = _mla_out(attn1, wts["mla_w_out"], x1, gt, wts["final_norm_w"], seq_len, tm)
    return y.reshape(nb, seq_len, d), (kv_t, s_f, s_b, ckv32, kpe32)


def kernel(x_prompt, x_sample, cache_a_k, cache_a_v, state_ssd_fwd, state_ssd_bwd, cache_mla_ckv, cache_mla_kpe,
           c, c_ctx, ada_w, ada_b, norm_w, ab_w_in, ab_sink, ab_conv_w, ab_conv_b, ab_dt_bias, ab_a_log,
           ab_d_skip, ab_gnorm_w, ab_w_out, mla_w_in, mla_q_norm_w, mla_kv_norm_w, mla_w_uq, mla_w_ukv,
           mla_w_out, final_norm_w):
    batch, seq, d = x_prompt.shape
    dec_batch, dec_seq, _ = x_sample.shape
    assert ada_w.shape[0] == 2 and ab_w_in.shape[0] == 1 and mla_w_in.shape[0] == 1
    assert dec_batch + 1 <= SUBLANES

    conds = jnp.concatenate([c_ctx[None, :], c, jnp.zeros((SUBLANES - 1 - dec_batch, d), F32)], axis=0)
    mods = _modulation(conds, ada_w, ada_b)

    n_main = sum(AB_COLS[:-1])
    w_ab = ab_w_in[0].astype(BF16)
    w_dt = jnp.concatenate([ab_w_in[0][:, n_main:], jnp.zeros((d, LANES - 2 * SSD_HEADS), F32)], axis=1).astype(BF16)
    mw = mla_w_in[0]
    o_kpe = Q_LORA + KV_LORA
    w_mla = jnp.concatenate([mw[:, :o_kpe], jnp.tile(mw[:, o_kpe:o_kpe + MLA_ROPE], (1, LANES // MLA_ROPE)),
                             mw[:, o_kpe + MLA_ROPE:]], axis=1).astype(BF16)
    wuq = mla_w_uq[0].reshape(Q_LORA, MLA_HEADS, MLA_NOPE + MLA_ROPE)
    wuq = jnp.concatenate([wuq[:, :, :MLA_NOPE].reshape(Q_LORA, -1), wuq[:, :, MLA_NOPE:].reshape(Q_LORA, -1)],
                          axis=1).astype(BF16)
    wukv = mla_w_ukv[0].reshape(KV_LORA, MLA_HEADS, MLA_NOPE + MLA_V)
    wukt = wukv[:, :, :MLA_NOPE].reshape(KV_LORA, -1).T.astype(BF16)
    wuv = wukv[:, :, MLA_NOPE:].reshape(KV_LORA, -1).astype(BF16)
    pad_lanes = lambda r: jnp.concatenate([r.reshape(1, -1), jnp.zeros((1, LANES - r.size), F32)], axis=1)
    ssd_par = jnp.concatenate([pad_lanes(ab_dt_bias[0]), pad_lanes(ab_a_log[0]),
                               jnp.zeros((SUBLANES - 2, LANES), F32)], axis=0)
    wts = dict(
        norm_w=norm_w, ab_w_in=w_ab, ab_w_dt=w_dt, ab_sink=ab_sink[0],
        ab_conv_w=jnp.concatenate([ab_conv_w[0], jnp.zeros((SUBLANES - CONV_K, CONV_CH), F32)], axis=0),
        ab_conv_b=ab_conv_b[0][None, :], ssd_par=ssd_par,
        ssd_dskip=jnp.repeat(ab_d_skip[0], SSD_HEAD_DIM)[None, :],
        ab_gnorm_w=ab_gnorm_w[0][None, :], ab_w_out=ab_w_out[0].astype(BF16),
        mla_w_in=w_mla, mla_q_norm_w=mla_q_norm_w[0][None, :], mla_kv_norm_w=mla_kv_norm_w[0][None, :],
        mla_w_uq=wuq, mla_w_ukt=wukt, mla_w_uv=wuv, mla_w_out=mla_w_out[0].astype(BF16),
        final_norm_w=final_norm_w[None, :],
    )
    tabs_a = _rope_tables(dec_seq, A_HEAD_DIM)
    tabs_c = _rope_tables(dec_seq, MLA_ROPE)

    y_prompt, ((k_t, v_t), s_f, s_b, ckv32, kpe32) = _group(
        x_prompt, seq, jnp.zeros((1,), jnp.int32), mods, wts, tabs_a, tabs_c, None)

    past = cache_a_k.shape[2]
    ctx_cache = (cache_a_k[:, 0].reshape(dec_batch, past, A_KV_WIDTH),
                 cache_a_v[:, 0].reshape(dec_batch, past, A_KV_WIDTH),
                 state_ssd_fwd[:, 0], state_ssd_bwd[:, 0],
                 cache_mla_ckv[:, 0].astype(BF16),
                 jnp.tile(cache_mla_kpe[:, 0], (1, 1, LANES // MLA_ROPE)).astype(BF16))
    y_sample, _ = _group(x_sample, dec_seq, 1 + jnp.arange(dec_batch), mods, wts, tabs_a, tabs_c, ctx_cache)

    cache_layout = lambda u: u.reshape(batch, 1, A_KV_HEADS, A_HEAD_DIM, seq).transpose(0, 1, 4, 2, 3)
    return (y_prompt, y_sample,
            cache_layout(k_t), cache_layout(v_t),
            s_f[:, None], s_b[:, None],
            ckv32.reshape(batch, 1, seq, KV_LORA), kpe32[:, :MLA_ROPE].reshape(batch, 1, seq, MLA_ROPE))
```
